```python
import math
import jax, jax.numpy as jnp
from jax import lax
import numpy as np

D_MODEL = 1024
BATCH = 8
SEQ = 4096
DEPTH = 1
DEC_BATCH = 1
DEC_SEQ = 16384
PAST_LEN = 128

ATT_HEADS = 8
ATT_KV_HEADS = 2
ATT_GROUP = ATT_HEADS // ATT_KV_HEADS
ATT_HEAD_DIM = D_MODEL // 16
ATT_WIDTH = ATT_HEADS * ATT_HEAD_DIM
ATT_KV_WIDTH = ATT_KV_HEADS * ATT_HEAD_DIM
WINDOW = 128
BLOCK = WINDOW
N_BUCKETS = 32
MAX_DISTANCE = 128
ML_HEADS = 4
ML_HEAD_DIM = D_MODEL // 8
ML_WIDTH = ML_HEADS * ML_HEAD_DIM
ML_CHUNK = 128
CONV_WIDTH = 5
N_GATES = 4 * ML_HEADS
MIX_WIDTH = ATT_WIDTH + ML_WIDTH
ATT_Q_END = ATT_WIDTH
ATT_K_END = ATT_Q_END + ATT_KV_WIDTH
ATT_V_END = ATT_K_END + ATT_KV_WIDTH
ML_QK_END = ATT_V_END + 2 * ML_WIDTH
ML_V_END = ML_QK_END + ML_WIDTH
ML_O_END = ML_V_END + ML_WIDTH
IN_WIDTH = ML_O_END + N_GATES
MEM_TOKENS = 256
MEM_HEADS = 4
MEM_HEAD_DIM = D_MODEL // MEM_HEADS
N_EXPERTS = 32
TOP_K = 4
D_FF = D_MODEL
SWIGLU_LIMIT = 7.0
SWIGLU_ALPHA = 1.702
ROW_BLOCK = 512
LN_EPS = 1e-5
DN_ALPHA = (2 * DEPTH) ** 0.25
DN_BETA = (8 * DEPTH) ** -0.25

kernel_name = "hybrid_swa_mlstm_moe_encoder"


def layer_norm(x, g, b):
    xf = x.astype(jnp.float32)
    mu = xf.mean(-1, keepdims=True)
    var = jnp.square(xf - mu).mean(-1, keepdims=True)
    return ((xf - mu) * lax.rsqrt(var + LN_EPS) * g + b).astype(x.dtype)


def t5_bucket(rel):
    half = N_BUCKETS // 2
    exact = half // 2
    n = np.abs(rel)
    large = exact + (np.log(np.maximum(n, 1) / exact) / np.log(MAX_DISTANCE / exact) * (half - exact)).astype(np.int32)
    large = np.minimum(large, half - 1)
    return ((rel > 0).astype(np.int32) * half + np.where(n < exact, n, large)).astype(np.int32)


def window_attention(q, k, v, sink, rel_bias):
    B, S = q.shape[0], q.shape[1]
    nb = S // BLOCK
    qb = q.reshape(B, nb, BLOCK, ATT_KV_HEADS, ATT_GROUP, ATT_HEAD_DIM).astype(jnp.float32)

    def band(t):
        tp = jnp.pad(t, ((0, 0), (WINDOW, WINDOW), (0, 0), (0, 0)))
        parts = [tp[:, o:o + S].reshape(B, nb, BLOCK, ATT_KV_HEADS, ATT_HEAD_DIM) for o in (0, BLOCK, 2 * BLOCK)]
        return jnp.concatenate(parts, axis=2).astype(jnp.float32)

    kb, vb = band(k), band(v)
    logits = jnp.einsum('bnqhgd,bnkhd->bnhgqk', qb, kb) * (ATT_HEAD_DIM ** -0.5)
    rel = np.arange(3 * BLOCK)[None, :] - BLOCK - np.arange(BLOCK)[:, None]
    bias = rel_bias[t5_bucket(rel)].astype(jnp.float32)
    bias = bias.transpose(2, 0, 1).reshape(ATT_KV_HEADS, ATT_GROUP, BLOCK, 3 * BLOCK)
    key_pos = np.arange(nb)[:, None] * BLOCK - BLOCK + np.arange(3 * BLOCK)[None, :]
    mask = (np.abs(rel) <= WINDOW)[None] & ((key_pos >= 0) & (key_pos < S))[:, None, :]
    logits = jnp.where(mask[None, :, None, None], logits + bias, -jnp.inf)
    sink_l = sink.astype(jnp.float32).reshape(ATT_KV_HEADS, ATT_GROUP)[:, :, None]
    m = jnp.maximum(logits.max(-1), sink_l)
    p = jnp.exp(logits - m[..., None])
    p = p / (p.sum(-1) + jnp.exp(sink_l - m))[..., None]
    out = jnp.einsum('bnhgqk,bnkhd->bnqhgd', p, vb)
    return out.reshape(B, S, ATT_WIDTH)


def mlstm_chunkwise(q, k, v, i_pre, f_pre):
    B, H, S, dh = q.shape
    nc = S // ML_CHUNK
    q = q.reshape(B, H, nc, ML_CHUNK, dh)
    k = k.reshape(B, H, nc, ML_CHUNK, dh) * (dh ** -0.5)
    v = v.reshape(B, H, nc, ML_CHUNK, dh)
    ig = i_pre.reshape(B, H, nc, ML_CHUNK)
    b = jnp.cumsum(jax.nn.log_sigmoid(f_pre).reshape(B, H, nc, ML_CHUNK), axis=-1)
    g = b[..., -1]
    a = g[..., None] - b + ig
    a_max = a.max(-1)
    wa = jnp.exp(a - a_max[..., None])
    c_loc = jnp.einsum('bhcs,bhcsk,bhcsv->bhckv', wa, k, v)
    n_loc = jnp.einsum('bhcs,bhcsk->bhck', wa, k)

    def step(carry, inp):
        c, n, m = carry
        cl, nl, am, gc = inp
        m_new = jnp.maximum(gc + m, am)
        s_old = jnp.exp(gc + m - m_new)
        s_new = jnp.exp(am - m_new)
        c_new = s_old[..., None, None] * c + s_new[..., None, None] * cl
        n_new = s_old[..., None] * n + s_new[..., None] * nl
        return (c_new, n_new, m_new), (c, n, m)

    init = (jnp.zeros((B, H, dh, dh), jnp.float32), jnp.zeros((B, H, dh), jnp.float32), jnp.zeros((B, H), jnp.float32))
    xs = tuple(jnp.moveaxis(t, 2, 0) for t in (c_loc, n_loc, a_max, g))
    _, (c_prev, n_prev, m_prev) = lax.scan(step, init, xs)
    c_prev = jnp.moveaxis(c_prev, 0, 2)
    n_prev = jnp.moveaxis(n_prev, 0, 2)
    m_prev = jnp.moveaxis(m_prev, 0, 2)
    lower = np.tril(np.ones((ML_CHUNK, ML_CHUNK), dtype=bool))
    d = jnp.where(lower, b[..., :, None] - b[..., None, :] + ig[..., None, :], -jnp.inf)
    inter = b + m_prev[..., None]
    m = jnp.maximum(inter, d.max(-1))
    s = jnp.einsum('bhcjd,bhcsd->bhcjs', q, k) * jnp.exp(d - m[..., None])
    w_inter = jnp.exp(inter - m)
    num = jnp.einsum('bhcjs,bhcsv->bhcjv', s, v) + w_inter[..., None] * jnp.einsum('bhcjk,bhckv->bhcjv', q, c_prev)
    den = s.sum(-1) + w_inter * jnp.einsum('bhcjk,bhck->bhcj', q, n_prev)
    h = num / jnp.maximum(jnp.abs(den), jnp.exp(-m))[..., None]
    return h.reshape(B, H, S, dh)


def centred_conv(x, w, b):
    c = x.shape[-1]
    y = lax.conv_general_dilated(x, w[:, None, :], window_strides=(1,),
                                 padding=[(CONV_WIDTH // 2, CONV_WIDTH // 2)],
                                 dimension_numbers=('NWC', 'WIO', 'NWC'), feature_group_count=c)
    return y + b


def hybrid_mixer(x, rel_bias, w_in, b_gates, conv_w, conv_b, mh_gain, attn_sink, w_out):
    B, S, _ = x.shape
    z = x @ w_in
    q_a = z[..., :ATT_Q_END].reshape(B, S, ATT_HEADS, ATT_HEAD_DIM)
    k_a = z[..., ATT_Q_END:ATT_K_END].reshape(B, S, ATT_KV_HEADS, ATT_HEAD_DIM)
    v_a = z[..., ATT_K_END:ATT_V_END].reshape(B, S, ATT_KV_HEADS, ATT_HEAD_DIM)
    att = window_attention(q_a, k_a, v_a, attn_sink, rel_bias)
    qk_m = jax.nn.silu(centred_conv(z[..., ATT_V_END:ML_QK_END], conv_w, conv_b))

    def heads(t):
        return t.reshape(B, S, ML_HEADS, ML_HEAD_DIM).transpose(0, 2, 1, 3).astype(jnp.float32)

    q_m = heads(qk_m[..., :ML_WIDTH])
    k_m = heads(qk_m[..., ML_WIDTH:])
    v_m = heads(z[..., ML_QK_END:ML_V_END])
    o_m = z[..., ML_V_END:ML_O_END].astype(jnp.float32)
    gates = (z[..., ML_O_END:] + b_gates).astype(jnp.float32).reshape(B, S, 4, ML_HEADS).transpose(2, 0, 3, 1)
    i_f, f_f, i_b, f_b = gates[0], gates[1], gates[2], gates[3]
    h_f = mlstm_chunkwise(q_m, k_m, v_m, i_f, f_f)
    rev = lambda t: jnp.flip(t, axis=2)
    h_b = rev(mlstm_chunkwise(rev(q_m), rev(k_m), rev(v_m), rev(i_b), rev(f_b)))
    h = h_f + h_b
    mu = h.mean(-1, keepdims=True)
    var = jnp.square(h - mu).mean(-1, keepdims=True)
    h = ((h - mu) * lax.rsqrt(var + LN_EPS)).transpose(0, 2, 1, 3).reshape(B, S, ML_WIDTH) * mh_gain
    h = h * jax.nn.sigmoid(o_m)
    mix = jnp.concatenate([att, h], axis=-1).astype(x.dtype)
    return mix @ w_out


def memory_cross_attention(x, mem, wq, wkv, wo):
    B, S, _ = x.shape
    q = (x @ wq).reshape(B, S, MEM_HEADS, MEM_HEAD_DIM).astype(jnp.float32)
    kv = (mem @ wkv).reshape(B, mem.shape[1], 2, MEM_HEADS, MEM_HEAD_DIM).astype(jnp.float32)
    logits = jnp.einsum('bshd,bmhd->bhsm', q, kv[:, :, 0]) * (MEM_HEAD_DIM ** -0.5)
    p = jax.nn.softmax(logits, axis=-1)
    out = jnp.einsum('bhsm,bmhd->bshd', p, kv[:, :, 1]).reshape(B, S, D_MODEL).astype(x.dtype)
    return out @ wo


def moe_ffn(x, w_router, b_router, w_up, b_up, w_down, b_down):
    B, S, D = x.shape
    T = B * S
    xt = x.reshape(T, D)
    logits = (xt @ w_router + b_router).astype(jnp.float32)
    top_vals, top_idx = lax.top_k(logits, TOP_K)
    gates = jax.nn.softmax(top_vals, axis=-1)
    flat_e = top_idx.reshape(-1)
    flat_tok = jnp.arange(T * TOP_K, dtype=jnp.int32) // TOP_K
    flat_g = gates.reshape(-1)
    order = jnp.argsort(flat_e)
    sorted_e = flat_e[order]
    counts = jnp.zeros((N_EXPERTS,), jnp.int32).at[flat_e].add(1)
    padded = ((counts + ROW_BLOCK - 1) // ROW_BLOCK) * ROW_BLOCK
    start_sorted = jnp.cumsum(counts) - counts
    cum_padded = jnp.cumsum(padded)
    start_padded = cum_padded - padded
    rank = jnp.arange(T * TOP_K, dtype=jnp.int32) - start_sorted[sorted_e]
    dest = start_padded[sorted_e] + rank
    n_rows = T * TOP_K + N_EXPERTS * ROW_BLOCK
    n_blk = n_rows // ROW_BLOCK
    row_tok = jnp.zeros((n_rows,), jnp.int32).at[dest].set(flat_tok[order])
    row_gate = jnp.zeros((n_rows,), jnp.float32).at[dest].set(flat_g[order])
    block_e = jnp.minimum(jnp.searchsorted(cum_padded, jnp.arange(n_blk) * ROW_BLOCK, side='right'), N_EXPERTS - 1)

    def block_fn(args):
        tok, gate, e = args
        hu = xt[tok] @ w_up[e] + b_up[e]
        h_glu = jnp.minimum(hu[:, :D_FF], SWIGLU_LIMIT)
        h_lin = jnp.clip(hu[:, D_FF:], -SWIGLU_LIMIT, SWIGLU_LIMIT)
        hh = h_glu * jax.nn.sigmoid(SWIGLU_ALPHA * h_glu) * (h_lin + 1.0)
        return (hh @ w_down[e] + b_down[e]).astype(jnp.float32) * gate[:, None]

    rows = lax.map(block_fn, (row_tok.reshape(n_blk, ROW_BLOCK), row_gate.reshape(n_blk, ROW_BLOCK), block_e))
    y = jnp.zeros((T, D), jnp.float32).at[row_tok].add(rows.reshape(n_rows, D))
    return y.astype(x.dtype).reshape(B, S, D)


def encoder_layer(x, mem, rel_bias, w_in, b_gates, conv_w, conv_b, mh_gain, attn_sink, w_out, ln1_g, ln1_b,
                  wq_mem, wkv_mem, wo_mem, ln2_g, ln2_b, w_router, b_router, w_up, b_up, w_down, b_down, ln3_g, ln3_b):
    x = layer_norm(DN_ALPHA * x + hybrid_mixer(x, rel_bias, w_in, b_gates, conv_w, conv_b, mh_gain, attn_sink, w_out), ln1_g, ln1_b)
    x = layer_norm(DN_ALPHA * x + memory_cross_attention(x, mem, wq_mem, wkv_mem, wo_mem), ln2_g, ln2_b)
    x = layer_norm(DN_ALPHA * x + moe_ffn(x, w_router, b_router, w_up, b_up, w_down, b_down), ln3_g, ln3_b)
    return x


def setup_inputs(seed: int = 0) -> dict:
    key = jax.random.key(seed)
    ks = jax.random.split(key, 32)
    nrm = lambda k, shape, scale: jax.random.normal(k, shape, jnp.float32) * scale
    fb = jnp.linspace(3.0, 6.0, ML_HEADS)
    zb = jnp.zeros((ML_HEADS,), jnp.float32)
    b_gates = (nrm(ks[6], (DEPTH, 4, ML_HEADS), 0.1) + jnp.stack([zb, fb, zb, fb])).reshape(DEPTH, N_GATES)
    return {
        'x_prompt': nrm(ks[0], (BATCH, SEQ, D_MODEL), 1.0),
        'x_sample': nrm(ks[1], (DEC_BATCH, DEC_SEQ, D_MODEL), 1.0),
        'mem_prompt': nrm(ks[2], (BATCH, MEM_TOKENS, D_MODEL), 1.0),
        'mem_sample': nrm(ks[3], (DEC_BATCH, MEM_TOKENS, D_MODEL), 1.0),
        'rel_bias': nrm(ks[4], (N_BUCKETS, ATT_HEADS), 0.5),
        'w_in': nrm(ks[5], (DEPTH, D_MODEL, IN_WIDTH), D_MODEL ** -0.5),
        'b_gates': b_gates,
        'conv_w': nrm(ks[7], (DEPTH, CONV_WIDTH, 2 * ML_WIDTH), CONV_WIDTH ** -0.5),
        'conv_b': nrm(ks[8], (DEPTH, 2 * ML_WIDTH), 0.02),
        'mh_gain': 1.0 + nrm(ks[9], (DEPTH, ML_WIDTH), 0.02),
        'attn_sink': nrm(ks[10], (DEPTH, ATT_HEADS), 0.5),
        'w_out': nrm(ks[11], (DEPTH, MIX_WIDTH, D_MODEL), MIX_WIDTH ** -0.5 * DN_BETA),
        'ln1_g': 1.0 + nrm(ks[12], (DEPTH, D_MODEL), 0.02),
        'ln1_b': nrm(ks[13], (DEPTH, D_MODEL), 0.02),
        'wq_mem': nrm(ks[14], (DEPTH, D_MODEL, D_MODEL), D_MODEL ** -0.5),
        'wkv_mem': nrm(ks[15], (DEPTH, D_MODEL, 2 * D_MODEL), D_MODEL ** -0.5),
        'wo_mem': nrm(ks[16], (DEPTH, D_MODEL, D_MODEL), D_MODEL ** -0.5 * DN_BETA),
        'ln2_g': 1.0 + nrm(ks[17], (DEPTH, D_MODEL), 0.02),
        'ln2_b': nrm(ks[18], (DEPTH, D_MODEL), 0.02),
        'w_router': nrm(ks[19], (DEPTH, D_MODEL, N_EXPERTS), D_MODEL ** -0.5),
        'b_router': nrm(ks[20], (DEPTH, N_EXPERTS), 0.01),
        'w_up': nrm(ks[21], (DEPTH, N_EXPERTS, D_MODEL, 2 * D_FF), D_MODEL ** -0.5),
        'b_up': nrm(ks[22], (DEPTH, N_EXPERTS, 2 * D_FF), 0.01),
        'w_down': nrm(ks[23], (DEPTH, N_EXPERTS, D_FF, D_MODEL), D_FF ** -0.5 * DN_BETA),
        'b_down': nrm(ks[24], (DEPTH, N_EXPERTS, D_MODEL), 0.01),
        'ln3_g': 1.0 + nrm(ks[25], (DEPTH, D_MODEL), 0.02),
        'ln3_b': nrm(ks[26], (DEPTH, D_MODEL), 0.02),
    }


def reference(x_prompt, x_sample, mem_prompt, mem_sample, rel_bias, w_in, b_gates, conv_w, conv_b, mh_gain,
              attn_sink, w_out, ln1_g, ln1_b, wq_mem, wkv_mem, wo_mem, ln2_g, ln2_b, w_router, b_router,
              w_up, b_up, w_down, b_down, ln3_g, ln3_b):
    y_prompt = x_prompt
    y_sample = x_sample
    for l in range(DEPTH):
        lw = [w[l] for w in (w_in, b_gates, conv_w, conv_b, mh_gain, attn_sink, w_out, ln1_g, ln1_b,
                             wq_mem, wkv_mem, wo_mem, ln2_g, ln2_b, w_router, b_router, w_up, b_up,
                             w_down, b_down, ln3_g, ln3_b)]
        y_prompt = encoder_layer(y_prompt, mem_prompt, rel_bias, *lw)
        y_sample = encoder_layer(y_sample, mem_sample, rel_bias, *lw)
    return (y_prompt, y_sample)
```

```python
import functools

import numpy as np
import jax
import jax.numpy as jnp
from jax import lax
from jax.experimental import pallas as pl
from jax.experimental.pallas import tpu as pltpu

F32 = jnp.float32
BF16 = jnp.bfloat16

D_MODEL = 1024
ATT_HEADS = 8
ATT_KV_HEADS = 2
ATT_GROUP = ATT_HEADS // ATT_KV_HEADS
ATT_HEAD_DIM = 64
ATT_WIDTH = ATT_HEADS * ATT_HEAD_DIM
WINDOW = 128
BLOCK = WINDOW
N_BUCKETS = 32
MAX_DISTANCE = 128
ML_HEADS = 4
ML_HEAD_DIM = 128
ML_WIDTH = ML_HEADS * ML_HEAD_DIM
ML_CHUNK = 128
CONV_WIDTH = 5
N_GATES = 4 * ML_HEADS
MEM_TOKENS = 256
MEM_HEADS = 4
MEM_HEAD_DIM = D_MODEL // MEM_HEADS
N_EXPERTS = 32
TOP_K = 4
D_FF = D_MODEL
SWIGLU_LIMIT = 7.0
SWIGLU_ALPHA = 1.702
LN_EPS = 1e-5
DN_ALPHA = 2.0 ** 0.25

Z_WIDTH = 2 * ML_WIDTH + ML_WIDTH + ML_WIDTH + ATT_WIDTH + 2 * ATT_KV_HEADS * ATT_HEAD_DIM
ZB_MLV = 2
ZB_MLO = 3
ZB_ATTQ = 4
ZB_ATTK = 20
ZB_ATTV = 21

ROW_TILE = 512
MOE_ROWS = 512
NEG = -1e30
VMEM_LIMIT = 56 * 1024 * 1024


def _cparams(sem):
    return pltpu.CompilerParams(dimension_semantics=sem, vmem_limit_bytes=VMEM_LIMIT)


def _dot(a, b):
    return jnp.dot(a, b, preferred_element_type=F32)


def _dot_nt(a, b):
    return lax.dot_general(a, b, (((1,), (1,)), ((), ())), preferred_element_type=F32)


def _dot_tn(a, b):
    return lax.dot_general(a, b, (((0,), (0,)), ((), ())), preferred_element_type=F32)


def _dot_exact(a, b):
    return jnp.dot(a, b, preferred_element_type=F32, precision=lax.Precision.HIGHEST)


def _layer_norm(y, g, b):
    mu = jnp.mean(y, axis=-1, keepdims=True)
    yc = y - mu
    var = jnp.mean(yc * yc, axis=-1, keepdims=True)
    return yc * lax.rsqrt(var + LN_EPS) * g + b


def _log_sigmoid(x):
    return jnp.minimum(x, 0.0) - jnp.log1p(jnp.exp(-jnp.abs(x)))


def _sigmoid(x):
    return 1.0 / (1.0 + jnp.exp(-x))


def _seq_pos(blk, n_p_blocks, p_blocks_per_seq, s_blocks_per_seq):
    in_p = blk < n_p_blocks
    local = jnp.where(in_p, blk % p_blocks_per_seq, (blk - n_p_blocks) % s_blocks_per_seq)
    per = jnp.where(in_p, p_blocks_per_seq, s_blocks_per_seq)
    return local == 0, local == per - 1


def _in_proj_kernel(n_p_tiles, xp_ref, xs_ref, w_ref, wg_ref, wgt_ref, bg_ref, bgt_ref, z_ref, gc_ref, gr_ref):
    i = pl.program_id(0)
    x = jnp.where(i < n_p_tiles, xp_ref[...], xs_ref[...]).astype(BF16)
    z_ref[...] = _dot(x, w_ref[...]).astype(BF16)
    gc_ref[...] = _dot(x, wg_ref[...]) + bg_ref[...]
    gr_ref[...] = _dot_nt(wgt_ref[...], x) + bgt_ref[...]


def _in_proj(xp, xs, w_main, w_g, b_g):
    tp, ts = xp.shape[0], xs.shape[0]
    t = tp + ts
    n_p = tp // ROW_TILE
    n = t // ROW_TILE
    const = lambda i: (0, 0)
    return pl.pallas_call(
        functools.partial(_in_proj_kernel, n_p),
        grid=(n,),
        in_specs=[
            pl.BlockSpec((ROW_TILE, D_MODEL), lambda i: (jnp.minimum(i, n_p - 1), 0)),
            pl.BlockSpec((ROW_TILE, D_MODEL), lambda i: (jnp.maximum(i - n_p, 0), 0)),
            pl.BlockSpec((D_MODEL, Z_WIDTH), const),
            pl.BlockSpec((D_MODEL, N_GATES), const),
            pl.BlockSpec((N_GATES, D_MODEL), const),
            pl.BlockSpec((1, N_GATES), const),
            pl.BlockSpec((N_GATES, 1), const),
        ],
        out_specs=[
            pl.BlockSpec((ROW_TILE, Z_WIDTH), lambda i: (i, 0)),
            pl.BlockSpec((ROW_TILE, N_GATES), lambda i: (i, 0)),
            pl.BlockSpec((N_GATES, ROW_TILE), lambda i: (0, i)),
        ],
        out_shape=[
            jax.ShapeDtypeStruct((t, Z_WIDTH), BF16),
            jax.ShapeDtypeStruct((t, N_GATES), F32),
            jax.ShapeDtypeStruct((N_GATES, t), F32),
        ],
        compiler_params=_cparams(("arbitrary",)),
        name="in_proj",
    )(xp, xs, w_main, w_g, w_g.T, b_g.reshape(1, N_GATES), b_g.reshape(N_GATES, 1))


ATT_TILE = 512
ATT_SUB = ATT_TILE // BLOCK


def _attention_kernel(seq, q_ref, kp_ref, kc_ref, kn_ref, vp_ref, vc_ref, vn_ref, bias_ref, sink_ref,
                      o_ref, klo_ref, khi_ref, v_ref):
    i = pl.program_id(0)
    lane = lax.broadcasted_iota(jnp.int32, (ATT_TILE + 2 * BLOCK, 2 * ATT_HEAD_DIM), 1)
    kband = jnp.concatenate([kp_ref[...], kc_ref[...], kn_ref[...]], axis=0)
    zero = jnp.zeros_like(kband)
    klo_ref[...] = jnp.where(lane < ATT_HEAD_DIM, kband, zero)
    khi_ref[...] = jnp.where(lane < ATT_HEAD_DIM, zero, kband)
    v_ref[...] = jnp.concatenate([vp_ref[...], vc_ref[...], vn_ref[...]], axis=0)
    out_lane = lax.broadcasted_iota(jnp.int32, (BLOCK, 2 * ATT_HEAD_DIM), 1)
    for s in range(ATT_SUB):
        first, last = _seq_pos(i * ATT_SUB + s, *seq)
        variant = jnp.where(first, 1, jnp.where(last, 2, 0))
        q = q_ref[s * BLOCK:(s + 1) * BLOCK, :]
        q_all = jnp.concatenate([q[:, t * 128:(t + 1) * 128] for t in range(ATT_GROUP)], axis=0)
        vband = v_ref[s * BLOCK:(s + 3) * BLOCK, :]
        outs = []
        for kv, k_ref in enumerate((klo_ref, khi_ref)):
            logits = _dot_nt(q_all, k_ref[s * BLOCK:(s + 3) * BLOCK, :]) + bias_ref[variant, kv]
            sink = sink_ref[kv]
            m = jnp.maximum(jnp.max(logits, axis=-1, keepdims=True), sink)
            p = jnp.exp(logits - m)
            den = jnp.sum(p, axis=-1, keepdims=True) + jnp.exp(sink - m)
            outs.append(_dot(p.astype(BF16), vband) * (1.0 / den))
        for t in range(ATT_GROUP):
            rows = slice(t * BLOCK, (t + 1) * BLOCK)
            tile = jnp.where(out_lane < ATT_HEAD_DIM, outs[0][rows], outs[1][rows])
            o_ref[s * BLOCK:(s + 1) * BLOCK, t * 128:(t + 1) * 128] = tile.astype(BF16)


def _attention(z, bias, sink, seq):
    t = z.shape[0]
    n = t // ATT_TILE
    nblk = t // BLOCK
    band = ATT_TILE + 2 * BLOCK
    prev = lambda i: jnp.maximum(i * ATT_SUB - 1, 0)
    nxt = lambda i: jnp.minimum((i + 1) * ATT_SUB, nblk - 1)
    return pl.pallas_call(
        functools.partial(_attention_kernel, seq),
        grid=(n,),
        in_specs=[
            pl.BlockSpec((ATT_TILE, ATT_WIDTH), lambda i: (i, ZB_ATTQ)),
            pl.BlockSpec((BLOCK, 128), lambda i: (prev(i), ZB_ATTK)),
            pl.BlockSpec((ATT_TILE, 128), lambda i: (i, ZB_ATTK)),
            pl.BlockSpec((BLOCK, 128), lambda i: (nxt(i), ZB_ATTK)),
            pl.BlockSpec((BLOCK, 128), lambda i: (prev(i), ZB_ATTV)),
            pl.BlockSpec((ATT_TILE, 128), lambda i: (i, ZB_ATTV)),
            pl.BlockSpec((BLOCK, 128), lambda i: (nxt(i), ZB_ATTV)),
            pl.BlockSpec((3, ATT_KV_HEADS, ATT_GROUP * BLOCK, 3 * BLOCK), lambda i: (0, 0, 0, 0)),
            pl.BlockSpec((ATT_KV_HEADS, ATT_GROUP * BLOCK, 1), lambda i: (0, 0, 0)),
        ],
        out_specs=pl.BlockSpec((ATT_TILE, ATT_WIDTH), lambda i: (i, 0)),
        out_shape=jax.ShapeDtypeStruct((t, ATT_WIDTH), BF16),
        scratch_shapes=[pltpu.VMEM((band, 128), BF16), pltpu.VMEM((band, 128), BF16), pltpu.VMEM((band, 128), BF16)],
        compiler_params=_cparams(("arbitrary",)),
        name="window_attention",
    )(z, z, z, z, z, z, z, bias, sink)


def _t5_bucket(rel):
    half = N_BUCKETS // 2
    exact = half // 2
    n = np.abs(rel)
    large = exact + (np.log(np.maximum(n, 1) / exact) / np.log(MAX_DISTANCE / exact) * (half - exact)).astype(np.int32)
    large = np.minimum(large, half - 1)
    return ((rel > 0).astype(np.int32) * half + np.where(n < exact, n, large)).astype(np.int32)


def _attention_tables(rel_bias, attn_sink):
    rel = np.arange(3 * BLOCK)[None, :] - BLOCK - np.arange(BLOCK)[:, None]
    bias = rel_bias[_t5_bucket(rel)].astype(F32).transpose(2, 0, 1)
    bias = jnp.where(jnp.asarray(np.abs(rel) <= WINDOW)[None], bias, NEG)
    bias = bias.reshape(ATT_KV_HEADS, ATT_GROUP * BLOCK, 3 * BLOCK)
    col = np.arange(3 * BLOCK)[None, None, :]
    first = jnp.where(jnp.asarray(col < BLOCK), NEG, bias)
    last = jnp.where(jnp.asarray(col >= 2 * BLOCK), NEG, bias)
    sink = jnp.repeat(attn_sink.astype(F32), BLOCK).reshape(ATT_KV_HEADS, ATT_GROUP * BLOCK, 1)
    return jnp.stack([bias, first, last]), sink


CONV_TILE = 512
CONV_HALO = 16
QK_WIDTH = 2 * ML_WIDTH


def _conv_kernel(seq, xp_ref, xc_ref, xn_ref, w_ref, b_ref, scale_ref, o_ref, buf_ref):
    i = pl.program_id(0)
    first, last = _seq_pos(i, *seq)
    buf_ref[0:8, :] = jnp.where(first, 0.0, xp_ref[8:16, :].astype(F32))
    buf_ref[8:8 + CONV_TILE, :] = xc_ref[...].astype(F32)
    buf_ref[8 + CONV_TILE:16 + CONV_TILE, :] = jnp.where(last, 0.0, xn_ref[0:8, :].astype(F32))
    acc = jnp.zeros((CONV_TILE, QK_WIDTH), F32) + b_ref[...]
    for j in range(CONV_WIDTH):
        off = 8 + j - CONV_WIDTH // 2
        acc = acc + buf_ref[off:off + CONV_TILE, :] * w_ref[j:j + 1, :]
    o_ref[...] = (acc * _sigmoid(acc) * scale_ref[...]).astype(BF16)


def _conv_silu(z, conv_w, conv_b, seq_tiles):
    t = z.shape[0]
    n = t // CONV_TILE
    r = CONV_TILE // CONV_HALO
    nh = t // CONV_HALO
    scale = jnp.concatenate([jnp.ones((1, ML_WIDTH), F32), jnp.full((1, ML_WIDTH), ML_HEAD_DIM ** -0.5, F32)], axis=1)
    return pl.pallas_call(
        functools.partial(_conv_kernel, seq_tiles),
        grid=(n,),
        in_specs=[
            pl.BlockSpec((CONV_HALO, QK_WIDTH), lambda i: (jnp.maximum(i * r - 1, 0), 0)),
            pl.BlockSpec((CONV_TILE, QK_WIDTH), lambda i: (i, 0)),
            pl.BlockSpec((CONV_HALO, QK_WIDTH), lambda i: (jnp.minimum((i + 1) * r, nh - 1), 0)),
            pl.BlockSpec((CONV_WIDTH, QK_WIDTH), lambda i: (0, 0)),
            pl.BlockSpec((1, QK_WIDTH), lambda i: (0, 0)),
            pl.BlockSpec((1, QK_WIDTH), lambda i: (0, 0)),
        ],
        out_specs=pl.BlockSpec((CONV_TILE, QK_WIDTH), lambda i: (i, 0)),
        out_shape=jax.ShapeDtypeStruct((t, QK_WIDTH), BF16),
        scratch_shapes=[pltpu.VMEM((CONV_TILE + 16, QK_WIDTH), F32)],
        compiler_params=_cparams(("arbitrary",)),
        name="conv_silu",
    )(z, z, z, conv_w, conv_b.reshape(1, QK_WIDTH), scale)


def _mlstm_kernel(seq, reverse, n_chunks, q_ref, k_ref, v_ref, gr_ref, gc_ref, o_ref, c_ref, n_ref, m_ref):
    step = pl.program_id(0)
    chunk = (n_chunks - 1 - step) if reverse else step
    first, last = _seq_pos(chunk, *seq)

    @pl.when(last if reverse else first)
    def _():
        c_ref[...] = jnp.zeros_like(c_ref)
        n_ref[...] = jnp.zeros_like(n_ref)
        m_ref[...] = jnp.zeros_like(m_ref)

    L = ML_CHUNK
    row = lax.broadcasted_iota(jnp.int32, (L, L), 0)
    col = lax.broadcasted_iota(jnp.int32, (L, L), 1)
    seen = (col >= row) if reverse else (col <= row)
    seen_t = (row >= col) if reverse else (row <= col)
    gr = gr_ref[...]
    gc = gc_ref[...]
    b_rows = _dot_exact(_log_sigmoid(gr), seen_t.astype(F32))
    b_cols = _dot_exact(seen.astype(F32), _log_sigmoid(gc))
    i_off = 2 * ML_HEADS if reverse else 0
    f_off = i_off + ML_HEADS
    end = 0 if reverse else L - 1
    for h in range(ML_HEADS):
        hs = slice(h * ML_HEAD_DIM, (h + 1) * ML_HEAD_DIM)
        q = q_ref[:, hs]
        k = k_ref[:, hs]
        v = v_ref[:, hs]
        b_row = b_rows[f_off + h:f_off + h + 1, :]
        b_col = b_cols[:, f_off + h:f_off + h + 1]
        u_row = gr[i_off + h:i_off + h + 1, :] - b_row
        u_col = gc[:, i_off + h:i_off + h + 1] - b_col
        g = b_row[:, end:end + 1]
        c_old = c_ref[h]
        n_old = n_ref[h]
        m_old = m_ref[h][:, 0:1]
        u_mat = jnp.where(seen, u_row, NEG)
        mm = jnp.maximum(jnp.max(u_mat, axis=-1, keepdims=True), m_old)
        decay = jnp.exp(u_mat - mm)
        s = _dot_nt(q, k) * decay
        w_inter = jnp.exp(m_old - mm)
        num = _dot(s.astype(BF16), v) + w_inter * _dot(q, c_old.astype(BF16))
        qf = q.astype(F32)
        den = jnp.sum(s, axis=-1, keepdims=True) + w_inter * jnp.sum(qf * n_old, axis=-1, keepdims=True)
        floor = jnp.exp(-(b_col + mm))
        o_ref[:, hs] = num / jnp.maximum(jnp.abs(den), floor)
        a_max = jnp.max(g + u_row, axis=-1, keepdims=True)
        m_new = jnp.maximum(g + m_old, a_max)
        s_old = jnp.exp(g + m_old - m_new)
        kw = k.astype(F32) * jnp.exp(g + u_col - m_new)
        c_ref[h] = s_old * c_old + _dot_tn(kw.astype(BF16), v)
        n_ref[h] = s_old * n_old + jnp.sum(kw, axis=0, keepdims=True)
        m_ref[h] = jnp.broadcast_to(m_new, (1, ML_HEAD_DIM))


def _mlstm(qk, z, gates_r, gates_c, seq, reverse):
    t = qk.shape[0]
    nc = t // ML_CHUNK
    ch = (lambda i: nc - 1 - i) if reverse else (lambda i: i)
    return pl.pallas_call(
        functools.partial(_mlstm_kernel, seq, reverse, nc),
        grid=(nc,),
        in_specs=[
            pl.BlockSpec((ML_CHUNK, ML_WIDTH), lambda i: (ch(i), 0)),
            pl.BlockSpec((ML_CHUNK, ML_WIDTH), lambda i: (ch(i), 1)),
            pl.BlockSpec((ML_CHUNK, ML_WIDTH), lambda i: (ch(i), ZB_MLV)),
            pl.BlockSpec((N_GATES, ML_CHUNK), lambda i: (0, ch(i))),
            pl.BlockSpec((ML_CHUNK, N_GATES), lambda i: (ch(i), 0)),
        ],
        out_specs=pl.BlockSpec((ML_CHUNK, ML_WIDTH), lambda i: (ch(i), 0)),
        out_shape=jax.ShapeDtypeStruct((t, ML_WIDTH), F32),
        scratch_shapes=[
            pltpu.VMEM((ML_HEADS, ML_HEAD_DIM, ML_HEAD_DIM), F32),
            pltpu.VMEM((ML_HEADS, 1, ML_HEAD_DIM), F32),
            pltpu.VMEM((ML_HEADS, 1, ML_HEAD_DIM), F32),
        ],
        compiler_params=_cparams(("arbitrary",)),
        name="mlstm_bwd" if reverse else "mlstm_fwd",
    )(qk, qk, z, gates_r, gates_c)


def _mix_out_kernel(n_p_tiles, xp_ref, xs_ref, att_ref, hf_ref, hb_ref, og_ref, wa_ref, wm_ref, gain_ref,
                    g_ref, b_ref, o_ref):
    i = pl.program_id(0)
    x = jnp.where(i < n_p_tiles, xp_ref[...], xs_ref[...])
    h = hf_ref[...] + hb_ref[...]
    parts = []
    for hd in range(ML_HEADS):
        hh = h[:, hd * ML_HEAD_DIM:(hd + 1) * ML_HEAD_DIM]
        mu = jnp.mean(hh, axis=-1, keepdims=True)
        hc = hh - mu
        var = jnp.mean(hc * hc, axis=-1, keepdims=True)
        parts.append(hc * lax.rsqrt(var + LN_EPS))
    hn = jnp.concatenate(parts, axis=1) * gain_ref[...] * _sigmoid(og_ref[...].astype(F32))
    mixed = _dot(att_ref[...], wa_ref[...]) + _dot(hn.astype(BF16), wm_ref[...])
    o_ref[...] = _layer_norm(DN_ALPHA * x + mixed, g_ref[...], b_ref[...])


def _mix_out(xp, xs, att, hf, hb, z, w_att, w_ml, gain, g, b):
    t = att.shape[0]
    n_p = xp.shape[0] // ROW_TILE
    n = t // ROW_TILE
    const = lambda i: (0, 0)
    tile = lambda i: (i, 0)
    return pl.pallas_call(
        functools.partial(_mix_out_kernel, n_p),
        grid=(n,),
        in_specs=[
            pl.BlockSpec((ROW_TILE, D_MODEL), lambda i: (jnp.minimum(i, n_p - 1), 0)),
            pl.BlockSpec((ROW_TILE, D_MODEL), lambda i: (jnp.maximum(i - n_p, 0), 0)),
            pl.BlockSpec((ROW_TILE, ATT_WIDTH), tile),
            pl.BlockSpec((ROW_TILE, ML_WIDTH), tile),
            pl.BlockSpec((ROW_TILE, ML_WIDTH), tile),
            pl.BlockSpec((ROW_TILE, ML_WIDTH), lambda i: (i, ZB_MLO)),
            pl.BlockSpec((ATT_WIDTH, D_MODEL), const),
            pl.BlockSpec((ML_WIDTH, D_MODEL), const),
            pl.BlockSpec((1, ML_WIDTH), const),
            pl.BlockSpec((1, D_MODEL), const),
            pl.BlockSpec((1, D_MODEL), const),
        ],
        out_specs=pl.BlockSpec((ROW_TILE, D_MODEL), tile),
        out_shape=jax.ShapeDtypeStruct((t, D_MODEL), F32),
        compiler_params=_cparams(("arbitrary",)),
        name="mix_out_ln1",
    )(xp, xs, att, hf, hb, z, w_att, w_ml, gain.reshape(1, ML_WIDTH), g.reshape(1, D_MODEL), b.reshape(1, D_MODEL))


def _mem_kv_kernel(m_ref, w_ref, o_ref):
    o_ref[...] = _dot(m_ref[...].astype(BF16), w_ref[...]).astype(BF16)


def _mem_kv(mem, wkv):
    rows = mem.shape[0]
    return pl.pallas_call(
        _mem_kv_kernel,
        grid=(rows // MEM_TOKENS,),
        in_specs=[pl.BlockSpec((MEM_TOKENS, D_MODEL), lambda i: (i, 0)),
                  pl.BlockSpec((D_MODEL, 2 * D_MODEL), lambda i: (0, 0))],
        out_specs=pl.BlockSpec((MEM_TOKENS, 2 * D_MODEL), lambda i: (i, 0)),
        out_shape=jax.ShapeDtypeStruct((rows, 2 * D_MODEL), BF16),
        compiler_params=_cparams(("arbitrary",)),
        name="mem_kv",
    )(mem, wkv)


def _pack_bf16_pairs(a, b):
    ua = pltpu.bitcast(a.astype(BF16).astype(F32), jnp.uint32)
    ub = pltpu.bitcast(b.astype(BF16).astype(F32), jnp.uint32)
    return (ua >> 16) | (ub & jnp.uint32(0xFFFF0000))


def _unpack_bf16_pairs(u):
    lo = pltpu.bitcast(u << 16, F32)
    hi = pltpu.bitcast(u & jnp.uint32(0xFFFF0000), F32)
    return lo, hi


def _cross_router_kernel(x_ref, kv_ref, wq_ref, wo_ref, g_ref, b_ref, wr_ref, br_ref, tri_ref,
                         x2_ref, x2p_ref, idx_ref, gate_ref, rank_ref, cnt_ref, carry_ref):
    i = pl.program_id(0)

    @pl.when(i == 0)
    def _():
        carry_ref[...] = jnp.zeros_like(carry_ref)

    x = x_ref[...]
    q = _dot(x.astype(BF16), wq_ref[...]).astype(BF16)
    heads = []
    for h in range(MEM_HEADS):
        hs = slice(h * MEM_HEAD_DIM, (h + 1) * MEM_HEAD_DIM)
        logits = _dot_nt(q[:, hs], kv_ref[:, hs])
        m = jnp.max(logits, axis=-1, keepdims=True)
        p = jnp.exp(logits - m)
        den = jnp.sum(p, axis=-1, keepdims=True)
        vs = slice(D_MODEL + h * MEM_HEAD_DIM, D_MODEL + (h + 1) * MEM_HEAD_DIM)
        heads.append((_dot(p.astype(BF16), kv_ref[:, vs]) * (1.0 / den)).astype(BF16))
    o = jnp.concatenate(heads, axis=1)
    x2 = _layer_norm(DN_ALPHA * x + _dot(o, wo_ref[...]), g_ref[...], b_ref[...])
    x2_ref[...] = x2
    x2p_ref[...] = _pack_bf16_pairs(x2[:, :D_MODEL // 2], x2[:, D_MODEL // 2:])

    logits = _dot_nt(wr_ref[...], x2.astype(BF16)) + br_ref[...]
    expert = lax.broadcasted_iota(jnp.int32, logits.shape, 0)
    work = logits
    vals, sels = [], []
    for k in range(TOP_K):
        mx = jnp.max(work, axis=0, keepdims=True)
        ix = jnp.min(jnp.where(work == mx, expert, N_EXPERTS), axis=0, keepdims=True)
        sel = expert == ix
        work = jnp.where(sel, -jnp.inf, work)
        vals.append(mx)
        sels.append(sel)
        idx_ref[k:k + 1, :] = ix
    es = [jnp.exp(v - vals[0]) for v in vals]
    tot = es[0] + es[1] + es[2] + es[3]
    chosen = jnp.zeros(logits.shape, F32)
    for k in range(TOP_K):
        gate_ref[k:k + 1, :] = es[k] / tot
        chosen = chosen + sels[k].astype(F32)
    carry = carry_ref[:, 0:1]
    before = _dot(chosen.astype(BF16), tri_ref[...]) + carry
    for k in range(TOP_K):
        rank_ref[k:k + 1, :] = jnp.sum(jnp.where(sels[k], before, 0.0), axis=0, keepdims=True).astype(jnp.int32)
    carry = carry + jnp.sum(chosen, axis=1, keepdims=True)
    carry_ref[...] = jnp.broadcast_to(carry, carry_ref.shape)
    cnt_ref[...] = jnp.broadcast_to(carry, cnt_ref.shape).astype(jnp.int32)


def _cross_router(x1, kv, wq, wo, g, b, w_router, b_router, mem_of_tile):
    t = x1.shape[0]
    n = t // ROW_TILE
    const = lambda i: (0, 0)
    tile = lambda i: (i, 0)
    lanes = lambda i: (0, i)
    tri = jnp.asarray(np.triu(np.ones((ROW_TILE, ROW_TILE), np.float32), 1), BF16)
    return pl.pallas_call(
        _cross_router_kernel,
        grid=(n,),
        in_specs=[
            pl.BlockSpec((ROW_TILE, D_MODEL), tile),
            pl.BlockSpec((MEM_TOKENS, 2 * D_MODEL), lambda i: (mem_of_tile(i), 0)),
            pl.BlockSpec((D_MODEL, D_MODEL), const),
            pl.BlockSpec((D_MODEL, D_MODEL), const),
            pl.BlockSpec((1, D_MODEL), const),
            pl.BlockSpec((1, D_MODEL), const),
            pl.BlockSpec((N_EXPERTS, D_MODEL), const),
            pl.BlockSpec((N_EXPERTS, 1), const),
            pl.BlockSpec((ROW_TILE, ROW_TILE), const),
        ],
        out_specs=[
            pl.BlockSpec((ROW_TILE, D_MODEL), tile),
            pl.BlockSpec((ROW_TILE, D_MODEL // 2), tile),
            pl.BlockSpec((TOP_K, ROW_TILE), lanes),
            pl.BlockSpec((TOP_K, ROW_TILE), lanes),
            pl.BlockSpec((TOP_K, ROW_TILE), lanes),
            pl.BlockSpec((N_EXPERTS, 128), const),
        ],
        out_shape=[
            jax.ShapeDtypeStruct((t, D_MODEL), F32),
            jax.ShapeDtypeStruct((t, D_MODEL // 2), jnp.uint32),
            jax.ShapeDtypeStruct((TOP_K, t), jnp.int32),
            jax.ShapeDtypeStruct((TOP_K, t), F32),
            jax.ShapeDtypeStruct((TOP_K, t), jnp.int32),
            jax.ShapeDtypeStruct((N_EXPERTS, 128), jnp.int32),
        ],
        scratch_shapes=[pltpu.VMEM((N_EXPERTS, 128), F32)],
        compiler_params=_cparams(("arbitrary",)),
        name="cross_attn_ln2_router",
    )(x1, kv, wq, wo, g.reshape(1, D_MODEL), b.reshape(1, D_MODEL), w_router.T.astype(BF16),
      b_router.reshape(N_EXPERTS, 1), tri)


GATHER_ROWS = 512


def _gather_kernel(idx_hbm, src_hbm, o_ref, idx_smem, idx_sem, row_sem):
    i = pl.program_id(0)
    load = pltpu.make_async_copy(idx_hbm.at[i], idx_smem, idx_sem)
    load.start()
    load.wait()

    def row_copy(r, src_row):
        return pltpu.make_async_copy(src_hbm.at[pl.ds(src_row, 1), :], o_ref.at[pl.ds(r, 1), :], row_sem)

    def issue(r, c):
        row_copy(r, idx_smem[r]).start()
        return c

    def drain(r, c):
        row_copy(r, 0).wait()
        return c

    lax.fori_loop(0, GATHER_ROWS, issue, 0, unroll=8)
    lax.fori_loop(0, GATHER_ROWS, drain, 0, unroll=8)


def _gather_rows(src, index):
    n = index.shape[0]
    width = src.shape[1]
    n_blk = n // GATHER_ROWS
    return pl.pallas_call(
        _gather_kernel,
        grid=(n_blk,),
        in_specs=[pl.BlockSpec(memory_space=pl.ANY), pl.BlockSpec(memory_space=pl.ANY)],
        out_specs=pl.BlockSpec((GATHER_ROWS, width), lambda i: (i, 0)),
        out_shape=jax.ShapeDtypeStruct((n, width), src.dtype),
        scratch_shapes=[pltpu.SMEM((GATHER_ROWS,), jnp.int32), pltpu.SemaphoreType.DMA, pltpu.SemaphoreType.DMA],
        compiler_params=_cparams(("arbitrary",)),
        name="gather_rows",
    )(index.reshape(n_blk, GATHER_ROWS), src)


def _expert_kernel(be_ref, na_ref, x_ref, wu_ref, bu_ref, wd_ref, bd_ref, o_ref):
    i = pl.program_id(0)

    @pl.when(i < na_ref[0])
    def _():
        lo, hi = _unpack_bf16_pairs(x_ref[...])
        x = jnp.concatenate([lo, hi], axis=1).astype(BF16)
        hu = _dot(x, wu_ref[0]) + bu_ref[0]
        h_glu = jnp.minimum(hu[:, :D_FF], SWIGLU_LIMIT)
        h_lin = jnp.clip(hu[:, D_FF:], -SWIGLU_LIMIT, SWIGLU_LIMIT)
        hh = h_glu * _sigmoid(SWIGLU_ALPHA * h_glu) * (h_lin + 1.0)
        y = _dot(hh.astype(BF16), wd_ref[0]) + bd_ref[0]
        o_ref[...] = _pack_bf16_pairs(y[:, :D_MODEL // 2], y[:, D_MODEL // 2:])

    @pl.when(i >= na_ref[0])
    def _():
        o_ref[...] = jnp.zeros_like(o_ref)


def _experts(xs, block_e, n_active, w_up, b_up, w_down, b_down):
    n_rows = xs.shape[0]
    n_blk = n_rows // MOE_ROWS
    return pl.pallas_call(
        _expert_kernel,
        grid_spec=pltpu.PrefetchScalarGridSpec(
            num_scalar_prefetch=2,
            grid=(n_blk,),
            in_specs=[
                pl.BlockSpec((MOE_ROWS, D_MODEL // 2), lambda i, be, na: (i, 0)),
                pl.BlockSpec((1, D_MODEL, 2 * D_FF), lambda i, be, na: (be[i], 0, 0)),
                pl.BlockSpec((1, 1, 2 * D_FF), lambda i, be, na: (be[i], 0, 0)),
                pl.BlockSpec((1, D_FF, D_MODEL), lambda i, be, na: (be[i], 0, 0)),
                pl.BlockSpec((1, 1, D_MODEL), lambda i, be, na: (be[i], 0, 0)),
            ],
            out_specs=pl.BlockSpec((MOE_ROWS, D_MODEL // 2), lambda i, be, na: (i, 0)),
        ),
        out_shape=jax.ShapeDtypeStruct((n_rows, D_MODEL // 2), jnp.uint32),
        compiler_params=_cparams(("arbitrary",)),
        name="experts",
    )(block_e, n_active, xs, w_up, b_up.reshape(N_EXPERTS, 1, 2 * D_FF), w_down, b_down.reshape(N_EXPERTS, 1, D_MODEL))


def _combine_kernel(x_ref, y0_ref, y1_ref, y2_ref, y3_ref, gate_ref, g_ref, b_ref, o_ref):
    gates = gate_ref[...]
    lo = jnp.zeros((ROW_TILE, D_MODEL // 2), F32)
    hi = jnp.zeros((ROW_TILE, D_MODEL // 2), F32)
    for k, y_ref in enumerate((y0_ref, y1_ref, y2_ref, y3_ref)):
        a, b = _unpack_bf16_pairs(y_ref[...])
        gk = gates[:, k:k + 1]
        lo = lo + gk * a
        hi = hi + gk * b
    y = jnp.concatenate([lo, hi], axis=1)
    o_ref[...] = _layer_norm(DN_ALPHA * x_ref[...] + y, g_ref[...], b_ref[...])


def _combine(x2, yg, gates_c, g, b, first_tile, n_tiles, n_total_tiles):
    const = lambda i: (0, 0)
    tile = lambda i: (first_tile + i, 0)
    plane = lambda k: (lambda i: (k * n_total_tiles + first_tile + i, 0))
    return pl.pallas_call(
        _combine_kernel,
        grid=(n_tiles,),
        in_specs=[pl.BlockSpec((ROW_TILE, D_MODEL), tile)]
        + [pl.BlockSpec((ROW_TILE, D_MODEL // 2), plane(k)) for k in range(TOP_K)]
        + [pl.BlockSpec((ROW_TILE, TOP_K), tile), pl.BlockSpec((1, D_MODEL), const), pl.BlockSpec((1, D_MODEL), const)],
        out_specs=pl.BlockSpec((ROW_TILE, D_MODEL), lambda i: (i, 0)),
        out_shape=jax.ShapeDtypeStruct((n_tiles * ROW_TILE, D_MODEL), F32),
        compiler_params=_cparams(("arbitrary",)),
        name="combine_ln3",
    )(x2, yg, yg, yg, yg, gates_c, g.reshape(1, D_MODEL), b.reshape(1, D_MODEL))


def kernel(x_prompt, x_sample, mem_prompt, mem_sample, rel_bias, w_in, b_gates, conv_w, conv_b, mh_gain, attn_sink,
           w_out, ln1_g, ln1_b, wq_mem, wkv_mem, wo_mem, ln2_g, ln2_b, w_router, b_router, w_up, b_up, w_down,
           b_down, ln3_g, ln3_b):
    assert w_in.shape[0] == 1, "single layer"
    bp, sp, _ = x_prompt.shape
    bs, ss, _ = x_sample.shape
    tp, ts = bp * sp, bs * ss
    t = tp + ts
    assert sp % ROW_TILE == 0 and ss % ROW_TILE == 0 and sp >= 2 * BLOCK and ss >= 2 * BLOCK
    xp = x_prompt.reshape(tp, D_MODEL)
    xs = x_sample.reshape(ts, D_MODEL)

    def seq_blocks(rows):
        return (tp // rows, sp // rows, ss // rows)

    w = w_in[0]
    q_end, k_end, v_end = ATT_WIDTH, ATT_WIDTH + 128, ATT_WIDTH + 256
    qk_end, mv_end, mo_end = v_end + 2 * ML_WIDTH, v_end + 3 * ML_WIDTH, v_end + 4 * ML_WIDTH
    head_order = np.concatenate([[h, ATT_GROUP + h] for h in range(ATT_GROUP)])
    att_perm = (head_order[:, None] * ATT_HEAD_DIM + np.arange(ATT_HEAD_DIM)[None, :]).reshape(-1)
    w_main = jnp.concatenate([
        w[:, v_end:qk_end], w[:, qk_end:mv_end], w[:, mv_end:mo_end],
        w[:, :q_end][:, att_perm] * (ATT_HEAD_DIM ** -0.5), w[:, q_end:k_end], w[:, k_end:v_end]], axis=1).astype(BF16)
    w_g = w[:, mo_end:].astype(BF16)
    w_att = w_out[0][:ATT_WIDTH][att_perm].astype(BF16)
    w_ml = w_out[0][ATT_WIDTH:].astype(BF16)

    z, gates_c, gates_r = _in_proj(xp, xs, w_main, w_g, b_gates[0])

    bias, sink = _attention_tables(rel_bias, attn_sink[0])
    att = _attention(z, bias, sink, seq_blocks(BLOCK))

    qk = _conv_silu(z, conv_w[0], conv_b[0], seq_blocks(CONV_TILE))
    h_f = _mlstm(qk, z, gates_r, gates_c, seq_blocks(ML_CHUNK), reverse=False)
    h_b = _mlstm(qk, z, gates_r, gates_c, seq_blocks(ML_CHUNK), reverse=True)

    x1 = _mix_out(xp, xs, att, h_f, h_b, z, w_att, w_ml, mh_gain[0], ln1_g[0], ln1_b[0])

    mem = jnp.concatenate([mem_prompt.reshape(bp * MEM_TOKENS, D_MODEL), mem_sample.reshape(bs * MEM_TOKENS, D_MODEL)])
    kv = _mem_kv(mem, wkv_mem[0].astype(BF16))
    n_p_tiles, p_tiles_per_seq, s_tiles_per_seq = seq_blocks(ROW_TILE)
    mem_of_tile = lambda i: jnp.where(i < n_p_tiles, i // p_tiles_per_seq, bp + (i - n_p_tiles) // s_tiles_per_seq)
    wq = (wq_mem[0] * (MEM_HEAD_DIM ** -0.5)).astype(BF16)
    x2, x2_packed, top_idx, top_gate, rank, counts = _cross_router(
        x1, kv, wq, wo_mem[0].astype(BF16), ln2_g[0], ln2_b[0], w_router[0], b_router[0], mem_of_tile)

    counts = counts[:, 0]
    padded = ((counts + MOE_ROWS - 1) // MOE_ROWS) * MOE_ROWS
    cum_padded = jnp.cumsum(padded)
    start = cum_padded - padded
    n_rows = t * TOP_K + N_EXPERTS * MOE_ROWS
    n_blk = n_rows // MOE_ROWS
    dest = start[top_idx] + rank
    token = jnp.broadcast_to(jnp.arange(t, dtype=jnp.int32)[None, :], (TOP_K, t))
    row_token = jnp.zeros((n_rows,), jnp.int32).at[dest.reshape(-1)].set(token.reshape(-1))
    block_e = jnp.minimum(jnp.searchsorted(cum_padded, jnp.arange(n_blk, dtype=jnp.int32) * MOE_ROWS, side='right'),
                          N_EXPERTS - 1).astype(jnp.int32)
    n_active = (cum_padded[-1:] // MOE_ROWS).astype(jnp.int32)

    rows_in = _gather_rows(x2_packed, row_token)
    rows_out = _experts(rows_in, block_e, n_active, w_up[0].astype(BF16), b_up[0], w_down[0].astype(BF16), b_down[0])
    picked = _gather_rows(rows_out, dest.reshape(-1))

    gates_t = top_gate.T
    n_tiles = t // ROW_TILE
    y_p = _combine(x2, picked, gates_t, ln3_g[0], ln3_b[0], 0, n_p_tiles, n_tiles)
    y_s = _combine(x2, picked, gates_t, ln3_g[0], ln3_b[0], n_p_tiles, n_tiles - n_p_tiles, n_tiles)
    return y_p.reshape(bp, sp, D_MODEL), y_s.reshape(bs, ss, D_MODEL)
```

```python
import functools

import numpy as np
import jax
import jax.numpy as jnp
from jax import lax
from jax.experimental import pallas as pl
from jax.experimental.pallas import tpu as pltpu

F32 = jnp.float32
BF16 = jnp.bfloat16

D_MODEL = 1024
ATT_HEADS = 8
ATT_KV_HEADS = 2
ATT_GROUP = ATT_HEADS // ATT_KV_HEADS
ATT_HEAD_DIM = 64
ATT_WIDTH = ATT_HEADS * ATT_HEAD_DIM
WINDOW = 128
BLOCK = WINDOW
N_BUCKETS = 32
MAX_DISTANCE = 128
ML_HEADS = 4
ML_HEAD_DIM = 128
ML_WIDTH = ML_HEADS * ML_HEAD_DIM
ML_CHUNK = 128
CONV_WIDTH = 5
N_GATES = 4 * ML_HEADS
MEM_TOKENS = 256
MEM_HEADS = 4
MEM_HEAD_DIM = D_MODEL // MEM_HEADS
N_EXPERTS = 32
TOP_K = 4
D_FF = D_MODEL
SWIGLU_LIMIT = 7.0
SWIGLU_ALPHA = 1.702
LN_EPS = 1e-5
DN_ALPHA = 2.0 ** 0.25

Z_WIDTH = 2 * ML_WIDTH + ML_WIDTH + ML_WIDTH + ATT_WIDTH + 2 * ATT_KV_HEADS * ATT_HEAD_DIM
ZB_MLV = 2
ZB_MLO = 3
ZB_ATTQ = 4
ZB_ATTK = 20
ZB_ATTV = 21

ROW_TILE = 512
MOE_ROWS = 512
NEG = -1e30
VMEM_LIMIT = 56 * 1024 * 1024


def _cparams(sem):
    return pltpu.CompilerParams(dimension_semantics=sem, vmem_limit_bytes=VMEM_LIMIT)


def _dot(a, b):
    return jnp.dot(a, b, preferred_element_type=F32)


def _dot_nt(a, b):
    return lax.dot_general(a, b, (((1,), (1,)), ((), ())), preferred_element_type=F32)


def _dot_tn(a, b):
    return lax.dot_general(a, b, (((0,), (0,)), ((), ())), preferred_element_type=F32)


def _dot_exact(a, b):
    return jnp.dot(a, b, preferred_element_type=F32, precision=lax.Precision.HIGHEST)


def _layer_norm(y, g, b):
    mu = jnp.mean(y, axis=-1, keepdims=True)
    yc = y - mu
    var = jnp.mean(yc * yc, axis=-1, keepdims=True)
    return yc * lax.rsqrt(var + LN_EPS) * g + b


def _log_sigmoid(x):
    return jnp.minimum(x, 0.0) - jnp.log1p(jnp.exp(-jnp.abs(x)))


def _sigmoid(x):
    return 1.0 / (1.0 + jnp.exp(-x))


def _seq_pos(blk, n_p_blocks, p_blocks_per_seq, s_blocks_per_seq):
    in_p = blk < n_p_blocks
    local = jnp.where(in_p, blk % p_blocks_per_seq, (blk - n_p_blocks) % s_blocks_per_seq)
    per = jnp.where(in_p, p_blocks_per_seq, s_blocks_per_seq)
    return local == 0, local == per - 1


def _in_proj_kernel(n_p_tiles, xp_ref, xs_ref, w_ref, wg_ref, wgt_ref, bg_ref, bgt_ref, z_ref, gc_ref, gr_ref):
    i = pl.program_id(0)
    x = jnp.where(i < n_p_tiles, xp_ref[...], xs_ref[...]).astype(BF16)
    z_ref[...] = _dot(x, w_ref[...]).astype(BF16)
    gc_ref[...] = _dot(x, wg_ref[...]) + bg_ref[...]
    gr_ref[...] = _dot_nt(wgt_ref[...], x) + bgt_ref[...]


def _in_proj(xp, xs, w_main, w_g, b_g):
    tp, ts = xp.shape[0], xs.shape[0]
    t = tp + ts
    n_p = tp // ROW_TILE
    n = t // ROW_TILE
    const = lambda i: (0, 0)
    return pl.pallas_call(
        functools.partial(_in_proj_kernel, n_p),
        grid=(n,),
        in_specs=[
            pl.BlockSpec((ROW_TILE, D_MODEL), lambda i: (jnp.minimum(i, n_p - 1), 0)),
            pl.BlockSpec((ROW_TILE, D_MODEL), lambda i: (jnp.maximum(i - n_p, 0), 0)),
            pl.BlockSpec((D_MODEL, Z_WIDTH), const),
            pl.BlockSpec((D_MODEL, N_GATES), const),
            pl.BlockSpec((N_GATES, D_MODEL), const),
            pl.BlockSpec((1, N_GATES), const),
            pl.BlockSpec((N_GATES, 1), const),
        ],
        out_specs=[
            pl.BlockSpec((ROW_TILE, Z_WIDTH), lambda i: (i, 0)),
            pl.BlockSpec((ROW_TILE, N_GATES), lambda i: (i, 0)),
            pl.BlockSpec((N_GATES, ROW_TILE), lambda i: (0, i)),
        ],
        out_shape=[
            jax.ShapeDtypeStruct((t, Z_WIDTH), BF16),
            jax.ShapeDtypeStruct((t, N_GATES), F32),
            jax.ShapeDtypeStruct((N_GATES, t), F32),
        ],
        compiler_params=_cparams(("arbitrary",)),
        name="in_proj",
    )(xp, xs, w_main, w_g, w_g.T, b_g.reshape(1, N_GATES), b_g.reshape(N_GATES, 1))


ATT_TILE = 512
ATT_SUB = ATT_TILE // BLOCK


def _attention_kernel(seq, q_ref, kp_ref, kc_ref, kn_ref, vp_ref, vc_ref, vn_ref, bias_ref, sink_ref,
                      o_ref, klo_ref, khi_ref, v_ref):
    i = pl.program_id(0)
    lane = lax.broadcasted_iota(jnp.int32, (ATT_TILE + 2 * BLOCK, 2 * ATT_HEAD_DIM), 1)
    kband = jnp.concatenate([kp_ref[...], kc_ref[...], kn_ref[...]], axis=0)
    zero = jnp.zeros_like(kband)
    klo_ref[...] = jnp.where(lane < ATT_HEAD_DIM, kband, zero)
    khi_ref[...] = jnp.where(lane < ATT_HEAD_DIM, zero, kband)
    v_ref[...] = jnp.concatenate([vp_ref[...], vc_ref[...], vn_ref[...]], axis=0)
    out_lane = lax.broadcasted_iota(jnp.int32, (BLOCK, 2 * ATT_HEAD_DIM), 1)
    for s in range(ATT_SUB):
        first, last = _seq_pos(i * ATT_SUB + s, *seq)
        variant = jnp.where(first, 1, jnp.where(last, 2, 0))
        q = q_ref[s * BLOCK:(s + 1) * BLOCK, :]
        q_all = jnp.concatenate([q[:, t * 128:(t + 1) * 128] for t in range(ATT_GROUP)], axis=0)
        vband = v_ref[s * BLOCK:(s + 3) * BLOCK, :]
        outs = []
        for kv, k_ref in enumerate((klo_ref, khi_ref)):
            logits = _dot_nt(q_all, k_ref[s * BLOCK:(s + 3) * BLOCK, :]) + bias_ref[variant, kv]
            sink = sink_ref[kv]
            m = jnp.maximum(jnp.max(logits, axis=-1, keepdims=True), sink)
            p = jnp.exp(logits - m)
            den = jnp.sum(p, axis=-1, keepdims=True) + jnp.exp(sink - m)
            outs.append(_dot(p.astype(BF16), vband) * (1.0 / den))
        for t in range(ATT_GROUP):
            rows = slice(t * BLOCK, (t + 1) * BLOCK)
            tile = jnp.where(out_lane < ATT_HEAD_DIM, outs[0][rows], outs[1][rows])
            o_ref[s * BLOCK:(s + 1) * BLOCK, t * 128:(t + 1) * 128] = tile.astype(BF16)


def _attention(z, bias, sink, seq):
    t = z.shape[0]
    n = t // ATT_TILE
    nblk = t // BLOCK
    band = ATT_TILE + 2 * BLOCK
    prev = lambda i: jnp.maximum(i * ATT_SUB - 1, 0)
    nxt = lambda i: jnp.minimum((i + 1) * ATT_SUB, nblk - 1)
    return pl.pallas_call(
        functools.partial(_attention_kernel, seq),
        grid=(n,),
        in_specs=[
            pl.BlockSpec((ATT_TILE, ATT_WIDTH), lambda i: (i, ZB_ATTQ)),
            pl.BlockSpec((BLOCK, 128), lambda i: (prev(i), ZB_ATTK)),
            pl.BlockSpec((ATT_TILE, 128), lambda i: (i, ZB_ATTK)),
            pl.BlockSpec((BLOCK, 128), lambda i: (nxt(i), ZB_ATTK)),
            pl.BlockSpec((BLOCK, 128), lambda i: (prev(i), ZB_ATTV)),
            pl.BlockSpec((ATT_TILE, 128), lambda i: (i, ZB_ATTV)),
            pl.BlockSpec((BLOCK, 128), lambda i: (nxt(i), ZB_ATTV)),
            pl.BlockSpec((3, ATT_KV_HEADS, ATT_GROUP * BLOCK, 3 * BLOCK), lambda i: (0, 0, 0, 0)),
            pl.BlockSpec((ATT_KV_HEADS, ATT_GROUP * BLOCK, 1), lambda i: (0, 0, 0)),
        ],
        out_specs=pl.BlockSpec((ATT_TILE, ATT_WIDTH), lambda i: (i, 0)),
        out_shape=jax.ShapeDtypeStruct((t, ATT_WIDTH), BF16),
        scratch_shapes=[pltpu.VMEM((band, 128), BF16), pltpu.VMEM((band, 128), BF16), pltpu.VMEM((band, 128), BF16)],
        compiler_params=_cparams(("arbitrary",)),
        name="window_attention",
    )(z, z, z, z, z, z, z, bias, sink)


def _t5_bucket(rel):
    half = N_BUCKETS // 2
    exact = half // 2
    n = np.abs(rel)
    large = exact + (np.log(np.maximum(n, 1) / exact) / np.log(MAX_DISTANCE / exact) * (half - exact)).astype(np.int32)
    large = np.minimum(large, half - 1)
    return ((rel > 0).astype(np.int32) * half + np.where(n < exact, n, large)).astype(np.int32)


def _attention_tables(rel_bias, attn_sink):
    rel = np.arange(3 * BLOCK)[None, :] - BLOCK - np.arange(BLOCK)[:, None]
    bias = rel_bias[_t5_bucket(rel)].astype(F32).transpose(2, 0, 1)
    bias = jnp.where(jnp.asarray(np.abs(rel) <= WINDOW)[None], bias, NEG)
    bias = bias.reshape(ATT_KV_HEADS, ATT_GROUP * BLOCK, 3 * BLOCK)
    col = np.arange(3 * BLOCK)[None, None, :]
    first = jnp.where(jnp.asarray(col < BLOCK), NEG, bias)
    last = jnp.where(jnp.asarray(col >= 2 * BLOCK), NEG, bias)
    sink = jnp.repeat(attn_sink.astype(F32), BLOCK).reshape(ATT_KV_HEADS, ATT_GROUP * BLOCK, 1)
    return jnp.stack([bias, first, last]), sink


CONV_TILE = 512
CONV_HALO = 16
QK_WIDTH = 2 * ML_WIDTH


def _conv_kernel(seq, xp_ref, xc_ref, xn_ref, w_ref, b_ref, scale_ref, o_ref, buf_ref):
    i = pl.program_id(0)
    first, last = _seq_pos(i, *seq)
    buf_ref[0:8, :] = jnp.where(first, 0.0, xp_ref[8:16, :].astype(F32))
    buf_ref[8:8 + CONV_TILE, :] = xc_ref[...].astype(F32)
    buf_ref[8 + CONV_TILE:16 + CONV_TILE, :] = jnp.where(last, 0.0, xn_ref[0:8, :].astype(F32))
    acc = jnp.zeros((CONV_TILE, QK_WIDTH), F32) + b_ref[...]
    for j in range(CONV_WIDTH):
        off = 8 + j - CONV_WIDTH // 2
        acc = acc + buf_ref[off:off + CONV_TILE, :] * w_ref[j:j + 1, :]
    o_ref[...] = (acc * _sigmoid(acc) * scale_ref[...]).astype(BF16)


def _conv_silu(z, conv_w, conv_b, seq_tiles):
    t = z.shape[0]
    n = t // CONV_TILE
    r = CONV_TILE // CONV_HALO
    nh = t // CONV_HALO
    scale = jnp.concatenate([jnp.ones((1, ML_WIDTH), F32), jnp.full((1, ML_WIDTH), ML_HEAD_DIM ** -0.5, F32)], axis=1)
    return pl.pallas_call(
        functools.partial(_conv_kernel, seq_tiles),
        grid=(n,),
        in_specs=[
            pl.BlockSpec((CONV_HALO, QK_WIDTH), lambda i: (jnp.maximum(i * r - 1, 0), 0)),
            pl.BlockSpec((CONV_TILE, QK_WIDTH), lambda i: (i, 0)),
            pl.BlockSpec((CONV_HALO, QK_WIDTH), lambda i: (jnp.minimum((i + 1) * r, nh - 1), 0)),
            pl.BlockSpec((CONV_WIDTH, QK_WIDTH), lambda i: (0, 0)),
            pl.BlockSpec((1, QK_WIDTH), lambda i: (0, 0)),
            pl.BlockSpec((1, QK_WIDTH), lambda i: (0, 0)),
        ],
        out_specs=pl.BlockSpec((CONV_TILE, QK_WIDTH), lambda i: (i, 0)),
        out_shape=jax.ShapeDtypeStruct((t, QK_WIDTH), BF16),
        scratch_shapes=[pltpu.VMEM((CONV_TILE + 16, QK_WIDTH), F32)],
        compiler_params=_cparams(("arbitrary",)),
        name="conv_silu",
    )(z, z, z, conv_w, conv_b.reshape(1, QK_WIDTH), scale)


def _mlstm_kernel(seq, reverse, n_chunks, q_ref, k_ref, v_ref, gr_ref, gc_ref, o_ref, c_ref, n_ref, m_ref):
    step = pl.program_id(0)
    chunk = (n_chunks - 1 - step) if reverse else step
    first, last = _seq_pos(chunk, *seq)

    @pl.when(last if reverse else first)
    def _():
        c_ref[...] = jnp.zeros_like(c_ref)
        n_ref[...] = jnp.zeros_like(n_ref)
        m_ref[...] = jnp.zeros_like(m_ref)

    L = ML_CHUNK
    row = lax.broadcasted_iota(jnp.int32, (L, L), 0)
    col = lax.broadcasted_iota(jnp.int32, (L, L), 1)
    seen = (col >= row) if reverse else (col <= row)
    seen_t = (row >= col) if reverse else (row <= col)
    gr = gr_ref[...]
    gc = gc_ref[...]
    b_rows = _dot_exact(_log_sigmoid(gr), seen_t.astype(F32))
    b_cols = _dot_exact(seen.astype(F32), _log_sigmoid(gc))
    i_off = 2 * ML_HEADS if reverse else 0
    f_off = i_off + ML_HEADS
    end = 0 if reverse else L - 1
    for h in range(ML_HEADS):
        hs = slice(h * ML_HEAD_DIM, (h + 1) * ML_HEAD_DIM)
        q = q_ref[:, hs]
        k = k_ref[:, hs]
        v = v_ref[:, hs]
        b_row = b_rows[f_off + h:f_off + h + 1, :]
        b_col = b_cols[:, f_off + h:f_off + h + 1]
        u_row = gr[i_off + h:i_off + h + 1, :] - b_row
        u_col = gc[:, i_off + h:i_off + h + 1] - b_col
        g = b_row[:, end:end + 1]
        c_old = c_ref[h]
        n_old = n_ref[h]
        m_old = m_ref[h][:, 0:1]
        u_mat = jnp.where(seen, u_row, NEG)
        mm = jnp.maximum(jnp.max(u_mat, axis=-1, keepdims=True), m_old)
        decay = jnp.exp(u_mat - mm)
        s = _dot_nt(q, k) * decay
        w_inter = jnp.exp(m_old - mm)
        num = _dot(s.astype(BF16), v) + w_inter * _dot(q, c_old.astype(BF16))
        qf = q.astype(F32)
        den = jnp.sum(s, axis=-1, keepdims=True) + w_inter * jnp.sum(qf * n_old, axis=-1, keepdims=True)
        floor = jnp.exp(-(b_col + mm))
        o_ref[:, hs] = num / jnp.maximum(jnp.abs(den), floor)
        a_max = jnp.max(g + u_row, axis=-1, keepdims=True)
        m_new = jnp.maximum(g + m_old, a_max)
        s_old = jnp.exp(g + m_old - m_new)
        kw = k.astype(F32) * jnp.exp(g + u_col - m_new)
        c_ref[h] = s_old * c_old + _dot_tn(kw.astype(BF16), v)
        n_ref[h] = s_old * n_old + jnp.sum(kw, axis=0, keepdims=True)
        m_ref[h] = jnp.broadcast_to(m_new, (1, ML_HEAD_DIM))


def _mlstm(qk, z, gates_r, gates_c, seq, reverse):
    t = qk.shape[0]
    nc = t // ML_CHUNK
    ch = (lambda i: nc - 1 - i) if reverse else (lambda i: i)
    return pl.pallas_call(
        functools.partial(_mlstm_kernel, seq, reverse, nc),
        grid=(nc,),
        in_specs=[
            pl.BlockSpec((ML_CHUNK, ML_WIDTH), lambda i: (ch(i), 0)),
            pl.BlockSpec((ML_CHUNK, ML_WIDTH), lambda i: (ch(i), 1)),
            pl.BlockSpec((ML_CHUNK, ML_WIDTH), lambda i: (ch(i), ZB_MLV)),
            pl.BlockSpec((N_GATES, ML_CHUNK), lambda i: (0, ch(i))),
            pl.BlockSpec((ML_CHUNK, N_GATES), lambda i: (ch(i), 0)),
        ],
        out_specs=pl.BlockSpec((ML_CHUNK, ML_WIDTH), lambda i: (ch(i), 0)),
        out_shape=jax.ShapeDtypeStruct((t, ML_WIDTH), F32),
        scratch_shapes=[
            pltpu.VMEM((ML_HEADS, ML_HEAD_DIM, ML_HEAD_DIM), F32),
            pltpu.VMEM((ML_HEADS, 1, ML_HEAD_DIM), F32),
            pltpu.VMEM((ML_HEADS, 1, ML_HEAD_DIM), F32),
        ],
        compiler_params=_cparams(("arbitrary",)),
        name="mlstm_bwd" if reverse else "mlstm_fwd",
    )(qk, qk, z, gates_r, gates_c)


def _mix_out_kernel(n_p_tiles, xp_ref, xs_ref, att_ref, hf_ref, hb_ref, og_ref, wa_ref, wm_ref, gain_ref,
                    g_ref, b_ref, o_ref):
    i = pl.program_id(0)
    x = jnp.where(i < n_p_tiles, xp_ref[...], xs_ref[...])
    h = hf_ref[...] + hb_ref[...]
    parts = []
    for hd in range(ML_HEADS):
        hh = h[:, hd * ML_HEAD_DIM:(hd + 1) * ML_HEAD_DIM]
        mu = jnp.mean(hh, axis=-1, keepdims=True)
        hc = hh - mu
        var = jnp.mean(hc * hc, axis=-1, keepdims=True)
        parts.append(hc * lax.rsqrt(var + LN_EPS))
    hn = jnp.concatenate(parts, axis=1) * gain_ref[...] * _sigmoid(og_ref[...].astype(F32))
    mixed = _dot(att_ref[...], wa_ref[...]) + _dot(hn.astype(BF16), wm_ref[...])
    o_ref[...] = _layer_norm(DN_ALPHA * x + mixed, g_ref[...], b_ref[...])


def _mix_out(xp, xs, att, hf, hb, z, w_att, w_ml, gain, g, b):
    t = att.shape[0]
    n_p = xp.shape[0] // ROW_TILE
    n = t // ROW_TILE
    const = lambda i: (0, 0)
    tile = lambda i: (i, 0)
    return pl.pallas_call(
        functools.partial(_mix_out_kernel, n_p),
        grid=(n,),
        in_specs=[
            pl.BlockSpec((ROW_TILE, D_MODEL), lambda i: (jnp.minimum(i, n_p - 1), 0)),
            pl.BlockSpec((ROW_TILE, D_MODEL), lambda i: (jnp.maximum(i - n_p, 0), 0)),
            pl.BlockSpec((ROW_TILE, ATT_WIDTH), tile),
            pl.BlockSpec((ROW_TILE, ML_WIDTH), tile),
            pl.BlockSpec((ROW_TILE, ML_WIDTH), tile),
            pl.BlockSpec((ROW_TILE, ML_WIDTH), lambda i: (i, ZB_MLO)),
            pl.BlockSpec((ATT_WIDTH, D_MODEL), const),
            pl.BlockSpec((ML_WIDTH, D_MODEL), const),
            pl.BlockSpec((1, ML_WIDTH), const),
            pl.BlockSpec((1, D_MODEL), const),
            pl.BlockSpec((1, D_MODEL), const),
        ],
        out_specs=pl.BlockSpec((ROW_TILE, D_MODEL), tile),
        out_shape=jax.ShapeDtypeStruct((t, D_MODEL), F32),
        compiler_params=_cparams(("arbitrary",)),
        name="mix_out_ln1",
    )(xp, xs, att, hf, hb, z, w_att, w_ml, gain.reshape(1, ML_WIDTH), g.reshape(1, D_MODEL), b.reshape(1, D_MODEL))


def _mem_kv_kernel(m_ref, w_ref, o_ref):
    o_ref[...] = _dot(m_ref[...].astype(BF16), w_ref[...]).astype(BF16)


def _mem_kv(mem, wkv):
    rows = mem.shape[0]
    return pl.pallas_call(
        _mem_kv_kernel,
        grid=(rows // MEM_TOKENS,),
        in_specs=[pl.BlockSpec((MEM_TOKENS, D_MODEL), lambda i: (i, 0)),
                  pl.BlockSpec((D_MODEL, 2 * D_MODEL), lambda i: (0, 0))],
        out_specs=pl.BlockSpec((MEM_TOKENS, 2 * D_MODEL), lambda i: (i, 0)),
        out_shape=jax.ShapeDtypeStruct((rows, 2 * D_MODEL), BF16),
        compiler_params=_cparams(("arbitrary",)),
        name="mem_kv",
    )(mem, wkv)


def _pack_bf16_pairs(a, b):
    ua = pltpu.bitcast(a.astype(BF16).astype(F32), jnp.uint32)
    ub = pltpu.bitcast(b.astype(BF16).astype(F32), jnp.uint32)
    return (ua >> 16) | (ub & jnp.uint32(0xFFFF0000))


def _unpack_bf16_pairs(u):
    lo = pltpu.bitcast(u << 16, F32)
    hi = pltpu.bitcast(u & jnp.uint32(0xFFFF0000), F32)
    return lo, hi


def _cross_router_kernel(x_ref, kv_ref, wq_ref, wo_ref, g_ref, b_ref, wr_ref, br_ref, tri_ref, etri_ref,
                         x2_ref, x2b_ref, gate_ref, slot_ref, chunk_ref):
    x = x_ref[...]
    q = _dot(x.astype(BF16), wq_ref[...]).astype(BF16)
    heads = []
    for h in range(MEM_HEADS):
        hs = slice(h * MEM_HEAD_DIM, (h + 1) * MEM_HEAD_DIM)
        logits = _dot_nt(q[:, hs], kv_ref[:, hs])
        m = jnp.max(logits, axis=-1, keepdims=True)
        p = jnp.exp(logits - m)
        den = jnp.sum(p, axis=-1, keepdims=True)
        vs = slice(D_MODEL + h * MEM_HEAD_DIM, D_MODEL + (h + 1) * MEM_HEAD_DIM)
        heads.append((_dot(p.astype(BF16), kv_ref[:, vs]) * (1.0 / den)).astype(BF16))
    o = jnp.concatenate(heads, axis=1)
    x2 = _layer_norm(DN_ALPHA * x + _dot(o, wo_ref[...]), g_ref[...], b_ref[...])
    x2_ref[...] = x2
    x2b = x2.astype(BF16)
    x2b_ref[...] = x2b

    logits = _dot_nt(wr_ref[...], x2b) + br_ref[...]
    expert = lax.broadcasted_iota(jnp.int32, logits.shape, 0)
    work = logits
    vals, sels = [], []
    for k in range(TOP_K):
        mx = jnp.max(work, axis=0, keepdims=True)
        ix = jnp.min(jnp.where(work == mx, expert, N_EXPERTS), axis=0, keepdims=True)
        sel = expert == ix
        work = jnp.where(sel, -jnp.inf, work)
        vals.append(mx)
        sels.append(sel)
    es = [jnp.exp(v - vals[0]) for v in vals]
    tot = es[0] + es[1] + es[2] + es[3]
    chosen = jnp.zeros(logits.shape, F32)
    for k in range(TOP_K):
        gate_ref[k:k + 1, :] = es[k] / tot
        chosen = chosen + sels[k].astype(F32)
    count = jnp.sum(chosen, axis=1, keepdims=True)
    chunks = jnp.floor((count + (SEG_ALIGN - 1)) * (1.0 / SEG_ALIGN))
    chunks_b = jnp.broadcast_to(chunks, (N_EXPERTS, 128))
    seg_first = _dot(etri_ref[...], chunks_b.astype(BF16))[:, 0:1] * SEG_ALIGN
    before = _dot(chosen.astype(BF16), tri_ref[...])
    local_row = before + seg_first
    for k in range(TOP_K):
        slot_ref[k:k + 1, :] = jnp.sum(jnp.where(sels[k], local_row, 0.0), axis=0, keepdims=True).astype(jnp.int32)
    chunk_ref[0] = chunks_b.astype(jnp.int32)


def _cross_router(x1, kv, wq, wo, g, b, w_router, b_router, mem_of_tile):
    t = x1.shape[0]
    n = t // ROW_TILE
    const = lambda i: (0, 0)
    tile = lambda i: (i, 0)
    lanes = lambda i: (0, i)
    tri = jnp.asarray(np.triu(np.ones((ROW_TILE, ROW_TILE), np.float32), 1), BF16)
    etri = jnp.asarray(np.tril(np.ones((N_EXPERTS, N_EXPERTS), np.float32), -1), BF16)
    return pl.pallas_call(
        _cross_router_kernel,
        grid=(n,),
        in_specs=[
            pl.BlockSpec((ROW_TILE, D_MODEL), tile),
            pl.BlockSpec((MEM_TOKENS, 2 * D_MODEL), lambda i: (mem_of_tile(i), 0)),
            pl.BlockSpec((D_MODEL, D_MODEL), const),
            pl.BlockSpec((D_MODEL, D_MODEL), const),
            pl.BlockSpec((1, D_MODEL), const),
            pl.BlockSpec((1, D_MODEL), const),
            pl.BlockSpec((N_EXPERTS, D_MODEL), const),
            pl.BlockSpec((N_EXPERTS, 1), const),
            pl.BlockSpec((ROW_TILE, ROW_TILE), const),
            pl.BlockSpec((N_EXPERTS, N_EXPERTS), const),
        ],
        out_specs=[
            pl.BlockSpec((ROW_TILE, D_MODEL), tile),
            pl.BlockSpec((ROW_TILE, D_MODEL), tile),
            pl.BlockSpec((TOP_K, ROW_TILE), lanes),
            pl.BlockSpec((TOP_K, ROW_TILE), lanes),
            pl.BlockSpec((1, N_EXPERTS, 128), lambda i: (i, 0, 0)),
        ],
        out_shape=[
            jax.ShapeDtypeStruct((t, D_MODEL), F32),
            jax.ShapeDtypeStruct((t, D_MODEL), BF16),
            jax.ShapeDtypeStruct((TOP_K, t), F32),
            jax.ShapeDtypeStruct((TOP_K, t), jnp.int32),
            jax.ShapeDtypeStruct((n, N_EXPERTS, 128), jnp.int32),
        ],
        compiler_params=_cparams(("arbitrary",)),
        name="cross_attn_ln2_router",
    )(x1, kv, wq, wo, g.reshape(1, D_MODEL), b.reshape(1, D_MODEL), w_router.T.astype(BF16),
      b_router.reshape(N_EXPERTS, 1), tri, etri)


SEG_ALIGN = 8
LOCAL_ROWS = ROW_TILE * TOP_K + N_EXPERTS * SEG_ALIGN
HALF = D_MODEL // 2


def _segment_copies(chunks_ref, seg_ref, loc_ref, tile, make_copy):
    def per_expert(e, carry):
        entry = tile * N_EXPERTS + e
        seg0 = seg_ref[entry]
        loc0 = loc_ref[entry]

        def per_chunk(c, carry):
            make_copy(pl.multiple_of((loc0 + c) * SEG_ALIGN, SEG_ALIGN),
                      pl.multiple_of((seg0 + c) * SEG_ALIGN, SEG_ALIGN)).start()
            return carry

        return lax.fori_loop(0, chunks_ref[entry], per_chunk, carry)

    lax.fori_loop(0, N_EXPERTS, per_expert, 0)


def _wait_copies(count, make_copy):
    def one(c, carry):
        make_copy(0, 0).wait()
        return carry

    lax.fori_loop(0, count, one, 0)


def _dispatch_kernel(chunks_ref, seg_ref, loc_ref, total_ref, tail_ref, tailn_ref, nact_ref,
                     x_ref, slot_ref, rows_hbm, local_ref, zero_ref, sems):
    i = pl.program_id(0)
    n = pl.num_programs(0)
    buf = i % 2

    def copy_out(b):
        return lambda loc, seg: pltpu.make_async_copy(
            local_ref.at[b, pl.ds(loc, SEG_ALIGN), :], rows_hbm.at[pl.ds(seg, SEG_ALIGN), :], sems.at[b])

    def run(b):
        @pl.when(i >= 2)
        def _():
            _wait_copies(total_ref[jnp.maximum(i - 2, 0)], copy_out(b))

        slots = slot_ref[...]
        row = lax.broadcasted_iota(jnp.int32, (LOCAL_ROWS, ROW_TILE), 0)
        hit = slots[0:1, :] == row
        for k in range(1, TOP_K):
            hit = hit | (slots[k:k + 1, :] == row)
        perm = jnp.where(hit, 1.0, 0.0).astype(BF16)
        rows = _dot(perm, x_ref[...])
        local_ref[b] = _pack_bf16_pairs(rows[:, :HALF], rows[:, HALF:])
        _segment_copies(chunks_ref, seg_ref, loc_ref, i, copy_out(b))

    for b in range(2):
        pl.when(buf == b)(functools.partial(run, b))

    @pl.when(i == n - 1)
    def _():
        for b in range(2):
            step = jnp.where(buf == b, i, i - 1)
            _wait_copies(total_ref[step], copy_out(b))
        zero_ref[...] = jnp.zeros_like(zero_ref)
        fill = lambda _, seg: pltpu.make_async_copy(
            zero_ref.at[pl.ds(0, SEG_ALIGN), :], rows_hbm.at[pl.ds(seg, SEG_ALIGN), :], sems.at[2])

        def per_expert(e, total):
            def per_chunk(c, carry):
                fill(0, pl.multiple_of((tail_ref[e] + c) * SEG_ALIGN, SEG_ALIGN)).start()
                return carry

            lax.fori_loop(0, tailn_ref[e], per_chunk, 0)
            return total + tailn_ref[e]

        total = lax.fori_loop(0, N_EXPERTS, per_expert, 0)
        _wait_copies(total, fill)
        fill_block = lambda blk: pltpu.make_async_copy(
            zero_ref, rows_hbm.at[pl.ds(pl.multiple_of(blk * MOE_ROWS, MOE_ROWS), MOE_ROWS), :], sems.at[2])
        n_blocks = rows_hbm.shape[0] // MOE_ROWS

        def start_block(blk, carry):
            fill_block(blk).start()
            return carry

        def wait_block(blk, carry):
            fill_block(0).wait()
            return carry

        lax.fori_loop(nact_ref[0], n_blocks, start_block, 0)
        lax.fori_loop(nact_ref[0], n_blocks, wait_block, 0)


def _dispatch(x2b, slots, tables, n_rows):
    n = x2b.shape[0] // ROW_TILE
    assert n >= 2
    return pl.pallas_call(
        _dispatch_kernel,
        grid_spec=pltpu.PrefetchScalarGridSpec(
            num_scalar_prefetch=7,
            grid=(n,),
            in_specs=[
                pl.BlockSpec((ROW_TILE, D_MODEL), lambda i, *_: (i, 0)),
                pl.BlockSpec((TOP_K, ROW_TILE), lambda i, *_: (0, i)),
            ],
            out_specs=pl.BlockSpec(memory_space=pl.ANY),
            scratch_shapes=[
                pltpu.VMEM((2, LOCAL_ROWS, HALF), jnp.uint32),
                pltpu.VMEM((MOE_ROWS, HALF), jnp.uint32),
                pltpu.SemaphoreType.DMA((3,)),
            ],
        ),
        out_shape=jax.ShapeDtypeStruct((n_rows, HALF), jnp.uint32),
        compiler_params=_cparams(("arbitrary",)),
        name="moe_dispatch",
    )(*tables, x2b, slots)


def _expert_kernel(be_ref, na_ref, x_ref, wu_ref, bu_ref, wd_ref, bd_ref, o_ref):
    i = pl.program_id(0)

    @pl.when(i < na_ref[0])
    def _():
        lo, hi = _unpack_bf16_pairs(x_ref[...])
        x = jnp.concatenate([lo, hi], axis=1).astype(BF16)
        hu = _dot(x, wu_ref[0]) + bu_ref[0]
        h_glu = jnp.minimum(hu[:, :D_FF], SWIGLU_LIMIT)
        h_lin = jnp.clip(hu[:, D_FF:], -SWIGLU_LIMIT, SWIGLU_LIMIT)
        hh = h_glu * _sigmoid(SWIGLU_ALPHA * h_glu) * (h_lin + 1.0)
        y = _dot(hh.astype(BF16), wd_ref[0]) + bd_ref[0]
        o_ref[...] = _pack_bf16_pairs(y[:, :D_MODEL // 2], y[:, D_MODEL // 2:])

    @pl.when(i >= na_ref[0])
    def _():
        o_ref[...] = jnp.zeros_like(o_ref)


def _experts(xs, block_e, n_active, w_up, b_up, w_down, b_down):
    n_rows = xs.shape[0]
    n_blk = n_rows // MOE_ROWS
    return pl.pallas_call(
        _expert_kernel,
        grid_spec=pltpu.PrefetchScalarGridSpec(
            num_scalar_prefetch=2,
            grid=(n_blk,),
            in_specs=[
                pl.BlockSpec((MOE_ROWS, D_MODEL // 2), lambda i, be, na: (jnp.minimum(i, na[0] - 1), 0)),
                pl.BlockSpec((1, D_MODEL, 2 * D_FF), lambda i, be, na: (be[i], 0, 0)),
                pl.BlockSpec((1, 1, 2 * D_FF), lambda i, be, na: (be[i], 0, 0)),
                pl.BlockSpec((1, D_FF, D_MODEL), lambda i, be, na: (be[i], 0, 0)),
                pl.BlockSpec((1, 1, D_MODEL), lambda i, be, na: (be[i], 0, 0)),
            ],
            out_specs=pl.BlockSpec((MOE_ROWS, D_MODEL // 2), lambda i, be, na: (i, 0)),
        ),
        out_shape=jax.ShapeDtypeStruct((n_rows, D_MODEL // 2), jnp.uint32),
        compiler_params=_cparams(("arbitrary",)),
        name="experts",
    )(block_e, n_active, xs, w_up, b_up.reshape(N_EXPERTS, 1, 2 * D_FF), w_down, b_down.reshape(N_EXPERTS, 1, D_MODEL))


def _combine_kernel(first_tile, chunks_ref, seg_ref, loc_ref, total_ref,
                    x_ref, rows_hbm, slot_ref, gate_ref, g_ref, b_ref, o_ref, local_ref, sems):
    i = pl.program_id(0)
    n = pl.num_programs(0)
    tile = first_tile + i
    buf = i % 2

    def copy_in(b):
        return lambda loc, seg: pltpu.make_async_copy(
            rows_hbm.at[pl.ds(seg, SEG_ALIGN), :], local_ref.at[b, pl.ds(loc, SEG_ALIGN), :], sems.at[b])

    @pl.when(i == 0)
    def _():
        local_ref[...] = jnp.zeros_like(local_ref)
        _segment_copies(chunks_ref, seg_ref, loc_ref, tile, copy_in(0))

    def run(b):
        @pl.when(i + 1 < n)
        def _():
            _segment_copies(chunks_ref, seg_ref, loc_ref, tile + 1, copy_in(1 - b))

        _wait_copies(total_ref[tile], copy_in(b))
        lo, hi = _unpack_bf16_pairs(local_ref[b])
        rows = jnp.concatenate([lo, hi], axis=1).astype(BF16)
        slots = slot_ref[...]
        gates = gate_ref[...]
        row = lax.broadcasted_iota(jnp.int32, (ROW_TILE, LOCAL_ROWS), 1)
        weight = jnp.zeros((ROW_TILE, LOCAL_ROWS), F32)
        for k in range(TOP_K):
            weight = weight + jnp.where(slots[:, k:k + 1] == row, gates[:, k:k + 1], 0.0)
        y = _dot(weight.astype(BF16), rows)
        o_ref[...] = _layer_norm(DN_ALPHA * x_ref[...] + y, g_ref[...], b_ref[...])

    for b in range(2):
        pl.when(buf == b)(functools.partial(run, b))


def _combine(x2, rows_out, slots_c, gates_c, tables, g, b, first_tile, n_tiles):
    const = lambda i, *_: (0, 0)
    tile = lambda i, *_: (first_tile + i, 0)
    return pl.pallas_call(
        functools.partial(_combine_kernel, first_tile),
        grid_spec=pltpu.PrefetchScalarGridSpec(
            num_scalar_prefetch=4,
            grid=(n_tiles,),
            in_specs=[
                pl.BlockSpec((ROW_TILE, D_MODEL), tile),
                pl.BlockSpec(memory_space=pl.ANY),
                pl.BlockSpec((ROW_TILE, TOP_K), tile),
                pl.BlockSpec((ROW_TILE, TOP_K), tile),
                pl.BlockSpec((1, D_MODEL), const),
                pl.BlockSpec((1, D_MODEL), const),
            ],
            out_specs=pl.BlockSpec((ROW_TILE, D_MODEL), lambda i, *_: (i, 0)),
            scratch_shapes=[pltpu.VMEM((2, LOCAL_ROWS, HALF), jnp.uint32), pltpu.SemaphoreType.DMA((2,))],
        ),
        out_shape=jax.ShapeDtypeStruct((n_tiles * ROW_TILE, D_MODEL), F32),
        compiler_params=_cparams(("arbitrary",)),
        name="combine_ln3",
    )(*tables[:4], x2, rows_out, slots_c, gates_c, g.reshape(1, D_MODEL), b.reshape(1, D_MODEL))


def kernel(x_prompt, x_sample, mem_prompt, mem_sample, rel_bias, w_in, b_gates, conv_w, conv_b, mh_gain, attn_sink,
           w_out, ln1_g, ln1_b, wq_mem, wkv_mem, wo_mem, ln2_g, ln2_b, w_router, b_router, w_up, b_up, w_down,
           b_down, ln3_g, ln3_b):
    assert w_in.shape[0] == 1, "single layer"
    bp, sp, _ = x_prompt.shape
    bs, ss, _ = x_sample.shape
    tp, ts = bp * sp, bs * ss
    t = tp + ts
    assert sp % ROW_TILE == 0 and ss % ROW_TILE == 0 and sp >= 2 * BLOCK and ss >= 2 * BLOCK
    xp = x_prompt.reshape(tp, D_MODEL)
    xs = x_sample.reshape(ts, D_MODEL)

    def seq_blocks(rows):
        return (tp // rows, sp // rows, ss // rows)

    w = w_in[0]
    q_end, k_end, v_end = ATT_WIDTH, ATT_WIDTH + 128, ATT_WIDTH + 256
    qk_end, mv_end, mo_end = v_end + 2 * ML_WIDTH, v_end + 3 * ML_WIDTH, v_end + 4 * ML_WIDTH
    head_order = np.concatenate([[h, ATT_GROUP + h] for h in range(ATT_GROUP)])
    att_perm = (head_order[:, None] * ATT_HEAD_DIM + np.arange(ATT_HEAD_DIM)[None, :]).reshape(-1)
    w_main = jnp.concatenate([
        w[:, v_end:qk_end], w[:, qk_end:mv_end], w[:, mv_end:mo_end],
        w[:, :q_end][:, att_perm] * (ATT_HEAD_DIM ** -0.5), w[:, q_end:k_end], w[:, k_end:v_end]], axis=1).astype(BF16)
    w_g = w[:, mo_end:].astype(BF16)
    w_att = w_out[0][:ATT_WIDTH][att_perm].astype(BF16)
    w_ml = w_out[0][ATT_WIDTH:].astype(BF16)

    z, gates_c, gates_r = _in_proj(xp, xs, w_main, w_g, b_gates[0])

    bias, sink = _attention_tables(rel_bias, attn_sink[0])
    att = _attention(z, bias, sink, seq_blocks(BLOCK))

    qk = _conv_silu(z, conv_w[0], conv_b[0], seq_blocks(CONV_TILE))
    h_f = _mlstm(qk, z, gates_r, gates_c, seq_blocks(ML_CHUNK), reverse=False)
    h_b = _mlstm(qk, z, gates_r, gates_c, seq_blocks(ML_CHUNK), reverse=True)

    x1 = _mix_out(xp, xs, att, h_f, h_b, z, w_att, w_ml, mh_gain[0], ln1_g[0], ln1_b[0])

    mem = jnp.concatenate([mem_prompt.reshape(bp * MEM_TOKENS, D_MODEL), mem_sample.reshape(bs * MEM_TOKENS, D_MODEL)])
    kv = _mem_kv(mem, wkv_mem[0].astype(BF16))
    n_p_tiles, p_tiles_per_seq, s_tiles_per_seq = seq_blocks(ROW_TILE)
    mem_of_tile = lambda i: jnp.where(i < n_p_tiles, i // p_tiles_per_seq, bp + (i - n_p_tiles) // s_tiles_per_seq)
    wq = (wq_mem[0] * (MEM_HEAD_DIM ** -0.5)).astype(BF16)
    x2, x2b, top_gate, slots, tile_chunks = _cross_router(
        x1, kv, wq, wo_mem[0].astype(BF16), ln2_g[0], ln2_b[0], w_router[0], b_router[0], mem_of_tile)

    n_tiles = t // ROW_TILE
    blk_chunks = MOE_ROWS // SEG_ALIGN
    chunks = tile_chunks[:, :, 0]
    used = jnp.sum(chunks, axis=0)
    region = ((used + blk_chunks - 1) // blk_chunks) * blk_chunks
    region_end = jnp.cumsum(region)
    region_start = region_end - region
    seg_start = region_start[None, :] + jnp.cumsum(chunks, axis=0) - chunks
    loc_start = jnp.cumsum(chunks, axis=1) - chunks
    n_blk = -(-(t * TOP_K + n_tiles * N_EXPERTS * (SEG_ALIGN - 1) + N_EXPERTS * (MOE_ROWS - 1)) // MOE_ROWS)
    blk_first = jnp.arange(n_blk, dtype=jnp.int32) * blk_chunks
    block_e = jnp.minimum(jnp.sum(blk_first[:, None] >= region_end[None, :], axis=1), N_EXPERTS - 1).astype(jnp.int32)
    n_active = (region_end[-1:] // blk_chunks).astype(jnp.int32)
    tables = (chunks.reshape(-1), seg_start.reshape(-1), loc_start.reshape(-1), jnp.sum(chunks, axis=1),
              region_start + used, region - used, n_active)
    tables = tuple(tb.astype(jnp.int32) for tb in tables)

    rows_in = _dispatch(x2b, slots, tables, n_blk * MOE_ROWS)
    rows_out = _experts(rows_in, block_e, n_active, w_up[0].astype(BF16), b_up[0], w_down[0].astype(BF16), b_down[0])

    slots_c, gates_c = slots.T, top_gate.T
    y_p = _combine(x2, rows_out, slots_c, gates_c, tables, ln3_g[0], ln3_b[0], 0, n_p_tiles)
    y_s = _combine(x2, rows_out, slots_c, gates_c, tables, ln3_g[0], ln3_b[0], n_p_tiles, n_tiles - n_p_tiles)
    return y_p.reshape(bp, sp, D_MODEL), y_s.reshape(bs, ss, D_MODEL)
```

```python
import functools

import numpy as np
import jax
import jax.numpy as jnp
from jax import lax
from jax.experimental import pallas as pl
from jax.experimental.pallas import tpu as pltpu

F32 = jnp.float32
BF16 = jnp.bfloat16

D_MODEL = 1024
ATT_HEADS = 8
ATT_KV_HEADS = 2
ATT_GROUP = ATT_HEADS // ATT_KV_HEADS
ATT_HEAD_DIM = 64
ATT_WIDTH = ATT_HEADS * ATT_HEAD_DIM
WINDOW = 128
BLOCK = WINDOW
N_BUCKETS = 32
MAX_DISTANCE = 128
ML_HEADS = 4
ML_HEAD_DIM = 128
ML_WIDTH = ML_HEADS * ML_HEAD_DIM
ML_CHUNK = 128
CONV_WIDTH = 5
N_GATES = 4 * ML_HEADS
MEM_TOKENS = 256
MEM_HEADS = 4
MEM_HEAD_DIM = D_MODEL // MEM_HEADS
N_EXPERTS = 32
TOP_K = 4
D_FF = D_MODEL
SWIGLU_LIMIT = 7.0
SWIGLU_ALPHA = 1.702
LN_EPS = 1e-5
DN_ALPHA = 2.0 ** 0.25

Z_WIDTH = 2 * ML_WIDTH + ML_WIDTH + ML_WIDTH + ATT_WIDTH + 2 * ATT_KV_HEADS * ATT_HEAD_DIM
ZB_MLV = 2
ZB_MLO = 3
ZB_ATTQ = 4
ZB_ATTK = 20
ZB_ATTV = 21

ROW_TILE = 512
MOE_ROWS = 512
NEG = -1e30
VMEM_LIMIT = 56 * 1024 * 1024


def _cparams(sem):
    return pltpu.CompilerParams(dimension_semantics=sem, vmem_limit_bytes=VMEM_LIMIT)


def _dot(a, b):
    return jnp.dot(a, b, preferred_element_type=F32)


def _dot_nt(a, b):
    return lax.dot_general(a, b, (((1,), (1,)), ((), ())), preferred_element_type=F32)


def _dot_tn(a, b):
    return lax.dot_general(a, b, (((0,), (0,)), ((), ())), preferred_element_type=F32)


def _dot_exact(a, b):
    return jnp.dot(a, b, preferred_element_type=F32, precision=lax.Precision.HIGHEST)


def _layer_norm(y, g, b):
    mu = jnp.mean(y, axis=-1, keepdims=True)
    yc = y - mu
    var = jnp.mean(yc * yc, axis=-1, keepdims=True)
    return yc * lax.rsqrt(var + LN_EPS) * g + b


def _log_sigmoid(x):
    return jnp.minimum(x, 0.0) - jnp.log1p(jnp.exp(-jnp.abs(x)))


def _sigmoid(x):
    return 1.0 / (1.0 + jnp.exp(-x))


def _seq_pos(blk, n_p_blocks, p_blocks_per_seq, s_blocks_per_seq):
    in_p = blk < n_p_blocks
    local = jnp.where(in_p, blk % p_blocks_per_seq, (blk - n_p_blocks) % s_blocks_per_seq)
    per = jnp.where(in_p, p_blocks_per_seq, s_blocks_per_seq)
    return local == 0, local == per - 1


def _in_proj_kernel(n_p_tiles, xp_ref, xs_ref, w_ref, wg_ref, wgt_ref, bg_ref, bgt_ref, z_ref, gc_ref, gr_ref):
    i = pl.program_id(0)
    x = jnp.where(i < n_p_tiles, xp_ref[...], xs_ref[...]).astype(BF16)
    z_ref[...] = _dot(x, w_ref[...]).astype(BF16)
    gc_ref[...] = _dot(x, wg_ref[...]) + bg_ref[...]
    gr_ref[...] = _dot_nt(wgt_ref[...], x) + bgt_ref[...]


def _in_proj(xp, xs, w_main, w_g, b_g):
    tp, ts = xp.shape[0], xs.shape[0]
    t = tp + ts
    n_p = tp // ROW_TILE
    n = t // ROW_TILE
    const = lambda i: (0, 0)
    return pl.pallas_call(
        functools.partial(_in_proj_kernel, n_p),
        grid=(n,),
        in_specs=[
            pl.BlockSpec((ROW_TILE, D_MODEL), lambda i: (jnp.minimum(i, n_p - 1), 0)),
            pl.BlockSpec((ROW_TILE, D_MODEL), lambda i: (jnp.maximum(i - n_p, 0), 0)),
            pl.BlockSpec((D_MODEL, Z_WIDTH), const),
            pl.BlockSpec((D_MODEL, N_GATES), const),
            pl.BlockSpec((N_GATES, D_MODEL), const),
            pl.BlockSpec((1, N_GATES), const),
            pl.BlockSpec((N_GATES, 1), const),
        ],
        out_specs=[
            pl.BlockSpec((ROW_TILE, Z_WIDTH), lambda i: (i, 0)),
            pl.BlockSpec((ROW_TILE, N_GATES), lambda i: (i, 0)),
            pl.BlockSpec((N_GATES, ROW_TILE), lambda i: (0, i)),
        ],
        out_shape=[
            jax.ShapeDtypeStruct((t, Z_WIDTH), BF16),
            jax.ShapeDtypeStruct((t, N_GATES), F32),
            jax.ShapeDtypeStruct((N_GATES, t), F32),
        ],
        compiler_params=_cparams(("arbitrary",)),
        name="in_proj",
    )(xp, xs, w_main, w_g, w_g.T, b_g.reshape(1, N_GATES), b_g.reshape(N_GATES, 1))


ATT_TILE = 512
ATT_SUB = ATT_TILE // BLOCK


ONES_ROWS = 16


def _attention_kernel(seq, q_ref, kp_ref, kc_ref, kn_ref, vp_ref, vc_ref, vn_ref, bias_ref, sink_ref,
                      o_ref, klo_ref, khi_ref, vt_ref):
    i = pl.program_id(0)
    lane = lax.broadcasted_iota(jnp.int32, (ATT_TILE + 2 * BLOCK, 2 * ATT_HEAD_DIM), 1)
    kband = jnp.concatenate([kp_ref[...], kc_ref[...], kn_ref[...]], axis=0)
    zero = jnp.zeros_like(kband)
    klo_ref[...] = jnp.where(lane < ATT_HEAD_DIM, kband, zero)
    khi_ref[...] = jnp.where(lane < ATT_HEAD_DIM, zero, kband)
    vband = jnp.concatenate([vp_ref[...], vc_ref[...], vn_ref[...]], axis=0)
    vt_ref[0:2 * ATT_HEAD_DIM, :] = vband.T
    vt_ref[2 * ATT_HEAD_DIM:, :] = jnp.ones((ONES_ROWS, ATT_TILE + 2 * BLOCK), BF16)
    feat = lax.broadcasted_iota(jnp.int32, (2 * ATT_HEAD_DIM, ATT_GROUP * BLOCK), 0)
    for s in range(ATT_SUB):
        first, last = _seq_pos(i * ATT_SUB + s, *seq)
        variant = jnp.where(first, 1, jnp.where(last, 2, 0))
        q = q_ref[s * BLOCK:(s + 1) * BLOCK, :]
        q_all = jnp.concatenate([q[:, t * 128:(t + 1) * 128] for t in range(ATT_GROUP)], axis=0)
        vt = vt_ref[:, s * BLOCK:(s + 3) * BLOCK]
        outs = []
        for kv, k_ref in enumerate((klo_ref, khi_ref)):
            logits = _dot_nt(k_ref[s * BLOCK:(s + 3) * BLOCK, :], q_all) + bias_ref[variant, kv]
            sink = sink_ref[kv]
            m = jnp.maximum(jnp.max(logits, axis=0, keepdims=True), sink)
            p = jnp.exp(logits - m).astype(BF16)
            ov = _dot(vt, p)
            den = ov[2 * ATT_HEAD_DIM:2 * ATT_HEAD_DIM + 1, :] + jnp.exp(sink - m)
            outs.append(ov[0:2 * ATT_HEAD_DIM, :] * (1.0 / den))
        both = jnp.where(feat < ATT_HEAD_DIM, outs[0], outs[1]).astype(BF16)
        for t in range(ATT_GROUP):
            o_ref[t * 128:(t + 1) * 128, s * BLOCK:(s + 1) * BLOCK] = both[:, t * BLOCK:(t + 1) * BLOCK]


def _attention(z, bias, sink, seq):
    t = z.shape[0]
    n = t // ATT_TILE
    nblk = t // BLOCK
    band = ATT_TILE + 2 * BLOCK
    prev = lambda i: jnp.maximum(i * ATT_SUB - 1, 0)
    nxt = lambda i: jnp.minimum((i + 1) * ATT_SUB, nblk - 1)
    return pl.pallas_call(
        functools.partial(_attention_kernel, seq),
        grid=(n,),
        in_specs=[
            pl.BlockSpec((ATT_TILE, ATT_WIDTH), lambda i: (i, ZB_ATTQ)),
            pl.BlockSpec((BLOCK, 128), lambda i: (prev(i), ZB_ATTK)),
            pl.BlockSpec((ATT_TILE, 128), lambda i: (i, ZB_ATTK)),
            pl.BlockSpec((BLOCK, 128), lambda i: (nxt(i), ZB_ATTK)),
            pl.BlockSpec((BLOCK, 128), lambda i: (prev(i), ZB_ATTV)),
            pl.BlockSpec((ATT_TILE, 128), lambda i: (i, ZB_ATTV)),
            pl.BlockSpec((BLOCK, 128), lambda i: (nxt(i), ZB_ATTV)),
            pl.BlockSpec((3, ATT_KV_HEADS, 3 * BLOCK, ATT_GROUP * BLOCK), lambda i: (0, 0, 0, 0)),
            pl.BlockSpec((ATT_KV_HEADS, 1, ATT_GROUP * BLOCK), lambda i: (0, 0, 0)),
        ],
        out_specs=pl.BlockSpec((ATT_WIDTH, ATT_TILE), lambda i: (0, i)),
        out_shape=jax.ShapeDtypeStruct((ATT_WIDTH, t), BF16),
        scratch_shapes=[pltpu.VMEM((band, 128), BF16), pltpu.VMEM((band, 128), BF16),
                        pltpu.VMEM((2 * ATT_HEAD_DIM + ONES_ROWS, band), BF16)],
        compiler_params=_cparams(("arbitrary",)),
        name="window_attention",
    )(z, z, z, z, z, z, z, bias, sink)


def _t5_bucket(rel):
    half = N_BUCKETS // 2
    exact = half // 2
    n = np.abs(rel)
    large = exact + (np.log(np.maximum(n, 1) / exact) / np.log(MAX_DISTANCE / exact) * (half - exact)).astype(np.int32)
    large = np.minimum(large, half - 1)
    return ((rel > 0).astype(np.int32) * half + np.where(n < exact, n, large)).astype(np.int32)


def _attention_tables(rel_bias, attn_sink):
    rel = np.arange(3 * BLOCK)[:, None] - BLOCK - np.arange(BLOCK)[None, :]
    onehot = jnp.asarray(_t5_bucket(rel)[..., None] == np.arange(N_BUCKETS), F32)
    bias = jnp.einsum('kqb,bh->hkq', onehot, rel_bias.astype(F32), precision=lax.Precision.HIGHEST)
    bias = jnp.where(jnp.asarray(np.abs(rel) <= WINDOW)[None], bias, NEG)
    bias = bias.reshape(ATT_KV_HEADS, ATT_GROUP, 3 * BLOCK, BLOCK).transpose(0, 2, 1, 3)
    bias = bias.reshape(ATT_KV_HEADS, 3 * BLOCK, ATT_GROUP * BLOCK)
    key = np.arange(3 * BLOCK)[None, :, None]
    first = jnp.where(jnp.asarray(key < BLOCK), NEG, bias)
    last = jnp.where(jnp.asarray(key >= 2 * BLOCK), NEG, bias)
    sink = jnp.repeat(attn_sink.astype(F32), BLOCK).reshape(ATT_KV_HEADS, 1, ATT_GROUP * BLOCK)
    return jnp.stack([bias, first, last]), sink


CONV_TILE = 512
CONV_HALO = 16
QK_WIDTH = 2 * ML_WIDTH


def _conv_kernel(seq, xp_ref, xc_ref, xn_ref, w_ref, b_ref, scale_ref, o_ref, buf_ref):
    i = pl.program_id(0)
    first, last = _seq_pos(i, *seq)
    buf_ref[0:8, :] = jnp.where(first, 0.0, xp_ref[8:16, :].astype(F32))
    buf_ref[8:8 + CONV_TILE, :] = xc_ref[...].astype(F32)
    buf_ref[8 + CONV_TILE:16 + CONV_TILE, :] = jnp.where(last, 0.0, xn_ref[0:8, :].astype(F32))
    acc = jnp.zeros((CONV_TILE, QK_WIDTH), F32) + b_ref[...]
    for j in range(CONV_WIDTH):
        off = 8 + j - CONV_WIDTH // 2
        acc = acc + buf_ref[off:off + CONV_TILE, :] * w_ref[j:j + 1, :]
    o_ref[...] = (acc * _sigmoid(acc) * scale_ref[...]).astype(BF16)


def _conv_silu(z, conv_w, conv_b, seq_tiles):
    t = z.shape[0]
    n = t // CONV_TILE
    r = CONV_TILE // CONV_HALO
    nh = t // CONV_HALO
    scale = jnp.concatenate([jnp.ones((1, ML_WIDTH), F32), jnp.full((1, ML_WIDTH), ML_HEAD_DIM ** -0.5, F32)], axis=1)
    return pl.pallas_call(
        functools.partial(_conv_kernel, seq_tiles),
        grid=(n,),
        in_specs=[
            pl.BlockSpec((CONV_HALO, QK_WIDTH), lambda i: (jnp.maximum(i * r - 1, 0), 0)),
            pl.BlockSpec((CONV_TILE, QK_WIDTH), lambda i: (i, 0)),
            pl.BlockSpec((CONV_HALO, QK_WIDTH), lambda i: (jnp.minimum((i + 1) * r, nh - 1), 0)),
            pl.BlockSpec((CONV_WIDTH, QK_WIDTH), lambda i: (0, 0)),
            pl.BlockSpec((1, QK_WIDTH), lambda i: (0, 0)),
            pl.BlockSpec((1, QK_WIDTH), lambda i: (0, 0)),
        ],
        out_specs=pl.BlockSpec((CONV_TILE, QK_WIDTH), lambda i: (i, 0)),
        out_shape=jax.ShapeDtypeStruct((t, QK_WIDTH), BF16),
        scratch_shapes=[pltpu.VMEM((CONV_TILE + 16, QK_WIDTH), F32)],
        compiler_params=_cparams(("arbitrary",)),
        name="conv_silu",
    )(z, z, z, conv_w, conv_b.reshape(1, QK_WIDTH), scale)


def _mlstm_kernel(seq, n_chunks, *refs):
    fwd_in, bwd_in, (of_ref, ob_ref), state = refs[0:5], refs[5:10], refs[10:12], refs[12:]
    _mlstm_scan(seq, False, n_chunks, *fwd_in, of_ref, *state[0:2])
    _mlstm_scan(seq, True, n_chunks, *bwd_in, ob_ref, *state[2:4])


def _mlstm_scan(seq, reverse, n_chunks, q_ref, k_ref, v_ref, gr_ref, gc_ref, o_ref, ct_ref, m_ref):
    step = pl.program_id(0)
    chunk = (n_chunks - 1 - step) if reverse else step
    first, last = _seq_pos(chunk, *seq)
    fresh = last if reverse else first

    L = ML_CHUNK
    row = lax.broadcasted_iota(jnp.int32, (L, L), 0)
    col = lax.broadcasted_iota(jnp.int32, (L, L), 1)
    vis = (row >= col) if reverse else (row <= col)
    vis_t = (col >= row) if reverse else (col <= row)
    gr = gr_ref[...]
    gc = gc_ref[...]
    b_rows = _dot_exact(_log_sigmoid(gr), vis.astype(F32))
    b_cols = _dot_exact(vis_t.astype(F32), _log_sigmoid(gc))
    i_off = 2 * ML_HEADS if reverse else 0
    f_off = i_off + ML_HEADS
    end = 0 if reverse else L - 1
    ones = jnp.ones((ONES_ROWS, L), BF16)
    for h in range(ML_HEADS):
        hs = slice(h * ML_HEAD_DIM, (h + 1) * ML_HEAD_DIM)
        q = q_ref[:, hs]
        k = k_ref[:, hs]
        vt1 = jnp.concatenate([v_ref[:, hs].T, ones], axis=0)
        b_row = b_rows[f_off + h:f_off + h + 1, :]
        u_row = gr[i_off + h:i_off + h + 1, :] - b_row
        u_col = gc[:, i_off + h:i_off + h + 1] - b_cols[:, f_off + h:f_off + h + 1]
        g = b_row[:, end:end + 1]
        ct_old = jnp.where(fresh, 0.0, ct_ref[h])
        m_old = jnp.where(fresh, 0.0, m_ref[h])[:, 0:1]
        u_mat = jnp.where(vis, u_col, NEG)
        mm = jnp.maximum(jnp.max(u_mat, axis=0, keepdims=True), m_old)
        st = (_dot_nt(k, q) * jnp.exp(u_mat - mm)).astype(BF16)
        w_inter = jnp.exp(m_old - mm)
        tot = _dot(vt1, st) + w_inter * _dot_nt(ct_old.astype(BF16), q)
        den = tot[ML_HEAD_DIM:ML_HEAD_DIM + 1, :]
        floor = jnp.exp(-(b_row + mm))
        o_ref[hs, :] = tot[0:ML_HEAD_DIM, :] * (1.0 / jnp.maximum(jnp.abs(den), floor))
        a_max = jnp.max(g + u_row, axis=-1, keepdims=True)
        m_new = jnp.maximum(g + m_old, a_max)
        s_old = jnp.exp(g + m_old - m_new)
        weighted = (vt1.astype(F32) * jnp.exp(g + u_row - m_new)).astype(BF16)
        ct_ref[h] = s_old * ct_old + _dot(weighted, k)
        m_ref[h] = jnp.broadcast_to(m_new, (1, ML_HEAD_DIM))


def _mlstm(qk, z, gates_r, gates_c, seq):
    t = qk.shape[0]
    nc = t // ML_CHUNK

    def chunk_specs(ch):
        return [
            pl.BlockSpec((ML_CHUNK, ML_WIDTH), lambda i: (ch(i), 0)),
            pl.BlockSpec((ML_CHUNK, ML_WIDTH), lambda i: (ch(i), 1)),
            pl.BlockSpec((ML_CHUNK, ML_WIDTH), lambda i: (ch(i), ZB_MLV)),
            pl.BlockSpec((N_GATES, ML_CHUNK), lambda i: (0, ch(i))),
            pl.BlockSpec((ML_CHUNK, N_GATES), lambda i: (ch(i), 0)),
        ]

    fwd = lambda i: i
    bwd = lambda i: nc - 1 - i
    state = [
        pltpu.VMEM((ML_HEADS, ML_HEAD_DIM + ONES_ROWS, ML_HEAD_DIM), F32),
        pltpu.VMEM((ML_HEADS, 1, ML_HEAD_DIM), F32),
    ]
    operands = (qk, qk, z, gates_r, gates_c)
    return pl.pallas_call(
        functools.partial(_mlstm_kernel, seq, nc),
        grid=(nc,),
        in_specs=chunk_specs(fwd) + chunk_specs(bwd),
        out_specs=[pl.BlockSpec((ML_WIDTH, ML_CHUNK), lambda i: (0, fwd(i))),
                   pl.BlockSpec((ML_WIDTH, ML_CHUNK), lambda i: (0, bwd(i)))],
        out_shape=[jax.ShapeDtypeStruct((ML_WIDTH, t), F32), jax.ShapeDtypeStruct((ML_WIDTH, t), F32)],
        scratch_shapes=state + state,
        compiler_params=_cparams(("arbitrary",)),
        name="mlstm",
    )(*operands, *operands)


def _mix_out_kernel(n_p_tiles, xp_ref, xs_ref, att_ref, hf_ref, hb_ref, og_ref, wa_ref, wm_ref, gain_ref,
                    g_ref, b_ref, o_ref):
    i = pl.program_id(0)
    x = jnp.where(i < n_p_tiles, xp_ref[...], xs_ref[...])
    h = hf_ref[...] + hb_ref[...]
    parts = []
    for hd in range(ML_HEADS):
        hh = h[hd * ML_HEAD_DIM:(hd + 1) * ML_HEAD_DIM, :]
        mu = jnp.mean(hh, axis=0, keepdims=True)
        hc = hh - mu
        var = jnp.mean(hc * hc, axis=0, keepdims=True)
        parts.append(hc * lax.rsqrt(var + LN_EPS))
    hn = jnp.concatenate(parts, axis=0) * gain_ref[...] * _sigmoid(og_ref[...].T.astype(F32))
    mixed = _dot(att_ref[...].T, wa_ref[...]) + _dot(hn.astype(BF16).T, wm_ref[...])
    o_ref[...] = _layer_norm(DN_ALPHA * x + mixed, g_ref[...], b_ref[...])


def _mix_out(xp, xs, att, hf, hb, z, w_att, w_ml, gain, g, b):
    t = att.shape[1]
    n_p = xp.shape[0] // ROW_TILE
    n = t // ROW_TILE
    const = lambda i: (0, 0)
    tile = lambda i: (i, 0)
    lanes = lambda i: (0, i)
    return pl.pallas_call(
        functools.partial(_mix_out_kernel, n_p),
        grid=(n,),
        in_specs=[
            pl.BlockSpec((ROW_TILE, D_MODEL), lambda i: (jnp.minimum(i, n_p - 1), 0)),
            pl.BlockSpec((ROW_TILE, D_MODEL), lambda i: (jnp.maximum(i - n_p, 0), 0)),
            pl.BlockSpec((ATT_WIDTH, ROW_TILE), lanes),
            pl.BlockSpec((ML_WIDTH, ROW_TILE), lanes),
            pl.BlockSpec((ML_WIDTH, ROW_TILE), lanes),
            pl.BlockSpec((ROW_TILE, ML_WIDTH), lambda i: (i, ZB_MLO)),
            pl.BlockSpec((ATT_WIDTH, D_MODEL), const),
            pl.BlockSpec((ML_WIDTH, D_MODEL), const),
            pl.BlockSpec((ML_WIDTH, 1), const),
            pl.BlockSpec((1, D_MODEL), const),
            pl.BlockSpec((1, D_MODEL), const),
        ],
        out_specs=pl.BlockSpec((ROW_TILE, D_MODEL), tile),
        out_shape=jax.ShapeDtypeStruct((t, D_MODEL), F32),
        compiler_params=_cparams(("arbitrary",)),
        name="mix_out_ln1",
    )(xp, xs, att, hf, hb, z, w_att, w_ml, gain.reshape(ML_WIDTH, 1), g.reshape(1, D_MODEL), b.reshape(1, D_MODEL))


def _mem_kv_kernel(m_ref, w_ref, o_ref):
    o_ref[...] = _dot(m_ref[...].astype(BF16), w_ref[...]).astype(BF16)


def _mem_kv(mem, wkv):
    rows = mem.shape[0]
    return pl.pallas_call(
        _mem_kv_kernel,
        grid=(rows // MEM_TOKENS,),
        in_specs=[pl.BlockSpec((MEM_TOKENS, D_MODEL), lambda i: (i, 0)),
                  pl.BlockSpec((D_MODEL, 2 * D_MODEL), lambda i: (0, 0))],
        out_specs=pl.BlockSpec((MEM_TOKENS, 2 * D_MODEL), lambda i: (i, 0)),
        out_shape=jax.ShapeDtypeStruct((rows, 2 * D_MODEL), BF16),
        compiler_params=_cparams(("arbitrary",)),
        name="mem_kv",
    )(mem, wkv)


def _pack_bf16_pairs(a, b):
    ua = pltpu.bitcast(a.astype(BF16).astype(F32), jnp.uint32)
    ub = pltpu.bitcast(b.astype(BF16).astype(F32), jnp.uint32)
    return (ua >> 16) | (ub & jnp.uint32(0xFFFF0000))


def _unpack_bf16_pairs(u):
    lo = pltpu.bitcast(u << 16, F32)
    hi = pltpu.bitcast(u & jnp.uint32(0xFFFF0000), F32)
    return lo, hi


def _cross_router_kernel(x_ref, kv_ref, wq_ref, wo_ref, g_ref, b_ref, wr_ref, br_ref, tri_ref, etri_ref,
                         x2_ref, x2b_ref, gate_ref, slot_ref, chunk_ref):
    x = x_ref[...]
    q = _dot(x.astype(BF16), wq_ref[...]).astype(BF16)
    heads = []
    for h in range(MEM_HEADS):
        hs = slice(h * MEM_HEAD_DIM, (h + 1) * MEM_HEAD_DIM)
        logits = _dot_nt(q[:, hs], kv_ref[:, hs])
        m = jnp.max(logits, axis=-1, keepdims=True)
        p = jnp.exp(logits - m)
        den = jnp.sum(p, axis=-1, keepdims=True)
        vs = slice(D_MODEL + h * MEM_HEAD_DIM, D_MODEL + (h + 1) * MEM_HEAD_DIM)
        heads.append((_dot(p.astype(BF16), kv_ref[:, vs]) * (1.0 / den)).astype(BF16))
    o = jnp.concatenate(heads, axis=1)
    x2 = _layer_norm(DN_ALPHA * x + _dot(o, wo_ref[...]), g_ref[...], b_ref[...])
    x2_ref[...] = x2
    x2b = x2.astype(BF16)
    x2b_ref[...] = x2b

    logits = _dot_nt(wr_ref[...], x2b) + br_ref[...]
    expert = lax.broadcasted_iota(jnp.int32, logits.shape, 0)
    work = logits
    vals, sels = [], []
    for k in range(TOP_K):
        mx = jnp.max(work, axis=0, keepdims=True)
        ix = jnp.min(jnp.where(work == mx, expert, N_EXPERTS), axis=0, keepdims=True)
        sel = expert == ix
        work = jnp.where(sel, -jnp.inf, work)
        vals.append(mx)
        sels.append(sel)
    es = [jnp.exp(v - vals[0]) for v in vals]
    tot = es[0] + es[1] + es[2] + es[3]
    chosen = jnp.zeros(logits.shape, F32)
    for k in range(TOP_K):
        gate_ref[k:k + 1, :] = es[k] / tot
        chosen = chosen + sels[k].astype(F32)
    count = jnp.sum(chosen, axis=1, keepdims=True)
    chunks = jnp.floor((count + (SEG_ALIGN - 1)) * (1.0 / SEG_ALIGN))
    chunks_b = jnp.broadcast_to(chunks, (N_EXPERTS, 128))
    seg_first = _dot(etri_ref[...], chunks_b.astype(BF16))[:, 0:1] * SEG_ALIGN
    before = _dot(chosen.astype(BF16), tri_ref[...])
    local_row = before + seg_first
    for k in range(TOP_K):
        slot_ref[k:k + 1, :] = jnp.sum(jnp.where(sels[k], local_row, 0.0), axis=0, keepdims=True).astype(jnp.int32)
    chunk_ref[0] = chunks_b.astype(jnp.int32)


def _cross_router(x1, kv, wq, wo, g, b, w_router, b_router, mem_of_tile):
    t = x1.shape[0]
    n = t // ROW_TILE
    const = lambda i: (0, 0)
    tile = lambda i: (i, 0)
    lanes = lambda i: (0, i)
    tri = jnp.asarray(np.triu(np.ones((ROW_TILE, ROW_TILE), np.float32), 1), BF16)
    etri = jnp.asarray(np.tril(np.ones((N_EXPERTS, N_EXPERTS), np.float32), -1), BF16)
    return pl.pallas_call(
        _cross_router_kernel,
        grid=(n,),
        in_specs=[
            pl.BlockSpec((ROW_TILE, D_MODEL), tile),
            pl.BlockSpec((MEM_TOKENS, 2 * D_MODEL), lambda i: (mem_of_tile(i), 0)),
            pl.BlockSpec((D_MODEL, D_MODEL), const),
            pl.BlockSpec((D_MODEL, D_MODEL), const),
            pl.BlockSpec((1, D_MODEL), const),
            pl.BlockSpec((1, D_MODEL), const),
            pl.BlockSpec((N_EXPERTS, D_MODEL), const),
            pl.BlockSpec((N_EXPERTS, 1), const),
            pl.BlockSpec((ROW_TILE, ROW_TILE), const),
            pl.BlockSpec((N_EXPERTS, N_EXPERTS), const),
        ],
        out_specs=[
            pl.BlockSpec((ROW_TILE, D_MODEL), tile),
            pl.BlockSpec((ROW_TILE, D_MODEL), tile),
            pl.BlockSpec((TOP_K, ROW_TILE), lanes),
            pl.BlockSpec((TOP_K, ROW_TILE), lanes),
            pl.BlockSpec((1, N_EXPERTS, 128), lambda i: (i, 0, 0)),
        ],
        out_shape=[
            jax.ShapeDtypeStruct((t, D_MODEL), F32),
            jax.ShapeDtypeStruct((t, D_MODEL), BF16),
            jax.ShapeDtypeStruct((TOP_K, t), F32),
            jax.ShapeDtypeStruct((TOP_K, t), jnp.int32),
            jax.ShapeDtypeStruct((n, N_EXPERTS, 128), jnp.int32),
        ],
        compiler_params=_cparams(("arbitrary",)),
        name="cross_attn_ln2_router",
    )(x1, kv, wq, wo, g.reshape(1, D_MODEL), b.reshape(1, D_MODEL), w_router.T.astype(BF16),
      b_router.reshape(N_EXPERTS, 1), tri, etri)


SEG_ALIGN = 8
LOCAL_ROWS = ROW_TILE * TOP_K + N_EXPERTS * SEG_ALIGN
HALF = D_MODEL // 2


def _segment_copies(chunks_ref, seg_ref, loc_ref, tile, make_copy):
    def per_expert(e, carry):
        entry = tile * N_EXPERTS + e
        seg0 = seg_ref[entry]
        loc0 = loc_ref[entry]

        def per_chunk(c, carry):
            make_copy(pl.multiple_of((loc0 + c) * SEG_ALIGN, SEG_ALIGN),
                      pl.multiple_of((seg0 + c) * SEG_ALIGN, SEG_ALIGN)).start()
            return carry

        return lax.fori_loop(0, chunks_ref[entry], per_chunk, carry)

    lax.fori_loop(0, N_EXPERTS, per_expert, 0)


def _wait_copies(count, make_copy):
    def one(c, carry):
        make_copy(0, 0).wait()
        return carry

    lax.fori_loop(0, count, one, 0)


def _dispatch_kernel(chunks_ref, seg_ref, loc_ref, total_ref, tail_ref, tailn_ref, nact_ref,
                     x_ref, slot_ref, rows_hbm, local_ref, zero_ref, sems):
    i = pl.program_id(0)
    n = pl.num_programs(0)
    buf = i % 2

    def copy_out(b):
        return lambda loc, seg: pltpu.make_async_copy(
            local_ref.at[b, pl.ds(loc, SEG_ALIGN), :], rows_hbm.at[pl.ds(seg, SEG_ALIGN), :], sems.at[b])

    def run(b):
        @pl.when(i >= 2)
        def _():
            _wait_copies(total_ref[jnp.maximum(i - 2, 0)], copy_out(b))

        slots = slot_ref[...]
        row = lax.broadcasted_iota(jnp.int32, (LOCAL_ROWS, ROW_TILE), 0)
        hit = slots[0:1, :] == row
        for k in range(1, TOP_K):
            hit = hit | (slots[k:k + 1, :] == row)
        perm = jnp.where(hit, 1.0, 0.0).astype(BF16)
        rows = _dot(perm, x_ref[...])
        local_ref[b] = _pack_bf16_pairs(rows[:, :HALF], rows[:, HALF:])
        _segment_copies(chunks_ref, seg_ref, loc_ref, i, copy_out(b))

    for b in range(2):
        pl.when(buf == b)(functools.partial(run, b))

    @pl.when(i == n - 1)
    def _():
        for b in range(2):
            step = jnp.where(buf == b, i, i - 1)
            _wait_copies(total_ref[step], copy_out(b))
        zero_ref[...] = jnp.zeros_like(zero_ref)
        fill = lambda _, seg: pltpu.make_async_copy(
            zero_ref.at[pl.ds(0, SEG_ALIGN), :], rows_hbm.at[pl.ds(seg, SEG_ALIGN), :], sems.at[2])

        def per_expert(e, total):
            def per_chunk(c, carry):
                fill(0, pl.multiple_of((tail_ref[e] + c) * SEG_ALIGN, SEG_ALIGN)).start()
                return carry

            lax.fori_loop(0, tailn_ref[e], per_chunk, 0)
            return total + tailn_ref[e]

        total = lax.fori_loop(0, N_EXPERTS, per_expert, 0)
        _wait_copies(total, fill)
        fill_block = lambda blk: pltpu.make_async_copy(
            zero_ref, rows_hbm.at[pl.ds(pl.multiple_of(blk * MOE_ROWS, MOE_ROWS), MOE_ROWS), :], sems.at[2])
        n_blocks = rows_hbm.shape[0] // MOE_ROWS

        def start_block(blk, carry):
            fill_block(blk).start()
            return carry

        def wait_block(blk, carry):
            fill_block(0).wait()
            return carry

        lax.fori_loop(nact_ref[0], n_blocks, start_block, 0)
        lax.fori_loop(nact_ref[0], n_blocks, wait_block, 0)


def _dispatch(x2b, slots, tables, n_rows):
    n = x2b.shape[0] // ROW_TILE
    assert n >= 2
    return pl.pallas_call(
        _dispatch_kernel,
        grid_spec=pltpu.PrefetchScalarGridSpec(
            num_scalar_prefetch=7,
            grid=(n,),
            in_specs=[
                pl.BlockSpec((ROW_TILE, D_MODEL), lambda i, *_: (i, 0)),
                pl.BlockSpec((TOP_K, ROW_TILE), lambda i, *_: (0, i)),
            ],
            out_specs=pl.BlockSpec(memory_space=pl.ANY),
            scratch_shapes=[
                pltpu.VMEM((2, LOCAL_ROWS, HALF), jnp.uint32),
                pltpu.VMEM((MOE_ROWS, HALF), jnp.uint32),
                pltpu.SemaphoreType.DMA((3,)),
            ],
        ),
        out_shape=jax.ShapeDtypeStruct((n_rows, HALF), jnp.uint32),
        compiler_params=_cparams(("arbitrary",)),
        name="moe_dispatch",
    )(*tables, x2b, slots)


def _expert_kernel(be_ref, na_ref, x_ref, wu_ref, bu_ref, wd_ref, bd_ref, o_ref):
    i = pl.program_id(0)

    @pl.when(i < na_ref[0])
    def _():
        lo, hi = _unpack_bf16_pairs(x_ref[...])
        x = jnp.concatenate([lo, hi], axis=1).astype(BF16)
        hu = _dot(x, wu_ref[0]) + bu_ref[0]
        h_glu = jnp.minimum(hu[:, :D_FF], SWIGLU_LIMIT)
        h_lin = jnp.clip(hu[:, D_FF:], -SWIGLU_LIMIT, SWIGLU_LIMIT)
        hh = h_glu * _sigmoid(SWIGLU_ALPHA * h_glu) * (h_lin + 1.0)
        y = _dot(hh.astype(BF16), wd_ref[0]) + bd_ref[0]
        o_ref[...] = _pack_bf16_pairs(y[:, :D_MODEL // 2], y[:, D_MODEL // 2:])

    @pl.when(i >= na_ref[0])
    def _():
        o_ref[...] = jnp.zeros_like(o_ref)


def _experts(xs, block_e, n_active, w_up, b_up, w_down, b_down):
    n_rows = xs.shape[0]
    n_blk = n_rows // MOE_ROWS
    return pl.pallas_call(
        _expert_kernel,
        grid_spec=pltpu.PrefetchScalarGridSpec(
            num_scalar_prefetch=2,
            grid=(n_blk,),
            in_specs=[
                pl.BlockSpec((MOE_ROWS, D_MODEL // 2), lambda i, be, na: (jnp.minimum(i, na[0] - 1), 0)),
                pl.BlockSpec((1, D_MODEL, 2 * D_FF), lambda i, be, na: (be[i], 0, 0)),
                pl.BlockSpec((1, 1, 2 * D_FF), lambda i, be, na: (be[i], 0, 0)),
                pl.BlockSpec((1, D_FF, D_MODEL), lambda i, be, na: (be[i], 0, 0)),
                pl.BlockSpec((1, 1, D_MODEL), lambda i, be, na: (be[i], 0, 0)),
            ],
            out_specs=pl.BlockSpec((MOE_ROWS, D_MODEL // 2), lambda i, be, na: (i, 0)),
        ),
        out_shape=jax.ShapeDtypeStruct((n_rows, D_MODEL // 2), jnp.uint32),
        compiler_params=_cparams(("arbitrary",)),
        name="experts",
    )(block_e, n_active, xs, w_up, b_up.reshape(N_EXPERTS, 1, 2 * D_FF), w_down, b_down.reshape(N_EXPERTS, 1, D_MODEL))


def _combine_kernel(first_tile, chunks_ref, seg_ref, loc_ref, total_ref,
                    x_ref, rows_hbm, slot_ref, gate_ref, g_ref, b_ref, o_ref, local_ref, sems):
    i = pl.program_id(0)
    n = pl.num_programs(0)
    tile = first_tile + i
    buf = i % 2

    def copy_in(b):
        return lambda loc, seg: pltpu.make_async_copy(
            rows_hbm.at[pl.ds(seg, SEG_ALIGN), :], local_ref.at[b, pl.ds(loc, SEG_ALIGN), :], sems.at[b])

    @pl.when(i == 0)
    def _():
        local_ref[...] = jnp.zeros_like(local_ref)
        _segment_copies(chunks_ref, seg_ref, loc_ref, tile, copy_in(0))

    def run(b):
        @pl.when(i + 1 < n)
        def _():
            _segment_copies(chunks_ref, seg_ref, loc_ref, tile + 1, copy_in(1 - b))

        _wait_copies(total_ref[tile], copy_in(b))
        lo, hi = _unpack_bf16_pairs(local_ref[b])
        rows = jnp.concatenate([lo, hi], axis=1).astype(BF16)
        slots = slot_ref[...]
        gates = gate_ref[...]
        row = lax.broadcasted_iota(jnp.int32, (ROW_TILE, LOCAL_ROWS), 1)
        weight = jnp.zeros((ROW_TILE, LOCAL_ROWS), F32)
        for k in range(TOP_K):
            weight = weight + jnp.where(slots[:, k:k + 1] == row, gates[:, k:k + 1], 0.0)
        y = _dot(weight.astype(BF16), rows)
        o_ref[...] = _layer_norm(DN_ALPHA * x_ref[...] + y, g_ref[...], b_ref[...])

    for b in range(2):
        pl.when(buf == b)(functools.partial(run, b))


def _combine(x2, rows_out, slots_c, gates_c, tables, g, b, first_tile, n_tiles):
    const = lambda i, *_: (0, 0)
    tile = lambda i, *_: (first_tile + i, 0)
    return pl.pallas_call(
        functools.partial(_combine_kernel, first_tile),
        grid_spec=pltpu.PrefetchScalarGridSpec(
            num_scalar_prefetch=4,
            grid=(n_tiles,),
            in_specs=[
                pl.BlockSpec((ROW_TILE, D_MODEL), tile),
                pl.BlockSpec(memory_space=pl.ANY),
                pl.BlockSpec((ROW_TILE, TOP_K), tile),
                pl.BlockSpec((ROW_TILE, TOP_K), tile),
                pl.BlockSpec((1, D_MODEL), const),
                pl.BlockSpec((1, D_MODEL), const),
            ],
            out_specs=pl.BlockSpec((ROW_TILE, D_MODEL), lambda i, *_: (i, 0)),
            scratch_shapes=[pltpu.VMEM((2, LOCAL_ROWS, HALF), jnp.uint32), pltpu.SemaphoreType.DMA((2,))],
        ),
        out_shape=jax.ShapeDtypeStruct((n_tiles * ROW_TILE, D_MODEL), F32),
        compiler_params=_cparams(("arbitrary",)),
        name="combine_ln3",
    )(*tables[:4], x2, rows_out, slots_c, gates_c, g.reshape(1, D_MODEL), b.reshape(1, D_MODEL))


def kernel(x_prompt, x_sample, mem_prompt, mem_sample, rel_bias, w_in, b_gates, conv_w, conv_b, mh_gain, attn_sink,
           w_out, ln1_g, ln1_b, wq_mem, wkv_mem, wo_mem, ln2_g, ln2_b, w_router, b_router, w_up, b_up, w_down,
           b_down, ln3_g, ln3_b):
    assert w_in.shape[0] == 1, "single layer"
    bp, sp, _ = x_prompt.shape
    bs, ss, _ = x_sample.shape
    tp, ts = bp * sp, bs * ss
    t = tp + ts
    assert sp % ROW_TILE == 0 and ss % ROW_TILE == 0 and sp >= 2 * BLOCK and ss >= 2 * BLOCK
    xp = x_prompt.reshape(tp, D_MODEL)
    xs = x_sample.reshape(ts, D_MODEL)

    def seq_blocks(rows):
        return (tp // rows, sp // rows, ss // rows)

    w = w_in[0]
    q_end, k_end, v_end = ATT_WIDTH, ATT_WIDTH + 128, ATT_WIDTH + 256
    qk_end, mv_end, mo_end = v_end + 2 * ML_WIDTH, v_end + 3 * ML_WIDTH, v_end + 4 * ML_WIDTH
    head_order = np.concatenate([[h, ATT_GROUP + h] for h in range(ATT_GROUP)])
    att_perm = (head_order[:, None] * ATT_HEAD_DIM + np.arange(ATT_HEAD_DIM)[None, :]).reshape(-1)
    w_main = jnp.concatenate([
        w[:, v_end:qk_end], w[:, qk_end:mv_end], w[:, mv_end:mo_end],
        w[:, :q_end][:, att_perm] * (ATT_HEAD_DIM ** -0.5), w[:, q_end:k_end], w[:, k_end:v_end]], axis=1).astype(BF16)
    w_g = w[:, mo_end:].astype(BF16)
    w_att = w_out[0][:ATT_WIDTH][att_perm].astype(BF16)
    w_ml = w_out[0][ATT_WIDTH:].astype(BF16)

    z, gates_c, gates_r = _in_proj(xp, xs, w_main, w_g, b_gates[0])

    bias, sink = _attention_tables(rel_bias, attn_sink[0])
    att = _attention(z, bias, sink, seq_blocks(BLOCK))

    qk = _conv_silu(z, conv_w[0], conv_b[0], seq_blocks(CONV_TILE))
    h_f, h_b = _mlstm(qk, z, gates_r, gates_c, seq_blocks(ML_CHUNK))

    x1 = _mix_out(xp, xs, att, h_f, h_b, z, w_att, w_ml, mh_gain[0], ln1_g[0], ln1_b[0])

    mem = jnp.concatenate([mem_prompt.reshape(bp * MEM_TOKENS, D_MODEL), mem_sample.reshape(bs * MEM_TOKENS, D_MODEL)])
    kv = _mem_kv(mem, wkv_mem[0].astype(BF16))
    n_p_tiles, p_tiles_per_seq, s_tiles_per_seq = seq_blocks(ROW_TILE)
    mem_of_tile = lambda i: jnp.where(i < n_p_tiles, i // p_tiles_per_seq, bp + (i - n_p_tiles) // s_tiles_per_seq)
    wq = (wq_mem[0] * (MEM_HEAD_DIM ** -0.5)).astype(BF16)
    x2, x2b, top_gate, slots, tile_chunks = _cross_router(
        x1, kv, wq, wo_mem[0].astype(BF16), ln2_g[0], ln2_b[0], w_router[0], b_router[0], mem_of_tile)

    n_tiles = t // ROW_TILE
    blk_chunks = MOE_ROWS // SEG_ALIGN
    chunks = tile_chunks[:, :, 0]
    used = jnp.sum(chunks, axis=0)
    region = ((used + blk_chunks - 1) // blk_chunks) * blk_chunks
    region_end = jnp.cumsum(region)
    region_start = region_end - region
    seg_start = region_start[None, :] + jnp.cumsum(chunks, axis=0) - chunks
    loc_start = jnp.cumsum(chunks, axis=1) - chunks
    n_blk = -(-(t * TOP_K + n_tiles * N_EXPERTS * (SEG_ALIGN - 1) + N_EXPERTS * (MOE_ROWS - 1)) // MOE_ROWS)
    blk_first = jnp.arange(n_blk, dtype=jnp.int32) * blk_chunks
    block_e = jnp.minimum(jnp.sum(blk_first[:, None] >= region_end[None, :], axis=1), N_EXPERTS - 1).astype(jnp.int32)
    n_active = (region_end[-1:] // blk_chunks).astype(jnp.int32)
    tables = (chunks.reshape(-1), seg_start.reshape(-1), loc_start.reshape(-1), jnp.sum(chunks, axis=1),
              region_start + used, region - used, n_active)
    tables = tuple(tb.astype(jnp.int32) for tb in tables)

    rows_in = _dispatch(x2b, slots, tables, n_blk * MOE_ROWS)
    rows_out = _experts(rows_in, block_e, n_active, w_up[0].astype(BF16), b_up[0], w_down[0].astype(BF16), b_down[0])

    slots_c, gates_c = slots.T, top_gate.T
    y_p = _combine(x2, rows_out, slots_c, gates_c, tables, ln3_g[0], ln3_b[0], 0, n_p_tiles)
    y_s = _combine(x2, rows_out, slots_c, gates_c, tables, ln3_g[0], ln3_b[0], n_p_tiles, n_tiles - n_p_tiles)
    return y_p.reshape(bp, sp, D_MODEL), y_s.reshape(bs, ss, D_MODEL)
```

```python
import functools

import numpy as np
import jax
import jax.numpy as jnp
from jax import lax
from jax.experimental import pallas as pl
from jax.experimental.pallas import tpu as pltpu

F32 = jnp.float32
BF16 = jnp.bfloat16

D_MODEL = 1024
ATT_HEADS = 8
ATT_KV_HEADS = 2
ATT_GROUP = ATT_HEADS // ATT_KV_HEADS
ATT_HEAD_DIM = 64
ATT_WIDTH = ATT_HEADS * ATT_HEAD_DIM
WINDOW = 128
BLOCK = WINDOW
N_BUCKETS = 32
MAX_DISTANCE = 128
ML_HEADS = 4
ML_HEAD_DIM = 128
ML_WIDTH = ML_HEADS * ML_HEAD_DIM
ML_CHUNK = 128
CONV_WIDTH = 5
N_GATES = 4 * ML_HEADS
MEM_TOKENS = 256
MEM_HEADS = 4
MEM_HEAD_DIM = D_MODEL // MEM_HEADS
N_EXPERTS = 32
TOP_K = 4
D_FF = D_MODEL
SWIGLU_LIMIT = 7.0
SWIGLU_ALPHA = 1.702
LN_EPS = 1e-5
DN_ALPHA = 2.0 ** 0.25

Z_WIDTH = 2 * ML_WIDTH + ML_WIDTH + ML_WIDTH + ATT_WIDTH + 2 * ATT_KV_HEADS * ATT_HEAD_DIM
ZB_MLV = 2
ZB_MLO = 3
ZB_ATTQ = 4
ZB_ATTK = 20
ZB_ATTV = 21

ROW_TILE = 512
MOE_ROWS = 512
NEG = -1e30
VMEM_LIMIT = 56 * 1024 * 1024


def _cparams(sem):
    return pltpu.CompilerParams(dimension_semantics=sem, vmem_limit_bytes=VMEM_LIMIT)


def _dot(a, b):
    return jnp.dot(a, b, preferred_element_type=F32)


def _dot_nt(a, b):
    return lax.dot_general(a, b, (((1,), (1,)), ((), ())), preferred_element_type=F32)


def _dot_tn(a, b):
    return lax.dot_general(a, b, (((0,), (0,)), ((), ())), preferred_element_type=F32)


def _dot_exact(a, b):
    return jnp.dot(a, b, preferred_element_type=F32, precision=lax.Precision.HIGHEST)


def _layer_norm(y, g, b):
    mu = jnp.mean(y, axis=-1, keepdims=True)
    yc = y - mu
    var = jnp.mean(yc * yc, axis=-1, keepdims=True)
    return yc * lax.rsqrt(var + LN_EPS) * g + b


def _log_sigmoid(x):
    return jnp.minimum(x, 0.0) - jnp.log1p(jnp.exp(-jnp.abs(x)))


def _sigmoid(x):
    return 1.0 / (1.0 + jnp.exp(-x))


def _seq_pos(blk, n_p_blocks, p_blocks_per_seq, s_blocks_per_seq):
    in_p = blk < n_p_blocks
    local = jnp.where(in_p, blk % p_blocks_per_seq, (blk - n_p_blocks) % s_blocks_per_seq)
    per = jnp.where(in_p, p_blocks_per_seq, s_blocks_per_seq)
    return local == 0, local == per - 1


def _in_proj_kernel(n_p_tiles, xp_ref, xs_ref, w_ref, wg_ref, wgt_ref, bg_ref, bgt_ref, z_ref, gc_ref, gr_ref):
    i = pl.program_id(0)
    x = jnp.where(i < n_p_tiles, xp_ref[...], xs_ref[...]).astype(BF16)
    z_ref[...] = _dot(x, w_ref[...]).astype(BF16)
    gc_ref[...] = _dot(x, wg_ref[...]) + bg_ref[...]
    gr_ref[...] = _dot_nt(wgt_ref[...], x) + bgt_ref[...]


def _in_proj(xp, xs, w_main, w_g, b_g):
    tp, ts = xp.shape[0], xs.shape[0]
    t = tp + ts
    n_p = tp // ROW_TILE
    n = t // ROW_TILE
    const = lambda i: (0, 0)
    return pl.pallas_call(
        functools.partial(_in_proj_kernel, n_p),
        grid=(n,),
        in_specs=[
            pl.BlockSpec((ROW_TILE, D_MODEL), lambda i: (jnp.minimum(i, n_p - 1), 0)),
            pl.BlockSpec((ROW_TILE, D_MODEL), lambda i: (jnp.maximum(i - n_p, 0), 0)),
            pl.BlockSpec((D_MODEL, Z_WIDTH), const),
            pl.BlockSpec((D_MODEL, N_GATES), const),
            pl.BlockSpec((N_GATES, D_MODEL), const),
            pl.BlockSpec((1, N_GATES), const),
            pl.BlockSpec((N_GATES, 1), const),
        ],
        out_specs=[
            pl.BlockSpec((ROW_TILE, Z_WIDTH), lambda i: (i, 0)),
            pl.BlockSpec((ROW_TILE, N_GATES), lambda i: (i, 0)),
            pl.BlockSpec((N_GATES, ROW_TILE), lambda i: (0, i)),
        ],
        out_shape=[
            jax.ShapeDtypeStruct((t, Z_WIDTH), BF16),
            jax.ShapeDtypeStruct((t, N_GATES), F32),
            jax.ShapeDtypeStruct((N_GATES, t), F32),
        ],
        compiler_params=_cparams(("arbitrary",)),
        name="in_proj",
    )(xp, xs, w_main, w_g, w_g.T, b_g.reshape(1, N_GATES), b_g.reshape(N_GATES, 1))


ATT_TILE = 512
ATT_SUB = ATT_TILE // BLOCK


ONES_ROWS = 16


def _attention_kernel(seq, q_ref, kp_ref, kc_ref, kn_ref, vp_ref, vc_ref, vn_ref, bias_ref, sink_ref,
                      o_ref, klo_ref, khi_ref, vt_ref):
    i = pl.program_id(0)
    lane = lax.broadcasted_iota(jnp.int32, (ATT_TILE + 2 * BLOCK, 2 * ATT_HEAD_DIM), 1)
    kband = jnp.concatenate([kp_ref[...], kc_ref[...], kn_ref[...]], axis=0)
    zero = jnp.zeros_like(kband)
    klo_ref[...] = jnp.where(lane < ATT_HEAD_DIM, kband, zero)
    khi_ref[...] = jnp.where(lane < ATT_HEAD_DIM, zero, kband)
    vband = jnp.concatenate([vp_ref[...], vc_ref[...], vn_ref[...]], axis=0)
    vt_ref[0:2 * ATT_HEAD_DIM, :] = vband.T
    vt_ref[2 * ATT_HEAD_DIM:, :] = jnp.ones((ONES_ROWS, ATT_TILE + 2 * BLOCK), BF16)
    feat = lax.broadcasted_iota(jnp.int32, (2 * ATT_HEAD_DIM, ATT_GROUP * BLOCK), 0)
    for s in range(ATT_SUB):
        first, last = _seq_pos(i * ATT_SUB + s, *seq)
        variant = jnp.where(first, 1, jnp.where(last, 2, 0))
        q = q_ref[s * BLOCK:(s + 1) * BLOCK, :]
        q_all = jnp.concatenate([q[:, t * 128:(t + 1) * 128] for t in range(ATT_GROUP)], axis=0)
        vt = vt_ref[:, s * BLOCK:(s + 3) * BLOCK]
        outs = []
        for kv, k_ref in enumerate((klo_ref, khi_ref)):
            logits = _dot_nt(k_ref[s * BLOCK:(s + 3) * BLOCK, :], q_all) + bias_ref[variant, kv]
            sink = sink_ref[kv]
            m = jnp.maximum(jnp.max(logits, axis=0, keepdims=True), sink)
            p = jnp.exp(logits - m).astype(BF16)
            ov = _dot(vt, p)
            den = ov[2 * ATT_HEAD_DIM:2 * ATT_HEAD_DIM + 1, :] + jnp.exp(sink - m)
            outs.append(ov[0:2 * ATT_HEAD_DIM, :] * (1.0 / den))
        both = jnp.where(feat < ATT_HEAD_DIM, outs[0], outs[1]).astype(BF16)
        for t in range(ATT_GROUP):
            o_ref[t * 128:(t + 1) * 128, s * BLOCK:(s + 1) * BLOCK] = both[:, t * BLOCK:(t + 1) * BLOCK]


def _attention(z, bias, sink, seq):
    t = z.shape[0]
    n = t // ATT_TILE
    nblk = t // BLOCK
    band = ATT_TILE + 2 * BLOCK
    prev = lambda i: jnp.maximum(i * ATT_SUB - 1, 0)
    nxt = lambda i: jnp.minimum((i + 1) * ATT_SUB, nblk - 1)
    return pl.pallas_call(
        functools.partial(_attention_kernel, seq),
        grid=(n,),
        in_specs=[
            pl.BlockSpec((ATT_TILE, ATT_WIDTH), lambda i: (i, ZB_ATTQ)),
            pl.BlockSpec((BLOCK, 128), lambda i: (prev(i), ZB_ATTK)),
            pl.BlockSpec((ATT_TILE, 128), lambda i: (i, ZB_ATTK)),
            pl.BlockSpec((BLOCK, 128), lambda i: (nxt(i), ZB_ATTK)),
            pl.BlockSpec((BLOCK, 128), lambda i: (prev(i), ZB_ATTV)),
            pl.BlockSpec((ATT_TILE, 128), lambda i: (i, ZB_ATTV)),
            pl.BlockSpec((BLOCK, 128), lambda i: (nxt(i), ZB_ATTV)),
            pl.BlockSpec((3, ATT_KV_HEADS, 3 * BLOCK, ATT_GROUP * BLOCK), lambda i: (0, 0, 0, 0)),
            pl.BlockSpec((ATT_KV_HEADS, 1, ATT_GROUP * BLOCK), lambda i: (0, 0, 0)),
        ],
        out_specs=pl.BlockSpec((ATT_WIDTH, ATT_TILE), lambda i: (0, i)),
        out_shape=jax.ShapeDtypeStruct((ATT_WIDTH, t), BF16),
        scratch_shapes=[pltpu.VMEM((band, 128), BF16), pltpu.VMEM((band, 128), BF16),
                        pltpu.VMEM((2 * ATT_HEAD_DIM + ONES_ROWS, band), BF16)],
        compiler_params=_cparams(("arbitrary",)),
        name="window_attention",
    )(z, z, z, z, z, z, z, bias, sink)


def _t5_bucket(rel):
    half = N_BUCKETS // 2
    exact = half // 2
    n = np.abs(rel)
    large = exact + (np.log(np.maximum(n, 1) / exact) / np.log(MAX_DISTANCE / exact) * (half - exact)).astype(np.int32)
    large = np.minimum(large, half - 1)
    return ((rel > 0).astype(np.int32) * half + np.where(n < exact, n, large)).astype(np.int32)


def _attention_tables(rel_bias, attn_sink):
    rel = np.arange(3 * BLOCK)[:, None] - BLOCK - np.arange(BLOCK)[None, :]
    onehot = jnp.asarray(_t5_bucket(rel)[..., None] == np.arange(N_BUCKETS), F32)
    bias = jnp.einsum('kqb,bh->hkq', onehot, rel_bias.astype(F32), precision=lax.Precision.HIGHEST)
    bias = jnp.where(jnp.asarray(np.abs(rel) <= WINDOW)[None], bias, NEG)
    bias = bias.reshape(ATT_KV_HEADS, ATT_GROUP, 3 * BLOCK, BLOCK).transpose(0, 2, 1, 3)
    bias = bias.reshape(ATT_KV_HEADS, 3 * BLOCK, ATT_GROUP * BLOCK)
    key = np.arange(3 * BLOCK)[None, :, None]
    first = jnp.where(jnp.asarray(key < BLOCK), NEG, bias)
    last = jnp.where(jnp.asarray(key >= 2 * BLOCK), NEG, bias)
    sink = jnp.repeat(attn_sink.astype(F32), BLOCK).reshape(ATT_KV_HEADS, 1, ATT_GROUP * BLOCK)
    return jnp.stack([bias, first, last]), sink


CONV_TILE = 512
CONV_HALO = 16
QK_WIDTH = 2 * ML_WIDTH


def _conv_kernel(seq, xp_ref, xc_ref, xn_ref, w_ref, b_ref, scale_ref, o_ref, buf_ref):
    i = pl.program_id(0)
    first, last = _seq_pos(i, *seq)
    buf_ref[0:8, :] = jnp.where(first, 0.0, xp_ref[8:16, :].astype(F32))
    buf_ref[8:8 + CONV_TILE, :] = xc_ref[...].astype(F32)
    buf_ref[8 + CONV_TILE:16 + CONV_TILE, :] = jnp.where(last, 0.0, xn_ref[0:8, :].astype(F32))
    acc = jnp.zeros((CONV_TILE, QK_WIDTH), F32) + b_ref[...]
    for j in range(CONV_WIDTH):
        off = 8 + j - CONV_WIDTH // 2
        acc = acc + buf_ref[off:off + CONV_TILE, :] * w_ref[j:j + 1, :]
    o_ref[...] = (acc * _sigmoid(acc) * scale_ref[...]).astype(BF16)


def _conv_silu(z, conv_w, conv_b, seq_tiles):
    t = z.shape[0]
    n = t // CONV_TILE
    r = CONV_TILE // CONV_HALO
    nh = t // CONV_HALO
    scale = jnp.concatenate([jnp.ones((1, ML_WIDTH), F32), jnp.full((1, ML_WIDTH), ML_HEAD_DIM ** -0.5, F32)], axis=1)
    return pl.pallas_call(
        functools.partial(_conv_kernel, seq_tiles),
        grid=(n,),
        in_specs=[
            pl.BlockSpec((CONV_HALO, QK_WIDTH), lambda i: (jnp.maximum(i * r - 1, 0), 0)),
            pl.BlockSpec((CONV_TILE, QK_WIDTH), lambda i: (i, 0)),
            pl.BlockSpec((CONV_HALO, QK_WIDTH), lambda i: (jnp.minimum((i + 1) * r, nh - 1), 0)),
            pl.BlockSpec((CONV_WIDTH, QK_WIDTH), lambda i: (0, 0)),
            pl.BlockSpec((1, QK_WIDTH), lambda i: (0, 0)),
            pl.BlockSpec((1, QK_WIDTH), lambda i: (0, 0)),
        ],
        out_specs=pl.BlockSpec((CONV_TILE, QK_WIDTH), lambda i: (i, 0)),
        out_shape=jax.ShapeDtypeStruct((t, QK_WIDTH), BF16),
        scratch_shapes=[pltpu.VMEM((CONV_TILE + 16, QK_WIDTH), F32)],
        compiler_params=_cparams(("arbitrary",)),
        name="conv_silu",
    )(z, z, z, conv_w, conv_b.reshape(1, QK_WIDTH), scale)


def _mlstm_kernel(seq, n_chunks, *refs):
    fwd_in, bwd_in, (of_ref, ob_ref), state = refs[0:5], refs[5:10], refs[10:12], refs[12:]
    _mlstm_scan(seq, False, n_chunks, *fwd_in, of_ref, *state[0:2])
    _mlstm_scan(seq, True, n_chunks, *bwd_in, ob_ref, *state[2:4])


def _mlstm_scan(seq, reverse, n_chunks, q_ref, k_ref, v_ref, gr_ref, gc_ref, o_ref, ct_ref, m_ref):
    step = pl.program_id(0)
    chunk = (n_chunks - 1 - step) if reverse else step
    first, last = _seq_pos(chunk, *seq)
    fresh = last if reverse else first

    L = ML_CHUNK
    row = lax.broadcasted_iota(jnp.int32, (L, L), 0)
    col = lax.broadcasted_iota(jnp.int32, (L, L), 1)
    vis = (row >= col) if reverse else (row <= col)
    vis_t = (col >= row) if reverse else (col <= row)
    gr = gr_ref[...]
    gc = gc_ref[...]
    b_rows = _dot_exact(_log_sigmoid(gr), vis.astype(F32))
    b_cols = _dot_exact(vis_t.astype(F32), _log_sigmoid(gc))
    i_off = 2 * ML_HEADS if reverse else 0
    f_off = i_off + ML_HEADS
    end = 0 if reverse else L - 1
    ones = jnp.ones((ONES_ROWS, L), BF16)
    for h in range(ML_HEADS):
        hs = slice(h * ML_HEAD_DIM, (h + 1) * ML_HEAD_DIM)
        q = q_ref[:, hs]
        k = k_ref[:, hs]
        vt1 = jnp.concatenate([v_ref[:, hs].T, ones], axis=0)
        b_row = b_rows[f_off + h:f_off + h + 1, :]
        u_row = gr[i_off + h:i_off + h + 1, :] - b_row
        u_col = gc[:, i_off + h:i_off + h + 1] - b_cols[:, f_off + h:f_off + h + 1]
        g = b_row[:, end:end + 1]
        ct_old = jnp.where(fresh, 0.0, ct_ref[h])
        m_old = jnp.where(fresh, 0.0, m_ref[h])[:, 0:1]
        u_mat = jnp.where(vis, u_col, NEG)
        mm = jnp.maximum(jnp.max(u_mat, axis=0, keepdims=True), m_old)
        st = (_dot_nt(k, q) * jnp.exp(u_mat - mm)).astype(BF16)
        w_inter = jnp.exp(m_old - mm)
        tot = _dot(vt1, st) + w_inter * _dot_nt(ct_old.astype(BF16), q)
        den = tot[ML_HEAD_DIM:ML_HEAD_DIM + 1, :]
        floor = jnp.exp(-(b_row + mm))
        o_ref[hs, :] = tot[0:ML_HEAD_DIM, :] * (1.0 / jnp.maximum(jnp.abs(den), floor))
        a_max = jnp.max(g + u_row, axis=-1, keepdims=True)
        m_new = jnp.maximum(g + m_old, a_max)
        s_old = jnp.exp(g + m_old - m_new)
        weighted = (vt1.astype(F32) * jnp.exp(g + u_row - m_new)).astype(BF16)
        ct_ref[h] = s_old * ct_old + _dot(weighted, k)
        m_ref[h] = jnp.broadcast_to(m_new, (1, ML_HEAD_DIM))


def _mlstm(qk, z, gates_r, gates_c, seq):
    t = qk.shape[0]
    nc = t // ML_CHUNK

    def chunk_specs(ch):
        return [
            pl.BlockSpec((ML_CHUNK, ML_WIDTH), lambda i: (ch(i), 0)),
            pl.BlockSpec((ML_CHUNK, ML_WIDTH), lambda i: (ch(i), 1)),
            pl.BlockSpec((ML_CHUNK, ML_WIDTH), lambda i: (ch(i), ZB_MLV)),
            pl.BlockSpec((N_GATES, ML_CHUNK), lambda i: (0, ch(i))),
            pl.BlockSpec((ML_CHUNK, N_GATES), lambda i: (ch(i), 0)),
        ]

    fwd = lambda i: i
    bwd = lambda i: nc - 1 - i
    state = [
        pltpu.VMEM((ML_HEADS, ML_HEAD_DIM + ONES_ROWS, ML_HEAD_DIM), F32),
        pltpu.VMEM((ML_HEADS, 1, ML_HEAD_DIM), F32),
    ]
    operands = (qk, qk, z, gates_r, gates_c)
    return pl.pallas_call(
        functools.partial(_mlstm_kernel, seq, nc),
        grid=(nc,),
        in_specs=chunk_specs(fwd) + chunk_specs(bwd),
        out_specs=[pl.BlockSpec((ML_WIDTH, ML_CHUNK), lambda i: (0, fwd(i))),
                   pl.BlockSpec((ML_WIDTH, ML_CHUNK), lambda i: (0, bwd(i)))],
        out_shape=[jax.ShapeDtypeStruct((ML_WIDTH, t), F32), jax.ShapeDtypeStruct((ML_WIDTH, t), F32)],
        scratch_shapes=state + state,
        compiler_params=_cparams(("arbitrary",)),
        name="mlstm",
    )(*operands, *operands)


def _mix_out_kernel(n_p_tiles, xp_ref, xs_ref, att_ref, hf_ref, hb_ref, og_ref, wa_ref, wm_ref, gain_ref,
                    g_ref, b_ref, o_ref):
    i = pl.program_id(0)
    x = jnp.where(i < n_p_tiles, xp_ref[...], xs_ref[...])
    h = hf_ref[...] + hb_ref[...]
    parts = []
    for hd in range(ML_HEADS):
        hh = h[hd * ML_HEAD_DIM:(hd + 1) * ML_HEAD_DIM, :]
        mu = jnp.mean(hh, axis=0, keepdims=True)
        hc = hh - mu
        var = jnp.mean(hc * hc, axis=0, keepdims=True)
        parts.append(hc * lax.rsqrt(var + LN_EPS))
    hn = jnp.concatenate(parts, axis=0) * gain_ref[...] * _sigmoid(og_ref[...].T.astype(F32))
    mixed = _dot(att_ref[...].T, wa_ref[...]) + _dot(hn.astype(BF16).T, wm_ref[...])
    o_ref[...] = _layer_norm(DN_ALPHA * x + mixed, g_ref[...], b_ref[...])


def _mix_out(xp, xs, att, hf, hb, z, w_att, w_ml, gain, g, b):
    t = att.shape[1]
    n_p = xp.shape[0] // ROW_TILE
    n = t // ROW_TILE
    const = lambda i: (0, 0)
    tile = lambda i: (i, 0)
    lanes = lambda i: (0, i)
    return pl.pallas_call(
        functools.partial(_mix_out_kernel, n_p),
        grid=(n,),
        in_specs=[
            pl.BlockSpec((ROW_TILE, D_MODEL), lambda i: (jnp.minimum(i, n_p - 1), 0)),
            pl.BlockSpec((ROW_TILE, D_MODEL), lambda i: (jnp.maximum(i - n_p, 0), 0)),
            pl.BlockSpec((ATT_WIDTH, ROW_TILE), lanes),
            pl.BlockSpec((ML_WIDTH, ROW_TILE), lanes),
            pl.BlockSpec((ML_WIDTH, ROW_TILE), lanes),
            pl.BlockSpec((ROW_TILE, ML_WIDTH), lambda i: (i, ZB_MLO)),
            pl.BlockSpec((ATT_WIDTH, D_MODEL), const),
            pl.BlockSpec((ML_WIDTH, D_MODEL), const),
            pl.BlockSpec((ML_WIDTH, 1), const),
            pl.BlockSpec((1, D_MODEL), const),
            pl.BlockSpec((1, D_MODEL), const),
        ],
        out_specs=pl.BlockSpec((ROW_TILE, D_MODEL), tile),
        out_shape=jax.ShapeDtypeStruct((t, D_MODEL), F32),
        compiler_params=_cparams(("arbitrary",)),
        name="mix_out_ln1",
    )(xp, xs, att, hf, hb, z, w_att, w_ml, gain.reshape(ML_WIDTH, 1), g.reshape(1, D_MODEL), b.reshape(1, D_MODEL))


def _mem_kv_kernel(m_ref, w_ref, o_ref):
    o_ref[...] = _dot(m_ref[...].astype(BF16), w_ref[...]).astype(BF16)


def _mem_kv(mem, wkv):
    rows = mem.shape[0]
    return pl.pallas_call(
        _mem_kv_kernel,
        grid=(rows // MEM_TOKENS,),
        in_specs=[pl.BlockSpec((MEM_TOKENS, D_MODEL), lambda i: (i, 0)),
                  pl.BlockSpec((D_MODEL, 2 * D_MODEL), lambda i: (0, 0))],
        out_specs=pl.BlockSpec((MEM_TOKENS, 2 * D_MODEL), lambda i: (i, 0)),
        out_shape=jax.ShapeDtypeStruct((rows, 2 * D_MODEL), BF16),
        compiler_params=_cparams(("arbitrary",)),
        name="mem_kv",
    )(mem, wkv)


def _pack_bf16_pairs(a, b):
    ua = pltpu.bitcast(a.astype(BF16).astype(F32), jnp.uint32)
    ub = pltpu.bitcast(b.astype(BF16).astype(F32), jnp.uint32)
    return (ua >> 16) | (ub & jnp.uint32(0xFFFF0000))


def _unpack_bf16_pairs(u):
    lo = pltpu.bitcast(u << 16, F32)
    hi = pltpu.bitcast(u & jnp.uint32(0xFFFF0000), F32)
    return lo, hi


def _cross_router_kernel(x_ref, kv_ref, wq_ref, wo_ref, g_ref, b_ref, wr_ref, br_ref, tri_ref, etri_ref,
                         x2_ref, x2b_ref, gate_ref, slot_ref, chunk_ref):
    x = x_ref[...]
    q = _dot(x.astype(BF16), wq_ref[...]).astype(BF16)
    heads = []
    for h in range(MEM_HEADS):
        hs = slice(h * MEM_HEAD_DIM, (h + 1) * MEM_HEAD_DIM)
        logits = _dot_nt(q[:, hs], kv_ref[:, hs])
        m = jnp.max(logits, axis=-1, keepdims=True)
        p = jnp.exp(logits - m)
        den = jnp.sum(p, axis=-1, keepdims=True)
        vs = slice(D_MODEL + h * MEM_HEAD_DIM, D_MODEL + (h + 1) * MEM_HEAD_DIM)
        heads.append((_dot(p.astype(BF16), kv_ref[:, vs]) * (1.0 / den)).astype(BF16))
    o = jnp.concatenate(heads, axis=1)
    x2 = _layer_norm(DN_ALPHA * x + _dot(o, wo_ref[...]), g_ref[...], b_ref[...])
    x2_ref[...] = x2
    x2b = x2.astype(BF16)
    x2b_ref[...] = x2b

    logits = _dot_nt(wr_ref[...], x2b) + br_ref[...]
    expert = lax.broadcasted_iota(jnp.int32, logits.shape, 0)
    work = logits
    vals, sels = [], []
    for k in range(TOP_K):
        mx = jnp.max(work, axis=0, keepdims=True)
        ix = jnp.min(jnp.where(work == mx, expert, N_EXPERTS), axis=0, keepdims=True)
        sel = expert == ix
        work = jnp.where(sel, -jnp.inf, work)
        vals.append(mx)
        sels.append(sel)
    es = [jnp.exp(v - vals[0]) for v in vals]
    tot = es[0] + es[1] + es[2] + es[3]
    chosen = jnp.zeros(logits.shape, F32)
    for k in range(TOP_K):
        gate_ref[k:k + 1, :] = es[k] / tot
        chosen = chosen + sels[k].astype(F32)
    count = jnp.sum(chosen, axis=1, keepdims=True)
    chunks = jnp.floor((count + (SEG_ALIGN - 1)) * (1.0 / SEG_ALIGN))
    chunks_b = jnp.broadcast_to(chunks, (N_EXPERTS, 128))
    seg_first = _dot(etri_ref[...], chunks_b.astype(BF16))[:, 0:1] * SEG_ALIGN
    before = _dot(chosen.astype(BF16), tri_ref[...])
    local_row = before + seg_first
    for k in range(TOP_K):
        slot_ref[k:k + 1, :] = jnp.sum(jnp.where(sels[k], local_row, 0.0), axis=0, keepdims=True).astype(jnp.int32)
    chunk_ref[0] = chunks_b.astype(jnp.int32)


def _cross_router(x1, kv, wq, wo, g, b, w_router, b_router, mem_of_tile):
    t = x1.shape[0]
    n = t // ROW_TILE
    const = lambda i: (0, 0)
    tile = lambda i: (i, 0)
    lanes = lambda i: (0, i)
    tri = jnp.asarray(np.triu(np.ones((ROW_TILE, ROW_TILE), np.float32), 1), BF16)
    etri = jnp.asarray(np.tril(np.ones((N_EXPERTS, N_EXPERTS), np.float32), -1), BF16)
    return pl.pallas_call(
        _cross_router_kernel,
        grid=(n,),
        in_specs=[
            pl.BlockSpec((ROW_TILE, D_MODEL), tile),
            pl.BlockSpec((MEM_TOKENS, 2 * D_MODEL), lambda i: (mem_of_tile(i), 0)),
            pl.BlockSpec((D_MODEL, D_MODEL), const),
            pl.BlockSpec((D_MODEL, D_MODEL), const),
            pl.BlockSpec((1, D_MODEL), const),
            pl.BlockSpec((1, D_MODEL), const),
            pl.BlockSpec((N_EXPERTS, D_MODEL), const),
            pl.BlockSpec((N_EXPERTS, 1), const),
            pl.BlockSpec((ROW_TILE, ROW_TILE), const),
            pl.BlockSpec((N_EXPERTS, N_EXPERTS), const),
        ],
        out_specs=[
            pl.BlockSpec((ROW_TILE, D_MODEL), tile),
            pl.BlockSpec((ROW_TILE, D_MODEL), tile),
            pl.BlockSpec((TOP_K, ROW_TILE), lanes),
            pl.BlockSpec((TOP_K, ROW_TILE), lanes),
            pl.BlockSpec((1, N_EXPERTS, 128), lambda i: (i, 0, 0)),
        ],
        out_shape=[
            jax.ShapeDtypeStruct((t, D_MODEL), F32),
            jax.ShapeDtypeStruct((t, D_MODEL), BF16),
            jax.ShapeDtypeStruct((TOP_K, t), F32),
            jax.ShapeDtypeStruct((TOP_K, t), jnp.int32),
            jax.ShapeDtypeStruct((n, N_EXPERTS, 128), jnp.int32),
        ],
        compiler_params=_cparams(("arbitrary",)),
        name="cross_attn_ln2_router",
    )(x1, kv, wq, wo, g.reshape(1, D_MODEL), b.reshape(1, D_MODEL), w_router.T.astype(BF16),
      b_router.reshape(N_EXPERTS, 1), tri, etri)


SEG_ALIGN = 8
LOCAL_ROWS = ROW_TILE * TOP_K + N_EXPERTS * SEG_ALIGN
HALF = D_MODEL // 2


def _rows(chunks):
    return pl.multiple_of(chunks * SEG_ALIGN, SEG_ALIGN)


def _segment_copies(chunks_ref, seg_ref, loc_ref, tile, make_copy):
    def per_expert(e, carry):
        entry = tile * N_EXPERTS + e

        @pl.when(chunks_ref[entry] > 0)
        def _():
            make_copy(_rows(loc_ref[entry]), _rows(seg_ref[entry]), _rows(chunks_ref[entry])).start()

        return carry

    lax.fori_loop(0, N_EXPERTS, per_expert, 0)


def _wait_copies(chunks, make_copy):
    @pl.when(chunks > 0)
    def _():
        make_copy(0, 0, _rows(chunks)).wait()


def _dispatch_kernel(chunks_ref, seg_ref, loc_ref, total_ref, tail_ref, tailn_ref, nact_ref,
                     x_ref, slot_ref, rows_hbm, local_ref, zero_ref, sems):
    i = pl.program_id(0)
    n = pl.num_programs(0)
    buf = i % 2

    def copy_out(b):
        return lambda loc, seg, rows: pltpu.make_async_copy(
            local_ref.at[b, pl.ds(loc, rows), :], rows_hbm.at[pl.ds(seg, rows), :], sems.at[b])

    def run(b):
        @pl.when(i >= 2)
        def _():
            _wait_copies(total_ref[jnp.maximum(i - 2, 0)], copy_out(b))

        slots = slot_ref[...]
        row = lax.broadcasted_iota(jnp.int32, (LOCAL_ROWS, ROW_TILE), 0)
        hit = slots[0:1, :] == row
        for k in range(1, TOP_K):
            hit = hit | (slots[k:k + 1, :] == row)
        perm = jnp.where(hit, 1.0, 0.0).astype(BF16)
        rows = _dot(perm, x_ref[...])
        local_ref[b] = _pack_bf16_pairs(rows[:, :HALF], rows[:, HALF:])
        _segment_copies(chunks_ref, seg_ref, loc_ref, i, copy_out(b))

    for b in range(2):
        pl.when(buf == b)(functools.partial(run, b))

    @pl.when(i == n - 1)
    def _():
        for b in range(2):
            step = jnp.where(buf == b, i, i - 1)
            _wait_copies(total_ref[step], copy_out(b))
        zero_ref[...] = jnp.zeros_like(zero_ref)
        fill = lambda _, seg, rows: pltpu.make_async_copy(
            zero_ref.at[pl.ds(0, rows), :], rows_hbm.at[pl.ds(seg, rows), :], sems.at[2])

        def per_expert(e, carry):
            @pl.when(tailn_ref[e] > 0)
            def _():
                fill(0, _rows(tail_ref[e]), _rows(tailn_ref[e])).start()

            _wait_copies(tailn_ref[e], fill)
            return carry

        lax.fori_loop(0, N_EXPERTS, per_expert, 0)
        fill_block = lambda blk: pltpu.make_async_copy(
            zero_ref, rows_hbm.at[pl.ds(pl.multiple_of(blk * MOE_ROWS, MOE_ROWS), MOE_ROWS), :], sems.at[2])
        n_blocks = rows_hbm.shape[0] // MOE_ROWS

        def start_block(blk, carry):
            fill_block(blk).start()
            return carry

        def wait_block(blk, carry):
            fill_block(0).wait()
            return carry

        lax.fori_loop(nact_ref[0], n_blocks, start_block, 0)
        lax.fori_loop(nact_ref[0], n_blocks, wait_block, 0)


def _dispatch(x2b, slots, tables, n_rows):
    n = x2b.shape[0] // ROW_TILE
    assert n >= 2
    return pl.pallas_call(
        _dispatch_kernel,
        grid_spec=pltpu.PrefetchScalarGridSpec(
            num_scalar_prefetch=7,
            grid=(n,),
            in_specs=[
                pl.BlockSpec((ROW_TILE, D_MODEL), lambda i, *_: (i, 0)),
                pl.BlockSpec((TOP_K, ROW_TILE), lambda i, *_: (0, i)),
            ],
            out_specs=pl.BlockSpec(memory_space=pl.ANY),
            scratch_shapes=[
                pltpu.VMEM((2, LOCAL_ROWS, HALF), jnp.uint32),
                pltpu.VMEM((MOE_ROWS, HALF), jnp.uint32),
                pltpu.SemaphoreType.DMA((3,)),
            ],
        ),
        out_shape=jax.ShapeDtypeStruct((n_rows, HALF), jnp.uint32),
        compiler_params=_cparams(("arbitrary",)),
        name="moe_dispatch",
    )(*tables, x2b, slots)


def _expert_kernel(be_ref, na_ref, x_ref, wu_ref, bu_ref, wd_ref, bd_ref, o_ref, wub_ref, wdb_ref):
    i = pl.program_id(0)

    @pl.when((i == 0) | (be_ref[i] != be_ref[jnp.maximum(i - 1, 0)]))
    def _():
        wub_ref[...] = wu_ref[0].astype(BF16)
        wdb_ref[...] = wd_ref[0].astype(BF16)

    @pl.when(i < na_ref[0])
    def _():
        lo, hi = _unpack_bf16_pairs(x_ref[...])
        x = jnp.concatenate([lo, hi], axis=1).astype(BF16)
        hu = _dot(x, wub_ref[...]) + bu_ref[0]
        h_glu = jnp.minimum(hu[:, :D_FF], SWIGLU_LIMIT)
        h_lin = jnp.clip(hu[:, D_FF:], -SWIGLU_LIMIT, SWIGLU_LIMIT)
        hh = h_glu * _sigmoid(SWIGLU_ALPHA * h_glu) * (h_lin + 1.0)
        y = _dot(hh.astype(BF16), wdb_ref[...]) + bd_ref[0]
        o_ref[...] = _pack_bf16_pairs(y[:, :D_MODEL // 2], y[:, D_MODEL // 2:])

    @pl.when(i >= na_ref[0])
    def _():
        o_ref[...] = jnp.zeros_like(o_ref)


def _experts(xs, block_e, n_active, w_up, b_up, w_down, b_down):
    n_rows = xs.shape[0]
    n_blk = n_rows // MOE_ROWS
    return pl.pallas_call(
        _expert_kernel,
        grid_spec=pltpu.PrefetchScalarGridSpec(
            num_scalar_prefetch=2,
            grid=(n_blk,),
            in_specs=[
                pl.BlockSpec((MOE_ROWS, D_MODEL // 2), lambda i, be, na: (jnp.minimum(i, na[0] - 1), 0)),
                pl.BlockSpec((1, D_MODEL, 2 * D_FF), lambda i, be, na: (be[i], 0, 0)),
                pl.BlockSpec((1, 1, 2 * D_FF), lambda i, be, na: (be[i], 0, 0)),
                pl.BlockSpec((1, D_FF, D_MODEL), lambda i, be, na: (be[i], 0, 0)),
                pl.BlockSpec((1, 1, D_MODEL), lambda i, be, na: (be[i], 0, 0)),
            ],
            out_specs=pl.BlockSpec((MOE_ROWS, D_MODEL // 2), lambda i, be, na: (i, 0)),
            scratch_shapes=[pltpu.VMEM((D_MODEL, 2 * D_FF), BF16), pltpu.VMEM((D_FF, D_MODEL), BF16)],
        ),
        out_shape=jax.ShapeDtypeStruct((n_rows, D_MODEL // 2), jnp.uint32),
        compiler_params=_cparams(("arbitrary",)),
        name="experts",
    )(block_e, n_active, xs, w_up, b_up.reshape(N_EXPERTS, 1, 2 * D_FF), w_down, b_down.reshape(N_EXPERTS, 1, D_MODEL))


def _combine_kernel(first_tile, chunks_ref, seg_ref, loc_ref, total_ref,
                    x_ref, rows_hbm, slot_ref, gate_ref, g_ref, b_ref, o_ref, local_ref, sems):
    i = pl.program_id(0)
    n = pl.num_programs(0)
    tile = first_tile + i
    buf = i % 2

    def copy_in(b):
        return lambda loc, seg, rows: pltpu.make_async_copy(
            rows_hbm.at[pl.ds(seg, rows), :], local_ref.at[b, pl.ds(loc, rows), :], sems.at[b])

    @pl.when(i == 0)
    def _():
        local_ref[...] = jnp.zeros_like(local_ref)
        _segment_copies(chunks_ref, seg_ref, loc_ref, tile, copy_in(0))

    def run(b):
        @pl.when(i + 1 < n)
        def _():
            _segment_copies(chunks_ref, seg_ref, loc_ref, tile + 1, copy_in(1 - b))

        _wait_copies(total_ref[tile], copy_in(b))
        lo, hi = _unpack_bf16_pairs(local_ref[b])
        rows = jnp.concatenate([lo, hi], axis=1).astype(BF16)
        slots = slot_ref[...]
        gates = gate_ref[...]
        row = lax.broadcasted_iota(jnp.int32, (ROW_TILE, LOCAL_ROWS), 1)
        weight = jnp.zeros((ROW_TILE, LOCAL_ROWS), F32)
        for k in range(TOP_K):
            weight = weight + jnp.where(slots[:, k:k + 1] == row, gates[:, k:k + 1], 0.0)
        y = _dot(weight.astype(BF16), rows)
        o_ref[...] = _layer_norm(DN_ALPHA * x_ref[...] + y, g_ref[...], b_ref[...])

    for b in range(2):
        pl.when(buf == b)(functools.partial(run, b))


def _combine(x2, rows_out, slots_c, gates_c, tables, g, b, first_tile, n_tiles):
    const = lambda i, *_: (0, 0)
    tile = lambda i, *_: (first_tile + i, 0)
    return pl.pallas_call(
        functools.partial(_combine_kernel, first_tile),
        grid_spec=pltpu.PrefetchScalarGridSpec(
            num_scalar_prefetch=4,
            grid=(n_tiles,),
            in_specs=[
                pl.BlockSpec((ROW_TILE, D_MODEL), tile),
                pl.BlockSpec(memory_space=pl.ANY),
                pl.BlockSpec((ROW_TILE, TOP_K), tile),
                pl.BlockSpec((ROW_TILE, TOP_K), tile),
                pl.BlockSpec((1, D_MODEL), const),
                pl.BlockSpec((1, D_MODEL), const),
            ],
            out_specs=pl.BlockSpec((ROW_TILE, D_MODEL), lambda i, *_: (i, 0)),
            scratch_shapes=[pltpu.VMEM((2, LOCAL_ROWS, HALF), jnp.uint32), pltpu.SemaphoreType.DMA((2,))],
        ),
        out_shape=jax.ShapeDtypeStruct((n_tiles * ROW_TILE, D_MODEL), F32),
        compiler_params=_cparams(("arbitrary",)),
        name="combine_ln3",
    )(*tables[:4], x2, rows_out, slots_c, gates_c, g.reshape(1, D_MODEL), b.reshape(1, D_MODEL))


def kernel(x_prompt, x_sample, mem_prompt, mem_sample, rel_bias, w_in, b_gates, conv_w, conv_b, mh_gain, attn_sink,
           w_out, ln1_g, ln1_b, wq_mem, wkv_mem, wo_mem, ln2_g, ln2_b, w_router, b_router, w_up, b_up, w_down,
           b_down, ln3_g, ln3_b):
    assert w_in.shape[0] == 1, "single layer"
    bp, sp, _ = x_prompt.shape
    bs, ss, _ = x_sample.shape
    tp, ts = bp * sp, bs * ss
    t = tp + ts
    assert sp % ROW_TILE == 0 and ss % ROW_TILE == 0 and sp >= 2 * BLOCK and ss >= 2 * BLOCK
    xp = x_prompt.reshape(tp, D_MODEL)
    xs = x_sample.reshape(ts, D_MODEL)

    def seq_blocks(rows):
        return (tp // rows, sp // rows, ss // rows)

    w = w_in[0]
    q_end, k_end, v_end = ATT_WIDTH, ATT_WIDTH + 128, ATT_WIDTH + 256
    qk_end, mv_end, mo_end = v_end + 2 * ML_WIDTH, v_end + 3 * ML_WIDTH, v_end + 4 * ML_WIDTH
    head_order = np.concatenate([[h, ATT_GROUP + h] for h in range(ATT_GROUP)])
    att_perm = (head_order[:, None] * ATT_HEAD_DIM + np.arange(ATT_HEAD_DIM)[None, :]).reshape(-1)
    w_main = jnp.concatenate([
        w[:, v_end:qk_end], w[:, qk_end:mv_end], w[:, mv_end:mo_end],
        w[:, :q_end][:, att_perm] * (ATT_HEAD_DIM ** -0.5), w[:, q_end:k_end], w[:, k_end:v_end]], axis=1).astype(BF16)
    w_g = w[:, mo_end:].astype(BF16)
    w_att = w_out[0][:ATT_WIDTH][att_perm].astype(BF16)
    w_ml = w_out[0][ATT_WIDTH:].astype(BF16)

    z, gates_c, gates_r = _in_proj(xp, xs, w_main, w_g, b_gates[0])

    bias, sink = _attention_tables(rel_bias, attn_sink[0])
    att = _attention(z, bias, sink, seq_blocks(BLOCK))

    qk = _conv_silu(z, conv_w[0], conv_b[0], seq_blocks(CONV_TILE))
    h_f, h_b = _mlstm(qk, z, gates_r, gates_c, seq_blocks(ML_CHUNK))

    x1 = _mix_out(xp, xs, att, h_f, h_b, z, w_att, w_ml, mh_gain[0], ln1_g[0], ln1_b[0])

    mem = jnp.concatenate([mem_prompt.reshape(bp * MEM_TOKENS, D_MODEL), mem_sample.reshape(bs * MEM_TOKENS, D_MODEL)])
    kv = _mem_kv(mem, wkv_mem[0].astype(BF16))
    n_p_tiles, p_tiles_per_seq, s_tiles_per_seq = seq_blocks(ROW_TILE)
    mem_of_tile = lambda i: jnp.where(i < n_p_tiles, i // p_tiles_per_seq, bp + (i - n_p_tiles) // s_tiles_per_seq)
    wq = (wq_mem[0] * (MEM_HEAD_DIM ** -0.5)).astype(BF16)
    x2, x2b, top_gate, slots, tile_chunks = _cross_router(
        x1, kv, wq, wo_mem[0].astype(BF16), ln2_g[0], ln2_b[0], w_router[0], b_router[0], mem_of_tile)

    n_tiles = t // ROW_TILE
    blk_chunks = MOE_ROWS // SEG_ALIGN
    chunks = tile_chunks[:, :, 0]
    used = jnp.sum(chunks, axis=0)
    region = ((used + blk_chunks - 1) // blk_chunks) * blk_chunks
    region_end = jnp.cumsum(region)
    region_start = region_end - region
    seg_start = region_start[None, :] + jnp.cumsum(chunks, axis=0) - chunks
    loc_start = jnp.cumsum(chunks, axis=1) - chunks
    n_blk = -(-(t * TOP_K + n_tiles * N_EXPERTS * (SEG_ALIGN - 1) + N_EXPERTS * (MOE_ROWS - 1)) // MOE_ROWS)
    blk_first = jnp.arange(n_blk, dtype=jnp.int32) * blk_chunks
    block_e = jnp.minimum(jnp.sum(blk_first[:, None] >= region_end[None, :], axis=1), N_EXPERTS - 1).astype(jnp.int32)
    n_active = (region_end[-1:] // blk_chunks).astype(jnp.int32)
    tables = (chunks.reshape(-1), seg_start.reshape(-1), loc_start.reshape(-1), jnp.sum(chunks, axis=1),
              region_start + used, region - used, n_active)
    tables = tuple(tb.astype(jnp.int32) for tb in tables)

    rows_in = _dispatch(x2b, slots, tables, n_blk * MOE_ROWS)
    rows_out = _experts(rows_in, block_e, n_active, w_up[0], b_up[0], w_down[0], b_down[0])

    slots_c, gates_c = slots.T, top_gate.T
    y_p = _combine(x2, rows_out, slots_c, gates_c, tables, ln3_g[0], ln3_b[0], 0, n_p_tiles)
    y_s = _combine(x2, rows_out, slots_c, gates_c, tables, ln3_g[0], ln3_b[0], n_p_tiles, n_tiles - n_p_tiles)
    return y_p.reshape(bp, sp, D_MODEL), y_s.reshape(bs, ss, D_MODEL)
```

```python
import functools

import numpy as np
import jax
import jax.numpy as jnp
from jax import lax
from jax.experimental import pallas as pl
from jax.experimental.pallas import tpu as pltpu

F32 = jnp.float32
BF16 = jnp.bfloat16

D_MODEL = 1024
ATT_HEADS = 8
ATT_KV_HEADS = 2
ATT_GROUP = ATT_HEADS // ATT_KV_HEADS
ATT_HEAD_DIM = 64
ATT_WIDTH = ATT_HEADS * ATT_HEAD_DIM
WINDOW = 128
BLOCK = WINDOW
N_BUCKETS = 32
MAX_DISTANCE = 128
ML_HEADS = 4
ML_HEAD_DIM = 128
ML_WIDTH = ML_HEADS * ML_HEAD_DIM
ML_CHUNK = 256
CONV_WIDTH = 5
N_GATES = 4 * ML_HEADS
MEM_TOKENS = 256
MEM_HEADS = 4
MEM_HEAD_DIM = D_MODEL // MEM_HEADS
N_EXPERTS = 32
TOP_K = 4
D_FF = D_MODEL
SWIGLU_LIMIT = 7.0
SWIGLU_ALPHA = 1.702
LN_EPS = 1e-5
DN_ALPHA = 2.0 ** 0.25

QK_WIDTH = 2 * ML_WIDTH
Z_WIDTH = ML_WIDTH + ML_WIDTH + ATT_WIDTH + 2 * ATT_KV_HEADS * ATT_HEAD_DIM
ZB_MLV = 0
ZB_MLO = 1
ZB_ATTQ = 2
ZB_ATTK = 12
ZB_ATTV = 13

ROW_TILE = 512
MOE_ROWS = 512
NEG = -1e30
VMEM_LIMIT = 56 * 1024 * 1024


def _cparams(sem):
    return pltpu.CompilerParams(dimension_semantics=sem, vmem_limit_bytes=VMEM_LIMIT)


def _dot(a, b):
    return jnp.dot(a, b, preferred_element_type=F32)


def _dot_nt(a, b):
    return lax.dot_general(a, b, (((1,), (1,)), ((), ())), preferred_element_type=F32)


def _dot_tn(a, b):
    return lax.dot_general(a, b, (((0,), (0,)), ((), ())), preferred_element_type=F32)


def _dot_exact(a, b):
    return jnp.dot(a, b, preferred_element_type=F32, precision=lax.Precision.HIGHEST)


def _layer_norm(y, g, b):
    mu = jnp.mean(y, axis=-1, keepdims=True)
    yc = y - mu
    var = jnp.mean(yc * yc, axis=-1, keepdims=True)
    return yc * lax.rsqrt(var + LN_EPS) * g + b


def _log_sigmoid(x):
    return jnp.minimum(x, 0.0) - jnp.log1p(jnp.exp(-jnp.abs(x)))


def _sigmoid(x):
    return 1.0 / (1.0 + jnp.exp(-x))


def _seq_pos(blk, n_p_blocks, p_blocks_per_seq, s_blocks_per_seq):
    in_p = blk < n_p_blocks
    local = jnp.where(in_p, blk % p_blocks_per_seq, (blk - n_p_blocks) % s_blocks_per_seq)
    per = jnp.where(in_p, p_blocks_per_seq, s_blocks_per_seq)
    return local == 0, local == per - 1


HALO = 8


def _in_proj_kernel(seq, xp_ref, xpl_ref, xpr_ref, xs_ref, xsl_ref, xsr_ref, wqk_ref, w_ref, wg_ref, wgt_ref,
                    bg_ref, bgt_ref, cw_ref, cb_ref, scale_ref, qk_ref, z_ref, gc_ref, gr_ref, buf_ref):
    i = pl.program_id(0)
    in_p = i < seq[0]
    first, last = _seq_pos(i, *seq)
    x = jnp.where(in_p, xp_ref[...], xs_ref[...])
    left = jnp.where(in_p, xpl_ref[...], xsl_ref[...])
    right = jnp.where(in_p, xpr_ref[...], xsr_ref[...])
    xb = x.astype(BF16)
    x_ext = jnp.concatenate([left, x, right], axis=0).astype(BF16)
    row = lax.broadcasted_iota(jnp.int32, (ROW_TILE + 2 * HALO, 1), 0)
    outside = (first & (row < HALO)) | (last & (row >= ROW_TILE + HALO))
    buf_ref[...] = jnp.where(outside, 0.0, _dot(x_ext, wqk_ref[...]))
    acc = jnp.zeros((ROW_TILE, QK_WIDTH), F32) + cb_ref[...]
    for j in range(CONV_WIDTH):
        off = HALO + j - CONV_WIDTH // 2
        acc = acc + buf_ref[off:off + ROW_TILE, :] * cw_ref[j:j + 1, :]
    qk_ref[...] = (acc * _sigmoid(acc) * scale_ref[...]).astype(BF16)
    z_ref[...] = _dot(xb, w_ref[...]).astype(BF16)
    gc_ref[...] = _dot(xb, wg_ref[...]) + bg_ref[...]
    gr_ref[...] = _dot_nt(wgt_ref[...], xb) + bgt_ref[...]


def _in_proj(xp, xs, w_qk, w_main, w_g, b_g, conv_w, conv_b, seq):
    tp, ts = xp.shape[0], xs.shape[0]
    t = tp + ts
    n_p = tp // ROW_TILE
    n = t // ROW_TILE
    r = ROW_TILE // HALO
    const = lambda i: (0, 0)
    p_tile = lambda i: jnp.minimum(i, n_p - 1)
    s_tile = lambda i: jnp.maximum(i - n_p, 0)
    scale = jnp.concatenate([jnp.ones((1, ML_WIDTH), F32), jnp.full((1, ML_WIDTH), ML_HEAD_DIM ** -0.5, F32)], axis=1)
    return pl.pallas_call(
        functools.partial(_in_proj_kernel, seq),
        grid=(n,),
        in_specs=[
            pl.BlockSpec((ROW_TILE, D_MODEL), lambda i: (p_tile(i), 0)),
            pl.BlockSpec((HALO, D_MODEL), lambda i: (jnp.maximum(p_tile(i) * r - 1, 0), 0)),
            pl.BlockSpec((HALO, D_MODEL), lambda i: (jnp.minimum((p_tile(i) + 1) * r, tp // HALO - 1), 0)),
            pl.BlockSpec((ROW_TILE, D_MODEL), lambda i: (s_tile(i), 0)),
            pl.BlockSpec((HALO, D_MODEL), lambda i: (jnp.maximum(s_tile(i) * r - 1, 0), 0)),
            pl.BlockSpec((HALO, D_MODEL), lambda i: (jnp.minimum((s_tile(i) + 1) * r, ts // HALO - 1), 0)),
            pl.BlockSpec((D_MODEL, QK_WIDTH), const),
            pl.BlockSpec((D_MODEL, Z_WIDTH), const),
            pl.BlockSpec((D_MODEL, N_GATES), const),
            pl.BlockSpec((N_GATES, D_MODEL), const),
            pl.BlockSpec((1, N_GATES), const),
            pl.BlockSpec((N_GATES, 1), const),
            pl.BlockSpec((CONV_WIDTH, QK_WIDTH), const),
            pl.BlockSpec((1, QK_WIDTH), const),
            pl.BlockSpec((1, QK_WIDTH), const),
        ],
        out_specs=[
            pl.BlockSpec((ROW_TILE, QK_WIDTH), lambda i: (i, 0)),
            pl.BlockSpec((ROW_TILE, Z_WIDTH), lambda i: (i, 0)),
            pl.BlockSpec((ROW_TILE, N_GATES), lambda i: (i, 0)),
            pl.BlockSpec((N_GATES, ROW_TILE), lambda i: (0, i)),
        ],
        out_shape=[
            jax.ShapeDtypeStruct((t, QK_WIDTH), BF16),
            jax.ShapeDtypeStruct((t, Z_WIDTH), BF16),
            jax.ShapeDtypeStruct((t, N_GATES), F32),
            jax.ShapeDtypeStruct((N_GATES, t), F32),
        ],
        scratch_shapes=[pltpu.VMEM((ROW_TILE + 2 * HALO, QK_WIDTH), F32)],
        compiler_params=_cparams(("arbitrary",)),
        name="in_proj_conv",
    )(xp, xp, xp, xs, xs, xs, w_qk, w_main, w_g, w_g.T, b_g.reshape(1, N_GATES), b_g.reshape(N_GATES, 1),
      conv_w, conv_b.reshape(1, QK_WIDTH), scale)


ATT_TILE = 512
ATT_SUB = ATT_TILE // BLOCK


ONES_ROWS = 16


def _attention_kernel(seq, q_ref, kp_ref, kc_ref, kn_ref, vp_ref, vc_ref, vn_ref, bias_ref, sink_ref,
                      o_ref, klo_ref, khi_ref, vt_ref):
    i = pl.program_id(0)
    lane = lax.broadcasted_iota(jnp.int32, (ATT_TILE + 2 * BLOCK, 2 * ATT_HEAD_DIM), 1)
    kband = jnp.concatenate([kp_ref[...], kc_ref[...], kn_ref[...]], axis=0)
    zero = jnp.zeros_like(kband)
    klo_ref[...] = jnp.where(lane < ATT_HEAD_DIM, kband, zero)
    khi_ref[...] = jnp.where(lane < ATT_HEAD_DIM, zero, kband)
    vband = jnp.concatenate([vp_ref[...], vc_ref[...], vn_ref[...]], axis=0)
    vt_ref[0:2 * ATT_HEAD_DIM, :] = vband.T
    vt_ref[2 * ATT_HEAD_DIM:, :] = jnp.ones((ONES_ROWS, ATT_TILE + 2 * BLOCK), BF16)
    feat = lax.broadcasted_iota(jnp.int32, (2 * ATT_HEAD_DIM, ATT_GROUP * BLOCK), 0)
    for s in range(ATT_SUB):
        first, last = _seq_pos(i * ATT_SUB + s, *seq)
        variant = jnp.where(first, 1, jnp.where(last, 2, 0))
        q = q_ref[s * BLOCK:(s + 1) * BLOCK, :]
        q_all = jnp.concatenate([q[:, t * 128:(t + 1) * 128] for t in range(ATT_GROUP)], axis=0)
        vt = vt_ref[:, s * BLOCK:(s + 3) * BLOCK]
        outs = []
        for kv, k_ref in enumerate((klo_ref, khi_ref)):
            logits = _dot_nt(k_ref[s * BLOCK:(s + 3) * BLOCK, :], q_all) + bias_ref[variant, kv]
            sink = sink_ref[kv]
            m = jnp.maximum(jnp.max(logits, axis=0, keepdims=True), sink)
            p = jnp.exp(logits - m).astype(BF16)
            ov = _dot(vt, p)
            den = ov[2 * ATT_HEAD_DIM:2 * ATT_HEAD_DIM + 1, :] + jnp.exp(sink - m)
            outs.append(ov[0:2 * ATT_HEAD_DIM, :] * (1.0 / den))
        both = jnp.where(feat < ATT_HEAD_DIM, outs[0], outs[1]).astype(BF16)
        for t in range(ATT_GROUP):
            o_ref[t * 128:(t + 1) * 128, s * BLOCK:(s + 1) * BLOCK] = both[:, t * BLOCK:(t + 1) * BLOCK]


def _attention(z, bias, sink, seq):
    t = z.shape[0]
    n = t // ATT_TILE
    nblk = t // BLOCK
    band = ATT_TILE + 2 * BLOCK
    prev = lambda i: jnp.maximum(i * ATT_SUB - 1, 0)
    nxt = lambda i: jnp.minimum((i + 1) * ATT_SUB, nblk - 1)
    return pl.pallas_call(
        functools.partial(_attention_kernel, seq),
        grid=(n,),
        in_specs=[
            pl.BlockSpec((ATT_TILE, ATT_WIDTH), lambda i: (i, ZB_ATTQ)),
            pl.BlockSpec((BLOCK, 128), lambda i: (prev(i), ZB_ATTK)),
            pl.BlockSpec((ATT_TILE, 128), lambda i: (i, ZB_ATTK)),
            pl.BlockSpec((BLOCK, 128), lambda i: (nxt(i), ZB_ATTK)),
            pl.BlockSpec((BLOCK, 128), lambda i: (prev(i), ZB_ATTV)),
            pl.BlockSpec((ATT_TILE, 128), lambda i: (i, ZB_ATTV)),
            pl.BlockSpec((BLOCK, 128), lambda i: (nxt(i), ZB_ATTV)),
            pl.BlockSpec((3, ATT_KV_HEADS, 3 * BLOCK, ATT_GROUP * BLOCK), lambda i: (0, 0, 0, 0)),
            pl.BlockSpec((ATT_KV_HEADS, 1, ATT_GROUP * BLOCK), lambda i: (0, 0, 0)),
        ],
        out_specs=pl.BlockSpec((ATT_WIDTH, ATT_TILE), lambda i: (0, i)),
        out_shape=jax.ShapeDtypeStruct((ATT_WIDTH, t), BF16),
        scratch_shapes=[pltpu.VMEM((band, 128), BF16), pltpu.VMEM((band, 128), BF16),
                        pltpu.VMEM((2 * ATT_HEAD_DIM + ONES_ROWS, band), BF16)],
        compiler_params=_cparams(("arbitrary",)),
        name="window_attention",
    )(z, z, z, z, z, z, z, bias, sink)


def _t5_bucket(rel):
    half = N_BUCKETS // 2
    exact = half // 2
    n = np.abs(rel)
    large = exact + (np.log(np.maximum(n, 1) / exact) / np.log(MAX_DISTANCE / exact) * (half - exact)).astype(np.int32)
    large = np.minimum(large, half - 1)
    return ((rel > 0).astype(np.int32) * half + np.where(n < exact, n, large)).astype(np.int32)


def _attention_tables(rel_bias, attn_sink):
    rel = np.arange(3 * BLOCK)[:, None] - BLOCK - np.arange(BLOCK)[None, :]
    onehot = jnp.asarray(_t5_bucket(rel)[..., None] == np.arange(N_BUCKETS), F32)
    bias = jnp.einsum('kqb,bh->hkq', onehot, rel_bias.astype(F32), precision=lax.Precision.HIGHEST)
    bias = jnp.where(jnp.asarray(np.abs(rel) <= WINDOW)[None], bias, NEG)
    bias = bias.reshape(ATT_KV_HEADS, ATT_GROUP, 3 * BLOCK, BLOCK).transpose(0, 2, 1, 3)
    bias = bias.reshape(ATT_KV_HEADS, 3 * BLOCK, ATT_GROUP * BLOCK)
    key = np.arange(3 * BLOCK)[None, :, None]
    first = jnp.where(jnp.asarray(key < BLOCK), NEG, bias)
    last = jnp.where(jnp.asarray(key >= 2 * BLOCK), NEG, bias)
    sink = jnp.repeat(attn_sink.astype(F32), BLOCK).reshape(ATT_KV_HEADS, 1, ATT_GROUP * BLOCK)
    return jnp.stack([bias, first, last]), sink


def _mlstm_kernel(seq, n_chunks, *refs):
    fwd_in, bwd_in, (of_ref, ob_ref), state = refs[0:5], refs[5:10], refs[10:12], refs[12:]
    _mlstm_scan(seq, False, n_chunks, *fwd_in, of_ref, *state[0:2])
    _mlstm_scan(seq, True, n_chunks, *bwd_in, ob_ref, *state[2:4])


def _mlstm_scan(seq, reverse, n_chunks, q_ref, k_ref, v_ref, gr_ref, gc_ref, o_ref, ct_ref, m_ref):
    step = pl.program_id(0)
    chunk = (n_chunks - 1 - step) if reverse else step
    first, last = _seq_pos(chunk, *seq)
    fresh = last if reverse else first

    L = ML_CHUNK
    row = lax.broadcasted_iota(jnp.int32, (L, L), 0)
    col = lax.broadcasted_iota(jnp.int32, (L, L), 1)
    vis = (row >= col) if reverse else (row <= col)
    vis_t = (col >= row) if reverse else (col <= row)
    gr = gr_ref[...]
    gc = gc_ref[...]
    b_rows = _dot_exact(_log_sigmoid(gr), vis.astype(F32))
    b_cols = _dot_exact(vis_t.astype(F32), _log_sigmoid(gc))
    i_off = 2 * ML_HEADS if reverse else 0
    f_off = i_off + ML_HEADS
    end = 0 if reverse else L - 1
    ones = jnp.ones((ONES_ROWS, L), BF16)
    for h in range(ML_HEADS):
        hs = slice(h * ML_HEAD_DIM, (h + 1) * ML_HEAD_DIM)
        q = q_ref[:, hs]
        k = k_ref[:, hs]
        vt1 = jnp.concatenate([v_ref[:, hs].T, ones], axis=0)
        b_row = b_rows[f_off + h:f_off + h + 1, :]
        u_row = gr[i_off + h:i_off + h + 1, :] - b_row
        u_col = gc[:, i_off + h:i_off + h + 1] - b_cols[:, f_off + h:f_off + h + 1]
        g = b_row[:, end:end + 1]
        ct_old = jnp.where(fresh, 0.0, ct_ref[h])
        m_old = jnp.where(fresh, 0.0, m_ref[h])[:, 0:1]
        u_mat = jnp.where(vis, u_col, NEG)
        mm = jnp.maximum(jnp.max(u_mat, axis=0, keepdims=True), m_old)
        st = (_dot_nt(k, q) * jnp.exp(u_mat - mm)).astype(BF16)
        w_inter = jnp.exp(m_old - mm)
        tot = _dot(vt1, st) + w_inter * _dot_nt(ct_old.astype(BF16), q)
        den = tot[ML_HEAD_DIM:ML_HEAD_DIM + 1, :]
        floor = jnp.exp(-(b_row + mm))
        o_ref[hs, :] = (tot[0:ML_HEAD_DIM, :] * (1.0 / jnp.maximum(jnp.abs(den), floor))).astype(BF16)
        a_max = jnp.max(g + u_row, axis=-1, keepdims=True)
        m_new = jnp.maximum(g + m_old, a_max)
        s_old = jnp.exp(g + m_old - m_new)
        weighted = (vt1.astype(F32) * jnp.exp(g + u_row - m_new)).astype(BF16)
        ct_ref[h] = s_old * ct_old + _dot(weighted, k)
        m_ref[h] = jnp.broadcast_to(m_new, (1, ML_HEAD_DIM))


def _mlstm(qk, z, gates_r, gates_c, seq):
    t = qk.shape[0]
    nc = t // ML_CHUNK

    def chunk_specs(ch):
        return [
            pl.BlockSpec((ML_CHUNK, ML_WIDTH), lambda i: (ch(i), 0)),
            pl.BlockSpec((ML_CHUNK, ML_WIDTH), lambda i: (ch(i), 1)),
            pl.BlockSpec((ML_CHUNK, ML_WIDTH), lambda i: (ch(i), ZB_MLV)),
            pl.BlockSpec((N_GATES, ML_CHUNK), lambda i: (0, ch(i))),
            pl.BlockSpec((ML_CHUNK, N_GATES), lambda i: (ch(i), 0)),
        ]

    fwd = lambda i: i
    bwd = lambda i: nc - 1 - i
    state = [
        pltpu.VMEM((ML_HEADS, ML_HEAD_DIM + ONES_ROWS, ML_HEAD_DIM), F32),
        pltpu.VMEM((ML_HEADS, 1, ML_HEAD_DIM), F32),
    ]
    operands = (qk, qk, z, gates_r, gates_c)
    return pl.pallas_call(
        functools.partial(_mlstm_kernel, seq, nc),
        grid=(nc,),
        in_specs=chunk_specs(fwd) + chunk_specs(bwd),
        out_specs=[pl.BlockSpec((ML_WIDTH, ML_CHUNK), lambda i: (0, fwd(i))),
                   pl.BlockSpec((ML_WIDTH, ML_CHUNK), lambda i: (0, bwd(i)))],
        out_shape=[jax.ShapeDtypeStruct((ML_WIDTH, t), BF16), jax.ShapeDtypeStruct((ML_WIDTH, t), BF16)],
        scratch_shapes=state + state,
        compiler_params=_cparams(("arbitrary",)),
        name="mlstm",
    )(*operands, *operands)


def _mix_out(n_p_tiles, xp_ref, xs_ref, att_ref, hf_ref, hb_ref, og_ref, wa_ref, wm_ref, gain_ref, g_ref, b_ref):
    i = pl.program_id(0)
    x = jnp.where(i < n_p_tiles, xp_ref[...], xs_ref[...])
    h = hf_ref[...].astype(F32) + hb_ref[...].astype(F32)
    parts = []
    for hd in range(ML_HEADS):
        hh = h[hd * ML_HEAD_DIM:(hd + 1) * ML_HEAD_DIM, :]
        mu = jnp.mean(hh, axis=0, keepdims=True)
        hc = hh - mu
        var = jnp.mean(hc * hc, axis=0, keepdims=True)
        parts.append(hc * lax.rsqrt(var + LN_EPS))
    hn = jnp.concatenate(parts, axis=0) * gain_ref[...] * _sigmoid(og_ref[...].T.astype(F32))
    mixed = _dot(att_ref[...].T, wa_ref[...]) + _dot(hn.astype(BF16).T, wm_ref[...])
    return _layer_norm(DN_ALPHA * x + mixed, g_ref[...], b_ref[...])


def _mem_kv_kernel(m_ref, w_ref, o_ref):
    o_ref[...] = _dot(m_ref[...].astype(BF16), w_ref[...]).astype(BF16)


def _mem_kv(mem, wkv):
    rows = mem.shape[0]
    return pl.pallas_call(
        _mem_kv_kernel,
        grid=(rows // MEM_TOKENS,),
        in_specs=[pl.BlockSpec((MEM_TOKENS, D_MODEL), lambda i: (i, 0)),
                  pl.BlockSpec((D_MODEL, 2 * D_MODEL), lambda i: (0, 0))],
        out_specs=pl.BlockSpec((MEM_TOKENS, 2 * D_MODEL), lambda i: (i, 0)),
        out_shape=jax.ShapeDtypeStruct((rows, 2 * D_MODEL), BF16),
        compiler_params=_cparams(("arbitrary",)),
        name="mem_kv",
    )(mem, wkv)


def _pack_bf16_pairs(a, b):
    ua = pltpu.bitcast(a.astype(BF16).astype(F32), jnp.uint32)
    ub = pltpu.bitcast(b.astype(BF16).astype(F32), jnp.uint32)
    return (ua >> 16) | (ub & jnp.uint32(0xFFFF0000))


def _unpack_bf16_pairs(u):
    lo = pltpu.bitcast(u << 16, F32)
    hi = pltpu.bitcast(u & jnp.uint32(0xFFFF0000), F32)
    return lo, hi


def _cross_router_kernel(n_p_tiles, *refs):
    mix_refs, (kv_ref, wq_ref, wo_ref, g_ref, b_ref, wr_ref, br_ref, tri_ref, etri_ref,
               x2_ref, x2b_ref, gate_ref, slot_ref, chunk_ref) = refs[:11], refs[11:]
    x = _mix_out(n_p_tiles, *mix_refs)
    q = _dot(x.astype(BF16), wq_ref[...]).astype(BF16)
    heads = []
    for h in range(MEM_HEADS):
        hs = slice(h * MEM_HEAD_DIM, (h + 1) * MEM_HEAD_DIM)
        logits = _dot_nt(q[:, hs], kv_ref[:, hs])
        m = jnp.max(logits, axis=-1, keepdims=True)
        p = jnp.exp(logits - m)
        den = jnp.sum(p, axis=-1, keepdims=True)
        vs = slice(D_MODEL + h * MEM_HEAD_DIM, D_MODEL + (h + 1) * MEM_HEAD_DIM)
        heads.append((_dot(p.astype(BF16), kv_ref[:, vs]) * (1.0 / den)).astype(BF16))
    o = jnp.concatenate(heads, axis=1)
    x2 = _layer_norm(DN_ALPHA * x + _dot(o, wo_ref[...]), g_ref[...], b_ref[...])
    x2_ref[...] = x2
    x2b = x2.astype(BF16)
    x2b_ref[...] = x2b

    logits = _dot_nt(wr_ref[...], x2b) + br_ref[...]
    expert = lax.broadcasted_iota(jnp.int32, logits.shape, 0)
    work = logits
    vals, sels = [], []
    for k in range(TOP_K):
        mx = jnp.max(work, axis=0, keepdims=True)
        ix = jnp.min(jnp.where(work == mx, expert, N_EXPERTS), axis=0, keepdims=True)
        sel = expert == ix
        work = jnp.where(sel, -jnp.inf, work)
        vals.append(mx)
        sels.append(sel)
    es = [jnp.exp(v - vals[0]) for v in vals]
    tot = es[0] + es[1] + es[2] + es[3]
    chosen = jnp.zeros(logits.shape, F32)
    for k in range(TOP_K):
        gate_ref[k:k + 1, :] = es[k] / tot
        chosen = chosen + sels[k].astype(F32)
    count = jnp.sum(chosen, axis=1, keepdims=True)
    chunks = jnp.floor((count + (SEG_ALIGN - 1)) * (1.0 / SEG_ALIGN))
    chunks_b = jnp.broadcast_to(chunks, (N_EXPERTS, 128))
    seg_first = _dot(etri_ref[...], chunks_b.astype(BF16))[:, 0:1] * SEG_ALIGN
    before = _dot(chosen.astype(BF16), tri_ref[...])
    local_row = before + seg_first
    for k in range(TOP_K):
        slot_ref[k:k + 1, :] = jnp.sum(jnp.where(sels[k], local_row, 0.0), axis=0, keepdims=True).astype(jnp.int32)
    chunk_ref[0] = chunks_b.astype(jnp.int32)


def _cross_router(xp, xs, att, hf, hb, z, w_att, w_ml, gain, g1, b1, kv, wq, wo, g, b, w_router, b_router,
                  mem_of_tile):
    t = att.shape[1]
    n_p = xp.shape[0] // ROW_TILE
    n = t // ROW_TILE
    const = lambda i: (0, 0)
    tile = lambda i: (i, 0)
    lanes = lambda i: (0, i)
    tri = jnp.asarray(np.triu(np.ones((ROW_TILE, ROW_TILE), np.float32), 1), BF16)
    etri = jnp.asarray(np.tril(np.ones((N_EXPERTS, N_EXPERTS), np.float32), -1), BF16)
    return pl.pallas_call(
        functools.partial(_cross_router_kernel, n_p),
        grid=(n,),
        in_specs=[
            pl.BlockSpec((ROW_TILE, D_MODEL), lambda i: (jnp.minimum(i, n_p - 1), 0)),
            pl.BlockSpec((ROW_TILE, D_MODEL), lambda i: (jnp.maximum(i - n_p, 0), 0)),
            pl.BlockSpec((ATT_WIDTH, ROW_TILE), lanes),
            pl.BlockSpec((ML_WIDTH, ROW_TILE), lanes),
            pl.BlockSpec((ML_WIDTH, ROW_TILE), lanes),
            pl.BlockSpec((ROW_TILE, ML_WIDTH), lambda i: (i, ZB_MLO)),
            pl.BlockSpec((ATT_WIDTH, D_MODEL), const),
            pl.BlockSpec((ML_WIDTH, D_MODEL), const),
            pl.BlockSpec((ML_WIDTH, 1), const),
            pl.BlockSpec((1, D_MODEL), const),
            pl.BlockSpec((1, D_MODEL), const),
            pl.BlockSpec((MEM_TOKENS, 2 * D_MODEL), lambda i: (mem_of_tile(i), 0)),
            pl.BlockSpec((D_MODEL, D_MODEL), const),
            pl.BlockSpec((D_MODEL, D_MODEL), const),
            pl.BlockSpec((1, D_MODEL), const),
            pl.BlockSpec((1, D_MODEL), const),
            pl.BlockSpec((N_EXPERTS, D_MODEL), const),
            pl.BlockSpec((N_EXPERTS, 1), const),
            pl.BlockSpec((ROW_TILE, ROW_TILE), const),
            pl.BlockSpec((N_EXPERTS, N_EXPERTS), const),
        ],
        out_specs=[
            pl.BlockSpec((ROW_TILE, D_MODEL), tile),
            pl.BlockSpec((ROW_TILE, D_MODEL), tile),
            pl.BlockSpec((TOP_K, ROW_TILE), lanes),
            pl.BlockSpec((TOP_K, ROW_TILE), lanes),
            pl.BlockSpec((1, N_EXPERTS, 128), lambda i: (i, 0, 0)),
        ],
        out_shape=[
            jax.ShapeDtypeStruct((t, D_MODEL), F32),
            jax.ShapeDtypeStruct((t, D_MODEL), BF16),
            jax.ShapeDtypeStruct((TOP_K, t), F32),
            jax.ShapeDtypeStruct((TOP_K, t), jnp.int32),
            jax.ShapeDtypeStruct((n, N_EXPERTS, 128), jnp.int32),
        ],
        compiler_params=_cparams(("arbitrary",)),
        name="mix_out_cross_attn_router",
    )(xp, xs, att, hf, hb, z, w_att, w_ml, gain.reshape(ML_WIDTH, 1), g1.reshape(1, D_MODEL), b1.reshape(1, D_MODEL),
      kv, wq, wo, g.reshape(1, D_MODEL), b.reshape(1, D_MODEL), w_router.T.astype(BF16),
      b_router.reshape(N_EXPERTS, 1), tri, etri)


SEG_ALIGN = 8
LOCAL_ROWS = ROW_TILE * TOP_K + N_EXPERTS * SEG_ALIGN
HALF = D_MODEL // 2


def _rows(chunks):
    return pl.multiple_of(chunks * SEG_ALIGN, SEG_ALIGN)


def _segment_copies(chunks_ref, seg_ref, loc_ref, tile, make_copy):
    def per_expert(e, carry):
        entry = tile * N_EXPERTS + e

        @pl.when(chunks_ref[entry] > 0)
        def _():
            make_copy(_rows(loc_ref[entry]), _rows(seg_ref[entry]), _rows(chunks_ref[entry])).start()

        return carry

    lax.fori_loop(0, N_EXPERTS, per_expert, 0)


def _wait_copies(chunks, make_copy):
    @pl.when(chunks > 0)
    def _():
        make_copy(0, 0, _rows(chunks)).wait()


def _dispatch_kernel(chunks_ref, seg_ref, loc_ref, total_ref, tail_ref, tailn_ref, nact_ref,
                     x_ref, slot_ref, rows_hbm, local_ref, zero_ref, sems):
    i = pl.program_id(0)
    n = pl.num_programs(0)
    buf = i % 2

    def copy_out(b):
        return lambda loc, seg, rows: pltpu.make_async_copy(
            local_ref.at[b, pl.ds(loc, rows), :], rows_hbm.at[pl.ds(seg, rows), :], sems.at[b])

    def run(b):
        @pl.when(i >= 2)
        def _():
            _wait_copies(total_ref[jnp.maximum(i - 2, 0)], copy_out(b))

        slots = slot_ref[...]
        row = lax.broadcasted_iota(jnp.int32, (LOCAL_ROWS, ROW_TILE), 0)
        hit = slots[0:1, :] == row
        for k in range(1, TOP_K):
            hit = hit | (slots[k:k + 1, :] == row)
        perm = jnp.where(hit, 1.0, 0.0).astype(BF16)
        rows = _dot(perm, x_ref[...])
        local_ref[b] = _pack_bf16_pairs(rows[:, :HALF], rows[:, HALF:])
        _segment_copies(chunks_ref, seg_ref, loc_ref, i, copy_out(b))

    for b in range(2):
        pl.when(buf == b)(functools.partial(run, b))

    @pl.when(i == n - 1)
    def _():
        for b in range(2):
            step = jnp.where(buf == b, i, i - 1)
            _wait_copies(total_ref[step], copy_out(b))
        zero_ref[...] = jnp.zeros_like(zero_ref)
        fill = lambda _, seg, rows: pltpu.make_async_copy(
            zero_ref.at[pl.ds(0, rows), :], rows_hbm.at[pl.ds(seg, rows), :], sems.at[2])

        def per_expert(e, carry):
            @pl.when(tailn_ref[e] > 0)
            def _():
                fill(0, _rows(tail_ref[e]), _rows(tailn_ref[e])).start()

            _wait_copies(tailn_ref[e], fill)
            return carry

        lax.fori_loop(0, N_EXPERTS, per_expert, 0)
        fill_block = lambda blk: pltpu.make_async_copy(
            zero_ref, rows_hbm.at[pl.ds(pl.multiple_of(blk * MOE_ROWS, MOE_ROWS), MOE_ROWS), :], sems.at[2])
        n_blocks = rows_hbm.shape[0] // MOE_ROWS

        def start_block(blk, carry):
            fill_block(blk).start()
            return carry

        def wait_block(blk, carry):
            fill_block(0).wait()
            return carry

        lax.fori_loop(nact_ref[0], n_blocks, start_block, 0)
        lax.fori_loop(nact_ref[0], n_blocks, wait_block, 0)


def _dispatch(x2b, slots, tables, n_rows):
    n = x2b.shape[0] // ROW_TILE
    assert n >= 2
    return pl.pallas_call(
        _dispatch_kernel,
        grid_spec=pltpu.PrefetchScalarGridSpec(
            num_scalar_prefetch=7,
            grid=(n,),
            in_specs=[
                pl.BlockSpec((ROW_TILE, D_MODEL), lambda i, *_: (i, 0)),
                pl.BlockSpec((TOP_K, ROW_TILE), lambda i, *_: (0, i)),
            ],
            out_specs=pl.BlockSpec(memory_space=pl.ANY),
            scratch_shapes=[
                pltpu.VMEM((2, LOCAL_ROWS, HALF), jnp.uint32),
                pltpu.VMEM((MOE_ROWS, HALF), jnp.uint32),
                pltpu.SemaphoreType.DMA((3,)),
            ],
        ),
        out_shape=jax.ShapeDtypeStruct((n_rows, HALF), jnp.uint32),
        compiler_params=_cparams(("arbitrary",)),
        name="moe_dispatch",
    )(*tables, x2b, slots)


def _expert_kernel(be_ref, na_ref, x_ref, wu_ref, bu_ref, wd_ref, bd_ref, o_ref, wub_ref, wdb_ref):
    i = pl.program_id(0)

    @pl.when((i == 0) | (be_ref[i] != be_ref[jnp.maximum(i - 1, 0)]))
    def _():
        wub_ref[...] = wu_ref[0].astype(BF16)
        wdb_ref[...] = wd_ref[0].astype(BF16)

    @pl.when(i < na_ref[0])
    def _():
        lo, hi = _unpack_bf16_pairs(x_ref[...])
        x = jnp.concatenate([lo, hi], axis=1).astype(BF16)
        hu = _dot(x, wub_ref[...]) + bu_ref[0]
        h_glu = jnp.minimum(hu[:, :D_FF], SWIGLU_LIMIT)
        h_lin = jnp.clip(hu[:, D_FF:], -SWIGLU_LIMIT, SWIGLU_LIMIT)
        hh = h_glu * _sigmoid(SWIGLU_ALPHA * h_glu) * (h_lin + 1.0)
        y = _dot(hh.astype(BF16), wdb_ref[...]) + bd_ref[0]
        o_ref[...] = _pack_bf16_pairs(y[:, :D_MODEL // 2], y[:, D_MODEL // 2:])

    @pl.when(i >= na_ref[0])
    def _():
        o_ref[...] = jnp.zeros_like(o_ref)


def _experts(xs, block_e, n_active, w_up, b_up, w_down, b_down):
    n_rows = xs.shape[0]
    n_blk = n_rows // MOE_ROWS
    return pl.pallas_call(
        _expert_kernel,
        grid_spec=pltpu.PrefetchScalarGridSpec(
            num_scalar_prefetch=2,
            grid=(n_blk,),
            in_specs=[
                pl.BlockSpec((MOE_ROWS, D_MODEL // 2), lambda i, be, na: (jnp.minimum(i, na[0] - 1), 0)),
                pl.BlockSpec((1, D_MODEL, 2 * D_FF), lambda i, be, na: (be[i], 0, 0)),
                pl.BlockSpec((1, 1, 2 * D_FF), lambda i, be, na: (be[i], 0, 0)),
                pl.BlockSpec((1, D_FF, D_MODEL), lambda i, be, na: (be[i], 0, 0)),
                pl.BlockSpec((1, 1, D_MODEL), lambda i, be, na: (be[i], 0, 0)),
            ],
            out_specs=pl.BlockSpec((MOE_ROWS, D_MODEL // 2), lambda i, be, na: (i, 0)),
            scratch_shapes=[pltpu.VMEM((D_MODEL, 2 * D_FF), BF16), pltpu.VMEM((D_FF, D_MODEL), BF16)],
        ),
        out_shape=jax.ShapeDtypeStruct((n_rows, D_MODEL // 2), jnp.uint32),
        compiler_params=_cparams(("arbitrary",)),
        name="experts",
    )(block_e, n_active, xs, w_up, b_up.reshape(N_EXPERTS, 1, 2 * D_FF), w_down, b_down.reshape(N_EXPERTS, 1, D_MODEL))


def _combine_kernel(first_tile, chunks_ref, seg_ref, loc_ref, total_ref,
                    x_ref, rows_hbm, slot_ref, gate_ref, g_ref, b_ref, o_ref, local_ref, sems):
    i = pl.program_id(0)
    n = pl.num_programs(0)
    tile = first_tile + i
    buf = i % 2

    def copy_in(b):
        return lambda loc, seg, rows: pltpu.make_async_copy(
            rows_hbm.at[pl.ds(seg, rows), :], local_ref.at[b, pl.ds(loc, rows), :], sems.at[b])

    @pl.when(i == 0)
    def _():
        local_ref[...] = jnp.zeros_like(local_ref)
        _segment_copies(chunks_ref, seg_ref, loc_ref, tile, copy_in(0))

    def run(b):
        @pl.when(i + 1 < n)
        def _():
            _segment_copies(chunks_ref, seg_ref, loc_ref, tile + 1, copy_in(1 - b))

        _wait_copies(total_ref[tile], copy_in(b))
        lo, hi = _unpack_bf16_pairs(local_ref[b])
        rows = jnp.concatenate([lo, hi], axis=1).astype(BF16)
        slots = slot_ref[...]
        gates = gate_ref[...]
        row = lax.broadcasted_iota(jnp.int32, (ROW_TILE, LOCAL_ROWS), 1)
        weight = jnp.zeros((ROW_TILE, LOCAL_ROWS), F32)
        for k in range(TOP_K):
            weight = weight + jnp.where(slots[:, k:k + 1] == row, gates[:, k:k + 1], 0.0)
        y = _dot(weight.astype(BF16), rows)
        o_ref[...] = _layer_norm(DN_ALPHA * x_ref[...] + y, g_ref[...], b_ref[...])

    for b in range(2):
        pl.when(buf == b)(functools.partial(run, b))


def _combine(x2, rows_out, slots_c, gates_c, tables, g, b, first_tile, n_tiles):
    const = lambda i, *_: (0, 0)
    tile = lambda i, *_: (first_tile + i, 0)
    return pl.pallas_call(
        functools.partial(_combine_kernel, first_tile),
        grid_spec=pltpu.PrefetchScalarGridSpec(
            num_scalar_prefetch=4,
            grid=(n_tiles,),
            in_specs=[
                pl.BlockSpec((ROW_TILE, D_MODEL), tile),
                pl.BlockSpec(memory_space=pl.ANY),
                pl.BlockSpec((ROW_TILE, TOP_K), tile),
                pl.BlockSpec((ROW_TILE, TOP_K), tile),
                pl.BlockSpec((1, D_MODEL), const),
                pl.BlockSpec((1, D_MODEL), const),
            ],
            out_specs=pl.BlockSpec((ROW_TILE, D_MODEL), lambda i, *_: (i, 0)),
            scratch_shapes=[pltpu.VMEM((2, LOCAL_ROWS, HALF), jnp.uint32), pltpu.SemaphoreType.DMA((2,))],
        ),
        out_shape=jax.ShapeDtypeStruct((n_tiles * ROW_TILE, D_MODEL), F32),
        compiler_params=_cparams(("arbitrary",)),
        name="combine_ln3",
    )(*tables[:4], x2, rows_out, slots_c, gates_c, g.reshape(1, D_MODEL), b.reshape(1, D_MODEL))


def kernel(x_prompt, x_sample, mem_prompt, mem_sample, rel_bias, w_in, b_gates, conv_w, conv_b, mh_gain, attn_sink,
           w_out, ln1_g, ln1_b, wq_mem, wkv_mem, wo_mem, ln2_g, ln2_b, w_router, b_router, w_up, b_up, w_down,
           b_down, ln3_g, ln3_b):
    assert w_in.shape[0] == 1, "single layer"
    bp, sp, _ = x_prompt.shape
    bs, ss, _ = x_sample.shape
    tp, ts = bp * sp, bs * ss
    t = tp + ts
    assert sp % ROW_TILE == 0 and ss % ROW_TILE == 0 and sp >= 2 * BLOCK and ss >= 2 * BLOCK
    xp = x_prompt.reshape(tp, D_MODEL)
    xs = x_sample.reshape(ts, D_MODEL)

    def seq_blocks(rows):
        return (tp // rows, sp // rows, ss // rows)

    w = w_in[0]
    q_end, k_end, v_end = ATT_WIDTH, ATT_WIDTH + 128, ATT_WIDTH + 256
    qk_end, mv_end, mo_end = v_end + 2 * ML_WIDTH, v_end + 3 * ML_WIDTH, v_end + 4 * ML_WIDTH
    head_order = np.concatenate([[h, ATT_GROUP + h] for h in range(ATT_GROUP)])
    att_perm = (head_order[:, None] * ATT_HEAD_DIM + np.arange(ATT_HEAD_DIM)[None, :]).reshape(-1)
    w_qk = w[:, v_end:qk_end].astype(BF16)
    w_main = jnp.concatenate([
        w[:, qk_end:mv_end], w[:, mv_end:mo_end],
        w[:, :q_end][:, att_perm] * (ATT_HEAD_DIM ** -0.5), w[:, q_end:k_end], w[:, k_end:v_end]], axis=1).astype(BF16)
    w_g = w[:, mo_end:].astype(BF16)
    w_att = w_out[0][:ATT_WIDTH][att_perm].astype(BF16)
    w_ml = w_out[0][ATT_WIDTH:].astype(BF16)

    qk, z, gates_c, gates_r = _in_proj(xp, xs, w_qk, w_main, w_g, b_gates[0], conv_w[0], conv_b[0],
                                       seq_blocks(ROW_TILE))

    bias, sink = _attention_tables(rel_bias, attn_sink[0])
    att = _attention(z, bias, sink, seq_blocks(BLOCK))

    h_f, h_b = _mlstm(qk, z, gates_r, gates_c, seq_blocks(ML_CHUNK))

    mem = jnp.concatenate([mem_prompt.reshape(bp * MEM_TOKENS, D_MODEL), mem_sample.reshape(bs * MEM_TOKENS, D_MODEL)])
    kv = _mem_kv(mem, wkv_mem[0].astype(BF16))
    n_p_tiles, p_tiles_per_seq, s_tiles_per_seq = seq_blocks(ROW_TILE)
    mem_of_tile = lambda i: jnp.where(i < n_p_tiles, i // p_tiles_per_seq, bp + (i - n_p_tiles) // s_tiles_per_seq)
    wq = (wq_mem[0] * (MEM_HEAD_DIM ** -0.5)).astype(BF16)
    x2, x2b, top_gate, slots, tile_chunks = _cross_router(
        xp, xs, att, h_f, h_b, z, w_att, w_ml, mh_gain[0], ln1_g[0], ln1_b[0],
        kv, wq, wo_mem[0].astype(BF16), ln2_g[0], ln2_b[0], w_router[0], b_router[0], mem_of_tile)

    n_tiles = t // ROW_TILE
    blk_chunks = MOE_ROWS // SEG_ALIGN
    chunks = tile_chunks[:, :, 0]
    used = jnp.sum(chunks, axis=0)
    region = ((used + blk_chunks - 1) // blk_chunks) * blk_chunks
    region_end = jnp.cumsum(region)
    region_start = region_end - region
    seg_start = region_start[None, :] + jnp.cumsum(chunks, axis=0) - chunks
    loc_start = jnp.cumsum(chunks, axis=1) - chunks
    n_blk = -(-(t * TOP_K + n_tiles * N_EXPERTS * (SEG_ALIGN - 1) + N_EXPERTS * (MOE_ROWS - 1)) // MOE_ROWS)
    blk_first = jnp.arange(n_blk, dtype=jnp.int32) * blk_chunks
    block_e = jnp.minimum(jnp.sum(blk_first[:, None] >= region_end[None, :], axis=1), N_EXPERTS - 1).astype(jnp.int32)
    n_active = (region_end[-1:] // blk_chunks).astype(jnp.int32)
    tables = (chunks.reshape(-1), seg_start.reshape(-1), loc_start.reshape(-1), jnp.sum(chunks, axis=1),
              region_start + used, region - used, n_active)
    tables = tuple(tb.astype(jnp.int32) for tb in tables)

    rows_in = _dispatch(x2b, slots, tables, n_blk * MOE_ROWS)
    rows_out = _experts(rows_in, block_e, n_active, w_up[0], b_up[0], w_down[0], b_down[0])

    slots_c, gates_c = slots.T, top_gate.T
    y_p = _combine(x2, rows_out, slots_c, gates_c, tables, ln3_g[0], ln3_b[0], 0, n_p_tiles)
    y_s = _combine(x2, rows_out, slots_c, gates_c, tables, ln3_g[0], ln3_b[0], n_p_tiles, n_tiles - n_p_tiles)
    return y_p.reshape(bp, sp, D_MODEL), y_s.reshape(bs, ss, D_MODEL)
```

```python
import functools

import numpy as np
import jax
import jax.numpy as jnp
from jax import lax
from jax.experimental import pallas as pl
from jax.experimental.pallas import tpu as pltpu

F32 = jnp.float32
BF16 = jnp.bfloat16

D_MODEL = 1024
ATT_HEADS = 8
ATT_KV_HEADS = 2
ATT_GROUP = ATT_HEADS // ATT_KV_HEADS
ATT_HEAD_DIM = 64
ATT_WIDTH = ATT_HEADS * ATT_HEAD_DIM
WINDOW = 128
BLOCK = WINDOW
N_BUCKETS = 32
MAX_DISTANCE = 128
ML_HEADS = 4
ML_HEAD_DIM = 128
ML_WIDTH = ML_HEADS * ML_HEAD_DIM
ML_CHUNK = 256
CONV_WIDTH = 5
N_GATES = 4 * ML_HEADS
MEM_TOKENS = 256
MEM_HEADS = 4
MEM_HEAD_DIM = D_MODEL // MEM_HEADS
N_EXPERTS = 32
TOP_K = 4
D_FF = D_MODEL
SWIGLU_LIMIT = 7.0
SWIGLU_ALPHA = 1.702
LN_EPS = 1e-5
DN_ALPHA = 2.0 ** 0.25

QK_WIDTH = 2 * ML_WIDTH
Z_WIDTH = ML_WIDTH + ML_WIDTH + ATT_WIDTH + 2 * ATT_KV_HEADS * ATT_HEAD_DIM
ZB_MLV = 0
ZB_MLO = 1
ZB_ATTQ = 2
ZB_ATTK = 12
ZB_ATTV = 13

ROW_TILE = 512
MOE_ROWS = 512
NEG = -1e30
VMEM_LIMIT = 56 * 1024 * 1024


def _cparams(sem):
    return pltpu.CompilerParams(dimension_semantics=sem, vmem_limit_bytes=VMEM_LIMIT)


def _dot(a, b):
    return jnp.dot(a, b, preferred_element_type=F32)


def _dot_nt(a, b):
    return lax.dot_general(a, b, (((1,), (1,)), ((), ())), preferred_element_type=F32)


def _dot_tn(a, b):
    return lax.dot_general(a, b, (((0,), (0,)), ((), ())), preferred_element_type=F32)


def _split3(x):
    hi = x.astype(BF16)
    rest = x - hi.astype(F32)
    mid = rest.astype(BF16)
    lo = (rest - mid.astype(F32)).astype(BF16)
    return hi, mid, lo


def _layer_norm(y, g, b):
    mu = jnp.mean(y, axis=-1, keepdims=True)
    yc = y - mu
    var = jnp.mean(yc * yc, axis=-1, keepdims=True)
    return yc * lax.rsqrt(var + LN_EPS) * g + b


def _log_sigmoid(x):
    return jnp.minimum(x, 0.0) - jnp.log1p(jnp.exp(-jnp.abs(x)))


def _sigmoid(x):
    return 1.0 / (1.0 + jnp.exp(-x))


def _seq_pos(blk, n_p_blocks, p_blocks_per_seq, s_blocks_per_seq):
    in_p = blk < n_p_blocks
    local = jnp.where(in_p, blk % p_blocks_per_seq, (blk - n_p_blocks) % s_blocks_per_seq)
    per = jnp.where(in_p, p_blocks_per_seq, s_blocks_per_seq)
    return local == 0, local == per - 1


HALO = 8


def _in_proj_kernel(seq, xp_ref, xpl_ref, xpr_ref, xs_ref, xsl_ref, xsr_ref, wqk_ref, w_ref, wg_ref, wgt_ref,
                    bg_ref, bgt_ref, cw_ref, cb_ref, scale_ref, qk_ref, z_ref, gc_ref, gr_ref, buf_ref):
    i = pl.program_id(0)
    in_p = i < seq[0]
    first, last = _seq_pos(i, *seq)
    x = jnp.where(in_p, xp_ref[...], xs_ref[...])
    left = jnp.where(in_p, xpl_ref[...], xsl_ref[...])
    right = jnp.where(in_p, xpr_ref[...], xsr_ref[...])
    xb = x.astype(BF16)
    x_ext = jnp.concatenate([left, x, right], axis=0).astype(BF16)
    row = lax.broadcasted_iota(jnp.int32, (ROW_TILE + 2 * HALO, 1), 0)
    outside = (first & (row < HALO)) | (last & (row >= ROW_TILE + HALO))
    buf_ref[...] = jnp.where(outside, 0.0, _dot(x_ext, wqk_ref[...]))
    acc = jnp.zeros((ROW_TILE, QK_WIDTH), F32) + cb_ref[...]
    for j in range(CONV_WIDTH):
        off = HALO + j - CONV_WIDTH // 2
        acc = acc + buf_ref[off:off + ROW_TILE, :] * cw_ref[j:j + 1, :]
    qk_ref[...] = (acc * _sigmoid(acc) * scale_ref[...]).astype(BF16)
    z_ref[...] = _dot(xb, w_ref[...]).astype(BF16)
    gc_ref[...] = _dot(xb, wg_ref[...]) + bg_ref[...]
    gr_ref[...] = _dot_nt(wgt_ref[...], xb) + bgt_ref[...]


def _in_proj(xp, xs, w_qk, w_main, w_g, b_g, conv_w, conv_b, seq):
    tp, ts = xp.shape[0], xs.shape[0]
    t = tp + ts
    n_p = tp // ROW_TILE
    n = t // ROW_TILE
    r = ROW_TILE // HALO
    const = lambda i: (0, 0)
    p_tile = lambda i: jnp.minimum(i, n_p - 1)
    s_tile = lambda i: jnp.maximum(i - n_p, 0)
    scale = jnp.concatenate([jnp.ones((1, ML_WIDTH), F32), jnp.full((1, ML_WIDTH), ML_HEAD_DIM ** -0.5, F32)], axis=1)
    return pl.pallas_call(
        functools.partial(_in_proj_kernel, seq),
        grid=(n,),
        in_specs=[
            pl.BlockSpec((ROW_TILE, D_MODEL), lambda i: (p_tile(i), 0)),
            pl.BlockSpec((HALO, D_MODEL), lambda i: (jnp.maximum(p_tile(i) * r - 1, 0), 0)),
            pl.BlockSpec((HALO, D_MODEL), lambda i: (jnp.minimum((p_tile(i) + 1) * r, tp // HALO - 1), 0)),
            pl.BlockSpec((ROW_TILE, D_MODEL), lambda i: (s_tile(i), 0)),
            pl.BlockSpec((HALO, D_MODEL), lambda i: (jnp.maximum(s_tile(i) * r - 1, 0), 0)),
            pl.BlockSpec((HALO, D_MODEL), lambda i: (jnp.minimum((s_tile(i) + 1) * r, ts // HALO - 1), 0)),
            pl.BlockSpec((D_MODEL, QK_WIDTH), const),
            pl.BlockSpec((D_MODEL, Z_WIDTH), const),
            pl.BlockSpec((D_MODEL, N_GATES), const),
            pl.BlockSpec((N_GATES, D_MODEL), const),
            pl.BlockSpec((1, N_GATES), const),
            pl.BlockSpec((N_GATES, 1), const),
            pl.BlockSpec((CONV_WIDTH, QK_WIDTH), const),
            pl.BlockSpec((1, QK_WIDTH), const),
            pl.BlockSpec((1, QK_WIDTH), const),
        ],
        out_specs=[
            pl.BlockSpec((ROW_TILE, QK_WIDTH), lambda i: (i, 0)),
            pl.BlockSpec((ROW_TILE, Z_WIDTH), lambda i: (i, 0)),
            pl.BlockSpec((ROW_TILE, N_GATES), lambda i: (i, 0)),
            pl.BlockSpec((N_GATES, ROW_TILE), lambda i: (0, i)),
        ],
        out_shape=[
            jax.ShapeDtypeStruct((t, QK_WIDTH), BF16),
            jax.ShapeDtypeStruct((t, Z_WIDTH), BF16),
            jax.ShapeDtypeStruct((t, N_GATES), F32),
            jax.ShapeDtypeStruct((N_GATES, t), F32),
        ],
        scratch_shapes=[pltpu.VMEM((ROW_TILE + 2 * HALO, QK_WIDTH), F32)],
        compiler_params=_cparams(("arbitrary",)),
        name="in_proj_conv",
    )(xp, xp, xp, xs, xs, xs, w_qk, w_main, w_g, w_g.T, b_g.reshape(1, N_GATES), b_g.reshape(N_GATES, 1),
      conv_w, conv_b.reshape(1, QK_WIDTH), scale)


ATT_TILE = 512
ATT_SUB = ATT_TILE // BLOCK


ONES_ROWS = 16


def _attention_kernel(seq, q_ref, kp_ref, kc_ref, kn_ref, vp_ref, vc_ref, vn_ref, bias_ref, sink_ref,
                      o_ref, klo_ref, khi_ref, vt_ref):
    i = pl.program_id(0)
    lane = lax.broadcasted_iota(jnp.int32, (ATT_TILE + 2 * BLOCK, 2 * ATT_HEAD_DIM), 1)
    kband = jnp.concatenate([kp_ref[...], kc_ref[...], kn_ref[...]], axis=0)
    zero = jnp.zeros_like(kband)
    klo_ref[...] = jnp.where(lane < ATT_HEAD_DIM, kband, zero)
    khi_ref[...] = jnp.where(lane < ATT_HEAD_DIM, zero, kband)
    vband = jnp.concatenate([vp_ref[...], vc_ref[...], vn_ref[...]], axis=0)
    vt_ref[0:2 * ATT_HEAD_DIM, :] = vband.T
    vt_ref[2 * ATT_HEAD_DIM:, :] = jnp.ones((ONES_ROWS, ATT_TILE + 2 * BLOCK), BF16)
    feat = lax.broadcasted_iota(jnp.int32, (2 * ATT_HEAD_DIM, ATT_GROUP * BLOCK), 0)
    def block(s):
        first, last = _seq_pos(i * ATT_SUB + s, *seq)
        variant = jnp.where(first, 1, jnp.where(last, 2, 0))
        q = q_ref[s * BLOCK:(s + 1) * BLOCK, :]
        q_all = jnp.concatenate([q[:, t * 128:(t + 1) * 128] for t in range(ATT_GROUP)], axis=0)
        vt = vt_ref[:, s * BLOCK:(s + 3) * BLOCK]
        scores = [_dot_nt(k_ref[s * BLOCK:(s + 3) * BLOCK, :], q_all) for k_ref in (klo_ref, khi_ref)]
        yield
        ms, ps = [], []
        for kv in range(ATT_KV_HEADS):
            logits = scores[kv] + bias_ref[variant, kv]
            ms.append(jnp.maximum(jnp.max(logits, axis=0, keepdims=True), sink_ref[kv]))
            ps.append(jnp.exp(logits - ms[kv]).astype(BF16))
        yield
        ovs = [_dot(vt, p) for p in ps]
        yield
        outs = []
        for kv in range(ATT_KV_HEADS):
            den = ovs[kv][2 * ATT_HEAD_DIM:2 * ATT_HEAD_DIM + 1, :] + jnp.exp(sink_ref[kv] - ms[kv])
            outs.append(ovs[kv][0:2 * ATT_HEAD_DIM, :] * (1.0 / den))
        both = jnp.where(feat < ATT_HEAD_DIM, outs[0], outs[1]).astype(BF16)
        for t in range(ATT_GROUP):
            o_ref[t * 128:(t + 1) * 128, s * BLOCK:(s + 1) * BLOCK] = both[:, t * BLOCK:(t + 1) * BLOCK]

    blocks = [block(s) for s in range(ATT_SUB)]
    while blocks:
        blocks = [b for b in blocks if next(b, "done") != "done"]


def _attention(z, bias, sink, seq):
    t = z.shape[0]
    n = t // ATT_TILE
    nblk = t // BLOCK
    band = ATT_TILE + 2 * BLOCK
    prev = lambda i: jnp.maximum(i * ATT_SUB - 1, 0)
    nxt = lambda i: jnp.minimum((i + 1) * ATT_SUB, nblk - 1)
    return pl.pallas_call(
        functools.partial(_attention_kernel, seq),
        grid=(n,),
        in_specs=[
            pl.BlockSpec((ATT_TILE, ATT_WIDTH), lambda i: (i, ZB_ATTQ)),
            pl.BlockSpec((BLOCK, 128), lambda i: (prev(i), ZB_ATTK)),
            pl.BlockSpec((ATT_TILE, 128), lambda i: (i, ZB_ATTK)),
            pl.BlockSpec((BLOCK, 128), lambda i: (nxt(i), ZB_ATTK)),
            pl.BlockSpec((BLOCK, 128), lambda i: (prev(i), ZB_ATTV)),
            pl.BlockSpec((ATT_TILE, 128), lambda i: (i, ZB_ATTV)),
            pl.BlockSpec((BLOCK, 128), lambda i: (nxt(i), ZB_ATTV)),
            pl.BlockSpec((3, ATT_KV_HEADS, 3 * BLOCK, ATT_GROUP * BLOCK), lambda i: (0, 0, 0, 0)),
            pl.BlockSpec((ATT_KV_HEADS, 1, ATT_GROUP * BLOCK), lambda i: (0, 0, 0)),
        ],
        out_specs=pl.BlockSpec((ATT_WIDTH, ATT_TILE), lambda i: (0, i)),
        out_shape=jax.ShapeDtypeStruct((ATT_WIDTH, t), BF16),
        scratch_shapes=[pltpu.VMEM((band, 128), BF16), pltpu.VMEM((band, 128), BF16),
                        pltpu.VMEM((2 * ATT_HEAD_DIM + ONES_ROWS, band), BF16)],
        compiler_params=_cparams(("arbitrary",)),
        name="window_attention",
    )(z, z, z, z, z, z, z, bias, sink)


def _t5_bucket(rel):
    half = N_BUCKETS // 2
    exact = half // 2
    n = np.abs(rel)
    large = exact + (np.log(np.maximum(n, 1) / exact) / np.log(MAX_DISTANCE / exact) * (half - exact)).astype(np.int32)
    large = np.minimum(large, half - 1)
    return ((rel > 0).astype(np.int32) * half + np.where(n < exact, n, large)).astype(np.int32)


def _attention_tables(rel_bias, attn_sink):
    rel = np.arange(3 * BLOCK)[:, None] - BLOCK - np.arange(BLOCK)[None, :]
    onehot = jnp.asarray(_t5_bucket(rel)[..., None] == np.arange(N_BUCKETS), F32)
    bias = jnp.einsum('kqb,bh->hkq', onehot, rel_bias.astype(F32), precision=lax.Precision.HIGHEST)
    bias = jnp.where(jnp.asarray(np.abs(rel) <= WINDOW)[None], bias, NEG)
    bias = bias.reshape(ATT_KV_HEADS, ATT_GROUP, 3 * BLOCK, BLOCK).transpose(0, 2, 1, 3)
    bias = bias.reshape(ATT_KV_HEADS, 3 * BLOCK, ATT_GROUP * BLOCK)
    key = np.arange(3 * BLOCK)[None, :, None]
    first = jnp.where(jnp.asarray(key < BLOCK), NEG, bias)
    last = jnp.where(jnp.asarray(key >= 2 * BLOCK), NEG, bias)
    sink = jnp.repeat(attn_sink.astype(F32), BLOCK).reshape(ATT_KV_HEADS, 1, ATT_GROUP * BLOCK)
    return jnp.stack([bias, first, last]), sink


def _mlstm_kernel(seq, n_chunks, *refs):
    fwd_in, bwd_in, (of_ref, ob_ref), state = refs[0:5], refs[5:10], refs[10:12], refs[12:]
    chains = (_mlstm_chains(seq, False, n_chunks, *fwd_in, of_ref, *state[0:2])
              + _mlstm_chains(seq, True, n_chunks, *bwd_in, ob_ref, *state[2:4]))
    while chains:
        chains = [c for c in chains if next(c, "done") != "done"]


def _mlstm_chains(seq, reverse, n_chunks, q_ref, k_ref, v_ref, gr_ref, gc_ref, o_ref, ct_ref, m_ref):
    step = pl.program_id(0)
    chunk = (n_chunks - 1 - step) if reverse else step
    first, last = _seq_pos(chunk, *seq)
    fresh = last if reverse else first

    L = ML_CHUNK
    row = lax.broadcasted_iota(jnp.int32, (L, L), 0)
    col = lax.broadcasted_iota(jnp.int32, (L, L), 1)
    vis = (row >= col) if reverse else (row <= col)
    vis_t = (col >= row) if reverse else (col <= row)
    gr = gr_ref[...]
    gc = gc_ref[...]
    r3 = _dot(jnp.concatenate(_split3(_log_sigmoid(gr)), axis=0), vis.astype(BF16))
    b_rows = r3[0:N_GATES] + r3[N_GATES:2 * N_GATES] + r3[2 * N_GATES:]
    vis_tb = vis_t.astype(BF16)
    c_hi, c_mid, c_lo = _split3(_log_sigmoid(gc))
    b_cols = _dot(vis_tb, c_hi) + _dot(vis_tb, c_mid) + _dot(vis_tb, c_lo)
    i_off = 2 * ML_HEADS if reverse else 0
    f_off = i_off + ML_HEADS
    end = 0 if reverse else L - 1
    ones = jnp.ones((ONES_ROWS, L), BF16)

    def chain(h):
        hs = slice(h * ML_HEAD_DIM, (h + 1) * ML_HEAD_DIM)
        q = q_ref[:, hs]
        k = k_ref[:, hs]
        vt1 = jnp.concatenate([v_ref[:, hs].T, ones], axis=0)
        b_row = b_rows[f_off + h:f_off + h + 1, :]
        u_row = gr[i_off + h:i_off + h + 1, :] - b_row
        u_col = gc[:, i_off + h:i_off + h + 1] - b_cols[:, f_off + h:f_off + h + 1]
        g = b_row[:, end:end + 1]
        ct_old = jnp.where(fresh, 0.0, ct_ref[h])
        m_old = jnp.where(fresh, 0.0, m_ref[h])[:, 0:1]
        kq = _dot_nt(k, q)
        from_state = _dot_nt(ct_old.astype(BF16), q)
        yield
        u_mat = jnp.where(vis, u_col, NEG)
        mm = jnp.maximum(jnp.max(u_mat, axis=0, keepdims=True), m_old)
        st = (kq * jnp.exp(u_mat - mm)).astype(BF16)
        yield
        tot = _dot(vt1, st) + jnp.exp(m_old - mm) * from_state
        a_max = jnp.max(g + u_row, axis=-1, keepdims=True)
        m_new = jnp.maximum(g + m_old, a_max)
        s_old = jnp.exp(g + m_old - m_new)
        weighted = (vt1.astype(F32) * jnp.exp(g + u_row - m_new)).astype(BF16)
        new_state = _dot(weighted, k)
        yield
        den = tot[ML_HEAD_DIM:ML_HEAD_DIM + 1, :]
        floor = jnp.exp(-(b_row + mm))
        o_ref[hs, :] = (tot[0:ML_HEAD_DIM, :] * (1.0 / jnp.maximum(jnp.abs(den), floor))).astype(BF16)
        ct_ref[h] = s_old * ct_old + new_state
        m_ref[h] = jnp.broadcast_to(m_new, (1, ML_HEAD_DIM))

    return [chain(h) for h in range(ML_HEADS)]


def _mlstm(qk, z, gates_r, gates_c, seq):
    t = qk.shape[0]
    nc = t // ML_CHUNK

    def chunk_specs(ch):
        return [
            pl.BlockSpec((ML_CHUNK, ML_WIDTH), lambda i: (ch(i), 0)),
            pl.BlockSpec((ML_CHUNK, ML_WIDTH), lambda i: (ch(i), 1)),
            pl.BlockSpec((ML_CHUNK, ML_WIDTH), lambda i: (ch(i), ZB_MLV)),
            pl.BlockSpec((N_GATES, ML_CHUNK), lambda i: (0, ch(i))),
            pl.BlockSpec((ML_CHUNK, N_GATES), lambda i: (ch(i), 0)),
        ]

    fwd = lambda i: i
    bwd = lambda i: nc - 1 - i
    state = [
        pltpu.VMEM((ML_HEADS, ML_HEAD_DIM + ONES_ROWS, ML_HEAD_DIM), F32),
        pltpu.VMEM((ML_HEADS, 1, ML_HEAD_DIM), F32),
    ]
    operands = (qk, qk, z, gates_r, gates_c)
    return pl.pallas_call(
        functools.partial(_mlstm_kernel, seq, nc),
        grid=(nc,),
        in_specs=chunk_specs(fwd) + chunk_specs(bwd),
        out_specs=[pl.BlockSpec((ML_WIDTH, ML_CHUNK), lambda i: (0, fwd(i))),
                   pl.BlockSpec((ML_WIDTH, ML_CHUNK), lambda i: (0, bwd(i)))],
        out_shape=[jax.ShapeDtypeStruct((ML_WIDTH, t), BF16), jax.ShapeDtypeStruct((ML_WIDTH, t), BF16)],
        scratch_shapes=state + state,
        compiler_params=_cparams(("arbitrary",)),
        name="mlstm",
    )(*operands, *operands)


def _mix_out(n_p_tiles, xp_ref, xs_ref, att_ref, hf_ref, hb_ref, og_ref, wa_ref, wm_ref, gain_ref, g_ref, b_ref):
    i = pl.program_id(0)
    x = jnp.where(i < n_p_tiles, xp_ref[...], xs_ref[...])
    h = hf_ref[...].astype(F32) + hb_ref[...].astype(F32)
    parts = []
    for hd in range(ML_HEADS):
        hh = h[hd * ML_HEAD_DIM:(hd + 1) * ML_HEAD_DIM, :]
        mu = jnp.mean(hh, axis=0, keepdims=True)
        hc = hh - mu
        var = jnp.mean(hc * hc, axis=0, keepdims=True)
        parts.append(hc * lax.rsqrt(var + LN_EPS))
    hn = jnp.concatenate(parts, axis=0) * gain_ref[...] * _sigmoid(og_ref[...].T.astype(F32))
    mixed = _dot(att_ref[...].T, wa_ref[...]) + _dot(hn.astype(BF16).T, wm_ref[...])
    return _layer_norm(DN_ALPHA * x + mixed, g_ref[...], b_ref[...])


def _mem_kv_kernel(m_ref, w_ref, o_ref):
    o_ref[...] = _dot(m_ref[...].astype(BF16), w_ref[...]).astype(BF16)


def _mem_kv(mem, wkv):
    rows = mem.shape[0]
    return pl.pallas_call(
        _mem_kv_kernel,
        grid=(rows // MEM_TOKENS,),
        in_specs=[pl.BlockSpec((MEM_TOKENS, D_MODEL), lambda i: (i, 0)),
                  pl.BlockSpec((D_MODEL, 2 * D_MODEL), lambda i: (0, 0))],
        out_specs=pl.BlockSpec((MEM_TOKENS, 2 * D_MODEL), lambda i: (i, 0)),
        out_shape=jax.ShapeDtypeStruct((rows, 2 * D_MODEL), BF16),
        compiler_params=_cparams(("arbitrary",)),
        name="mem_kv",
    )(mem, wkv)


def _pack_bf16_pairs(a, b):
    ua = pltpu.bitcast(a.astype(BF16).astype(F32), jnp.uint32)
    ub = pltpu.bitcast(b.astype(BF16).astype(F32), jnp.uint32)
    return (ua >> 16) | (ub & jnp.uint32(0xFFFF0000))


def _unpack_bf16_pairs(u):
    lo = pltpu.bitcast(u << 16, F32)
    hi = pltpu.bitcast(u & jnp.uint32(0xFFFF0000), F32)
    return lo, hi


def _cross_router_kernel(n_p_tiles, *refs):
    mix_refs, (kv_ref, wq_ref, wo_ref, g_ref, b_ref, wr_ref, br_ref, tri_ref, etri_ref,
               x2_ref, x2b_ref, gate_ref, slot_ref, chunk_ref) = refs[:11], refs[11:]
    x = _mix_out(n_p_tiles, *mix_refs)
    q = _dot(x.astype(BF16), wq_ref[...]).astype(BF16)
    head_cols = [slice(h * MEM_HEAD_DIM, (h + 1) * MEM_HEAD_DIM) for h in range(MEM_HEADS)]
    scores = [_dot_nt(q[:, hs], kv_ref[:, hs]) for hs in head_cols]
    probs, dens = [], []
    for logits in scores:
        p = jnp.exp(logits - jnp.max(logits, axis=-1, keepdims=True))
        dens.append(jnp.sum(p, axis=-1, keepdims=True))
        probs.append(p.astype(BF16))
    values = [_dot(p, kv_ref[:, D_MODEL + hs.start:D_MODEL + hs.stop]) for p, hs in zip(probs, head_cols)]
    o = jnp.concatenate([(v * (1.0 / den)).astype(BF16) for v, den in zip(values, dens)], axis=1)
    x2 = _layer_norm(DN_ALPHA * x + _dot(o, wo_ref[...]), g_ref[...], b_ref[...])
    x2_ref[...] = x2
    x2b = x2.astype(BF16)
    x2b_ref[...] = x2b

    logits = _dot_nt(wr_ref[...], x2b) + br_ref[...]
    expert = lax.broadcasted_iota(jnp.int32, logits.shape, 0)
    work = logits
    vals, sels = [], []
    for k in range(TOP_K):
        mx = jnp.max(work, axis=0, keepdims=True)
        ix = jnp.min(jnp.where(work == mx, expert, N_EXPERTS), axis=0, keepdims=True)
        sel = expert == ix
        work = jnp.where(sel, -jnp.inf, work)
        vals.append(mx)
        sels.append(sel)
    es = [jnp.exp(v - vals[0]) for v in vals]
    tot = es[0] + es[1] + es[2] + es[3]
    chosen = jnp.zeros(logits.shape, F32)
    for k in range(TOP_K):
        gate_ref[k:k + 1, :] = es[k] / tot
        chosen = chosen + sels[k].astype(F32)
    count = jnp.sum(chosen, axis=1, keepdims=True)
    chunks = jnp.floor((count + (SEG_ALIGN - 1)) * (1.0 / SEG_ALIGN))
    chunks_b = jnp.broadcast_to(chunks, (N_EXPERTS, 128))
    seg_first = _dot(etri_ref[...], chunks_b.astype(BF16))[:, 0:1] * SEG_ALIGN
    before = _dot(chosen.astype(BF16), tri_ref[...])
    local_row = before + seg_first
    for k in range(TOP_K):
        slot_ref[k:k + 1, :] = jnp.sum(jnp.where(sels[k], local_row, 0.0), axis=0, keepdims=True).astype(jnp.int32)
    chunk_ref[0] = chunks_b.astype(jnp.int32)


def _cross_router(xp, xs, att, hf, hb, z, w_att, w_ml, gain, g1, b1, kv, wq, wo, g, b, w_router, b_router,
                  mem_of_tile):
    t = att.shape[1]
    n_p = xp.shape[0] // ROW_TILE
    n = t // ROW_TILE
    const = lambda i: (0, 0)
    tile = lambda i: (i, 0)
    lanes = lambda i: (0, i)
    tri = jnp.asarray(np.triu(np.ones((ROW_TILE, ROW_TILE), np.float32), 1), BF16)
    etri = jnp.asarray(np.tril(np.ones((N_EXPERTS, N_EXPERTS), np.float32), -1), BF16)
    return pl.pallas_call(
        functools.partial(_cross_router_kernel, n_p),
        grid=(n,),
        in_specs=[
            pl.BlockSpec((ROW_TILE, D_MODEL), lambda i: (jnp.minimum(i, n_p - 1), 0)),
            pl.BlockSpec((ROW_TILE, D_MODEL), lambda i: (jnp.maximum(i - n_p, 0), 0)),
            pl.BlockSpec((ATT_WIDTH, ROW_TILE), lanes),
            pl.BlockSpec((ML_WIDTH, ROW_TILE), lanes),
            pl.BlockSpec((ML_WIDTH, ROW_TILE), lanes),
            pl.BlockSpec((ROW_TILE, ML_WIDTH), lambda i: (i, ZB_MLO)),
            pl.BlockSpec((ATT_WIDTH, D_MODEL), const),
            pl.BlockSpec((ML_WIDTH, D_MODEL), const),
            pl.BlockSpec((ML_WIDTH, 1), const),
            pl.BlockSpec((1, D_MODEL), const),
            pl.BlockSpec((1, D_MODEL), const),
            pl.BlockSpec((MEM_TOKENS, 2 * D_MODEL), lambda i: (mem_of_tile(i), 0)),
            pl.BlockSpec((D_MODEL, D_MODEL), const),
            pl.BlockSpec((D_MODEL, D_MODEL), const),
            pl.BlockSpec((1, D_MODEL), const),
            pl.BlockSpec((1, D_MODEL), const),
            pl.BlockSpec((N_EXPERTS, D_MODEL), const),
            pl.BlockSpec((N_EXPERTS, 1), const),
            pl.BlockSpec((ROW_TILE, ROW_TILE), const),
            pl.BlockSpec((N_EXPERTS, N_EXPERTS), const),
        ],
        out_specs=[
            pl.BlockSpec((ROW_TILE, D_MODEL), tile),
            pl.BlockSpec((ROW_TILE, D_MODEL), tile),
            pl.BlockSpec((TOP_K, ROW_TILE), lanes),
            pl.BlockSpec((TOP_K, ROW_TILE), lanes),
            pl.BlockSpec((1, N_EXPERTS, 128), lambda i: (i, 0, 0)),
        ],
        out_shape=[
            jax.ShapeDtypeStruct((t, D_MODEL), F32),
            jax.ShapeDtypeStruct((t, D_MODEL), BF16),
            jax.ShapeDtypeStruct((TOP_K, t), F32),
            jax.ShapeDtypeStruct((TOP_K, t), jnp.int32),
            jax.ShapeDtypeStruct((n, N_EXPERTS, 128), jnp.int32),
        ],
        compiler_params=_cparams(("arbitrary",)),
        name="mix_out_cross_attn_router",
    )(xp, xs, att, hf, hb, z, w_att, w_ml, gain.reshape(ML_WIDTH, 1), g1.reshape(1, D_MODEL), b1.reshape(1, D_MODEL),
      kv, wq, wo, g.reshape(1, D_MODEL), b.reshape(1, D_MODEL), w_router.T.astype(BF16),
      b_router.reshape(N_EXPERTS, 1), tri, etri)


SEG_ALIGN = 8
LOCAL_ROWS = ROW_TILE * TOP_K + N_EXPERTS * SEG_ALIGN
HALF = D_MODEL // 2


def _rows(chunks):
    return pl.multiple_of(chunks * SEG_ALIGN, SEG_ALIGN)


def _segment_copies(chunks_ref, seg_ref, loc_ref, tile, make_copy):
    def per_expert(e, carry):
        entry = tile * N_EXPERTS + e

        @pl.when(chunks_ref[entry] > 0)
        def _():
            make_copy(_rows(loc_ref[entry]), _rows(seg_ref[entry]), _rows(chunks_ref[entry])).start()

        return carry

    lax.fori_loop(0, N_EXPERTS, per_expert, 0)


def _wait_copies(chunks, make_copy):
    @pl.when(chunks > 0)
    def _():
        make_copy(0, 0, _rows(chunks)).wait()


def _dispatch_kernel(chunks_ref, seg_ref, loc_ref, total_ref, tail_ref, tailn_ref, nact_ref,
                     x_ref, slot_ref, rows_hbm, local_ref, zero_ref, sems):
    i = pl.program_id(0)
    n = pl.num_programs(0)
    buf = i % 2

    def copy_out(b):
        return lambda loc, seg, rows: pltpu.make_async_copy(
            local_ref.at[b, pl.ds(loc, rows), :], rows_hbm.at[pl.ds(seg, rows), :], sems.at[b])

    def run(b):
        @pl.when(i >= 2)
        def _():
            _wait_copies(total_ref[jnp.maximum(i - 2, 0)], copy_out(b))

        slots = slot_ref[...]
        row = lax.broadcasted_iota(jnp.int32, (LOCAL_ROWS, ROW_TILE), 0)
        hit = slots[0:1, :] == row
        for k in range(1, TOP_K):
            hit = hit | (slots[k:k + 1, :] == row)
        perm = jnp.where(hit, 1.0, 0.0).astype(BF16)
        rows = _dot(perm, x_ref[...])
        local_ref[b] = _pack_bf16_pairs(rows[:, :HALF], rows[:, HALF:])
        _segment_copies(chunks_ref, seg_ref, loc_ref, i, copy_out(b))

    for b in range(2):
        pl.when(buf == b)(functools.partial(run, b))

    @pl.when(i == n - 1)
    def _():
        for b in range(2):
            step = jnp.where(buf == b, i, i - 1)
            _wait_copies(total_ref[step], copy_out(b))
        zero_ref[...] = jnp.zeros_like(zero_ref)
        fill = lambda _, seg, rows: pltpu.make_async_copy(
            zero_ref.at[pl.ds(0, rows), :], rows_hbm.at[pl.ds(seg, rows), :], sems.at[2])

        def per_expert(e, carry):
            @pl.when(tailn_ref[e] > 0)
            def _():
                fill(0, _rows(tail_ref[e]), _rows(tailn_ref[e])).start()

            _wait_copies(tailn_ref[e], fill)
            return carry

        lax.fori_loop(0, N_EXPERTS, per_expert, 0)
        fill_block = lambda blk: pltpu.make_async_copy(
            zero_ref, rows_hbm.at[pl.ds(pl.multiple_of(blk * MOE_ROWS, MOE_ROWS), MOE_ROWS), :], sems.at[2])
        n_blocks = rows_hbm.shape[0] // MOE_ROWS

        def start_block(blk, carry):
            fill_block(blk).start()
            return carry

        def wait_block(blk, carry):
            fill_block(0).wait()
            return carry

        lax.fori_loop(nact_ref[0], n_blocks, start_block, 0)
        lax.fori_loop(nact_ref[0], n_blocks, wait_block, 0)


def _dispatch(x2b, slots, tables, n_rows):
    n = x2b.shape[0] // ROW_TILE
    assert n >= 2
    return pl.pallas_call(
        _dispatch_kernel,
        grid_spec=pltpu.PrefetchScalarGridSpec(
            num_scalar_prefetch=7,
            grid=(n,),
            in_specs=[
                pl.BlockSpec((ROW_TILE, D_MODEL), lambda i, *_: (i, 0)),
                pl.BlockSpec((TOP_K, ROW_TILE), lambda i, *_: (0, i)),
            ],
            out_specs=pl.BlockSpec(memory_space=pl.ANY),
            scratch_shapes=[
                pltpu.VMEM((2, LOCAL_ROWS, HALF), jnp.uint32),
                pltpu.VMEM((MOE_ROWS, HALF), jnp.uint32),
                pltpu.SemaphoreType.DMA((3,)),
            ],
        ),
        out_shape=jax.ShapeDtypeStruct((n_rows, HALF), jnp.uint32),
        compiler_params=_cparams(("arbitrary",)),
        name="moe_dispatch",
    )(*tables, x2b, slots)


def _expert_kernel(be_ref, na_ref, x_ref, wu_ref, bu_ref, wd_ref, bd_ref, o_ref, wub_ref, wdb_ref):
    i = pl.program_id(0)

    @pl.when((i == 0) | (be_ref[i] != be_ref[jnp.maximum(i - 1, 0)]))
    def _():
        wub_ref[...] = wu_ref[0].astype(BF16)
        wdb_ref[...] = wd_ref[0].astype(BF16)

    @pl.when(i < na_ref[0])
    def _():
        lo, hi = _unpack_bf16_pairs(x_ref[...])
        x = jnp.concatenate([lo, hi], axis=1).astype(BF16)
        hu = _dot(x, wub_ref[...]) + bu_ref[0]
        h_glu = jnp.minimum(hu[:, :D_FF], SWIGLU_LIMIT)
        h_lin = jnp.clip(hu[:, D_FF:], -SWIGLU_LIMIT, SWIGLU_LIMIT)
        hh = h_glu * _sigmoid(SWIGLU_ALPHA * h_glu) * (h_lin + 1.0)
        y = _dot(hh.astype(BF16), wdb_ref[...]) + bd_ref[0]
        o_ref[...] = _pack_bf16_pairs(y[:, :D_MODEL // 2], y[:, D_MODEL // 2:])

    @pl.when(i >= na_ref[0])
    def _():
        o_ref[...] = jnp.zeros_like(o_ref)


def _experts(xs, block_e, n_active, w_up, b_up, w_down, b_down):
    n_rows = xs.shape[0]
    n_blk = n_rows // MOE_ROWS
    return pl.pallas_call(
        _expert_kernel,
        grid_spec=pltpu.PrefetchScalarGridSpec(
            num_scalar_prefetch=2,
            grid=(n_blk,),
            in_specs=[
                pl.BlockSpec((MOE_ROWS, D_MODEL // 2), lambda i, be, na: (jnp.minimum(i, na[0] - 1), 0)),
                pl.BlockSpec((1, D_MODEL, 2 * D_FF), lambda i, be, na: (be[i], 0, 0)),
                pl.BlockSpec((1, 1, 2 * D_FF), lambda i, be, na: (be[i], 0, 0)),
                pl.BlockSpec((1, D_FF, D_MODEL), lambda i, be, na: (be[i], 0, 0)),
                pl.BlockSpec((1, 1, D_MODEL), lambda i, be, na: (be[i], 0, 0)),
            ],
            out_specs=pl.BlockSpec((MOE_ROWS, D_MODEL // 2), lambda i, be, na: (i, 0)),
            scratch_shapes=[pltpu.VMEM((D_MODEL, 2 * D_FF), BF16), pltpu.VMEM((D_FF, D_MODEL), BF16)],
        ),
        out_shape=jax.ShapeDtypeStruct((n_rows, D_MODEL // 2), jnp.uint32),
        compiler_params=_cparams(("arbitrary",)),
        name="experts",
    )(block_e, n_active, xs, w_up, b_up.reshape(N_EXPERTS, 1, 2 * D_FF), w_down, b_down.reshape(N_EXPERTS, 1, D_MODEL))


def _combine_kernel(first_tile, chunks_ref, seg_ref, loc_ref, total_ref,
                    x_ref, rows_hbm, slot_ref, gate_ref, g_ref, b_ref, o_ref, local_ref, sems):
    i = pl.program_id(0)
    n = pl.num_programs(0)
    tile = first_tile + i
    buf = i % 2

    def copy_in(b):
        return lambda loc, seg, rows: pltpu.make_async_copy(
            rows_hbm.at[pl.ds(seg, rows), :], local_ref.at[b, pl.ds(loc, rows), :], sems.at[b])

    @pl.when(i == 0)
    def _():
        local_ref[...] = jnp.zeros_like(local_ref)
        _segment_copies(chunks_ref, seg_ref, loc_ref, tile, copy_in(0))

    def run(b):
        @pl.when(i + 1 < n)
        def _():
            _segment_copies(chunks_ref, seg_ref, loc_ref, tile + 1, copy_in(1 - b))

        _wait_copies(total_ref[tile], copy_in(b))
        lo, hi = _unpack_bf16_pairs(local_ref[b])
        rows = jnp.concatenate([lo, hi], axis=1).astype(BF16)
        slots = slot_ref[...]
        gates = gate_ref[...]
        row = lax.broadcasted_iota(jnp.int32, (ROW_TILE, LOCAL_ROWS), 1)
        weight = jnp.zeros((ROW_TILE, LOCAL_ROWS), F32)
        for k in range(TOP_K):
            weight = weight + jnp.where(slots[:, k:k + 1] == row, gates[:, k:k + 1], 0.0)
        y = _dot(weight.astype(BF16), rows)
        o_ref[...] = _layer_norm(DN_ALPHA * x_ref[...] + y, g_ref[...], b_ref[...])

    for b in range(2):
        pl.when(buf == b)(functools.partial(run, b))


def _combine(x2, rows_out, slots_c, gates_c, tables, g, b, first_tile, n_tiles):
    const = lambda i, *_: (0, 0)
    tile = lambda i, *_: (first_tile + i, 0)
    return pl.pallas_call(
        functools.partial(_combine_kernel, first_tile),
        grid_spec=pltpu.PrefetchScalarGridSpec(
            num_scalar_prefetch=4,
            grid=(n_tiles,),
            in_specs=[
                pl.BlockSpec((ROW_TILE, D_MODEL), tile),
                pl.BlockSpec(memory_space=pl.ANY),
                pl.BlockSpec((ROW_TILE, TOP_K), tile),
                pl.BlockSpec((ROW_TILE, TOP_K), tile),
                pl.BlockSpec((1, D_MODEL), const),
                pl.BlockSpec((1, D_MODEL), const),
            ],
            out_specs=pl.BlockSpec((ROW_TILE, D_MODEL), lambda i, *_: (i, 0)),
            scratch_shapes=[pltpu.VMEM((2, LOCAL_ROWS, HALF), jnp.uint32), pltpu.SemaphoreType.DMA((2,))],
        ),
        out_shape=jax.ShapeDtypeStruct((n_tiles * ROW_TILE, D_MODEL), F32),
        compiler_params=_cparams(("arbitrary",)),
        name="combine_ln3",
    )(*tables[:4], x2, rows_out, slots_c, gates_c, g.reshape(1, D_MODEL), b.reshape(1, D_MODEL))


def kernel(x_prompt, x_sample, mem_prompt, mem_sample, rel_bias, w_in, b_gates, conv_w, conv_b, mh_gain, attn_sink,
           w_out, ln1_g, ln1_b, wq_mem, wkv_mem, wo_mem, ln2_g, ln2_b, w_router, b_router, w_up, b_up, w_down,
           b_down, ln3_g, ln3_b):
    assert w_in.shape[0] == 1, "single layer"
    bp, sp, _ = x_prompt.shape
    bs, ss, _ = x_sample.shape
    tp, ts = bp * sp, bs * ss
    t = tp + ts
    assert sp % ROW_TILE == 0 and ss % ROW_TILE == 0 and sp >= 2 * BLOCK and ss >= 2 * BLOCK
    xp = x_prompt.reshape(tp, D_MODEL)
    xs = x_sample.reshape(ts, D_MODEL)

    def seq_blocks(rows):
        return (tp // rows, sp // rows, ss // rows)

    w = w_in[0]
    q_end, k_end, v_end = ATT_WIDTH, ATT_WIDTH + 128, ATT_WIDTH + 256
    qk_end, mv_end, mo_end = v_end + 2 * ML_WIDTH, v_end + 3 * ML_WIDTH, v_end + 4 * ML_WIDTH
    head_order = np.concatenate([[h, ATT_GROUP + h] for h in range(ATT_GROUP)])
    att_perm = (head_order[:, None] * ATT_HEAD_DIM + np.arange(ATT_HEAD_DIM)[None, :]).reshape(-1)
    w_qk = w[:, v_end:qk_end].astype(BF16)
    w_main = jnp.concatenate([
        w[:, qk_end:mv_end], w[:, mv_end:mo_end],
        w[:, :q_end][:, att_perm] * (ATT_HEAD_DIM ** -0.5), w[:, q_end:k_end], w[:, k_end:v_end]], axis=1).astype(BF16)
    w_g = w[:, mo_end:].astype(BF16)
    w_att = w_out[0][:ATT_WIDTH][att_perm].astype(BF16)
    w_ml = w_out[0][ATT_WIDTH:].astype(BF16)

    qk, z, gates_c, gates_r = _in_proj(xp, xs, w_qk, w_main, w_g, b_gates[0], conv_w[0], conv_b[0],
                                       seq_blocks(ROW_TILE))

    bias, sink = _attention_tables(rel_bias, attn_sink[0])
    att = _attention(z, bias, sink, seq_blocks(BLOCK))

    h_f, h_b = _mlstm(qk, z, gates_r, gates_c, seq_blocks(ML_CHUNK))

    mem = jnp.concatenate([mem_prompt.reshape(bp * MEM_TOKENS, D_MODEL), mem_sample.reshape(bs * MEM_TOKENS, D_MODEL)])
    kv = _mem_kv(mem, wkv_mem[0].astype(BF16))
    n_p_tiles, p_tiles_per_seq, s_tiles_per_seq = seq_blocks(ROW_TILE)
    mem_of_tile = lambda i: jnp.where(i < n_p_tiles, i // p_tiles_per_seq, bp + (i - n_p_tiles) // s_tiles_per_seq)
    wq = (wq_mem[0] * (MEM_HEAD_DIM ** -0.5)).astype(BF16)
    x2, x2b, top_gate, slots, tile_chunks = _cross_router(
        xp, xs, att, h_f, h_b, z, w_att, w_ml, mh_gain[0], ln1_g[0], ln1_b[0],
        kv, wq, wo_mem[0].astype(BF16), ln2_g[0], ln2_b[0], w_router[0], b_router[0], mem_of_tile)

    n_tiles = t // ROW_TILE
    blk_chunks = MOE_ROWS // SEG_ALIGN
    chunks = tile_chunks[:, :, 0]
    used = jnp.sum(chunks, axis=0)
    region = ((used + blk_chunks - 1) // blk_chunks) * blk_chunks
    region_end = jnp.cumsum(region)
    region_start = region_end - region
    seg_start = region_start[None, :] + jnp.cumsum(chunks, axis=0) - chunks
    loc_start = jnp.cumsum(chunks, axis=1) - chunks
    n_blk = -(-(t * TOP_K + n_tiles * N_EXPERTS * (SEG_ALIGN - 1) + N_EXPERTS * (MOE_ROWS - 1)) // MOE_ROWS)
    blk_first = jnp.arange(n_blk, dtype=jnp.int32) * blk_chunks
    block_e = jnp.minimum(jnp.sum(blk_first[:, None] >= region_end[None, :], axis=1), N_EXPERTS - 1).astype(jnp.int32)
    n_active = (region_end[-1:] // blk_chunks).astype(jnp.int32)
    tables = (chunks.reshape(-1), seg_start.reshape(-1), loc_start.reshape(-1), jnp.sum(chunks, axis=1),
              region_start + used, region - used, n_active)
    tables = tuple(tb.astype(jnp.int32) for tb in tables)

    rows_in = _dispatch(x2b, slots, tables, n_blk * MOE_ROWS)
    rows_out = _experts(rows_in, block_e, n_active, w_up[0], b_up[0], w_down[0], b_down[0])

    slots_c, gates_c = slots.T, top_gate.T
    y_p = _combine(x2, rows_out, slots_c, gates_c, tables, ln3_g[0], ln3_b[0], 0, n_p_tiles)
    y_s = _combine(x2, rows_out, slots_c, gates_c, tables, ln3_g[0], ln3_b[0], n_p_tiles, n_tiles - n_p_tiles)
    return y_p.reshape(bp, sp, D_MODEL), y_s.reshape(bs, ss, D_MODEL)
```

```python
import functools

import numpy as np
import jax
import jax.numpy as jnp
from jax import lax
from jax.experimental import pallas as pl
from jax.experimental.pallas import tpu as pltpu

F32 = jnp.float32
BF16 = jnp.bfloat16

D_MODEL = 1024
ATT_HEADS = 8
ATT_KV_HEADS = 2
ATT_GROUP = ATT_HEADS // ATT_KV_HEADS
ATT_HEAD_DIM = 64
ATT_WIDTH = ATT_HEADS * ATT_HEAD_DIM
WINDOW = 128
BLOCK = WINDOW
N_BUCKETS = 32
MAX_DISTANCE = 128
ML_HEADS = 4
ML_HEAD_DIM = 128
ML_WIDTH = ML_HEADS * ML_HEAD_DIM
ML_CHUNK = 256
CONV_WIDTH = 5
N_GATES = 4 * ML_HEADS
MEM_TOKENS = 256
MEM_HEADS = 4
MEM_HEAD_DIM = D_MODEL // MEM_HEADS
N_EXPERTS = 32
TOP_K = 4
D_FF = D_MODEL
SWIGLU_LIMIT = 7.0
SWIGLU_ALPHA = 1.702
LN_EPS = 1e-5
DN_ALPHA = 2.0 ** 0.25

QK_WIDTH = 2 * ML_WIDTH
Z_WIDTH = ML_WIDTH + ML_WIDTH + ATT_WIDTH + 2 * ATT_KV_HEADS * ATT_HEAD_DIM
ZB_MLV = 0
ZB_MLO = 1
ZB_ATTQ = 2
ZB_ATTK = 12
ZB_ATTV = 13

ROW_TILE = 512
MOE_ROWS = 512
NEG = -1e30
VMEM_LIMIT = 56 * 1024 * 1024


def _cparams(sem):
    return pltpu.CompilerParams(dimension_semantics=sem, vmem_limit_bytes=VMEM_LIMIT)


def _dot(a, b):
    return jnp.dot(a, b, preferred_element_type=F32)


def _dot_nt(a, b):
    return lax.dot_general(a, b, (((1,), (1,)), ((), ())), preferred_element_type=F32)


def _dot_tn(a, b):
    return lax.dot_general(a, b, (((0,), (0,)), ((), ())), preferred_element_type=F32)


def _split3(x):
    hi = x.astype(BF16)
    rest = x - hi.astype(F32)
    mid = rest.astype(BF16)
    lo = (rest - mid.astype(F32)).astype(BF16)
    return hi, mid, lo


def _layer_norm(y, g, b):
    mu = jnp.mean(y, axis=-1, keepdims=True)
    yc = y - mu
    var = jnp.mean(yc * yc, axis=-1, keepdims=True)
    return yc * lax.rsqrt(var + LN_EPS) * g + b


def _log_sigmoid(x):
    return jnp.minimum(x, 0.0) - jnp.log1p(jnp.exp(-jnp.abs(x)))


def _sigmoid(x):
    return 1.0 / (1.0 + jnp.exp(-x))


def _trace_stagewise(chains):
    while chains:
        chains = [c for c in chains if next(c, "done") != "done"]


def _seq_pos(blk, n_p_blocks, p_blocks_per_seq, s_blocks_per_seq):
    in_p = blk < n_p_blocks
    local = jnp.where(in_p, blk % p_blocks_per_seq, (blk - n_p_blocks) % s_blocks_per_seq)
    per = jnp.where(in_p, p_blocks_per_seq, s_blocks_per_seq)
    return local == 0, local == per - 1


HALO = 8


def _in_proj_kernel(seq, xp_ref, xpl_ref, xpr_ref, xs_ref, xsl_ref, xsr_ref, wqk_ref, w_ref, wg_ref, wgt_ref,
                    bg_ref, bgt_ref, cw_ref, cb_ref, scale_ref, qk_ref, z_ref, gc_ref, gr_ref, buf_ref):
    i = pl.program_id(0)
    in_p = i < seq[0]
    first, last = _seq_pos(i, *seq)
    x = jnp.where(in_p, xp_ref[...], xs_ref[...])
    left = jnp.where(in_p, xpl_ref[...], xsl_ref[...])
    right = jnp.where(in_p, xpr_ref[...], xsr_ref[...])
    xb = x.astype(BF16)
    x_ext = jnp.concatenate([left, x, right], axis=0).astype(BF16)
    row = lax.broadcasted_iota(jnp.int32, (ROW_TILE + 2 * HALO, 1), 0)
    outside = (first & (row < HALO)) | (last & (row >= ROW_TILE + HALO))
    buf_ref[...] = jnp.where(outside, 0.0, _dot(x_ext, wqk_ref[...]))
    acc = jnp.zeros((ROW_TILE, QK_WIDTH), F32) + cb_ref[...]
    for j in range(CONV_WIDTH):
        off = HALO + j - CONV_WIDTH // 2
        acc = acc + buf_ref[off:off + ROW_TILE, :] * cw_ref[j:j + 1, :]
    qk_ref[...] = (acc * _sigmoid(acc) * scale_ref[...]).astype(BF16)
    z_ref[...] = _dot(xb, w_ref[...]).astype(BF16)
    gc_ref[...] = _dot(xb, wg_ref[...]) + bg_ref[...]
    gr_ref[...] = _dot_nt(wgt_ref[...], xb) + bgt_ref[...]


def _in_proj(xp, xs, w_qk, w_main, w_g, b_g, conv_w, conv_b, seq):
    tp, ts = xp.shape[0], xs.shape[0]
    t = tp + ts
    n_p = tp // ROW_TILE
    n = t // ROW_TILE
    r = ROW_TILE // HALO
    const = lambda i: (0, 0)
    p_tile = lambda i: jnp.minimum(i, n_p - 1)
    s_tile = lambda i: jnp.maximum(i - n_p, 0)
    scale = jnp.concatenate([jnp.ones((1, ML_WIDTH), F32), jnp.full((1, ML_WIDTH), ML_HEAD_DIM ** -0.5, F32)], axis=1)
    return pl.pallas_call(
        functools.partial(_in_proj_kernel, seq),
        grid=(n,),
        in_specs=[
            pl.BlockSpec((ROW_TILE, D_MODEL), lambda i: (p_tile(i), 0)),
            pl.BlockSpec((HALO, D_MODEL), lambda i: (jnp.maximum(p_tile(i) * r - 1, 0), 0)),
            pl.BlockSpec((HALO, D_MODEL), lambda i: (jnp.minimum((p_tile(i) + 1) * r, tp // HALO - 1), 0)),
            pl.BlockSpec((ROW_TILE, D_MODEL), lambda i: (s_tile(i), 0)),
            pl.BlockSpec((HALO, D_MODEL), lambda i: (jnp.maximum(s_tile(i) * r - 1, 0), 0)),
            pl.BlockSpec((HALO, D_MODEL), lambda i: (jnp.minimum((s_tile(i) + 1) * r, ts // HALO - 1), 0)),
            pl.BlockSpec((D_MODEL, QK_WIDTH), const),
            pl.BlockSpec((D_MODEL, Z_WIDTH), const),
            pl.BlockSpec((D_MODEL, N_GATES), const),
            pl.BlockSpec((N_GATES, D_MODEL), const),
            pl.BlockSpec((1, N_GATES), const),
            pl.BlockSpec((N_GATES, 1), const),
            pl.BlockSpec((CONV_WIDTH, QK_WIDTH), const),
            pl.BlockSpec((1, QK_WIDTH), const),
            pl.BlockSpec((1, QK_WIDTH), const),
        ],
        out_specs=[
            pl.BlockSpec((ROW_TILE, QK_WIDTH), lambda i: (i, 0)),
            pl.BlockSpec((ROW_TILE, Z_WIDTH), lambda i: (i, 0)),
            pl.BlockSpec((ROW_TILE, N_GATES), lambda i: (i, 0)),
            pl.BlockSpec((N_GATES, ROW_TILE), lambda i: (0, i)),
        ],
        out_shape=[
            jax.ShapeDtypeStruct((t, QK_WIDTH), BF16),
            jax.ShapeDtypeStruct((t, Z_WIDTH), BF16),
            jax.ShapeDtypeStruct((t, N_GATES), F32),
            jax.ShapeDtypeStruct((N_GATES, t), F32),
        ],
        scratch_shapes=[pltpu.VMEM((ROW_TILE + 2 * HALO, QK_WIDTH), F32)],
        compiler_params=_cparams(("arbitrary",)),
        name="in_proj_conv",
    )(xp, xp, xp, xs, xs, xs, w_qk, w_main, w_g, w_g.T, b_g.reshape(1, N_GATES), b_g.reshape(N_GATES, 1),
      conv_w, conv_b.reshape(1, QK_WIDTH), scale)


ATT_TILE = 512
ATT_SUB = ATT_TILE // BLOCK


ONES_ROWS = 16


def _attention_kernel(seq, q_ref, kp_ref, kc_ref, kn_ref, vp_ref, vc_ref, vn_ref, bias_ref, sink_ref,
                      o_ref, klo_ref, khi_ref, vt_ref):
    i = pl.program_id(0)
    lane = lax.broadcasted_iota(jnp.int32, (ATT_TILE + 2 * BLOCK, 2 * ATT_HEAD_DIM), 1)
    kband = jnp.concatenate([kp_ref[...], kc_ref[...], kn_ref[...]], axis=0)
    zero = jnp.zeros_like(kband)
    klo_ref[...] = jnp.where(lane < ATT_HEAD_DIM, kband, zero)
    khi_ref[...] = jnp.where(lane < ATT_HEAD_DIM, zero, kband)
    vband = jnp.concatenate([vp_ref[...], vc_ref[...], vn_ref[...]], axis=0)
    vt_ref[0:2 * ATT_HEAD_DIM, :] = vband.T
    vt_ref[2 * ATT_HEAD_DIM:, :] = jnp.ones((ONES_ROWS, ATT_TILE + 2 * BLOCK), BF16)
    feat = lax.broadcasted_iota(jnp.int32, (2 * ATT_HEAD_DIM, ATT_GROUP * BLOCK), 0)
    def block(s):
        first, last = _seq_pos(i * ATT_SUB + s, *seq)
        variant = jnp.where(first, 1, jnp.where(last, 2, 0))
        q = q_ref[s * BLOCK:(s + 1) * BLOCK, :]
        q_all = jnp.concatenate([q[:, t * 128:(t + 1) * 128] for t in range(ATT_GROUP)], axis=0)
        vt = vt_ref[:, s * BLOCK:(s + 3) * BLOCK]
        scores = [_dot_nt(k_ref[s * BLOCK:(s + 3) * BLOCK, :], q_all) for k_ref in (klo_ref, khi_ref)]
        yield
        ms, ps = [], []
        for kv in range(ATT_KV_HEADS):
            logits = scores[kv] + bias_ref[variant, kv]
            ms.append(jnp.maximum(jnp.max(logits, axis=0, keepdims=True), sink_ref[kv]))
            ps.append(jnp.exp(logits - ms[kv]).astype(BF16))
        yield
        ovs = [_dot(vt, p) for p in ps]
        yield
        outs = []
        for kv in range(ATT_KV_HEADS):
            den = ovs[kv][2 * ATT_HEAD_DIM:2 * ATT_HEAD_DIM + 1, :] + jnp.exp(sink_ref[kv] - ms[kv])
            outs.append(ovs[kv][0:2 * ATT_HEAD_DIM, :] * (1.0 / den))
        both = jnp.where(feat < ATT_HEAD_DIM, outs[0], outs[1]).astype(BF16)
        for t in range(ATT_GROUP):
            o_ref[t * 128:(t + 1) * 128, s * BLOCK:(s + 1) * BLOCK] = both[:, t * BLOCK:(t + 1) * BLOCK]

    _trace_stagewise([block(s) for s in range(ATT_SUB)])


def _attention(z, bias, sink, seq):
    t = z.shape[0]
    n = t // ATT_TILE
    nblk = t // BLOCK
    band = ATT_TILE + 2 * BLOCK
    prev = lambda i: jnp.maximum(i * ATT_SUB - 1, 0)
    nxt = lambda i: jnp.minimum((i + 1) * ATT_SUB, nblk - 1)
    return pl.pallas_call(
        functools.partial(_attention_kernel, seq),
        grid=(n,),
        in_specs=[
            pl.BlockSpec((ATT_TILE, ATT_WIDTH), lambda i: (i, ZB_ATTQ)),
            pl.BlockSpec((BLOCK, 128), lambda i: (prev(i), ZB_ATTK)),
            pl.BlockSpec((ATT_TILE, 128), lambda i: (i, ZB_ATTK)),
            pl.BlockSpec((BLOCK, 128), lambda i: (nxt(i), ZB_ATTK)),
            pl.BlockSpec((BLOCK, 128), lambda i: (prev(i), ZB_ATTV)),
            pl.BlockSpec((ATT_TILE, 128), lambda i: (i, ZB_ATTV)),
            pl.BlockSpec((BLOCK, 128), lambda i: (nxt(i), ZB_ATTV)),
            pl.BlockSpec((3, ATT_KV_HEADS, 3 * BLOCK, ATT_GROUP * BLOCK), lambda i: (0, 0, 0, 0)),
            pl.BlockSpec((ATT_KV_HEADS, 1, ATT_GROUP * BLOCK), lambda i: (0, 0, 0)),
        ],
        out_specs=pl.BlockSpec((ATT_WIDTH, ATT_TILE), lambda i: (0, i)),
        out_shape=jax.ShapeDtypeStruct((ATT_WIDTH, t), BF16),
        scratch_shapes=[pltpu.VMEM((band, 128), BF16), pltpu.VMEM((band, 128), BF16),
                        pltpu.VMEM((2 * ATT_HEAD_DIM + ONES_ROWS, band), BF16)],
        compiler_params=_cparams(("arbitrary",)),
        name="window_attention",
    )(z, z, z, z, z, z, z, bias, sink)


def _t5_bucket(rel):
    half = N_BUCKETS // 2
    exact = half // 2
    n = np.abs(rel)
    large = exact + (np.log(np.maximum(n, 1) / exact) / np.log(MAX_DISTANCE / exact) * (half - exact)).astype(np.int32)
    large = np.minimum(large, half - 1)
    return ((rel > 0).astype(np.int32) * half + np.where(n < exact, n, large)).astype(np.int32)


def _attention_tables(rel_bias, attn_sink):
    rel = np.arange(3 * BLOCK)[:, None] - BLOCK - np.arange(BLOCK)[None, :]
    onehot = jnp.asarray(_t5_bucket(rel)[..., None] == np.arange(N_BUCKETS), F32)
    bias = jnp.einsum('kqb,bh->hkq', onehot, rel_bias.astype(F32), precision=lax.Precision.HIGHEST)
    bias = jnp.where(jnp.asarray(np.abs(rel) <= WINDOW)[None], bias, NEG)
    bias = bias.reshape(ATT_KV_HEADS, ATT_GROUP, 3 * BLOCK, BLOCK).transpose(0, 2, 1, 3)
    bias = bias.reshape(ATT_KV_HEADS, 3 * BLOCK, ATT_GROUP * BLOCK)
    key = np.arange(3 * BLOCK)[None, :, None]
    first = jnp.where(jnp.asarray(key < BLOCK), NEG, bias)
    last = jnp.where(jnp.asarray(key >= 2 * BLOCK), NEG, bias)
    sink = jnp.repeat(attn_sink.astype(F32), BLOCK).reshape(ATT_KV_HEADS, 1, ATT_GROUP * BLOCK)
    return jnp.stack([bias, first, last]), sink


def _mlstm_kernel(seq, n_chunks, *refs):
    fwd_in, bwd_in, (of_ref, ob_ref), state = refs[0:5], refs[5:10], refs[10:12], refs[12:]
    _trace_stagewise(_mlstm_chains(seq, False, n_chunks, *fwd_in, of_ref, *state[0:2])
                     + _mlstm_chains(seq, True, n_chunks, *bwd_in, ob_ref, *state[2:4]))


def _mlstm_chains(seq, reverse, n_chunks, q_ref, k_ref, v_ref, gr_ref, gc_ref, o_ref, ct_ref, m_ref):
    step = pl.program_id(0)
    chunk = (n_chunks - 1 - step) if reverse else step
    first, last = _seq_pos(chunk, *seq)
    fresh = last if reverse else first

    L = ML_CHUNK
    row = lax.broadcasted_iota(jnp.int32, (L, L), 0)
    col = lax.broadcasted_iota(jnp.int32, (L, L), 1)
    vis = (row >= col) if reverse else (row <= col)
    vis_t = (col >= row) if reverse else (col <= row)
    gr = gr_ref[...]
    gc = gc_ref[...]
    r3 = _dot(jnp.concatenate(_split3(_log_sigmoid(gr)), axis=0), vis.astype(BF16))
    b_rows = r3[0:N_GATES] + r3[N_GATES:2 * N_GATES] + r3[2 * N_GATES:]
    vis_tb = vis_t.astype(BF16)
    c_hi, c_mid, c_lo = _split3(_log_sigmoid(gc))
    b_cols = _dot(vis_tb, c_hi) + _dot(vis_tb, c_mid) + _dot(vis_tb, c_lo)
    i_off = 2 * ML_HEADS if reverse else 0
    f_off = i_off + ML_HEADS
    end = 0 if reverse else L - 1
    ones = jnp.ones((ONES_ROWS, L), BF16)

    def chain(h):
        hs = slice(h * ML_HEAD_DIM, (h + 1) * ML_HEAD_DIM)
        q = q_ref[:, hs]
        k = k_ref[:, hs]
        vt1 = jnp.concatenate([v_ref[:, hs].T, ones], axis=0)
        b_row = b_rows[f_off + h:f_off + h + 1, :]
        u_row = gr[i_off + h:i_off + h + 1, :] - b_row
        u_col = gc[:, i_off + h:i_off + h + 1] - b_cols[:, f_off + h:f_off + h + 1]
        g = b_row[:, end:end + 1]
        ct_old = jnp.where(fresh, 0.0, ct_ref[h])
        m_old = jnp.where(fresh, 0.0, m_ref[h])[:, 0:1]
        kq = _dot_nt(k, q)
        from_state = _dot_nt(ct_old.astype(BF16), q)
        yield
        u_mat = jnp.where(vis, u_col, NEG)
        mm = jnp.maximum(jnp.max(u_mat, axis=0, keepdims=True), m_old)
        st = (kq * jnp.exp(u_mat - mm)).astype(BF16)
        yield
        tot = _dot(vt1, st) + jnp.exp(m_old - mm) * from_state
        a_max = jnp.max(g + u_row, axis=-1, keepdims=True)
        m_new = jnp.maximum(g + m_old, a_max)
        s_old = jnp.exp(g + m_old - m_new)
        weighted = (vt1.astype(F32) * jnp.exp(g + u_row - m_new)).astype(BF16)
        new_state = _dot(weighted, k)
        yield
        den = tot[ML_HEAD_DIM:ML_HEAD_DIM + 1, :]
        floor = jnp.exp(-(b_row + mm))
        o_ref[hs, :] = (tot[0:ML_HEAD_DIM, :] * (1.0 / jnp.maximum(jnp.abs(den), floor))).astype(BF16)
        ct_ref[h] = s_old * ct_old + new_state
        m_ref[h] = jnp.broadcast_to(m_new, (1, ML_HEAD_DIM))

    return [chain(h) for h in range(ML_HEADS)]


def _mlstm(qk, z, gates_r, gates_c, seq):
    t = qk.shape[0]
    nc = t // ML_CHUNK

    def chunk_specs(ch):
        return [
            pl.BlockSpec((ML_CHUNK, ML_WIDTH), lambda i: (ch(i), 0)),
            pl.BlockSpec((ML_CHUNK, ML_WIDTH), lambda i: (ch(i), 1)),
            pl.BlockSpec((ML_CHUNK, ML_WIDTH), lambda i: (ch(i), ZB_MLV)),
            pl.BlockSpec((N_GATES, ML_CHUNK), lambda i: (0, ch(i))),
            pl.BlockSpec((ML_CHUNK, N_GATES), lambda i: (ch(i), 0)),
        ]

    fwd = lambda i: i
    bwd = lambda i: nc - 1 - i
    state = [
        pltpu.VMEM((ML_HEADS, ML_HEAD_DIM + ONES_ROWS, ML_HEAD_DIM), F32),
        pltpu.VMEM((ML_HEADS, 1, ML_HEAD_DIM), F32),
    ]
    operands = (qk, qk, z, gates_r, gates_c)
    return pl.pallas_call(
        functools.partial(_mlstm_kernel, seq, nc),
        grid=(nc,),
        in_specs=chunk_specs(fwd) + chunk_specs(bwd),
        out_specs=[pl.BlockSpec((ML_WIDTH, ML_CHUNK), lambda i: (0, fwd(i))),
                   pl.BlockSpec((ML_WIDTH, ML_CHUNK), lambda i: (0, bwd(i)))],
        out_shape=[jax.ShapeDtypeStruct((ML_WIDTH, t), BF16), jax.ShapeDtypeStruct((ML_WIDTH, t), BF16)],
        scratch_shapes=state + state,
        compiler_params=_cparams(("arbitrary",)),
        name="mlstm",
    )(*operands, *operands)


def _mix_out(n_p_tiles, xp_ref, xs_ref, att_ref, hf_ref, hb_ref, og_ref, wa_ref, wm_ref, gain_ref, g_ref, b_ref):
    i = pl.program_id(0)
    x = jnp.where(i < n_p_tiles, xp_ref[...], xs_ref[...])
    h = hf_ref[...].astype(F32) + hb_ref[...].astype(F32)
    parts = []
    for hd in range(ML_HEADS):
        hh = h[hd * ML_HEAD_DIM:(hd + 1) * ML_HEAD_DIM, :]
        mu = jnp.mean(hh, axis=0, keepdims=True)
        hc = hh - mu
        var = jnp.mean(hc * hc, axis=0, keepdims=True)
        parts.append(hc * lax.rsqrt(var + LN_EPS))
    hn = jnp.concatenate(parts, axis=0) * gain_ref[...] * _sigmoid(og_ref[...].T.astype(F32))
    mixed = _dot(att_ref[...].T, wa_ref[...]) + _dot(hn.astype(BF16).T, wm_ref[...])
    return _layer_norm(DN_ALPHA * x + mixed, g_ref[...], b_ref[...])


def _mem_kv_kernel(m_ref, w_ref, o_ref):
    o_ref[...] = _dot(m_ref[...].astype(BF16), w_ref[...]).astype(BF16)


def _mem_kv(mem, wkv):
    rows = mem.shape[0]
    return pl.pallas_call(
        _mem_kv_kernel,
        grid=(rows // MEM_TOKENS,),
        in_specs=[pl.BlockSpec((MEM_TOKENS, D_MODEL), lambda i: (i, 0)),
                  pl.BlockSpec((D_MODEL, 2 * D_MODEL), lambda i: (0, 0))],
        out_specs=pl.BlockSpec((MEM_TOKENS, 2 * D_MODEL), lambda i: (i, 0)),
        out_shape=jax.ShapeDtypeStruct((rows, 2 * D_MODEL), BF16),
        compiler_params=_cparams(("arbitrary",)),
        name="mem_kv",
    )(mem, wkv)


def _pack_bf16_pairs(a, b, exact=False):
    if not exact:
        a, b = a.astype(BF16).astype(F32), b.astype(BF16).astype(F32)
    ua, ub = pltpu.bitcast(a, jnp.uint32), pltpu.bitcast(b, jnp.uint32)
    return (ua >> 16) | (ub & jnp.uint32(0xFFFF0000))


def _unpack_bf16_pairs(u):
    lo = pltpu.bitcast(u << 16, F32)
    hi = pltpu.bitcast(u & jnp.uint32(0xFFFF0000), F32)
    return lo, hi


def _cross_router_kernel(n_p_tiles, *refs):
    mix_refs, (kv_ref, wq_ref, wo_ref, g_ref, b_ref, wr_ref, br_ref, tri_ref, etri_ref,
               x2_ref, x2b_ref, gate_ref, slot_ref, chunk_ref) = refs[:11], refs[11:]
    x = _mix_out(n_p_tiles, *mix_refs)
    q = _dot(x.astype(BF16), wq_ref[...]).astype(BF16)
    head_cols = [slice(h * MEM_HEAD_DIM, (h + 1) * MEM_HEAD_DIM) for h in range(MEM_HEADS)]
    scores = [_dot_nt(q[:, hs], kv_ref[:, hs]) for hs in head_cols]
    probs, dens = [], []
    for logits in scores:
        p = jnp.exp(logits - jnp.max(logits, axis=-1, keepdims=True))
        dens.append(jnp.sum(p, axis=-1, keepdims=True))
        probs.append(p.astype(BF16))
    values = [_dot(p, kv_ref[:, D_MODEL + hs.start:D_MODEL + hs.stop]) for p, hs in zip(probs, head_cols)]
    o = jnp.concatenate([(v * (1.0 / den)).astype(BF16) for v, den in zip(values, dens)], axis=1)
    x2 = _layer_norm(DN_ALPHA * x + _dot(o, wo_ref[...]), g_ref[...], b_ref[...])
    x2_ref[...] = x2
    x2b = x2.astype(BF16)
    x2b_ref[...] = x2b

    logits = _dot_nt(wr_ref[...], x2b) + br_ref[...]
    expert = lax.broadcasted_iota(jnp.int32, logits.shape, 0)
    work = logits
    vals, sels = [], []
    for k in range(TOP_K):
        mx = jnp.max(work, axis=0, keepdims=True)
        ix = jnp.min(jnp.where(work == mx, expert, N_EXPERTS), axis=0, keepdims=True)
        sel = expert == ix
        work = jnp.where(sel, -jnp.inf, work)
        vals.append(mx)
        sels.append(sel)
    es = [jnp.exp(v - vals[0]) for v in vals]
    tot = es[0] + es[1] + es[2] + es[3]
    chosen = jnp.zeros(logits.shape, F32)
    for k in range(TOP_K):
        gate_ref[k:k + 1, :] = es[k] / tot
        chosen = chosen + sels[k].astype(F32)
    count = jnp.sum(chosen, axis=1, keepdims=True)
    chunks = jnp.floor((count + (SEG_ALIGN - 1)) * (1.0 / SEG_ALIGN))
    chunks_b = jnp.broadcast_to(chunks, (N_EXPERTS, 128))
    seg_first = _dot(etri_ref[...], chunks_b.astype(BF16))[:, 0:1] * SEG_ALIGN
    before = _dot(chosen.astype(BF16), tri_ref[...])
    local_row = before + seg_first
    for k in range(TOP_K):
        slot_ref[k:k + 1, :] = jnp.sum(jnp.where(sels[k], local_row, 0.0), axis=0, keepdims=True).astype(jnp.int32)
    chunk_ref[0] = chunks_b.astype(jnp.int32)


def _cross_router(xp, xs, att, hf, hb, z, w_att, w_ml, gain, g1, b1, kv, wq, wo, g, b, w_router, b_router,
                  mem_of_tile):
    t = att.shape[1]
    n_p = xp.shape[0] // ROW_TILE
    n = t // ROW_TILE
    const = lambda i: (0, 0)
    tile = lambda i: (i, 0)
    lanes = lambda i: (0, i)
    tri = jnp.asarray(np.triu(np.ones((ROW_TILE, ROW_TILE), np.float32), 1), BF16)
    etri = jnp.asarray(np.tril(np.ones((N_EXPERTS, N_EXPERTS), np.float32), -1), BF16)
    return pl.pallas_call(
        functools.partial(_cross_router_kernel, n_p),
        grid=(n,),
        in_specs=[
            pl.BlockSpec((ROW_TILE, D_MODEL), lambda i: (jnp.minimum(i, n_p - 1), 0)),
            pl.BlockSpec((ROW_TILE, D_MODEL), lambda i: (jnp.maximum(i - n_p, 0), 0)),
            pl.BlockSpec((ATT_WIDTH, ROW_TILE), lanes),
            pl.BlockSpec((ML_WIDTH, ROW_TILE), lanes),
            pl.BlockSpec((ML_WIDTH, ROW_TILE), lanes),
            pl.BlockSpec((ROW_TILE, ML_WIDTH), lambda i: (i, ZB_MLO)),
            pl.BlockSpec((ATT_WIDTH, D_MODEL), const),
            pl.BlockSpec((ML_WIDTH, D_MODEL), const),
            pl.BlockSpec((ML_WIDTH, 1), const),
            pl.BlockSpec((1, D_MODEL), const),
            pl.BlockSpec((1, D_MODEL), const),
            pl.BlockSpec((MEM_TOKENS, 2 * D_MODEL), lambda i: (mem_of_tile(i), 0)),
            pl.BlockSpec((D_MODEL, D_MODEL), const),
            pl.BlockSpec((D_MODEL, D_MODEL), const),
            pl.BlockSpec((1, D_MODEL), const),
            pl.BlockSpec((1, D_MODEL), const),
            pl.BlockSpec((N_EXPERTS, D_MODEL), const),
            pl.BlockSpec((N_EXPERTS, 1), const),
            pl.BlockSpec((ROW_TILE, ROW_TILE), const),
            pl.BlockSpec((N_EXPERTS, N_EXPERTS), const),
        ],
        out_specs=[
            pl.BlockSpec((ROW_TILE, D_MODEL), tile),
            pl.BlockSpec((ROW_TILE, D_MODEL), tile),
            pl.BlockSpec((TOP_K, ROW_TILE), lanes),
            pl.BlockSpec((TOP_K, ROW_TILE), lanes),
            pl.BlockSpec((1, N_EXPERTS, 128), lambda i: (i, 0, 0)),
        ],
        out_shape=[
            jax.ShapeDtypeStruct((t, D_MODEL), F32),
            jax.ShapeDtypeStruct((t, D_MODEL), BF16),
            jax.ShapeDtypeStruct((TOP_K, t), F32),
            jax.ShapeDtypeStruct((TOP_K, t), jnp.int32),
            jax.ShapeDtypeStruct((n, N_EXPERTS, 128), jnp.int32),
        ],
        compiler_params=_cparams(("arbitrary",)),
        name="mix_out_cross_attn_router",
    )(xp, xs, att, hf, hb, z, w_att, w_ml, gain.reshape(ML_WIDTH, 1), g1.reshape(1, D_MODEL), b1.reshape(1, D_MODEL),
      kv, wq, wo, g.reshape(1, D_MODEL), b.reshape(1, D_MODEL), w_router.T.astype(BF16),
      b_router.reshape(N_EXPERTS, 1), tri, etri)


SEG_ALIGN = 8
LOCAL_ROWS = ROW_TILE * TOP_K + N_EXPERTS * SEG_ALIGN
HALF = D_MODEL // 2


def _rows(chunks):
    return pl.multiple_of(chunks * SEG_ALIGN, SEG_ALIGN)


def _segment_copies(chunks_ref, seg_ref, loc_ref, tile, make_copy):
    def per_expert(e, carry):
        entry = tile * N_EXPERTS + e

        @pl.when(chunks_ref[entry] > 0)
        def _():
            make_copy(_rows(loc_ref[entry]), _rows(seg_ref[entry]), _rows(chunks_ref[entry])).start()

        return carry

    lax.fori_loop(0, N_EXPERTS, per_expert, 0)


def _wait_copies(chunks, make_copy):
    @pl.when(chunks > 0)
    def _():
        make_copy(0, 0, _rows(chunks)).wait()


def _dispatch_kernel(chunks_ref, seg_ref, loc_ref, total_ref, tail_ref, tailn_ref, nact_ref,
                     x_ref, slot_ref, rows_hbm, local_ref, zero_ref, sems):
    i = pl.program_id(0)
    n = pl.num_programs(0)
    buf = i % 2

    def copy_out(b):
        return lambda loc, seg, rows: pltpu.make_async_copy(
            local_ref.at[b, pl.ds(loc, rows), :], rows_hbm.at[pl.ds(seg, rows), :], sems.at[b])

    def run(b):
        @pl.when(i >= 2)
        def _():
            _wait_copies(total_ref[jnp.maximum(i - 2, 0)], copy_out(b))

        slots = slot_ref[...].astype(F32)
        row = lax.broadcasted_iota(jnp.int32, (LOCAL_ROWS, ROW_TILE), 0).astype(F32)
        miss = (slots[0:1, :] - row) * (slots[1:2, :] - row)
        for k in range(2, TOP_K):
            miss = miss * (slots[k:k + 1, :] - row)
        perm = jnp.where(miss == 0.0, 1.0, 0.0).astype(BF16)
        picked = _dot(perm, x_ref[...])
        local_ref[b] = _pack_bf16_pairs(picked[:, :HALF], picked[:, HALF:], exact=True)
        _segment_copies(chunks_ref, seg_ref, loc_ref, i, copy_out(b))

    for b in range(2):
        pl.when(buf == b)(functools.partial(run, b))

    @pl.when(i == n - 1)
    def _():
        for b in range(2):
            step = jnp.where(buf == b, i, i - 1)
            _wait_copies(total_ref[step], copy_out(b))
        zero_ref[...] = jnp.zeros_like(zero_ref)
        fill = lambda _, seg, rows: pltpu.make_async_copy(
            zero_ref.at[pl.ds(0, rows), :], rows_hbm.at[pl.ds(seg, rows), :], sems.at[2])

        def per_expert(e, carry):
            @pl.when(tailn_ref[e] > 0)
            def _():
                fill(0, _rows(tail_ref[e]), _rows(tailn_ref[e])).start()

            _wait_copies(tailn_ref[e], fill)
            return carry

        lax.fori_loop(0, N_EXPERTS, per_expert, 0)
        fill_block = lambda blk: pltpu.make_async_copy(
            zero_ref, rows_hbm.at[pl.ds(pl.multiple_of(blk * MOE_ROWS, MOE_ROWS), MOE_ROWS), :], sems.at[2])
        n_blocks = rows_hbm.shape[0] // MOE_ROWS

        def start_block(blk, carry):
            fill_block(blk).start()
            return carry

        def wait_block(blk, carry):
            fill_block(0).wait()
            return carry

        lax.fori_loop(nact_ref[0], n_blocks, start_block, 0)
        lax.fori_loop(nact_ref[0], n_blocks, wait_block, 0)


def _dispatch(x2b, slots, tables, n_rows):
    n = x2b.shape[0] // ROW_TILE
    assert n >= 2
    return pl.pallas_call(
        _dispatch_kernel,
        grid_spec=pltpu.PrefetchScalarGridSpec(
            num_scalar_prefetch=7,
            grid=(n,),
            in_specs=[
                pl.BlockSpec((ROW_TILE, D_MODEL), lambda i, *_: (i, 0)),
                pl.BlockSpec((TOP_K, ROW_TILE), lambda i, *_: (0, i)),
            ],
            out_specs=pl.BlockSpec(memory_space=pl.ANY),
            scratch_shapes=[
                pltpu.VMEM((2, LOCAL_ROWS, HALF), jnp.uint32),
                pltpu.VMEM((MOE_ROWS, HALF), jnp.uint32),
                pltpu.SemaphoreType.DMA((3,)),
            ],
        ),
        out_shape=jax.ShapeDtypeStruct((n_rows, HALF), jnp.uint32),
        compiler_params=_cparams(("arbitrary",)),
        name="moe_dispatch",
    )(*tables, x2b, slots)


def _expert_kernel(be_ref, na_ref, x_ref, wu_ref, bu_ref, wd_ref, bd_ref, o_ref, wub_ref, wdb_ref):
    i = pl.program_id(0)

    @pl.when((i == 0) | (be_ref[i] != be_ref[jnp.maximum(i - 1, 0)]))
    def _():
        wub_ref[...] = wu_ref[0].astype(BF16)
        wdb_ref[...] = wd_ref[0].astype(BF16)

    @pl.when(i < na_ref[0])
    def _():
        lo, hi = _unpack_bf16_pairs(x_ref[...])
        x = jnp.concatenate([lo, hi], axis=1).astype(BF16)
        hu = _dot(x, wub_ref[...]) + bu_ref[0]
        h_glu = jnp.minimum(hu[:, :D_FF], SWIGLU_LIMIT)
        h_lin = jnp.clip(hu[:, D_FF:], -SWIGLU_LIMIT, SWIGLU_LIMIT)
        hh = h_glu * _sigmoid(SWIGLU_ALPHA * h_glu) * (h_lin + 1.0)
        y = _dot(hh.astype(BF16), wdb_ref[...]) + bd_ref[0]
        o_ref[...] = _pack_bf16_pairs(y[:, :D_MODEL // 2], y[:, D_MODEL // 2:])

    @pl.when(i >= na_ref[0])
    def _():
        o_ref[...] = jnp.zeros_like(o_ref)


def _experts(xs, block_e, n_active, w_up, b_up, w_down, b_down):
    n_rows = xs.shape[0]
    n_blk = n_rows // MOE_ROWS
    return pl.pallas_call(
        _expert_kernel,
        grid_spec=pltpu.PrefetchScalarGridSpec(
            num_scalar_prefetch=2,
            grid=(n_blk,),
            in_specs=[
                pl.BlockSpec((MOE_ROWS, D_MODEL // 2), lambda i, be, na: (jnp.minimum(i, na[0] - 1), 0)),
                pl.BlockSpec((1, D_MODEL, 2 * D_FF), lambda i, be, na: (be[i], 0, 0)),
                pl.BlockSpec((1, 1, 2 * D_FF), lambda i, be, na: (be[i], 0, 0)),
                pl.BlockSpec((1, D_FF, D_MODEL), lambda i, be, na: (be[i], 0, 0)),
                pl.BlockSpec((1, 1, D_MODEL), lambda i, be, na: (be[i], 0, 0)),
            ],
            out_specs=pl.BlockSpec((MOE_ROWS, D_MODEL // 2), lambda i, be, na: (i, 0)),
            scratch_shapes=[pltpu.VMEM((D_MODEL, 2 * D_FF), BF16), pltpu.VMEM((D_FF, D_MODEL), BF16)],
        ),
        out_shape=jax.ShapeDtypeStruct((n_rows, D_MODEL // 2), jnp.uint32),
        compiler_params=_cparams(("arbitrary",)),
        name="experts",
    )(block_e, n_active, xs, w_up, b_up.reshape(N_EXPERTS, 1, 2 * D_FF), w_down, b_down.reshape(N_EXPERTS, 1, D_MODEL))


def _combine_kernel(first_tile, chunks_ref, seg_ref, loc_ref, total_ref,
                    x_ref, rows_hbm, slot_ref, gate_ref, g_ref, b_ref, o_ref, local_ref, sems):
    i = pl.program_id(0)
    n = pl.num_programs(0)
    tile = first_tile + i
    buf = i % 2

    def copy_in(b):
        return lambda loc, seg, rows: pltpu.make_async_copy(
            rows_hbm.at[pl.ds(seg, rows), :], local_ref.at[b, pl.ds(loc, rows), :], sems.at[b])

    @pl.when(i == 0)
    def _():
        local_ref[...] = jnp.zeros_like(local_ref)
        _segment_copies(chunks_ref, seg_ref, loc_ref, tile, copy_in(0))

    def run(b):
        @pl.when(i + 1 < n)
        def _():
            _segment_copies(chunks_ref, seg_ref, loc_ref, tile + 1, copy_in(1 - b))

        _wait_copies(total_ref[tile], copy_in(b))
        lo, hi = _unpack_bf16_pairs(local_ref[b])
        rows = jnp.concatenate([lo, hi], axis=1).astype(BF16)
        slots = slot_ref[...]
        gates = gate_ref[...]
        row = lax.broadcasted_iota(jnp.int32, (ROW_TILE, LOCAL_ROWS), 1)
        weight = jnp.zeros((ROW_TILE, LOCAL_ROWS), F32)
        for k in range(TOP_K):
            weight = jnp.where(slots[:, k:k + 1] == row, gates[:, k:k + 1], weight)
        y = _dot(weight.astype(BF16), rows)
        o_ref[...] = _layer_norm(DN_ALPHA * x_ref[...] + y, g_ref[...], b_ref[...])

    for b in range(2):
        pl.when(buf == b)(functools.partial(run, b))


def _combine(x2, rows_out, slots_c, gates_c, tables, g, b, first_tile, n_tiles):
    const = lambda i, *_: (0, 0)
    tile = lambda i, *_: (first_tile + i, 0)
    return pl.pallas_call(
        functools.partial(_combine_kernel, first_tile),
        grid_spec=pltpu.PrefetchScalarGridSpec(
            num_scalar_prefetch=4,
            grid=(n_tiles,),
            in_specs=[
                pl.BlockSpec((ROW_TILE, D_MODEL), tile),
                pl.BlockSpec(memory_space=pl.ANY),
                pl.BlockSpec((ROW_TILE, TOP_K), tile),
                pl.BlockSpec((ROW_TILE, TOP_K), tile),
                pl.BlockSpec((1, D_MODEL), const),
                pl.BlockSpec((1, D_MODEL), const),
            ],
            out_specs=pl.BlockSpec((ROW_TILE, D_MODEL), lambda i, *_: (i, 0)),
            scratch_shapes=[pltpu.VMEM((2, LOCAL_ROWS, HALF), jnp.uint32), pltpu.SemaphoreType.DMA((2,))],
        ),
        out_shape=jax.ShapeDtypeStruct((n_tiles * ROW_TILE, D_MODEL), F32),
        compiler_params=_cparams(("arbitrary",)),
        name="combine_ln3",
    )(*tables[:4], x2, rows_out, slots_c, gates_c, g.reshape(1, D_MODEL), b.reshape(1, D_MODEL))


def kernel(x_prompt, x_sample, mem_prompt, mem_sample, rel_bias, w_in, b_gates, conv_w, conv_b, mh_gain, attn_sink,
           w_out, ln1_g, ln1_b, wq_mem, wkv_mem, wo_mem, ln2_g, ln2_b, w_router, b_router, w_up, b_up, w_down,
           b_down, ln3_g, ln3_b):
    assert w_in.shape[0] == 1, "single layer"
    bp, sp, _ = x_prompt.shape
    bs, ss, _ = x_sample.shape
    tp, ts = bp * sp, bs * ss
    t = tp + ts
    assert sp % ROW_TILE == 0 and ss % ROW_TILE == 0 and sp >= 2 * BLOCK and ss >= 2 * BLOCK
    xp = x_prompt.reshape(tp, D_MODEL)
    xs = x_sample.reshape(ts, D_MODEL)

    def seq_blocks(rows):
        return (tp // rows, sp // rows, ss // rows)

    w = w_in[0]
    q_end, k_end, v_end = ATT_WIDTH, ATT_WIDTH + 128, ATT_WIDTH + 256
    qk_end, mv_end, mo_end = v_end + 2 * ML_WIDTH, v_end + 3 * ML_WIDTH, v_end + 4 * ML_WIDTH
    head_order = np.concatenate([[h, ATT_GROUP + h] for h in range(ATT_GROUP)])
    att_perm = (head_order[:, None] * ATT_HEAD_DIM + np.arange(ATT_HEAD_DIM)[None, :]).reshape(-1)
    w_qk = w[:, v_end:qk_end].astype(BF16)
    w_main = jnp.concatenate([
        w[:, qk_end:mv_end], w[:, mv_end:mo_end],
        w[:, :q_end][:, att_perm] * (ATT_HEAD_DIM ** -0.5), w[:, q_end:k_end], w[:, k_end:v_end]], axis=1).astype(BF16)
    w_g = w[:, mo_end:].astype(BF16)
    w_att = w_out[0][:ATT_WIDTH][att_perm].astype(BF16)
    w_ml = w_out[0][ATT_WIDTH:].astype(BF16)

    qk, z, gates_c, gates_r = _in_proj(xp, xs, w_qk, w_main, w_g, b_gates[0], conv_w[0], conv_b[0],
                                       seq_blocks(ROW_TILE))

    bias, sink = _attention_tables(rel_bias, attn_sink[0])
    att = _attention(z, bias, sink, seq_blocks(BLOCK))

    h_f, h_b = _mlstm(qk, z, gates_r, gates_c, seq_blocks(ML_CHUNK))

    mem = jnp.concatenate([mem_prompt.reshape(bp * MEM_TOKENS, D_MODEL), mem_sample.reshape(bs * MEM_TOKENS, D_MODEL)])
    kv = _mem_kv(mem, wkv_mem[0].astype(BF16))
    n_p_tiles, p_tiles_per_seq, s_tiles_per_seq = seq_blocks(ROW_TILE)
    mem_of_tile = lambda i: jnp.where(i < n_p_tiles, i // p_tiles_per_seq, bp + (i - n_p_tiles) // s_tiles_per_seq)
    wq = (wq_mem[0] * (MEM_HEAD_DIM ** -0.5)).astype(BF16)
    x2, x2b, top_gate, slots, tile_chunks = _cross_router(
        xp, xs, att, h_f, h_b, z, w_att, w_ml, mh_gain[0], ln1_g[0], ln1_b[0],
        kv, wq, wo_mem[0].astype(BF16), ln2_g[0], ln2_b[0], w_router[0], b_router[0], mem_of_tile)

    n_tiles = t // ROW_TILE
    blk_chunks = MOE_ROWS // SEG_ALIGN
    chunks = tile_chunks[:, :, 0]
    used = jnp.sum(chunks, axis=0)
    region = ((used + blk_chunks - 1) // blk_chunks) * blk_chunks
    region_end = jnp.cumsum(region)
    region_start = region_end - region
    seg_start = region_start[None, :] + jnp.cumsum(chunks, axis=0) - chunks
    loc_start = jnp.cumsum(chunks, axis=1) - chunks
    n_blk = -(-(t * TOP_K + n_tiles * N_EXPERTS * (SEG_ALIGN - 1) + N_EXPERTS * (MOE_ROWS - 1)) // MOE_ROWS)
    blk_first = jnp.arange(n_blk, dtype=jnp.int32) * blk_chunks
    block_e = jnp.minimum(jnp.sum(blk_first[:, None] >= region_end[None, :], axis=1), N_EXPERTS - 1).astype(jnp.int32)
    n_active = (region_end[-1:] // blk_chunks).astype(jnp.int32)
    tables = (chunks.reshape(-1), seg_start.reshape(-1), loc_start.reshape(-1), jnp.sum(chunks, axis=1),
              region_start + used, region - used, n_active)
    tables = tuple(tb.astype(jnp.int32) for tb in tables)

    rows_in = _dispatch(x2b, slots, tables, n_blk * MOE_ROWS)
    rows_out = _experts(rows_in, block_e, n_active, w_up[0], b_up[0], w_down[0], b_down[0])

    slots_c, gates_c = slots.T, top_gate.T
    y_p = _combine(x2, rows_out, slots_c, gates_c, tables, ln3_g[0], ln3_b[0], 0, n_p_tiles)
    y_s = _combine(x2, rows_out, slots_c, gates_c, tables, ln3_g[0], ln3_b[0], n_p_tiles, n_tiles - n_p_tiles)
    return y_p.reshape(bp, sp, D_MODEL), y_s.reshape(bs, ss, D_MODEL)
```

```python
import functools

import numpy as np
import jax
import jax.numpy as jnp
from jax import lax
from jax.experimental import pallas as pl
from jax.experimental.pallas import tpu as pltpu

F32 = jnp.float32
BF16 = jnp.bfloat16

D_MODEL = 1024
ATT_HEADS = 8
ATT_KV_HEADS = 2
ATT_GROUP = ATT_HEADS // ATT_KV_HEADS
ATT_HEAD_DIM = 64
ATT_WIDTH = ATT_HEADS * ATT_HEAD_DIM
WINDOW = 128
BLOCK = WINDOW
N_BUCKETS = 32
MAX_DISTANCE = 128
ML_HEADS = 4
ML_HEAD_DIM = 128
ML_WIDTH = ML_HEADS * ML_HEAD_DIM
ML_CHUNK = 256
CONV_WIDTH = 5
N_GATES = 4 * ML_HEADS
MEM_TOKENS = 256
MEM_HEADS = 4
MEM_HEAD_DIM = D_MODEL // MEM_HEADS
N_EXPERTS = 32
TOP_K = 4
D_FF = D_MODEL
SWIGLU_LIMIT = 7.0
SWIGLU_ALPHA = 1.702
LN_EPS = 1e-5
DN_ALPHA = 2.0 ** 0.25

QK_WIDTH = 2 * ML_WIDTH
Z_WIDTH = ML_WIDTH + ML_WIDTH + ATT_WIDTH + 2 * ATT_KV_HEADS * ATT_HEAD_DIM
ZB_MLV = 0
ZB_MLO = 1
ZB_ATTQ = 2
ZB_ATTK = 12
ZB_ATTV = 13

ROW_TILE = 512
MOE_ROWS = 512
NEG = -1e30
VMEM_LIMIT = 56 * 1024 * 1024


def _cparams(sem):
    return pltpu.CompilerParams(dimension_semantics=sem, vmem_limit_bytes=VMEM_LIMIT)


def _dot(a, b):
    return jnp.dot(a, b, preferred_element_type=F32)


def _dot_nt(a, b):
    return lax.dot_general(a, b, (((1,), (1,)), ((), ())), preferred_element_type=F32)


def _dot_tn(a, b):
    return lax.dot_general(a, b, (((0,), (0,)), ((), ())), preferred_element_type=F32)


def _split3(x):
    hi = x.astype(BF16)
    rest = x - hi.astype(F32)
    mid = rest.astype(BF16)
    lo = (rest - mid.astype(F32)).astype(BF16)
    return hi, mid, lo


def _layer_norm(y, g, b):
    mu = jnp.mean(y, axis=-1, keepdims=True)
    yc = y - mu
    var = jnp.mean(yc * yc, axis=-1, keepdims=True)
    return yc * lax.rsqrt(var + LN_EPS) * g + b


def _log_sigmoid(x):
    return jnp.minimum(x, 0.0) - jnp.log1p(jnp.exp(-jnp.abs(x)))


def _sigmoid(x):
    return 1.0 / (1.0 + jnp.exp(-x))


def _trace_stagewise(chains):
    while chains:
        chains = [c for c in chains if next(c, "done") != "done"]


def _seq_pos(blk, n_p_blocks, p_blocks_per_seq, s_blocks_per_seq):
    in_p = blk < n_p_blocks
    local = jnp.where(in_p, blk % p_blocks_per_seq, (blk - n_p_blocks) % s_blocks_per_seq)
    per = jnp.where(in_p, p_blocks_per_seq, s_blocks_per_seq)
    return local == 0, local == per - 1


HALO = 8


def _in_proj_kernel(seq, xp_ref, xpl_ref, xpr_ref, xs_ref, xsl_ref, xsr_ref, wqk_ref, w_ref, bg_ref,
                    cw_ref, cb_ref, scale_ref, qk_ref, z_ref, gc_ref, gr_ref, buf_ref):
    i = pl.program_id(0)
    in_p = i < seq[0]
    first, last = _seq_pos(i, *seq)
    x = jnp.where(in_p, xp_ref[...], xs_ref[...])
    left = jnp.where(in_p, xpl_ref[...], xsl_ref[...])
    right = jnp.where(in_p, xpr_ref[...], xsr_ref[...])
    xb = x.astype(BF16)
    x_ext = jnp.concatenate([left, x, right], axis=0).astype(BF16)
    row = lax.broadcasted_iota(jnp.int32, (ROW_TILE + 2 * HALO, 1), 0)
    outside = (first & (row < HALO)) | (last & (row >= ROW_TILE + HALO))
    buf_ref[...] = jnp.where(outside, 0.0, _dot(x_ext, wqk_ref[...]))
    acc = jnp.zeros((ROW_TILE, QK_WIDTH), F32) + cb_ref[...]
    for j in range(CONV_WIDTH):
        off = HALO + j - CONV_WIDTH // 2
        acc = acc + buf_ref[off:off + ROW_TILE, :] * cw_ref[j:j + 1, :]
    qk_ref[...] = (acc * _sigmoid(acc) * scale_ref[...]).astype(BF16)
    zg = _dot(xb, w_ref[...])
    z_ref[...] = zg[:, :Z_WIDTH].astype(BF16)
    gates = zg[:, Z_WIDTH:] + bg_ref[...]
    gc_ref[...] = gates[:, :N_GATES]
    gr_ref[...] = gates.T[:N_GATES, :]


def _in_proj(xp, xs, w_qk, w_main, w_g, b_g, conv_w, conv_b, seq):
    w_main = jnp.pad(jnp.concatenate([w_main, w_g], axis=1), ((0, 0), (0, 128 - N_GATES)))
    b_g = jnp.pad(b_g, (0, 128 - N_GATES)).reshape(1, 128)
    tp, ts = xp.shape[0], xs.shape[0]
    t = tp + ts
    n_p = tp // ROW_TILE
    n = t // ROW_TILE
    r = ROW_TILE // HALO
    const = lambda i: (0, 0)
    p_tile = lambda i: jnp.minimum(i, n_p - 1)
    s_tile = lambda i: jnp.maximum(i - n_p, 0)
    scale = jnp.concatenate([jnp.ones((1, ML_WIDTH), F32), jnp.full((1, ML_WIDTH), ML_HEAD_DIM ** -0.5, F32)], axis=1)
    return pl.pallas_call(
        functools.partial(_in_proj_kernel, seq),
        grid=(n,),
        in_specs=[
            pl.BlockSpec((ROW_TILE, D_MODEL), lambda i: (p_tile(i), 0)),
            pl.BlockSpec((HALO, D_MODEL), lambda i: (jnp.maximum(p_tile(i) * r - 1, 0), 0)),
            pl.BlockSpec((HALO, D_MODEL), lambda i: (jnp.minimum((p_tile(i) + 1) * r, tp // HALO - 1), 0)),
            pl.BlockSpec((ROW_TILE, D_MODEL), lambda i: (s_tile(i), 0)),
            pl.BlockSpec((HALO, D_MODEL), lambda i: (jnp.maximum(s_tile(i) * r - 1, 0), 0)),
            pl.BlockSpec((HALO, D_MODEL), lambda i: (jnp.minimum((s_tile(i) + 1) * r, ts // HALO - 1), 0)),
            pl.BlockSpec((D_MODEL, QK_WIDTH), const),
            pl.BlockSpec((D_MODEL, Z_WIDTH + 128), const),
            pl.BlockSpec((1, 128), const),
            pl.BlockSpec((CONV_WIDTH, QK_WIDTH), const),
            pl.BlockSpec((1, QK_WIDTH), const),
            pl.BlockSpec((1, QK_WIDTH), const),
        ],
        out_specs=[
            pl.BlockSpec((ROW_TILE, QK_WIDTH), lambda i: (i, 0)),
            pl.BlockSpec((ROW_TILE, Z_WIDTH), lambda i: (i, 0)),
            pl.BlockSpec((ROW_TILE, N_GATES), lambda i: (i, 0)),
            pl.BlockSpec((N_GATES, ROW_TILE), lambda i: (0, i)),
        ],
        out_shape=[
            jax.ShapeDtypeStruct((t, QK_WIDTH), BF16),
            jax.ShapeDtypeStruct((t, Z_WIDTH), BF16),
            jax.ShapeDtypeStruct((t, N_GATES), F32),
            jax.ShapeDtypeStruct((N_GATES, t), F32),
        ],
        scratch_shapes=[pltpu.VMEM((ROW_TILE + 2 * HALO, QK_WIDTH), F32)],
        compiler_params=_cparams(("arbitrary",)),
        name="in_proj_conv",
    )(xp, xp, xp, xs, xs, xs, w_qk, w_main, b_g, conv_w, conv_b.reshape(1, QK_WIDTH), scale)


ATT_TILE = 512
ATT_SUB = ATT_TILE // BLOCK


ONES_ROWS = 16


def _attention_kernel(seq, q_ref, kp_ref, kc_ref, kn_ref, vp_ref, vc_ref, vn_ref, bias_ref, sink_ref,
                      o_ref, klo_ref, khi_ref, vt_ref):
    i = pl.program_id(0)
    lane = lax.broadcasted_iota(jnp.int32, (ATT_TILE + 2 * BLOCK, 2 * ATT_HEAD_DIM), 1)
    kband = jnp.concatenate([kp_ref[...], kc_ref[...], kn_ref[...]], axis=0)
    zero = jnp.zeros_like(kband)
    klo_ref[...] = jnp.where(lane < ATT_HEAD_DIM, kband, zero)
    khi_ref[...] = jnp.where(lane < ATT_HEAD_DIM, zero, kband)
    vband = jnp.concatenate([vp_ref[...], vc_ref[...], vn_ref[...]], axis=0)
    vt_ref[0:2 * ATT_HEAD_DIM, :] = vband.T
    vt_ref[2 * ATT_HEAD_DIM:, :] = jnp.ones((ONES_ROWS, ATT_TILE + 2 * BLOCK), BF16)
    feat = lax.broadcasted_iota(jnp.int32, (2 * ATT_HEAD_DIM, ATT_GROUP * BLOCK), 0)
    def block(s):
        first, last = _seq_pos(i * ATT_SUB + s, *seq)
        variant = jnp.where(first, 1, jnp.where(last, 2, 0))
        q = q_ref[s * BLOCK:(s + 1) * BLOCK, :]
        q_all = jnp.concatenate([q[:, t * 128:(t + 1) * 128] for t in range(ATT_GROUP)], axis=0)
        vt = vt_ref[:, s * BLOCK:(s + 3) * BLOCK]
        scores = [_dot_nt(k_ref[s * BLOCK:(s + 3) * BLOCK, :], q_all) for k_ref in (klo_ref, khi_ref)]
        yield
        ms, ps = [], []
        for kv in range(ATT_KV_HEADS):
            logits = scores[kv] + bias_ref[variant, kv]
            ms.append(jnp.maximum(jnp.max(logits, axis=0, keepdims=True), sink_ref[kv]))
            ps.append(jnp.exp(logits - ms[kv]).astype(BF16))
        yield
        ovs = [_dot(vt, p) for p in ps]
        yield
        outs = []
        for kv in range(ATT_KV_HEADS):
            den = ovs[kv][2 * ATT_HEAD_DIM:2 * ATT_HEAD_DIM + 1, :] + jnp.exp(sink_ref[kv] - ms[kv])
            outs.append(ovs[kv][0:2 * ATT_HEAD_DIM, :] * (1.0 / den))
        both = jnp.where(feat < ATT_HEAD_DIM, outs[0], outs[1]).astype(BF16)
        for t in range(ATT_GROUP):
            o_ref[t * 128:(t + 1) * 128, s * BLOCK:(s + 1) * BLOCK] = both[:, t * BLOCK:(t + 1) * BLOCK]

    _trace_stagewise([block(s) for s in range(ATT_SUB)])


def _attention(z, bias, sink, seq):
    t = z.shape[0]
    n = t // ATT_TILE
    nblk = t // BLOCK
    band = ATT_TILE + 2 * BLOCK
    prev = lambda i: jnp.maximum(i * ATT_SUB - 1, 0)
    nxt = lambda i: jnp.minimum((i + 1) * ATT_SUB, nblk - 1)
    return pl.pallas_call(
        functools.partial(_attention_kernel, seq),
        grid=(n,),
        in_specs=[
            pl.BlockSpec((ATT_TILE, ATT_WIDTH), lambda i: (i, ZB_ATTQ)),
            pl.BlockSpec((BLOCK, 128), lambda i: (prev(i), ZB_ATTK)),
            pl.BlockSpec((ATT_TILE, 128), lambda i: (i, ZB_ATTK)),
            pl.BlockSpec((BLOCK, 128), lambda i: (nxt(i), ZB_ATTK)),
            pl.BlockSpec((BLOCK, 128), lambda i: (prev(i), ZB_ATTV)),
            pl.BlockSpec((ATT_TILE, 128), lambda i: (i, ZB_ATTV)),
            pl.BlockSpec((BLOCK, 128), lambda i: (nxt(i), ZB_ATTV)),
            pl.BlockSpec((3, ATT_KV_HEADS, 3 * BLOCK, ATT_GROUP * BLOCK), lambda i: (0, 0, 0, 0)),
            pl.BlockSpec((ATT_KV_HEADS, 1, ATT_GROUP * BLOCK), lambda i: (0, 0, 0)),
        ],
        out_specs=pl.BlockSpec((ATT_WIDTH, ATT_TILE), lambda i: (0, i)),
        out_shape=jax.ShapeDtypeStruct((ATT_WIDTH, t), BF16),
        scratch_shapes=[pltpu.VMEM((band, 128), BF16), pltpu.VMEM((band, 128), BF16),
                        pltpu.VMEM((2 * ATT_HEAD_DIM + ONES_ROWS, band), BF16)],
        compiler_params=_cparams(("arbitrary",)),
        name="window_attention",
    )(z, z, z, z, z, z, z, bias, sink)


def _t5_bucket(rel):
    half = N_BUCKETS // 2
    exact = half // 2
    n = np.abs(rel)
    large = exact + (np.log(np.maximum(n, 1) / exact) / np.log(MAX_DISTANCE / exact) * (half - exact)).astype(np.int32)
    large = np.minimum(large, half - 1)
    return ((rel > 0).astype(np.int32) * half + np.where(n < exact, n, large)).astype(np.int32)


def _attention_tables(rel_bias, attn_sink):
    rel = np.arange(3 * BLOCK)[:, None] - BLOCK - np.arange(BLOCK)[None, :]
    onehot = jnp.asarray(_t5_bucket(rel)[..., None] == np.arange(N_BUCKETS), F32)
    bias = jnp.einsum('kqb,bh->hkq', onehot, rel_bias.astype(F32), precision=lax.Precision.HIGHEST)
    bias = jnp.where(jnp.asarray(np.abs(rel) <= WINDOW)[None], bias, NEG)
    bias = bias.reshape(ATT_KV_HEADS, ATT_GROUP, 3 * BLOCK, BLOCK).transpose(0, 2, 1, 3)
    bias = bias.reshape(ATT_KV_HEADS, 3 * BLOCK, ATT_GROUP * BLOCK)
    key = np.arange(3 * BLOCK)[None, :, None]
    first = jnp.where(jnp.asarray(key < BLOCK), NEG, bias)
    last = jnp.where(jnp.asarray(key >= 2 * BLOCK), NEG, bias)
    sink = jnp.repeat(attn_sink.astype(F32), BLOCK).reshape(ATT_KV_HEADS, 1, ATT_GROUP * BLOCK)
    return jnp.stack([bias, first, last]), sink


def _mlstm_kernel(seq, n_chunks, *refs):
    fwd_in, bwd_in, (of_ref, ob_ref), state = refs[0:5], refs[5:10], refs[10:12], refs[12:]
    _trace_stagewise(_mlstm_chains(seq, False, n_chunks, *fwd_in, of_ref, *state[0:2])
                     + _mlstm_chains(seq, True, n_chunks, *bwd_in, ob_ref, *state[2:4]))


def _mlstm_chains(seq, reverse, n_chunks, q_ref, k_ref, v_ref, gr_ref, gc_ref, o_ref, ct_ref, m_ref):
    step = pl.program_id(0)
    chunk = (n_chunks - 1 - step) if reverse else step
    first, last = _seq_pos(chunk, *seq)
    fresh = last if reverse else first

    L = ML_CHUNK
    row = lax.broadcasted_iota(jnp.int32, (L, L), 0)
    col = lax.broadcasted_iota(jnp.int32, (L, L), 1)
    vis = (row >= col) if reverse else (row <= col)
    vis_t = (col >= row) if reverse else (col <= row)
    gr = gr_ref[...]
    gc = gc_ref[...]
    r3 = _dot(jnp.concatenate(_split3(_log_sigmoid(gr)), axis=0), vis.astype(BF16))
    b_rows = r3[0:N_GATES] + r3[N_GATES:2 * N_GATES] + r3[2 * N_GATES:]
    vis_tb = vis_t.astype(BF16)
    c_hi, c_mid, c_lo = _split3(_log_sigmoid(gc))
    b_cols = _dot(vis_tb, c_hi) + _dot(vis_tb, c_mid) + _dot(vis_tb, c_lo)
    i_off = 2 * ML_HEADS if reverse else 0
    f_off = i_off + ML_HEADS
    end = 0 if reverse else L - 1
    ones = jnp.ones((ONES_ROWS, L), BF16)

    def chain(h):
        hs = slice(h * ML_HEAD_DIM, (h + 1) * ML_HEAD_DIM)
        q = q_ref[:, hs]
        k = k_ref[:, hs]
        vt1 = jnp.concatenate([v_ref[:, hs].T, ones], axis=0)
        b_row = b_rows[f_off + h:f_off + h + 1, :]
        u_row = gr[i_off + h:i_off + h + 1, :] - b_row
        u_col = gc[:, i_off + h:i_off + h + 1] - b_cols[:, f_off + h:f_off + h + 1]
        g = b_row[:, end:end + 1]
        ct_old = jnp.where(fresh, 0.0, ct_ref[h])
        m_old = jnp.where(fresh, 0.0, m_ref[h])[:, 0:1]
        kq = _dot_nt(k, q)
        from_state = _dot_nt(ct_old.astype(BF16), q)
        yield
        u_mat = jnp.where(vis, u_col, NEG)
        mm = jnp.maximum(jnp.max(u_mat, axis=0, keepdims=True), m_old)
        st = (kq * jnp.exp(u_mat - mm)).astype(BF16)
        yield
        tot = _dot(vt1, st) + jnp.exp(m_old - mm) * from_state
        a_max = jnp.max(g + u_row, axis=-1, keepdims=True)
        m_new = jnp.maximum(g + m_old, a_max)
        s_old = jnp.exp(g + m_old - m_new)
        weighted = (vt1.astype(F32) * jnp.exp(g + u_row - m_new)).astype(BF16)
        new_state = _dot(weighted, k)
        yield
        den = tot[ML_HEAD_DIM:ML_HEAD_DIM + 1, :]
        floor = jnp.exp(-(b_row + mm))
        o_ref[hs, :] = (tot[0:ML_HEAD_DIM, :] * (1.0 / jnp.maximum(jnp.abs(den), floor))).astype(BF16)
        ct_ref[h] = s_old * ct_old + new_state
        m_ref[h] = jnp.broadcast_to(m_new, (1, ML_HEAD_DIM))

    return [chain(h) for h in range(ML_HEADS)]


def _mlstm(qk, z, gates_r, gates_c, seq):
    t = qk.shape[0]
    nc = t // ML_CHUNK

    def chunk_specs(ch):
        return [
            pl.BlockSpec((ML_CHUNK, ML_WIDTH), lambda i: (ch(i), 0)),
            pl.BlockSpec((ML_CHUNK, ML_WIDTH), lambda i: (ch(i), 1)),
            pl.BlockSpec((ML_CHUNK, ML_WIDTH), lambda i: (ch(i), ZB_MLV)),
            pl.BlockSpec((N_GATES, ML_CHUNK), lambda i: (0, ch(i))),
            pl.BlockSpec((ML_CHUNK, N_GATES), lambda i: (ch(i), 0)),
        ]

    fwd = lambda i: i
    bwd = lambda i: nc - 1 - i
    state = [
        pltpu.VMEM((ML_HEADS, ML_HEAD_DIM + ONES_ROWS, ML_HEAD_DIM), F32),
        pltpu.VMEM((ML_HEADS, 1, ML_HEAD_DIM), F32),
    ]
    operands = (qk, qk, z, gates_r, gates_c)
    return pl.pallas_call(
        functools.partial(_mlstm_kernel, seq, nc),
        grid=(nc,),
        in_specs=chunk_specs(fwd) + chunk_specs(bwd),
        out_specs=[pl.BlockSpec((ML_WIDTH, ML_CHUNK), lambda i: (0, fwd(i))),
                   pl.BlockSpec((ML_WIDTH, ML_CHUNK), lambda i: (0, bwd(i)))],
        out_shape=[jax.ShapeDtypeStruct((ML_WIDTH, t), BF16), jax.ShapeDtypeStruct((ML_WIDTH, t), BF16)],
        scratch_shapes=state + state,
        compiler_params=_cparams(("arbitrary",)),
        name="mlstm",
    )(*operands, *operands)


def _mix_out(n_p_tiles, toks, xp_ref, xs_ref, att_ref, hf_ref, hb_ref, og_ref, wa_ref, wm_ref, gain_ref, g_ref,
             b_ref):
    i = pl.program_id(0)
    x = jnp.where(i < n_p_tiles, xp_ref[toks, :], xs_ref[toks, :])
    h = hf_ref[:, toks].astype(F32) + hb_ref[:, toks].astype(F32)
    parts = []
    for hd in range(ML_HEADS):
        hh = h[hd * ML_HEAD_DIM:(hd + 1) * ML_HEAD_DIM, :]
        mu = jnp.mean(hh, axis=0, keepdims=True)
        hc = hh - mu
        var = jnp.mean(hc * hc, axis=0, keepdims=True)
        parts.append(hc * lax.rsqrt(var + LN_EPS))
    hn = jnp.concatenate(parts, axis=0) * gain_ref[...] * _sigmoid(og_ref[toks, :].T.astype(F32))
    mixed = _dot(att_ref[:, toks].T, wa_ref[...]) + _dot(hn.astype(BF16).T, wm_ref[...])
    return _layer_norm(DN_ALPHA * x + mixed, g_ref[...], b_ref[...])


def _mem_kv_kernel(m_ref, w_ref, o_ref):
    o_ref[...] = _dot(m_ref[...].astype(BF16), w_ref[...]).astype(BF16)


def _mem_kv(mem, wkv):
    rows = mem.shape[0]
    return pl.pallas_call(
        _mem_kv_kernel,
        grid=(rows // MEM_TOKENS,),
        in_specs=[pl.BlockSpec((MEM_TOKENS, D_MODEL), lambda i: (i, 0)),
                  pl.BlockSpec((D_MODEL, 2 * D_MODEL), lambda i: (0, 0))],
        out_specs=pl.BlockSpec((MEM_TOKENS, 2 * D_MODEL), lambda i: (i, 0)),
        out_shape=jax.ShapeDtypeStruct((rows, 2 * D_MODEL), BF16),
        compiler_params=_cparams(("arbitrary",)),
        name="mem_kv",
    )(mem, wkv)


def _pack_bf16_pairs(a, b, exact=False):
    if not exact:
        a, b = a.astype(BF16).astype(F32), b.astype(BF16).astype(F32)
    ua, ub = pltpu.bitcast(a, jnp.uint32), pltpu.bitcast(b, jnp.uint32)
    return (ua >> 16) | (ub & jnp.uint32(0xFFFF0000))


def _unpack_bf16_pairs(u):
    lo = pltpu.bitcast(u << 16, F32)
    hi = pltpu.bitcast(u & jnp.uint32(0xFFFF0000), F32)
    return lo, hi


TOKEN_GROUPS = 2


def _cross_router_kernel(n_p_tiles, *refs):
    mix_refs, (kv_ref, wq_ref, wo_ref, g_ref, b_ref, wr_ref, br_ref, tri_ref, etri_ref,
               x2_ref, x2b_ref, gate_ref, slot_ref, chunk_ref) = refs[:11], refs[11:]
    head_cols = [slice(h * MEM_HEAD_DIM, (h + 1) * MEM_HEAD_DIM) for h in range(MEM_HEADS)]

    def token_group(r):
        toks = slice(r * ROW_TILE // TOKEN_GROUPS, (r + 1) * ROW_TILE // TOKEN_GROUPS)
        x = _mix_out(n_p_tiles, toks, *mix_refs)
        yield
        q = _dot(x.astype(BF16), wq_ref[...]).astype(BF16)
        scores = [_dot_nt(q[:, hs], kv_ref[:, hs]) for hs in head_cols]
        yield
        probs, dens = [], []
        for logits in scores:
            p = jnp.exp(logits - jnp.max(logits, axis=-1, keepdims=True))
            dens.append(jnp.sum(p, axis=-1, keepdims=True))
            probs.append(p.astype(BF16))
        yield
        values = [_dot(p, kv_ref[:, D_MODEL + hs.start:D_MODEL + hs.stop]) for p, hs in zip(probs, head_cols)]
        o = jnp.concatenate([(v * (1.0 / den)).astype(BF16) for v, den in zip(values, dens)], axis=1)
        y = _dot(o, wo_ref[...])
        yield
        x2 = _layer_norm(DN_ALPHA * x + y, g_ref[...], b_ref[...])
        x2_ref[toks, :] = x2
        x2b_ref[toks, :] = x2.astype(BF16)

    _trace_stagewise([token_group(r) for r in range(TOKEN_GROUPS)])
    x2b = x2b_ref[...]

    logits = _dot_nt(wr_ref[...], x2b) + br_ref[...]
    expert = lax.broadcasted_iota(jnp.int32, logits.shape, 0)
    work = logits
    vals, sels = [], []
    for k in range(TOP_K):
        mx = jnp.max(work, axis=0, keepdims=True)
        ix = jnp.min(jnp.where(work == mx, expert, N_EXPERTS), axis=0, keepdims=True)
        sel = expert == ix
        work = jnp.where(sel, -jnp.inf, work)
        vals.append(mx)
        sels.append(sel)
    es = [jnp.exp(v - vals[0]) for v in vals]
    tot = es[0] + es[1] + es[2] + es[3]
    chosen = jnp.zeros(logits.shape, F32)
    for k in range(TOP_K):
        gate_ref[k:k + 1, :] = es[k] / tot
        chosen = chosen + sels[k].astype(F32)
    count = jnp.sum(chosen, axis=1, keepdims=True)
    chunks = jnp.floor((count + (SEG_ALIGN - 1)) * (1.0 / SEG_ALIGN))
    chunks_b = jnp.broadcast_to(chunks, (N_EXPERTS, 128))
    seg_first = _dot(etri_ref[...], chunks_b.astype(BF16))[:, 0:1] * SEG_ALIGN
    before = _dot(chosen.astype(BF16), tri_ref[...])
    local_row = before + seg_first
    for k in range(TOP_K):
        slot_ref[k:k + 1, :] = jnp.sum(jnp.where(sels[k], local_row, 0.0), axis=0, keepdims=True).astype(jnp.int32)
    chunk_ref[0] = chunks_b.astype(jnp.int32)


def _cross_router(xp, xs, att, hf, hb, z, w_att, w_ml, gain, g1, b1, kv, wq, wo, g, b, w_router, b_router,
                  mem_of_tile):
    t = att.shape[1]
    n_p = xp.shape[0] // ROW_TILE
    n = t // ROW_TILE
    const = lambda i: (0, 0)
    tile = lambda i: (i, 0)
    lanes = lambda i: (0, i)
    tri = jnp.asarray(np.triu(np.ones((ROW_TILE, ROW_TILE), np.float32), 1), BF16)
    etri = jnp.asarray(np.tril(np.ones((N_EXPERTS, N_EXPERTS), np.float32), -1), BF16)
    return pl.pallas_call(
        functools.partial(_cross_router_kernel, n_p),
        grid=(n,),
        in_specs=[
            pl.BlockSpec((ROW_TILE, D_MODEL), lambda i: (jnp.minimum(i, n_p - 1), 0)),
            pl.BlockSpec((ROW_TILE, D_MODEL), lambda i: (jnp.maximum(i - n_p, 0), 0)),
            pl.BlockSpec((ATT_WIDTH, ROW_TILE), lanes),
            pl.BlockSpec((ML_WIDTH, ROW_TILE), lanes),
            pl.BlockSpec((ML_WIDTH, ROW_TILE), lanes),
            pl.BlockSpec((ROW_TILE, ML_WIDTH), lambda i: (i, ZB_MLO)),
            pl.BlockSpec((ATT_WIDTH, D_MODEL), const),
            pl.BlockSpec((ML_WIDTH, D_MODEL), const),
            pl.BlockSpec((ML_WIDTH, 1), const),
            pl.BlockSpec((1, D_MODEL), const),
            pl.BlockSpec((1, D_MODEL), const),
            pl.BlockSpec((MEM_TOKENS, 2 * D_MODEL), lambda i: (mem_of_tile(i), 0)),
            pl.BlockSpec((D_MODEL, D_MODEL), const),
            pl.BlockSpec((D_MODEL, D_MODEL), const),
            pl.BlockSpec((1, D_MODEL), const),
            pl.BlockSpec((1, D_MODEL), const),
            pl.BlockSpec((N_EXPERTS, D_MODEL), const),
            pl.BlockSpec((N_EXPERTS, 1), const),
            pl.BlockSpec((ROW_TILE, ROW_TILE), const),
            pl.BlockSpec((N_EXPERTS, N_EXPERTS), const),
        ],
        out_specs=[
            pl.BlockSpec((ROW_TILE, D_MODEL), tile),
            pl.BlockSpec((ROW_TILE, D_MODEL), tile),
            pl.BlockSpec((TOP_K, ROW_TILE), lanes),
            pl.BlockSpec((TOP_K, ROW_TILE), lanes),
            pl.BlockSpec((1, N_EXPERTS, 128), lambda i: (i, 0, 0)),
        ],
        out_shape=[
            jax.ShapeDtypeStruct((t, D_MODEL), F32),
            jax.ShapeDtypeStruct((t, D_MODEL), BF16),
            jax.ShapeDtypeStruct((TOP_K, t), F32),
            jax.ShapeDtypeStruct((TOP_K, t), jnp.int32),
            jax.ShapeDtypeStruct((n, N_EXPERTS, 128), jnp.int32),
        ],
        compiler_params=_cparams(("arbitrary",)),
        name="mix_out_cross_attn_router",
    )(xp, xs, att, hf, hb, z, w_att, w_ml, gain.reshape(ML_WIDTH, 1), g1.reshape(1, D_MODEL), b1.reshape(1, D_MODEL),
      kv, wq, wo, g.reshape(1, D_MODEL), b.reshape(1, D_MODEL), w_router.T.astype(BF16),
      b_router.reshape(N_EXPERTS, 1), tri, etri)


SEG_ALIGN = 8
LOCAL_ROWS = ROW_TILE * TOP_K + N_EXPERTS * SEG_ALIGN
HALF = D_MODEL // 2


def _rows(chunks):
    return pl.multiple_of(chunks * SEG_ALIGN, SEG_ALIGN)


def _segment_copies(chunks_ref, seg_ref, loc_ref, tile, make_copy):
    def per_expert(e, carry):
        entry = tile * N_EXPERTS + e

        @pl.when(chunks_ref[entry] > 0)
        def _():
            make_copy(_rows(loc_ref[entry]), _rows(seg_ref[entry]), _rows(chunks_ref[entry])).start()

        return carry

    lax.fori_loop(0, N_EXPERTS, per_expert, 0)


def _wait_copies(chunks, make_copy):
    @pl.when(chunks > 0)
    def _():
        make_copy(0, 0, _rows(chunks)).wait()


def _dispatch_kernel(chunks_ref, seg_ref, loc_ref, total_ref, tail_ref, tailn_ref, nact_ref,
                     x_ref, slot_ref, rows_hbm, local_ref, zero_ref, sems):
    i = pl.program_id(0)
    n = pl.num_programs(0)
    buf = i % 2

    def copy_out(b):
        return lambda loc, seg, rows: pltpu.make_async_copy(
            local_ref.at[b, pl.ds(loc, rows), :], rows_hbm.at[pl.ds(seg, rows), :], sems.at[b])

    def run(b):
        @pl.when(i >= 2)
        def _():
            _wait_copies(total_ref[jnp.maximum(i - 2, 0)], copy_out(b))

        slots = slot_ref[...].astype(F32)
        row = lax.broadcasted_iota(jnp.int32, (LOCAL_ROWS, ROW_TILE), 0).astype(F32)
        miss = (slots[0:1, :] - row) * (slots[1:2, :] - row)
        for k in range(2, TOP_K):
            miss = miss * (slots[k:k + 1, :] - row)
        perm = jnp.where(miss == 0.0, 1.0, 0.0).astype(BF16)
        picked = _dot(perm, x_ref[...])
        local_ref[b] = _pack_bf16_pairs(picked[:, :HALF], picked[:, HALF:], exact=True)
        _segment_copies(chunks_ref, seg_ref, loc_ref, i, copy_out(b))

    for b in range(2):
        pl.when(buf == b)(functools.partial(run, b))

    @pl.when(i == n - 1)
    def _():
        for b in range(2):
            step = jnp.where(buf == b, i, i - 1)
            _wait_copies(total_ref[step], copy_out(b))
        zero_ref[...] = jnp.zeros_like(zero_ref)
        fill = lambda _, seg, rows: pltpu.make_async_copy(
            zero_ref.at[pl.ds(0, rows), :], rows_hbm.at[pl.ds(seg, rows), :], sems.at[2])

        def per_expert(e, carry):
            @pl.when(tailn_ref[e] > 0)
            def _():
                fill(0, _rows(tail_ref[e]), _rows(tailn_ref[e])).start()

            _wait_copies(tailn_ref[e], fill)
            return carry

        lax.fori_loop(0, N_EXPERTS, per_expert, 0)
        fill_block = lambda blk: pltpu.make_async_copy(
            zero_ref, rows_hbm.at[pl.ds(pl.multiple_of(blk * MOE_ROWS, MOE_ROWS), MOE_ROWS), :], sems.at[2])
        n_blocks = rows_hbm.shape[0] // MOE_ROWS

        def start_block(blk, carry):
            fill_block(blk).start()
            return carry

        def wait_block(blk, carry):
            fill_block(0).wait()
            return carry

        lax.fori_loop(nact_ref[0], n_blocks, start_block, 0)
        lax.fori_loop(nact_ref[0], n_blocks, wait_block, 0)


def _dispatch(x2b, slots, tables, n_rows):
    n = x2b.shape[0] // ROW_TILE
    assert n >= 2
    return pl.pallas_call(
        _dispatch_kernel,
        grid_spec=pltpu.PrefetchScalarGridSpec(
            num_scalar_prefetch=7,
            grid=(n,),
            in_specs=[
                pl.BlockSpec((ROW_TILE, D_MODEL), lambda i, *_: (i, 0)),
                pl.BlockSpec((TOP_K, ROW_TILE), lambda i, *_: (0, i)),
            ],
            out_specs=pl.BlockSpec(memory_space=pl.ANY),
            scratch_shapes=[
                pltpu.VMEM((2, LOCAL_ROWS, HALF), jnp.uint32),
                pltpu.VMEM((MOE_ROWS, HALF), jnp.uint32),
                pltpu.SemaphoreType.DMA((3,)),
            ],
        ),
        out_shape=jax.ShapeDtypeStruct((n_rows, HALF), jnp.uint32),
        compiler_params=_cparams(("arbitrary",)),
        name="moe_dispatch",
    )(*tables, x2b, slots)


def _expert_kernel(be_ref, na_ref, x_ref, wu_ref, bu_ref, wd_ref, bd_ref, o_ref, wub_ref, wdb_ref):
    i = pl.program_id(0)

    @pl.when((i == 0) | (be_ref[i] != be_ref[jnp.maximum(i - 1, 0)]))
    def _():
        wub_ref[...] = wu_ref[0].astype(BF16)
        wdb_ref[...] = wd_ref[0].astype(BF16)

    @pl.when(i < na_ref[0])
    def _():
        lo, hi = _unpack_bf16_pairs(x_ref[...])
        x = jnp.concatenate([lo, hi], axis=1).astype(BF16)
        hu = _dot(x, wub_ref[...]) + bu_ref[0]
        h_glu = jnp.minimum(hu[:, :D_FF], SWIGLU_LIMIT)
        h_lin = jnp.clip(hu[:, D_FF:], -SWIGLU_LIMIT, SWIGLU_LIMIT)
        hh = h_glu * _sigmoid(SWIGLU_ALPHA * h_glu) * (h_lin + 1.0)
        y = _dot(hh.astype(BF16), wdb_ref[...]) + bd_ref[0]
        o_ref[...] = _pack_bf16_pairs(y[:, :D_MODEL // 2], y[:, D_MODEL // 2:])

    @pl.when(i >= na_ref[0])
    def _():
        o_ref[...] = jnp.zeros_like(o_ref)


def _experts(xs, block_e, n_active, w_up, b_up, w_down, b_down):
    n_rows = xs.shape[0]
    n_blk = n_rows // MOE_ROWS
    return pl.pallas_call(
        _expert_kernel,
        grid_spec=pltpu.PrefetchScalarGridSpec(
            num_scalar_prefetch=2,
            grid=(n_blk,),
            in_specs=[
                pl.BlockSpec((MOE_ROWS, D_MODEL // 2), lambda i, be, na: (jnp.minimum(i, na[0] - 1), 0)),
                pl.BlockSpec((1, D_MODEL, 2 * D_FF), lambda i, be, na: (be[i], 0, 0)),
                pl.BlockSpec((1, 1, 2 * D_FF), lambda i, be, na: (be[i], 0, 0)),
                pl.BlockSpec((1, D_FF, D_MODEL), lambda i, be, na: (be[i], 0, 0)),
                pl.BlockSpec((1, 1, D_MODEL), lambda i, be, na: (be[i], 0, 0)),
            ],
            out_specs=pl.BlockSpec((MOE_ROWS, D_MODEL // 2), lambda i, be, na: (i, 0)),
            scratch_shapes=[pltpu.VMEM((D_MODEL, 2 * D_FF), BF16), pltpu.VMEM((D_FF, D_MODEL), BF16)],
        ),
        out_shape=jax.ShapeDtypeStruct((n_rows, D_MODEL // 2), jnp.uint32),
        compiler_params=_cparams(("arbitrary",)),
        name="experts",
    )(block_e, n_active, xs, w_up, b_up.reshape(N_EXPERTS, 1, 2 * D_FF), w_down, b_down.reshape(N_EXPERTS, 1, D_MODEL))


def _combine_kernel(first_tile, chunks_ref, seg_ref, loc_ref, total_ref,
                    x_ref, rows_hbm, slot_ref, gate_ref, g_ref, b_ref, o_ref, local_ref, sems):
    i = pl.program_id(0)
    n = pl.num_programs(0)
    tile = first_tile + i
    buf = i % 2

    def copy_in(b):
        return lambda loc, seg, rows: pltpu.make_async_copy(
            rows_hbm.at[pl.ds(seg, rows), :], local_ref.at[b, pl.ds(loc, rows), :], sems.at[b])

    @pl.when(i == 0)
    def _():
        local_ref[...] = jnp.zeros_like(local_ref)
        _segment_copies(chunks_ref, seg_ref, loc_ref, tile, copy_in(0))

    def run(b):
        @pl.when(i + 1 < n)
        def _():
            _segment_copies(chunks_ref, seg_ref, loc_ref, tile + 1, copy_in(1 - b))

        _wait_copies(total_ref[tile], copy_in(b))
        lo, hi = _unpack_bf16_pairs(local_ref[b])
        rows = jnp.concatenate([lo, hi], axis=1).astype(BF16)
        slots = slot_ref[...]
        gates = gate_ref[...]
        row = lax.broadcasted_iota(jnp.int32, (ROW_TILE, LOCAL_ROWS), 1)
        weight = jnp.zeros((ROW_TILE, LOCAL_ROWS), F32)
        for k in range(TOP_K):
            weight = jnp.where(slots[:, k:k + 1] == row, gates[:, k:k + 1], weight)
        y = _dot(weight.astype(BF16), rows)
        o_ref[...] = _layer_norm(DN_ALPHA * x_ref[...] + y, g_ref[...], b_ref[...])

    for b in range(2):
        pl.when(buf == b)(functools.partial(run, b))


def _combine(x2, rows_out, slots_c, gates_c, tables, g, b, first_tile, n_tiles):
    const = lambda i, *_: (0, 0)
    tile = lambda i, *_: (first_tile + i, 0)
    return pl.pallas_call(
        functools.partial(_combine_kernel, first_tile),
        grid_spec=pltpu.PrefetchScalarGridSpec(
            num_scalar_prefetch=4,
            grid=(n_tiles,),
            in_specs=[
                pl.BlockSpec((ROW_TILE, D_MODEL), tile),
                pl.BlockSpec(memory_space=pl.ANY),
                pl.BlockSpec((ROW_TILE, TOP_K), tile),
                pl.BlockSpec((ROW_TILE, TOP_K), tile),
                pl.BlockSpec((1, D_MODEL), const),
                pl.BlockSpec((1, D_MODEL), const),
            ],
            out_specs=pl.BlockSpec((ROW_TILE, D_MODEL), lambda i, *_: (i, 0)),
            scratch_shapes=[pltpu.VMEM((2, LOCAL_ROWS, HALF), jnp.uint32), pltpu.SemaphoreType.DMA((2,))],
        ),
        out_shape=jax.ShapeDtypeStruct((n_tiles * ROW_TILE, D_MODEL), F32),
        compiler_params=_cparams(("arbitrary",)),
        name="combine_ln3",
    )(*tables[:4], x2, rows_out, slots_c, gates_c, g.reshape(1, D_MODEL), b.reshape(1, D_MODEL))


def kernel(x_prompt, x_sample, mem_prompt, mem_sample, rel_bias, w_in, b_gates, conv_w, conv_b, mh_gain, attn_sink,
           w_out, ln1_g, ln1_b, wq_mem, wkv_mem, wo_mem, ln2_g, ln2_b, w_router, b_router, w_up, b_up, w_down,
           b_down, ln3_g, ln3_b):
    assert w_in.shape[0] == 1, "single layer"
    bp, sp, _ = x_prompt.shape
    bs, ss, _ = x_sample.shape
    tp, ts = bp * sp, bs * ss
    t = tp + ts
    assert sp % ROW_TILE == 0 and ss % ROW_TILE == 0 and sp >= 2 * BLOCK and ss >= 2 * BLOCK
    xp = x_prompt.reshape(tp, D_MODEL)
    xs = x_sample.reshape(ts, D_MODEL)

    def seq_blocks(rows):
        return (tp // rows, sp // rows, ss // rows)

    w = w_in[0]
    q_end, k_end, v_end = ATT_WIDTH, ATT_WIDTH + 128, ATT_WIDTH + 256
    qk_end, mv_end, mo_end = v_end + 2 * ML_WIDTH, v_end + 3 * ML_WIDTH, v_end + 4 * ML_WIDTH
    head_order = np.concatenate([[h, ATT_GROUP + h] for h in range(ATT_GROUP)])
    att_perm = (head_order[:, None] * ATT_HEAD_DIM + np.arange(ATT_HEAD_DIM)[None, :]).reshape(-1)
    w_qk = w[:, v_end:qk_end].astype(BF16)
    w_main = jnp.concatenate([
        w[:, qk_end:mv_end], w[:, mv_end:mo_end],
        w[:, :q_end][:, att_perm] * (ATT_HEAD_DIM ** -0.5), w[:, q_end:k_end], w[:, k_end:v_end]], axis=1).astype(BF16)
    w_g = w[:, mo_end:].astype(BF16)
    w_att = w_out[0][:ATT_WIDTH][att_perm].astype(BF16)
    w_ml = w_out[0][ATT_WIDTH:].astype(BF16)

    qk, z, gates_c, gates_r = _in_proj(xp, xs, w_qk, w_main, w_g, b_gates[0], conv_w[0], conv_b[0],
                                       seq_blocks(ROW_TILE))

    bias, sink = _attention_tables(rel_bias, attn_sink[0])
    att = _attention(z, bias, sink, seq_blocks(BLOCK))

    h_f, h_b = _mlstm(qk, z, gates_r, gates_c, seq_blocks(ML_CHUNK))

    mem = jnp.concatenate([mem_prompt.reshape(bp * MEM_TOKENS, D_MODEL), mem_sample.reshape(bs * MEM_TOKENS, D_MODEL)])
    kv = _mem_kv(mem, wkv_mem[0].astype(BF16))
    n_p_tiles, p_tiles_per_seq, s_tiles_per_seq = seq_blocks(ROW_TILE)
    mem_of_tile = lambda i: jnp.where(i < n_p_tiles, i // p_tiles_per_seq, bp + (i - n_p_tiles) // s_tiles_per_seq)
    wq = (wq_mem[0] * (MEM_HEAD_DIM ** -0.5)).astype(BF16)
    x2, x2b, top_gate, slots, tile_chunks = _cross_router(
        xp, xs, att, h_f, h_b, z, w_att, w_ml, mh_gain[0], ln1_g[0], ln1_b[0],
        kv, wq, wo_mem[0].astype(BF16), ln2_g[0], ln2_b[0], w_router[0], b_router[0], mem_of_tile)

    n_tiles = t // ROW_TILE
    blk_chunks = MOE_ROWS // SEG_ALIGN
    chunks = tile_chunks[:, :, 0]
    used = jnp.sum(chunks, axis=0)
    region = ((used + blk_chunks - 1) // blk_chunks) * blk_chunks
    region_end = jnp.cumsum(region)
    region_start = region_end - region
    seg_start = region_start[None, :] + jnp.cumsum(chunks, axis=0) - chunks
    loc_start = jnp.cumsum(chunks, axis=1) - chunks
    n_blk = -(-(t * TOP_K + n_tiles * N_EXPERTS * (SEG_ALIGN - 1) + N_EXPERTS * (MOE_ROWS - 1)) // MOE_ROWS)
    blk_first = jnp.arange(n_blk, dtype=jnp.int32) * blk_chunks
    block_e = jnp.minimum(jnp.sum(blk_first[:, None] >= region_end[None, :], axis=1), N_EXPERTS - 1).astype(jnp.int32)
    n_active = (region_end[-1:] // blk_chunks).astype(jnp.int32)
    tables = (chunks.reshape(-1), seg_start.reshape(-1), loc_start.reshape(-1), jnp.sum(chunks, axis=1),
              region_start + used, region - used, n_active)
    tables = tuple(tb.astype(jnp.int32) for tb in tables)

    rows_in = _dispatch(x2b, slots, tables, n_blk * MOE_ROWS)
    rows_out = _experts(rows_in, block_e, n_active, w_up[0], b_up[0], w_down[0], b_down[0])

    slots_c, gates_c = slots.T, top_gate.T
    y_p = _combine(x2, rows_out, slots_c, gates_c, tables, ln3_g[0], ln3_b[0], 0, n_p_tiles)
    y_s = _combine(x2, rows_out, slots_c, gates_c, tables, ln3_g[0], ln3_b[0], n_p_tiles, n_tiles - n_p_tiles)
    return y_p.reshape(bp, sp, D_MODEL), y_s.reshape(bs, ss, D_MODEL)
```

```python
import functools

import numpy as np
import jax
import jax.numpy as jnp
from jax import lax
from jax.experimental import pallas as pl
from jax.experimental.pallas import tpu as pltpu

F32 = jnp.float32
BF16 = jnp.bfloat16

D_MODEL = 1024
ATT_HEADS = 8
ATT_KV_HEADS = 2
ATT_GROUP = ATT_HEADS // ATT_KV_HEADS
ATT_HEAD_DIM = 64
ATT_WIDTH = ATT_HEADS * ATT_HEAD_DIM
WINDOW = 128
BLOCK = WINDOW
N_BUCKETS = 32
MAX_DISTANCE = 128
ML_HEADS = 4
ML_HEAD_DIM = 128
ML_WIDTH = ML_HEADS * ML_HEAD_DIM
ML_CHUNK = 256
CONV_WIDTH = 5
N_GATES = 4 * ML_HEADS
MEM_TOKENS = 256
MEM_HEADS = 4
MEM_HEAD_DIM = D_MODEL // MEM_HEADS
N_EXPERTS = 32
TOP_K = 4
D_FF = D_MODEL
SWIGLU_LIMIT = 7.0
SWIGLU_ALPHA = 1.702
LN_EPS = 1e-5
DN_ALPHA = 2.0 ** 0.25

QK_WIDTH = 2 * ML_WIDTH
Z_WIDTH = ML_WIDTH + ML_WIDTH + ATT_WIDTH + 2 * ATT_KV_HEADS * ATT_HEAD_DIM
ZB_MLV = 0
ZB_MLO = 1
ZB_ATTQ = 2
ZB_ATTK = 12
ZB_ATTV = 13

ROW_TILE = 512
MOE_ROWS = 512
NEG = -1e30
VMEM_LIMIT = 56 * 1024 * 1024


def _cparams(sem):
    return pltpu.CompilerParams(dimension_semantics=sem, vmem_limit_bytes=VMEM_LIMIT)


def _dot(a, b):
    return jnp.dot(a, b, preferred_element_type=F32)


def _dot_nt(a, b):
    return lax.dot_general(a, b, (((1,), (1,)), ((), ())), preferred_element_type=F32)


def _dot_tn(a, b):
    return lax.dot_general(a, b, (((0,), (0,)), ((), ())), preferred_element_type=F32)


def _split3(x):
    hi = x.astype(BF16)
    rest = x - hi.astype(F32)
    mid = rest.astype(BF16)
    lo = (rest - mid.astype(F32)).astype(BF16)
    return hi, mid, lo


def _layer_norm(y, g, b):
    mu = jnp.mean(y, axis=-1, keepdims=True)
    yc = y - mu
    var = jnp.mean(yc * yc, axis=-1, keepdims=True)
    return yc * lax.rsqrt(var + LN_EPS) * g + b


def _log_sigmoid(x):
    return jnp.minimum(x, 0.0) - jnp.log1p(jnp.exp(-jnp.abs(x)))


def _sigmoid(x):
    return 1.0 / (1.0 + jnp.exp(-x))


def _trace_stagewise(chains):
    while chains:
        chains = [c for c in chains if next(c, "done") != "done"]


def _seq_pos(blk, n_p_blocks, p_blocks_per_seq, s_blocks_per_seq):
    in_p = blk < n_p_blocks
    local = jnp.where(in_p, blk % p_blocks_per_seq, (blk - n_p_blocks) % s_blocks_per_seq)
    per = jnp.where(in_p, p_blocks_per_seq, s_blocks_per_seq)
    return local == 0, local == per - 1


HALO = 8


def _in_proj_kernel(seq, xp_ref, xpl_ref, xpr_ref, xs_ref, xsl_ref, xsr_ref, wqk_ref, w_ref, bg_ref,
                    cw_ref, cb_ref, scale_ref, qk_ref, z_ref, gc_ref, gr_ref, buf_ref):
    i = pl.program_id(0)
    in_p = i < seq[0]
    first, last = _seq_pos(i, *seq)
    x = jnp.where(in_p, xp_ref[...], xs_ref[...])
    left = jnp.where(in_p, xpl_ref[...], xsl_ref[...])
    right = jnp.where(in_p, xpr_ref[...], xsr_ref[...])
    xb = x.astype(BF16)
    x_ext = jnp.concatenate([left, x, right], axis=0).astype(BF16)
    row = lax.broadcasted_iota(jnp.int32, (ROW_TILE + 2 * HALO, 1), 0)
    outside = (first & (row < HALO)) | (last & (row >= ROW_TILE + HALO))
    buf_ref[...] = jnp.where(outside, 0.0, _dot(x_ext, wqk_ref[...]))
    acc = jnp.zeros((ROW_TILE, QK_WIDTH), F32) + cb_ref[...]
    for j in range(CONV_WIDTH):
        off = HALO + j - CONV_WIDTH // 2
        acc = acc + buf_ref[off:off + ROW_TILE, :] * cw_ref[j:j + 1, :]
    qk_ref[...] = (acc * _sigmoid(acc) * scale_ref[...]).astype(BF16)
    zg = _dot(xb, w_ref[...])
    z_ref[...] = zg[:, :Z_WIDTH].astype(BF16)
    gates = zg[:, Z_WIDTH:] + bg_ref[...]
    gc_ref[...] = gates[:, :N_GATES]
    gr_ref[...] = gates.T[:N_GATES, :]


def _in_proj(xp, xs, w_qk, w_main, w_g, b_g, conv_w, conv_b, seq):
    w_main = jnp.pad(jnp.concatenate([w_main, w_g], axis=1), ((0, 0), (0, 128 - N_GATES)))
    b_g = jnp.pad(b_g, (0, 128 - N_GATES)).reshape(1, 128)
    tp, ts = xp.shape[0], xs.shape[0]
    t = tp + ts
    n_p = tp // ROW_TILE
    n = t // ROW_TILE
    r = ROW_TILE // HALO
    const = lambda i: (0, 0)
    p_tile = lambda i: jnp.minimum(i, n_p - 1)
    s_tile = lambda i: jnp.maximum(i - n_p, 0)
    scale = jnp.concatenate([jnp.ones((1, ML_WIDTH), F32), jnp.full((1, ML_WIDTH), ML_HEAD_DIM ** -0.5, F32)], axis=1)
    return pl.pallas_call(
        functools.partial(_in_proj_kernel, seq),
        grid=(n,),
        in_specs=[
            pl.BlockSpec((ROW_TILE, D_MODEL), lambda i: (p_tile(i), 0)),
            pl.BlockSpec((HALO, D_MODEL), lambda i: (jnp.maximum(p_tile(i) * r - 1, 0), 0)),
            pl.BlockSpec((HALO, D_MODEL), lambda i: (jnp.minimum((p_tile(i) + 1) * r, tp // HALO - 1), 0)),
            pl.BlockSpec((ROW_TILE, D_MODEL), lambda i: (s_tile(i), 0)),
            pl.BlockSpec((HALO, D_MODEL), lambda i: (jnp.maximum(s_tile(i) * r - 1, 0), 0)),
            pl.BlockSpec((HALO, D_MODEL), lambda i: (jnp.minimum((s_tile(i) + 1) * r, ts // HALO - 1), 0)),
            pl.BlockSpec((D_MODEL, QK_WIDTH), const),
            pl.BlockSpec((D_MODEL, Z_WIDTH + 128), const),
            pl.BlockSpec((1, 128), const),
            pl.BlockSpec((CONV_WIDTH, QK_WIDTH), const),
            pl.BlockSpec((1, QK_WIDTH), const),
            pl.BlockSpec((1, QK_WIDTH), const),
        ],
        out_specs=[
            pl.BlockSpec((ROW_TILE, QK_WIDTH), lambda i: (i, 0)),
            pl.BlockSpec((ROW_TILE, Z_WIDTH), lambda i: (i, 0)),
            pl.BlockSpec((ROW_TILE, N_GATES), lambda i: (i, 0)),
            pl.BlockSpec((N_GATES, ROW_TILE), lambda i: (0, i)),
        ],
        out_shape=[
            jax.ShapeDtypeStruct((t, QK_WIDTH), BF16),
            jax.ShapeDtypeStruct((t, Z_WIDTH), BF16),
            jax.ShapeDtypeStruct((t, N_GATES), F32),
            jax.ShapeDtypeStruct((N_GATES, t), F32),
        ],
        scratch_shapes=[pltpu.VMEM((ROW_TILE + 2 * HALO, QK_WIDTH), F32)],
        compiler_params=_cparams(("arbitrary",)),
        name="in_proj_conv",
    )(xp, xp, xp, xs, xs, xs, w_qk, w_main, b_g, conv_w, conv_b.reshape(1, QK_WIDTH), scale)


ATT_TILE = 1024
ATT_SUB = ATT_TILE // BLOCK


ONES_ROWS = 16


def _attention_kernel(seq, q_ref, kp_ref, kc_ref, kn_ref, vp_ref, vc_ref, vn_ref, bias_ref, sink_ref,
                      o_ref, klo_ref, khi_ref, vt_ref):
    i = pl.program_id(0)
    lane = lax.broadcasted_iota(jnp.int32, (ATT_TILE + 2 * BLOCK, 2 * ATT_HEAD_DIM), 1)
    kband = jnp.concatenate([kp_ref[...], kc_ref[...], kn_ref[...]], axis=0)
    zero = jnp.zeros_like(kband)
    klo_ref[...] = jnp.where(lane < ATT_HEAD_DIM, kband, zero)
    khi_ref[...] = jnp.where(lane < ATT_HEAD_DIM, zero, kband)
    vband = jnp.concatenate([vp_ref[...], vc_ref[...], vn_ref[...]], axis=0)
    vt_ref[0:2 * ATT_HEAD_DIM, :] = vband.T
    vt_ref[2 * ATT_HEAD_DIM:, :] = jnp.ones((ONES_ROWS, ATT_TILE + 2 * BLOCK), BF16)
    feat = lax.broadcasted_iota(jnp.int32, (2 * ATT_HEAD_DIM, ATT_GROUP * BLOCK), 0)
    def block(s):
        first, last = _seq_pos(i * ATT_SUB + s, *seq)
        variant = jnp.where(first, 1, jnp.where(last, 2, 0))
        q = q_ref[s * BLOCK:(s + 1) * BLOCK, :]
        q_all = jnp.concatenate([q[:, t * 128:(t + 1) * 128] for t in range(ATT_GROUP)], axis=0)
        vt = vt_ref[:, s * BLOCK:(s + 3) * BLOCK]
        scores = [_dot_nt(k_ref[s * BLOCK:(s + 3) * BLOCK, :], q_all) for k_ref in (klo_ref, khi_ref)]
        yield
        ms, ps = [], []
        for kv in range(ATT_KV_HEADS):
            logits = scores[kv] + bias_ref[variant, kv]
            ms.append(jnp.maximum(jnp.max(logits, axis=0, keepdims=True), sink_ref[kv]))
            ps.append(jnp.exp(logits - ms[kv]).astype(BF16))
        yield
        ovs = [_dot(vt, p) for p in ps]
        yield
        outs = []
        for kv in range(ATT_KV_HEADS):
            den = ovs[kv][2 * ATT_HEAD_DIM:2 * ATT_HEAD_DIM + 1, :] + jnp.exp(sink_ref[kv] - ms[kv])
            outs.append(ovs[kv][0:2 * ATT_HEAD_DIM, :] * (1.0 / den))
        both = jnp.where(feat < ATT_HEAD_DIM, outs[0], outs[1]).astype(BF16)
        for t in range(ATT_GROUP):
            o_ref[t * 128:(t + 1) * 128, s * BLOCK:(s + 1) * BLOCK] = both[:, t * BLOCK:(t + 1) * BLOCK]

    _trace_stagewise([block(s) for s in range(ATT_SUB)])


def _attention(z, bias, sink, seq):
    t = z.shape[0]
    n = t // ATT_TILE
    nblk = t // BLOCK
    band = ATT_TILE + 2 * BLOCK
    prev = lambda i: jnp.maximum(i * ATT_SUB - 1, 0)
    nxt = lambda i: jnp.minimum((i + 1) * ATT_SUB, nblk - 1)
    return pl.pallas_call(
        functools.partial(_attention_kernel, seq),
        grid=(n,),
        in_specs=[
            pl.BlockSpec((ATT_TILE, ATT_WIDTH), lambda i: (i, ZB_ATTQ)),
            pl.BlockSpec((BLOCK, 128), lambda i: (prev(i), ZB_ATTK)),
            pl.BlockSpec((ATT_TILE, 128), lambda i: (i, ZB_ATTK)),
            pl.BlockSpec((BLOCK, 128), lambda i: (nxt(i), ZB_ATTK)),
            pl.BlockSpec((BLOCK, 128), lambda i: (prev(i), ZB_ATTV)),
            pl.BlockSpec((ATT_TILE, 128), lambda i: (i, ZB_ATTV)),
            pl.BlockSpec((BLOCK, 128), lambda i: (nxt(i), ZB_ATTV)),
            pl.BlockSpec((3, ATT_KV_HEADS, 3 * BLOCK, ATT_GROUP * BLOCK), lambda i: (0, 0, 0, 0)),
            pl.BlockSpec((ATT_KV_HEADS, 1, ATT_GROUP * BLOCK), lambda i: (0, 0, 0)),
        ],
        out_specs=pl.BlockSpec((ATT_WIDTH, ATT_TILE), lambda i: (0, i)),
        out_shape=jax.ShapeDtypeStruct((ATT_WIDTH, t), BF16),
        scratch_shapes=[pltpu.VMEM((band, 128), BF16), pltpu.VMEM((band, 128), BF16),
                        pltpu.VMEM((2 * ATT_HEAD_DIM + ONES_ROWS, band), BF16)],
        compiler_params=_cparams(("arbitrary",)),
        name="window_attention",
    )(z, z, z, z, z, z, z, bias, sink)


def _t5_bucket(rel):
    half = N_BUCKETS // 2
    exact = half // 2
    n = np.abs(rel)
    large = exact + (np.log(np.maximum(n, 1) / exact) / np.log(MAX_DISTANCE / exact) * (half - exact)).astype(np.int32)
    large = np.minimum(large, half - 1)
    return ((rel > 0).astype(np.int32) * half + np.where(n < exact, n, large)).astype(np.int32)


def _attention_tables(rel_bias, attn_sink):
    rel = np.arange(3 * BLOCK)[:, None] - BLOCK - np.arange(BLOCK)[None, :]
    onehot = jnp.asarray(_t5_bucket(rel)[..., None] == np.arange(N_BUCKETS), F32)
    bias = jnp.einsum('kqb,bh->hkq', onehot, rel_bias.astype(F32), precision=lax.Precision.HIGHEST)
    bias = jnp.where(jnp.asarray(np.abs(rel) <= WINDOW)[None], bias, NEG)
    bias = bias.reshape(ATT_KV_HEADS, ATT_GROUP, 3 * BLOCK, BLOCK).transpose(0, 2, 1, 3)
    bias = bias.reshape(ATT_KV_HEADS, 3 * BLOCK, ATT_GROUP * BLOCK)
    key = np.arange(3 * BLOCK)[None, :, None]
    first = jnp.where(jnp.asarray(key < BLOCK), NEG, bias)
    last = jnp.where(jnp.asarray(key >= 2 * BLOCK), NEG, bias)
    sink = jnp.repeat(attn_sink.astype(F32), BLOCK).reshape(ATT_KV_HEADS, 1, ATT_GROUP * BLOCK)
    return jnp.stack([bias, first, last]), sink


def _mlstm_kernel(seq, n_chunks, *refs):
    fwd_in, bwd_in, (of_ref, ob_ref), state = refs[0:5], refs[5:10], refs[10:12], refs[12:]
    _trace_stagewise(_mlstm_chains(seq, False, n_chunks, *fwd_in, of_ref, *state[0:2])
                     + _mlstm_chains(seq, True, n_chunks, *bwd_in, ob_ref, *state[2:4]))


def _mlstm_chains(seq, reverse, n_chunks, q_ref, k_ref, v_ref, gr_ref, gc_ref, o_ref, ct_ref, m_ref):
    step = pl.program_id(0)
    chunk = (n_chunks - 1 - step) if reverse else step
    first, last = _seq_pos(chunk, *seq)
    fresh = last if reverse else first

    L = ML_CHUNK
    row = lax.broadcasted_iota(jnp.int32, (L, L), 0)
    col = lax.broadcasted_iota(jnp.int32, (L, L), 1)
    vis = (row >= col) if reverse else (row <= col)
    vis_t = (col >= row) if reverse else (col <= row)
    gr = gr_ref[...]
    gc = gc_ref[...]
    r3 = _dot(jnp.concatenate(_split3(_log_sigmoid(gr)), axis=0), vis.astype(BF16))
    b_rows = r3[0:N_GATES] + r3[N_GATES:2 * N_GATES] + r3[2 * N_GATES:]
    vis_tb = vis_t.astype(BF16)
    c_hi, c_mid, c_lo = _split3(_log_sigmoid(gc))
    b_cols = _dot(vis_tb, c_hi) + _dot(vis_tb, c_mid) + _dot(vis_tb, c_lo)
    i_off = 2 * ML_HEADS if reverse else 0
    f_off = i_off + ML_HEADS
    end = 0 if reverse else L - 1
    ones = jnp.ones((ONES_ROWS, L), BF16)

    def chain(h):
        hs = slice(h * ML_HEAD_DIM, (h + 1) * ML_HEAD_DIM)
        q = q_ref[:, hs]
        k = k_ref[:, hs]
        vt1 = jnp.concatenate([v_ref[:, hs].T, ones], axis=0)
        b_row = b_rows[f_off + h:f_off + h + 1, :]
        u_row = gr[i_off + h:i_off + h + 1, :] - b_row
        u_col = gc[:, i_off + h:i_off + h + 1] - b_cols[:, f_off + h:f_off + h + 1]
        g = b_row[:, end:end + 1]
        ct_old = jnp.where(fresh, 0.0, ct_ref[h])
        m_old = jnp.where(fresh, 0.0, m_ref[h])[:, 0:1]
        kq = _dot_nt(k, q)
        from_state = _dot_nt(ct_old.astype(BF16), q)
        yield
        u_mat = jnp.where(vis, u_col, NEG)
        mm = jnp.maximum(jnp.max(u_mat, axis=0, keepdims=True), m_old)
        st = (kq * jnp.exp(u_mat - mm)).astype(BF16)
        yield
        tot = _dot(vt1, st) + jnp.exp(m_old - mm) * from_state
        a_max = jnp.max(g + u_row, axis=-1, keepdims=True)
        m_new = jnp.maximum(g + m_old, a_max)
        s_old = jnp.exp(g + m_old - m_new)
        weighted = (vt1.astype(F32) * jnp.exp(g + u_row - m_new)).astype(BF16)
        new_state = _dot(weighted, k)
        yield
        den = tot[ML_HEAD_DIM:ML_HEAD_DIM + 1, :]
        floor = jnp.exp(-(b_row + mm))
        o_ref[hs, :] = (tot[0:ML_HEAD_DIM, :] * (1.0 / jnp.maximum(jnp.abs(den), floor))).astype(BF16)
        ct_ref[h] = s_old * ct_old + new_state
        m_ref[h] = jnp.broadcast_to(m_new, (1, ML_HEAD_DIM))

    return [chain(h) for h in range(ML_HEADS)]


def _mlstm(qk, z, gates_r, gates_c, seq):
    t = qk.shape[0]
    nc = t // ML_CHUNK

    def chunk_specs(ch):
        return [
            pl.BlockSpec((ML_CHUNK, ML_WIDTH), lambda i: (ch(i), 0)),
            pl.BlockSpec((ML_CHUNK, ML_WIDTH), lambda i: (ch(i), 1)),
            pl.BlockSpec((ML_CHUNK, ML_WIDTH), lambda i: (ch(i), ZB_MLV)),
            pl.BlockSpec((N_GATES, ML_CHUNK), lambda i: (0, ch(i))),
            pl.BlockSpec((ML_CHUNK, N_GATES), lambda i: (ch(i), 0)),
        ]

    fwd = lambda i: i
    bwd = lambda i: nc - 1 - i
    state = [
        pltpu.VMEM((ML_HEADS, ML_HEAD_DIM + ONES_ROWS, ML_HEAD_DIM), F32),
        pltpu.VMEM((ML_HEADS, 1, ML_HEAD_DIM), F32),
    ]
    operands = (qk, qk, z, gates_r, gates_c)
    return pl.pallas_call(
        functools.partial(_mlstm_kernel, seq, nc),
        grid=(nc,),
        in_specs=chunk_specs(fwd) + chunk_specs(bwd),
        out_specs=[pl.BlockSpec((ML_WIDTH, ML_CHUNK), lambda i: (0, fwd(i))),
                   pl.BlockSpec((ML_WIDTH, ML_CHUNK), lambda i: (0, bwd(i)))],
        out_shape=[jax.ShapeDtypeStruct((ML_WIDTH, t), BF16), jax.ShapeDtypeStruct((ML_WIDTH, t), BF16)],
        scratch_shapes=state + state,
        compiler_params=_cparams(("arbitrary",)),
        name="mlstm",
    )(*operands, *operands)


def _mix_out(n_p_tiles, toks, xp_ref, xs_ref, att_ref, hf_ref, hb_ref, og_ref, wa_ref, wm_ref, gain_ref, g_ref,
             b_ref):
    i = pl.program_id(0)
    x = jnp.where(i < n_p_tiles, xp_ref[toks, :], xs_ref[toks, :])
    h = hf_ref[:, toks].astype(F32) + hb_ref[:, toks].astype(F32)
    parts = []
    for hd in range(ML_HEADS):
        hh = h[hd * ML_HEAD_DIM:(hd + 1) * ML_HEAD_DIM, :]
        mu = jnp.mean(hh, axis=0, keepdims=True)
        hc = hh - mu
        var = jnp.mean(hc * hc, axis=0, keepdims=True)
        parts.append(hc * lax.rsqrt(var + LN_EPS))
    hn = jnp.concatenate(parts, axis=0) * gain_ref[...] * _sigmoid(og_ref[toks, :].T.astype(F32))
    mixed = _dot(att_ref[:, toks].T, wa_ref[...]) + _dot(hn.astype(BF16).T, wm_ref[...])
    return _layer_norm(DN_ALPHA * x + mixed, g_ref[...], b_ref[...])


def _mem_kv_kernel(m_ref, w_ref, o_ref):
    o_ref[...] = _dot(m_ref[...].astype(BF16), w_ref[...]).astype(BF16)


def _mem_kv(mem, wkv):
    rows = mem.shape[0]
    return pl.pallas_call(
        _mem_kv_kernel,
        grid=(rows // MEM_TOKENS,),
        in_specs=[pl.BlockSpec((MEM_TOKENS, D_MODEL), lambda i: (i, 0)),
                  pl.BlockSpec((D_MODEL, 2 * D_MODEL), lambda i: (0, 0))],
        out_specs=pl.BlockSpec((MEM_TOKENS, 2 * D_MODEL), lambda i: (i, 0)),
        out_shape=jax.ShapeDtypeStruct((rows, 2 * D_MODEL), BF16),
        compiler_params=_cparams(("arbitrary",)),
        name="mem_kv",
    )(mem, wkv)


def _pack_bf16_pairs(a, b, exact=False):
    if not exact:
        a, b = a.astype(BF16).astype(F32), b.astype(BF16).astype(F32)
    ua, ub = pltpu.bitcast(a, jnp.uint32), pltpu.bitcast(b, jnp.uint32)
    return (ua >> 16) | (ub & jnp.uint32(0xFFFF0000))


def _unpack_bf16_pairs(u):
    lo = pltpu.bitcast(u << 16, F32)
    hi = pltpu.bitcast(u & jnp.uint32(0xFFFF0000), F32)
    return lo, hi


TOKEN_GROUPS = 2


def _cross_router_kernel(n_p_tiles, *refs):
    mix_refs, (kv_ref, wq_ref, wo_ref, g_ref, b_ref, wr_ref, br_ref, tri_ref, etri_ref,
               x2_ref, x2b_ref, gate_ref, slot_ref, chunk_ref) = refs[:11], refs[11:]
    head_cols = [slice(h * MEM_HEAD_DIM, (h + 1) * MEM_HEAD_DIM) for h in range(MEM_HEADS)]

    def token_group(r):
        toks = slice(r * ROW_TILE // TOKEN_GROUPS, (r + 1) * ROW_TILE // TOKEN_GROUPS)
        x = _mix_out(n_p_tiles, toks, *mix_refs)
        yield
        q = _dot(x.astype(BF16), wq_ref[...]).astype(BF16)
        scores = [_dot_nt(q[:, hs], kv_ref[:, hs]) for hs in head_cols]
        yield
        probs, dens = [], []
        for logits in scores:
            p = jnp.exp(logits - jnp.max(logits, axis=-1, keepdims=True))
            dens.append(jnp.sum(p, axis=-1, keepdims=True))
            probs.append(p.astype(BF16))
        yield
        values = [_dot(p, kv_ref[:, D_MODEL + hs.start:D_MODEL + hs.stop]) for p, hs in zip(probs, head_cols)]
        o = jnp.concatenate([(v * (1.0 / den)).astype(BF16) for v, den in zip(values, dens)], axis=1)
        y = _dot(o, wo_ref[...])
        yield
        x2 = _layer_norm(DN_ALPHA * x + y, g_ref[...], b_ref[...])
        x2_ref[toks, :] = x2
        x2b_ref[toks, :] = x2.astype(BF16)

    _trace_stagewise([token_group(r) for r in range(TOKEN_GROUPS)])
    x2b = x2b_ref[...]

    logits = _dot_nt(wr_ref[...], x2b) + br_ref[...]
    expert = lax.broadcasted_iota(jnp.int32, logits.shape, 0)
    work = logits
    vals, sels = [], []
    for k in range(TOP_K):
        mx = jnp.max(work, axis=0, keepdims=True)
        ix = jnp.min(jnp.where(work == mx, expert, N_EXPERTS), axis=0, keepdims=True)
        sel = expert == ix
        work = jnp.where(sel, -jnp.inf, work)
        vals.append(mx)
        sels.append(sel)
    es = [jnp.exp(v - vals[0]) for v in vals]
    tot = es[0] + es[1] + es[2] + es[3]
    chosen = jnp.zeros(logits.shape, F32)
    for k in range(TOP_K):
        gate_ref[k:k + 1, :] = es[k] / tot
        chosen = chosen + sels[k].astype(F32)
    count = jnp.sum(chosen, axis=1, keepdims=True)
    chunks = jnp.floor((count + (SEG_ALIGN - 1)) * (1.0 / SEG_ALIGN))
    chunks_b = jnp.broadcast_to(chunks, (N_EXPERTS, 128))
    seg_first = _dot(etri_ref[...], chunks_b.astype(BF16))[:, 0:1] * SEG_ALIGN
    before = _dot(chosen.astype(BF16), tri_ref[...])
    local_row = before + seg_first
    for k in range(TOP_K):
        slot_ref[k:k + 1, :] = jnp.sum(jnp.where(sels[k], local_row, 0.0), axis=0, keepdims=True).astype(jnp.int32)
    chunk_ref[0] = chunks_b.astype(jnp.int32)


def _cross_router(xp, xs, att, hf, hb, z, w_att, w_ml, gain, g1, b1, kv, wq, wo, g, b, w_router, b_router,
                  mem_of_tile):
    t = att.shape[1]
    n_p = xp.shape[0] // ROW_TILE
    n = t // ROW_TILE
    const = lambda i: (0, 0)
    tile = lambda i: (i, 0)
    lanes = lambda i: (0, i)
    tri = jnp.asarray(np.triu(np.ones((ROW_TILE, ROW_TILE), np.float32), 1), BF16)
    etri = jnp.asarray(np.tril(np.ones((N_EXPERTS, N_EXPERTS), np.float32), -1), BF16)
    return pl.pallas_call(
        functools.partial(_cross_router_kernel, n_p),
        grid=(n,),
        in_specs=[
            pl.BlockSpec((ROW_TILE, D_MODEL), lambda i: (jnp.minimum(i, n_p - 1), 0)),
            pl.BlockSpec((ROW_TILE, D_MODEL), lambda i: (jnp.maximum(i - n_p, 0), 0)),
            pl.BlockSpec((ATT_WIDTH, ROW_TILE), lanes),
            pl.BlockSpec((ML_WIDTH, ROW_TILE), lanes),
            pl.BlockSpec((ML_WIDTH, ROW_TILE), lanes),
            pl.BlockSpec((ROW_TILE, ML_WIDTH), lambda i: (i, ZB_MLO)),
            pl.BlockSpec((ATT_WIDTH, D_MODEL), const),
            pl.BlockSpec((ML_WIDTH, D_MODEL), const),
            pl.BlockSpec((ML_WIDTH, 1), const),
            pl.BlockSpec((1, D_MODEL), const),
            pl.BlockSpec((1, D_MODEL), const),
            pl.BlockSpec((MEM_TOKENS, 2 * D_MODEL), lambda i: (mem_of_tile(i), 0)),
            pl.BlockSpec((D_MODEL, D_MODEL), const),
            pl.BlockSpec((D_MODEL, D_MODEL), const),
            pl.BlockSpec((1, D_MODEL), const),
            pl.BlockSpec((1, D_MODEL), const),
            pl.BlockSpec((N_EXPERTS, D_MODEL), const),
            pl.BlockSpec((N_EXPERTS, 1), const),
            pl.BlockSpec((ROW_TILE, ROW_TILE), const),
            pl.BlockSpec((N_EXPERTS, N_EXPERTS), const),
        ],
        out_specs=[
            pl.BlockSpec((ROW_TILE, D_MODEL), tile),
            pl.BlockSpec((ROW_TILE, D_MODEL), tile),
            pl.BlockSpec((TOP_K, ROW_TILE), lanes),
            pl.BlockSpec((TOP_K, ROW_TILE), lanes),
            pl.BlockSpec((1, N_EXPERTS, 128), lambda i: (i, 0, 0)),
        ],
        out_shape=[
            jax.ShapeDtypeStruct((t, D_MODEL), F32),
            jax.ShapeDtypeStruct((t, D_MODEL), BF16),
            jax.ShapeDtypeStruct((TOP_K, t), F32),
            jax.ShapeDtypeStruct((TOP_K, t), jnp.int32),
            jax.ShapeDtypeStruct((n, N_EXPERTS, 128), jnp.int32),
        ],
        compiler_params=_cparams(("arbitrary",)),
        name="mix_out_cross_attn_router",
    )(xp, xs, att, hf, hb, z, w_att, w_ml, gain.reshape(ML_WIDTH, 1), g1.reshape(1, D_MODEL), b1.reshape(1, D_MODEL),
      kv, wq, wo, g.reshape(1, D_MODEL), b.reshape(1, D_MODEL), w_router.T.astype(BF16),
      b_router.reshape(N_EXPERTS, 1), tri, etri)


SEG_ALIGN = 8
LOCAL_ROWS = ROW_TILE * TOP_K + N_EXPERTS * SEG_ALIGN
HALF = D_MODEL // 2


def _rows(chunks):
    return pl.multiple_of(chunks * SEG_ALIGN, SEG_ALIGN)


def _segment_copies(chunks_ref, seg_ref, loc_ref, tile, make_copy):
    def per_expert(e, carry):
        entry = tile * N_EXPERTS + e

        @pl.when(chunks_ref[entry] > 0)
        def _():
            make_copy(_rows(loc_ref[entry]), _rows(seg_ref[entry]), _rows(chunks_ref[entry])).start()

        return carry

    lax.fori_loop(0, N_EXPERTS, per_expert, 0)


def _wait_copies(chunks, make_copy):
    @pl.when(chunks > 0)
    def _():
        make_copy(0, 0, _rows(chunks)).wait()


def _dispatch_kernel(chunks_ref, seg_ref, loc_ref, total_ref, tail_ref, tailn_ref, nact_ref,
                     x_ref, slot_ref, rows_hbm, local_ref, zero_ref, sems):
    i = pl.program_id(0)
    n = pl.num_programs(0)
    buf = i % 2

    def copy_out(b):
        return lambda loc, seg, rows: pltpu.make_async_copy(
            local_ref.at[b, pl.ds(loc, rows), :], rows_hbm.at[pl.ds(seg, rows), :], sems.at[b])

    def run(b):
        @pl.when(i >= 2)
        def _():
            _wait_copies(total_ref[jnp.maximum(i - 2, 0)], copy_out(b))

        slots = slot_ref[...].astype(F32)
        row = lax.broadcasted_iota(jnp.int32, (LOCAL_ROWS, ROW_TILE), 0).astype(F32)
        miss = (slots[0:1, :] - row) * (slots[1:2, :] - row)
        for k in range(2, TOP_K):
            miss = miss * (slots[k:k + 1, :] - row)
        perm = jnp.where(miss == 0.0, 1.0, 0.0).astype(BF16)
        picked = _dot(perm, x_ref[...])
        local_ref[b] = _pack_bf16_pairs(picked[:, :HALF], picked[:, HALF:], exact=True)
        _segment_copies(chunks_ref, seg_ref, loc_ref, i, copy_out(b))

    for b in range(2):
        pl.when(buf == b)(functools.partial(run, b))

    @pl.when(i == n - 1)
    def _():
        for b in range(2):
            step = jnp.where(buf == b, i, i - 1)
            _wait_copies(total_ref[step], copy_out(b))
        zero_ref[...] = jnp.zeros_like(zero_ref)
        fill = lambda _, seg, rows: pltpu.make_async_copy(
            zero_ref.at[pl.ds(0, rows), :], rows_hbm.at[pl.ds(seg, rows), :], sems.at[2])

        def per_expert(e, carry):
            @pl.when(tailn_ref[e] > 0)
            def _():
                fill(0, _rows(tail_ref[e]), _rows(tailn_ref[e])).start()

            _wait_copies(tailn_ref[e], fill)
            return carry

        lax.fori_loop(0, N_EXPERTS, per_expert, 0)
        fill_block = lambda blk: pltpu.make_async_copy(
            zero_ref, rows_hbm.at[pl.ds(pl.multiple_of(blk * MOE_ROWS, MOE_ROWS), MOE_ROWS), :], sems.at[2])
        n_blocks = rows_hbm.shape[0] // MOE_ROWS

        def start_block(blk, carry):
            fill_block(blk).start()
            return carry

        def wait_block(blk, carry):
            fill_block(0).wait()
            return carry

        lax.fori_loop(nact_ref[0], n_blocks, start_block, 0)
        lax.fori_loop(nact_ref[0], n_blocks, wait_block, 0)


def _dispatch(x2b, slots, tables, n_rows):
    n = x2b.shape[0] // ROW_TILE
    assert n >= 2
    return pl.pallas_call(
        _dispatch_kernel,
        grid_spec=pltpu.PrefetchScalarGridSpec(
            num_scalar_prefetch=7,
            grid=(n,),
            in_specs=[
                pl.BlockSpec((ROW_TILE, D_MODEL), lambda i, *_: (i, 0)),
                pl.BlockSpec((TOP_K, ROW_TILE), lambda i, *_: (0, i)),
            ],
            out_specs=pl.BlockSpec(memory_space=pl.ANY),
            scratch_shapes=[
                pltpu.VMEM((2, LOCAL_ROWS, HALF), jnp.uint32),
                pltpu.VMEM((MOE_ROWS, HALF), jnp.uint32),
                pltpu.SemaphoreType.DMA((3,)),
            ],
        ),
        out_shape=jax.ShapeDtypeStruct((n_rows, HALF), jnp.uint32),
        compiler_params=_cparams(("arbitrary",)),
        name="moe_dispatch",
    )(*tables, x2b, slots)


def _expert_kernel(be_ref, na_ref, x_ref, wu_ref, bu_ref, wd_ref, bd_ref, o_ref, wub_ref, wdb_ref):
    i = pl.program_id(0)

    @pl.when((i == 0) | (be_ref[i] != be_ref[jnp.maximum(i - 1, 0)]))
    def _():
        wub_ref[...] = wu_ref[0].astype(BF16)
        wdb_ref[...] = wd_ref[0].astype(BF16)

    @pl.when(i < na_ref[0])
    def _():
        lo, hi = _unpack_bf16_pairs(x_ref[...])
        x = jnp.concatenate([lo, hi], axis=1).astype(BF16)
        hu = _dot(x, wub_ref[...]) + bu_ref[0]
        h_glu = jnp.minimum(hu[:, :D_FF], SWIGLU_LIMIT)
        h_lin = jnp.clip(hu[:, D_FF:], -SWIGLU_LIMIT, SWIGLU_LIMIT)
        hh = h_glu * _sigmoid(SWIGLU_ALPHA * h_glu) * (h_lin + 1.0)
        y = _dot(hh.astype(BF16), wdb_ref[...]) + bd_ref[0]
        o_ref[...] = _pack_bf16_pairs(y[:, :D_MODEL // 2], y[:, D_MODEL // 2:])

    @pl.when(i >= na_ref[0])
    def _():
        o_ref[...] = jnp.zeros_like(o_ref)


def _experts(xs, block_e, n_active, w_up, b_up, w_down, b_down):
    n_rows = xs.shape[0]
    n_blk = n_rows // MOE_ROWS
    return pl.pallas_call(
        _expert_kernel,
        grid_spec=pltpu.PrefetchScalarGridSpec(
            num_scalar_prefetch=2,
            grid=(n_blk,),
            in_specs=[
                pl.BlockSpec((MOE_ROWS, D_MODEL // 2), lambda i, be, na: (jnp.minimum(i, na[0] - 1), 0)),
                pl.BlockSpec((1, D_MODEL, 2 * D_FF), lambda i, be, na: (be[i], 0, 0)),
                pl.BlockSpec((1, 1, 2 * D_FF), lambda i, be, na: (be[i], 0, 0)),
                pl.BlockSpec((1, D_FF, D_MODEL), lambda i, be, na: (be[i], 0, 0)),
                pl.BlockSpec((1, 1, D_MODEL), lambda i, be, na: (be[i], 0, 0)),
            ],
            out_specs=pl.BlockSpec((MOE_ROWS, D_MODEL // 2), lambda i, be, na: (i, 0)),
            scratch_shapes=[pltpu.VMEM((D_MODEL, 2 * D_FF), BF16), pltpu.VMEM((D_FF, D_MODEL), BF16)],
        ),
        out_shape=jax.ShapeDtypeStruct((n_rows, D_MODEL // 2), jnp.uint32),
        compiler_params=_cparams(("arbitrary",)),
        name="experts",
    )(block_e, n_active, xs, w_up, b_up.reshape(N_EXPERTS, 1, 2 * D_FF), w_down, b_down.reshape(N_EXPERTS, 1, D_MODEL))


def _combine_kernel(first_tile, chunks_ref, seg_ref, loc_ref, total_ref,
                    x_ref, rows_hbm, slot_ref, gate_ref, g_ref, b_ref, o_ref, local_ref, sems):
    i = pl.program_id(0)
    n = pl.num_programs(0)
    tile = first_tile + i
    buf = i % 2

    def copy_in(b):
        return lambda loc, seg, rows: pltpu.make_async_copy(
            rows_hbm.at[pl.ds(seg, rows), :], local_ref.at[b, pl.ds(loc, rows), :], sems.at[b])

    @pl.when(i == 0)
    def _():
        local_ref[...] = jnp.zeros_like(local_ref)
        _segment_copies(chunks_ref, seg_ref, loc_ref, tile, copy_in(0))

    def run(b):
        @pl.when(i + 1 < n)
        def _():
            _segment_copies(chunks_ref, seg_ref, loc_ref, tile + 1, copy_in(1 - b))

        _wait_copies(total_ref[tile], copy_in(b))
        lo, hi = _unpack_bf16_pairs(local_ref[b])
        rows = jnp.concatenate([lo, hi], axis=1).astype(BF16)
        slots = slot_ref[...]
        gates = gate_ref[...]
        row = lax.broadcasted_iota(jnp.int32, (ROW_TILE, LOCAL_ROWS), 1)
        weight = jnp.zeros((ROW_TILE, LOCAL_ROWS), F32)
        for k in range(TOP_K):
            weight = jnp.where(slots[:, k:k + 1] == row, gates[:, k:k + 1], weight)
        y = _dot(weight.astype(BF16), rows)
        o_ref[...] = _layer_norm(DN_ALPHA * x_ref[...] + y, g_ref[...], b_ref[...])

    for b in range(2):
        pl.when(buf == b)(functools.partial(run, b))


def _combine(x2, rows_out, slots_c, gates_c, tables, g, b, first_tile, n_tiles):
    const = lambda i, *_: (0, 0)
    tile = lambda i, *_: (first_tile + i, 0)
    return pl.pallas_call(
        functools.partial(_combine_kernel, first_tile),
        grid_spec=pltpu.PrefetchScalarGridSpec(
            num_scalar_prefetch=4,
            grid=(n_tiles,),
            in_specs=[
                pl.BlockSpec((ROW_TILE, D_MODEL), tile),
                pl.BlockSpec(memory_space=pl.ANY),
                pl.BlockSpec((ROW_TILE, TOP_K), tile),
                pl.BlockSpec((ROW_TILE, TOP_K), tile),
                pl.BlockSpec((1, D_MODEL), const),
                pl.BlockSpec((1, D_MODEL), const),
            ],
            out_specs=pl.BlockSpec((ROW_TILE, D_MODEL), lambda i, *_: (i, 0)),
            scratch_shapes=[pltpu.VMEM((2, LOCAL_ROWS, HALF), jnp.uint32), pltpu.SemaphoreType.DMA((2,))],
        ),
        out_shape=jax.ShapeDtypeStruct((n_tiles * ROW_TILE, D_MODEL), F32),
        compiler_params=_cparams(("arbitrary",)),
        name="combine_ln3",
    )(*tables[:4], x2, rows_out, slots_c, gates_c, g.reshape(1, D_MODEL), b.reshape(1, D_MODEL))


def kernel(x_prompt, x_sample, mem_prompt, mem_sample, rel_bias, w_in, b_gates, conv_w, conv_b, mh_gain, attn_sink,
           w_out, ln1_g, ln1_b, wq_mem, wkv_mem, wo_mem, ln2_g, ln2_b, w_router, b_router, w_up, b_up, w_down,
           b_down, ln3_g, ln3_b):
    assert w_in.shape[0] == 1, "single layer"
    bp, sp, _ = x_prompt.shape
    bs, ss, _ = x_sample.shape
    tp, ts = bp * sp, bs * ss
    t = tp + ts
    assert sp % ATT_TILE == 0 and ss % ATT_TILE == 0 and ATT_TILE % ROW_TILE == 0 and ATT_TILE % ML_CHUNK == 0
    xp = x_prompt.reshape(tp, D_MODEL)
    xs = x_sample.reshape(ts, D_MODEL)

    def seq_blocks(rows):
        return (tp // rows, sp // rows, ss // rows)

    w = w_in[0]
    q_end, k_end, v_end = ATT_WIDTH, ATT_WIDTH + 128, ATT_WIDTH + 256
    qk_end, mv_end, mo_end = v_end + 2 * ML_WIDTH, v_end + 3 * ML_WIDTH, v_end + 4 * ML_WIDTH
    head_order = np.concatenate([[h, ATT_GROUP + h] for h in range(ATT_GROUP)])
    att_perm = (head_order[:, None] * ATT_HEAD_DIM + np.arange(ATT_HEAD_DIM)[None, :]).reshape(-1)
    w_qk = w[:, v_end:qk_end].astype(BF16)
    w_main = jnp.concatenate([
        w[:, qk_end:mv_end], w[:, mv_end:mo_end],
        w[:, :q_end][:, att_perm] * (ATT_HEAD_DIM ** -0.5), w[:, q_end:k_end], w[:, k_end:v_end]], axis=1).astype(BF16)
    w_g = w[:, mo_end:].astype(BF16)
    w_att = w_out[0][:ATT_WIDTH][att_perm].astype(BF16)
    w_ml = w_out[0][ATT_WIDTH:].astype(BF16)

    qk, z, gates_c, gates_r = _in_proj(xp, xs, w_qk, w_main, w_g, b_gates[0], conv_w[0], conv_b[0],
                                       seq_blocks(ROW_TILE))

    bias, sink = _attention_tables(rel_bias, attn_sink[0])
    att = _attention(z, bias, sink, seq_blocks(BLOCK))

    h_f, h_b = _mlstm(qk, z, gates_r, gates_c, seq_blocks(ML_CHUNK))

    mem = jnp.concatenate([mem_prompt.reshape(bp * MEM_TOKENS, D_MODEL), mem_sample.reshape(bs * MEM_TOKENS, D_MODEL)])
    kv = _mem_kv(mem, wkv_mem[0].astype(BF16))
    n_p_tiles, p_tiles_per_seq, s_tiles_per_seq = seq_blocks(ROW_TILE)
    mem_of_tile = lambda i: jnp.where(i < n_p_tiles, i // p_tiles_per_seq, bp + (i - n_p_tiles) // s_tiles_per_seq)
    wq = (wq_mem[0] * (MEM_HEAD_DIM ** -0.5)).astype(BF16)
    x2, x2b, top_gate, slots, tile_chunks = _cross_router(
        xp, xs, att, h_f, h_b, z, w_att, w_ml, mh_gain[0], ln1_g[0], ln1_b[0],
        kv, wq, wo_mem[0].astype(BF16), ln2_g[0], ln2_b[0], w_router[0], b_router[0], mem_of_tile)

    n_tiles = t // ROW_TILE
    blk_chunks = MOE_ROWS // SEG_ALIGN
    chunks = tile_chunks[:, :, 0]
    used = jnp.sum(chunks, axis=0)
    region = ((used + blk_chunks - 1) // blk_chunks) * blk_chunks
    region_end = jnp.cumsum(region)
    region_start = region_end - region
    seg_start = region_start[None, :] + jnp.cumsum(chunks, axis=0) - chunks
    loc_start = jnp.cumsum(chunks, axis=1) - chunks
    n_blk = -(-(t * TOP_K + n_tiles * N_EXPERTS * (SEG_ALIGN - 1) + N_EXPERTS * (MOE_ROWS - 1)) // MOE_ROWS)
    blk_first = jnp.arange(n_blk, dtype=jnp.int32) * blk_chunks
    block_e = jnp.minimum(jnp.sum(blk_first[:, None] >= region_end[None, :], axis=1), N_EXPERTS - 1).astype(jnp.int32)
    n_active = (region_end[-1:] // blk_chunks).astype(jnp.int32)
    tables = (chunks.reshape(-1), seg_start.reshape(-1), loc_start.reshape(-1), jnp.sum(chunks, axis=1),
              region_start + used, region - used, n_active)
    tables = tuple(tb.astype(jnp.int32) for tb in tables)

    rows_in = _dispatch(x2b, slots, tables, n_blk * MOE_ROWS)
    rows_out = _experts(rows_in, block_e, n_active, w_up[0], b_up[0], w_down[0], b_down[0])

    slots_c, gates_c = slots.T, top_gate.T
    y_p = _combine(x2, rows_out, slots_c, gates_c, tables, ln3_g[0], ln3_b[0], 0, n_p_tiles)
    y_s = _combine(x2, rows_out, slots_c, gates_c, tables, ln3_g[0], ln3_b[0], n_p_tiles, n_tiles - n_p_tiles)
    return y_p.reshape(bp, sp, D_MODEL), y_s.reshape(bs, ss, D_MODEL)
```

```python
import functools

import numpy as np
import jax
import jax.numpy as jnp
from jax import lax
from jax.experimental import pallas as pl
from jax.experimental.pallas import tpu as pltpu

F32 = jnp.float32
BF16 = jnp.bfloat16

D_MODEL = 1024
ATT_HEADS = 8
ATT_KV_HEADS = 2
ATT_GROUP = ATT_HEADS // ATT_KV_HEADS
ATT_HEAD_DIM = 64
ATT_WIDTH = ATT_HEADS * ATT_HEAD_DIM
WINDOW = 128
BLOCK = WINDOW
N_BUCKETS = 32
MAX_DISTANCE = 128
ML_HEADS = 4
ML_HEAD_DIM = 128
ML_WIDTH = ML_HEADS * ML_HEAD_DIM
ML_CHUNK = 256
CONV_WIDTH = 5
N_GATES = 4 * ML_HEADS
MEM_TOKENS = 256
MEM_HEADS = 4
MEM_HEAD_DIM = D_MODEL // MEM_HEADS
N_EXPERTS = 32
TOP_K = 4
D_FF = D_MODEL
SWIGLU_LIMIT = 7.0
SWIGLU_ALPHA = 1.702
LN_EPS = 1e-5
DN_ALPHA = 2.0 ** 0.25

QK_WIDTH = 2 * ML_WIDTH
Z_WIDTH = ML_WIDTH + ML_WIDTH + ATT_WIDTH + 2 * ATT_KV_HEADS * ATT_HEAD_DIM
ZB_MLV = 0
ZB_MLO = 1
ZB_ATTQ = 2
ZB_ATTK = 12
ZB_ATTV = 13

ROW_TILE = 512
MOE_ROWS = 1024
NEG = -1e30
VMEM_LIMIT = 56 * 1024 * 1024


def _cparams(sem):
    return pltpu.CompilerParams(dimension_semantics=sem, vmem_limit_bytes=VMEM_LIMIT)


def _dot(a, b):
    return jnp.dot(a, b, preferred_element_type=F32)


def _dot_nt(a, b):
    return lax.dot_general(a, b, (((1,), (1,)), ((), ())), preferred_element_type=F32)


def _dot_tn(a, b):
    return lax.dot_general(a, b, (((0,), (0,)), ((), ())), preferred_element_type=F32)


def _split3(x):
    hi = x.astype(BF16)
    rest = x - hi.astype(F32)
    mid = rest.astype(BF16)
    lo = (rest - mid.astype(F32)).astype(BF16)
    return hi, mid, lo


def _layer_norm(y, g, b):
    mu = jnp.mean(y, axis=-1, keepdims=True)
    yc = y - mu
    var = jnp.mean(yc * yc, axis=-1, keepdims=True)
    return yc * lax.rsqrt(var + LN_EPS) * g + b


def _log_sigmoid(x):
    return jnp.minimum(x, 0.0) - jnp.log1p(jnp.exp(-jnp.abs(x)))


def _sigmoid(x):
    return 1.0 / (1.0 + jnp.exp(-x))


def _trace_stagewise(chains):
    while chains:
        chains = [c for c in chains if next(c, "done") != "done"]


def _seq_pos(blk, n_p_blocks, p_blocks_per_seq, s_blocks_per_seq):
    in_p = blk < n_p_blocks
    local = jnp.where(in_p, blk % p_blocks_per_seq, (blk - n_p_blocks) % s_blocks_per_seq)
    per = jnp.where(in_p, p_blocks_per_seq, s_blocks_per_seq)
    return local == 0, local == per - 1


HALO = 8


def _in_proj_kernel(seq, xp_ref, xpl_ref, xpr_ref, xs_ref, xsl_ref, xsr_ref, wqk_ref, w_ref, bg_ref,
                    cw_ref, cb_ref, scale_ref, qk_ref, z_ref, gc_ref, gr_ref, buf_ref):
    i = pl.program_id(0)
    in_p = i < seq[0]
    first, last = _seq_pos(i, *seq)
    x = jnp.where(in_p, xp_ref[...], xs_ref[...])
    left = jnp.where(in_p, xpl_ref[...], xsl_ref[...])
    right = jnp.where(in_p, xpr_ref[...], xsr_ref[...])
    xb = x.astype(BF16)
    x_ext = jnp.concatenate([left, x, right], axis=0).astype(BF16)
    row = lax.broadcasted_iota(jnp.int32, (ROW_TILE + 2 * HALO, 1), 0)
    outside = (first & (row < HALO)) | (last & (row >= ROW_TILE + HALO))
    buf_ref[...] = jnp.where(outside, 0.0, _dot(x_ext, wqk_ref[...]))
    acc = jnp.zeros((ROW_TILE, QK_WIDTH), F32) + cb_ref[...]
    for j in range(CONV_WIDTH):
        off = HALO + j - CONV_WIDTH // 2
        acc = acc + buf_ref[off:off + ROW_TILE, :] * cw_ref[j:j + 1, :]
    qk_ref[...] = (acc * _sigmoid(acc) * scale_ref[...]).astype(BF16)
    zg = _dot(xb, w_ref[...])
    z_ref[...] = zg[:, :Z_WIDTH].astype(BF16)
    gates = zg[:, Z_WIDTH:] + bg_ref[...]
    gc_ref[...] = gates[:, :N_GATES]
    gr_ref[...] = gates.T[:N_GATES, :]


def _in_proj(xp, xs, w_qk, w_main, w_g, b_g, conv_w, conv_b, seq):
    w_main = jnp.pad(jnp.concatenate([w_main, w_g], axis=1), ((0, 0), (0, 128 - N_GATES)))
    b_g = jnp.pad(b_g, (0, 128 - N_GATES)).reshape(1, 128)
    tp, ts = xp.shape[0], xs.shape[0]
    t = tp + ts
    n_p = tp // ROW_TILE
    n = t // ROW_TILE
    r = ROW_TILE // HALO
    const = lambda i: (0, 0)
    p_tile = lambda i: jnp.minimum(i, n_p - 1)
    s_tile = lambda i: jnp.maximum(i - n_p, 0)
    scale = jnp.concatenate([jnp.ones((1, ML_WIDTH), F32), jnp.full((1, ML_WIDTH), ML_HEAD_DIM ** -0.5, F32)], axis=1)
    return pl.pallas_call(
        functools.partial(_in_proj_kernel, seq),
        grid=(n,),
        in_specs=[
            pl.BlockSpec((ROW_TILE, D_MODEL), lambda i: (p_tile(i), 0)),
            pl.BlockSpec((HALO, D_MODEL), lambda i: (jnp.maximum(p_tile(i) * r - 1, 0), 0)),
            pl.BlockSpec((HALO, D_MODEL), lambda i: (jnp.minimum((p_tile(i) + 1) * r, tp // HALO - 1), 0)),
            pl.BlockSpec((ROW_TILE, D_MODEL), lambda i: (s_tile(i), 0)),
            pl.BlockSpec((HALO, D_MODEL), lambda i: (jnp.maximum(s_tile(i) * r - 1, 0), 0)),
            pl.BlockSpec((HALO, D_MODEL), lambda i: (jnp.minimum((s_tile(i) + 1) * r, ts // HALO - 1), 0)),
            pl.BlockSpec((D_MODEL, QK_WIDTH), const),
            pl.BlockSpec((D_MODEL, Z_WIDTH + 128), const),
            pl.BlockSpec((1, 128), const),
            pl.BlockSpec((CONV_WIDTH, QK_WIDTH), const),
            pl.BlockSpec((1, QK_WIDTH), const),
            pl.BlockSpec((1, QK_WIDTH), const),
        ],
        out_specs=[
            pl.BlockSpec((ROW_TILE, QK_WIDTH), lambda i: (i, 0)),
            pl.BlockSpec((ROW_TILE, Z_WIDTH), lambda i: (i, 0)),
            pl.BlockSpec((ROW_TILE, N_GATES), lambda i: (i, 0)),
            pl.BlockSpec((N_GATES, ROW_TILE), lambda i: (0, i)),
        ],
        out_shape=[
            jax.ShapeDtypeStruct((t, QK_WIDTH), BF16),
            jax.ShapeDtypeStruct((t, Z_WIDTH), BF16),
            jax.ShapeDtypeStruct((t, N_GATES), F32),
            jax.ShapeDtypeStruct((N_GATES, t), F32),
        ],
        scratch_shapes=[pltpu.VMEM((ROW_TILE + 2 * HALO, QK_WIDTH), F32)],
        compiler_params=_cparams(("arbitrary",)),
        name="in_proj_conv",
    )(xp, xp, xp, xs, xs, xs, w_qk, w_main, b_g, conv_w, conv_b.reshape(1, QK_WIDTH), scale)


ATT_TILE = 1024
ATT_SUB = ATT_TILE // BLOCK


ONES_ROWS = 16


def _attention_kernel(seq, q_ref, kp_ref, kc_ref, kn_ref, vp_ref, vc_ref, vn_ref, bias_ref, sink_ref,
                      o_ref, klo_ref, khi_ref, vt_ref):
    i = pl.program_id(0)
    lane = lax.broadcasted_iota(jnp.int32, (ATT_TILE + 2 * BLOCK, 2 * ATT_HEAD_DIM), 1)
    kband = jnp.concatenate([kp_ref[...], kc_ref[...], kn_ref[...]], axis=0)
    zero = jnp.zeros_like(kband)
    klo_ref[...] = jnp.where(lane < ATT_HEAD_DIM, kband, zero)
    khi_ref[...] = jnp.where(lane < ATT_HEAD_DIM, zero, kband)
    vband = jnp.concatenate([vp_ref[...], vc_ref[...], vn_ref[...]], axis=0)
    vt_ref[0:2 * ATT_HEAD_DIM, :] = vband.T
    vt_ref[2 * ATT_HEAD_DIM:, :] = jnp.ones((ONES_ROWS, ATT_TILE + 2 * BLOCK), BF16)
    feat = lax.broadcasted_iota(jnp.int32, (2 * ATT_HEAD_DIM, ATT_GROUP * BLOCK), 0)
    def block(s):
        first, last = _seq_pos(i * ATT_SUB + s, *seq)
        variant = jnp.where(first, 1, jnp.where(last, 2, 0))
        q = q_ref[s * BLOCK:(s + 1) * BLOCK, :]
        q_all = jnp.concatenate([q[:, t * 128:(t + 1) * 128] for t in range(ATT_GROUP)], axis=0)
        vt = vt_ref[:, s * BLOCK:(s + 3) * BLOCK]
        scores = [_dot_nt(k_ref[s * BLOCK:(s + 3) * BLOCK, :], q_all) for k_ref in (klo_ref, khi_ref)]
        yield
        ms, ps = [], []
        for kv in range(ATT_KV_HEADS):
            logits = scores[kv] + bias_ref[variant, kv]
            ms.append(jnp.maximum(jnp.max(logits, axis=0, keepdims=True), sink_ref[kv]))
            ps.append(jnp.exp(logits - ms[kv]).astype(BF16))
        yield
        ovs = [_dot(vt, p) for p in ps]
        yield
        outs = []
        for kv in range(ATT_KV_HEADS):
            den = ovs[kv][2 * ATT_HEAD_DIM:2 * ATT_HEAD_DIM + 1, :] + jnp.exp(sink_ref[kv] - ms[kv])
            outs.append(ovs[kv][0:2 * ATT_HEAD_DIM, :] * (1.0 / den))
        both = jnp.where(feat < ATT_HEAD_DIM, outs[0], outs[1]).astype(BF16)
        for t in range(ATT_GROUP):
            o_ref[t * 128:(t + 1) * 128, s * BLOCK:(s + 1) * BLOCK] = both[:, t * BLOCK:(t + 1) * BLOCK]

    _trace_stagewise([block(s) for s in range(ATT_SUB)])


def _attention(z, bias, sink, seq):
    t = z.shape[0]
    n = t // ATT_TILE
    nblk = t // BLOCK
    band = ATT_TILE + 2 * BLOCK
    prev = lambda i: jnp.maximum(i * ATT_SUB - 1, 0)
    nxt = lambda i: jnp.minimum((i + 1) * ATT_SUB, nblk - 1)
    return pl.pallas_call(
        functools.partial(_attention_kernel, seq),
        grid=(n,),
        in_specs=[
            pl.BlockSpec((ATT_TILE, ATT_WIDTH), lambda i: (i, ZB_ATTQ)),
            pl.BlockSpec((BLOCK, 128), lambda i: (prev(i), ZB_ATTK)),
            pl.BlockSpec((ATT_TILE, 128), lambda i: (i, ZB_ATTK)),
            pl.BlockSpec((BLOCK, 128), lambda i: (nxt(i), ZB_ATTK)),
            pl.BlockSpec((BLOCK, 128), lambda i: (prev(i), ZB_ATTV)),
            pl.BlockSpec((ATT_TILE, 128), lambda i: (i, ZB_ATTV)),
            pl.BlockSpec((BLOCK, 128), lambda i: (nxt(i), ZB_ATTV)),
            pl.BlockSpec((3, ATT_KV_HEADS, 3 * BLOCK, ATT_GROUP * BLOCK), lambda i: (0, 0, 0, 0)),
            pl.BlockSpec((ATT_KV_HEADS, 1, ATT_GROUP * BLOCK), lambda i: (0, 0, 0)),
        ],
        out_specs=pl.BlockSpec((ATT_WIDTH, ATT_TILE), lambda i: (0, i)),
        out_shape=jax.ShapeDtypeStruct((ATT_WIDTH, t), BF16),
        scratch_shapes=[pltpu.VMEM((band, 128), BF16), pltpu.VMEM((band, 128), BF16),
                        pltpu.VMEM((2 * ATT_HEAD_DIM + ONES_ROWS, band), BF16)],
        compiler_params=_cparams(("arbitrary",)),
        name="window_attention",
    )(z, z, z, z, z, z, z, bias, sink)


def _t5_bucket(rel):
    half = N_BUCKETS // 2
    exact = half // 2
    n = np.abs(rel)
    large = exact + (np.log(np.maximum(n, 1) / exact) / np.log(MAX_DISTANCE / exact) * (half - exact)).astype(np.int32)
    large = np.minimum(large, half - 1)
    return ((rel > 0).astype(np.int32) * half + np.where(n < exact, n, large)).astype(np.int32)


def _attention_tables(rel_bias, attn_sink):
    rel = np.arange(3 * BLOCK)[:, None] - BLOCK - np.arange(BLOCK)[None, :]
    onehot = jnp.asarray(_t5_bucket(rel)[..., None] == np.arange(N_BUCKETS), F32)
    bias = jnp.einsum('kqb,bh->hkq', onehot, rel_bias.astype(F32), precision=lax.Precision.HIGHEST)
    bias = jnp.where(jnp.asarray(np.abs(rel) <= WINDOW)[None], bias, NEG)
    bias = bias.reshape(ATT_KV_HEADS, ATT_GROUP, 3 * BLOCK, BLOCK).transpose(0, 2, 1, 3)
    bias = bias.reshape(ATT_KV_HEADS, 3 * BLOCK, ATT_GROUP * BLOCK)
    key = np.arange(3 * BLOCK)[None, :, None]
    first = jnp.where(jnp.asarray(key < BLOCK), NEG, bias)
    last = jnp.where(jnp.asarray(key >= 2 * BLOCK), NEG, bias)
    sink = jnp.repeat(attn_sink.astype(F32), BLOCK).reshape(ATT_KV_HEADS, 1, ATT_GROUP * BLOCK)
    return jnp.stack([bias, first, last]), sink


def _mlstm_kernel(seq, n_chunks, *refs):
    fwd_in, bwd_in, (of_ref, ob_ref), state = refs[0:5], refs[5:10], refs[10:12], refs[12:]
    _trace_stagewise(_mlstm_chains(seq, False, n_chunks, *fwd_in, of_ref, *state[0:2])
                     + _mlstm_chains(seq, True, n_chunks, *bwd_in, ob_ref, *state[2:4]))


def _mlstm_chains(seq, reverse, n_chunks, q_ref, k_ref, v_ref, gr_ref, gc_ref, o_ref, ct_ref, m_ref):
    step = pl.program_id(0)
    chunk = (n_chunks - 1 - step) if reverse else step
    first, last = _seq_pos(chunk, *seq)
    fresh = last if reverse else first

    L = ML_CHUNK
    row = lax.broadcasted_iota(jnp.int32, (L, L), 0)
    col = lax.broadcasted_iota(jnp.int32, (L, L), 1)
    vis = (row >= col) if reverse else (row <= col)
    vis_t = (col >= row) if reverse else (col <= row)
    gr = gr_ref[...]
    gc = gc_ref[...]
    r3 = _dot(jnp.concatenate(_split3(_log_sigmoid(gr)), axis=0), vis.astype(BF16))
    b_rows = r3[0:N_GATES] + r3[N_GATES:2 * N_GATES] + r3[2 * N_GATES:]
    vis_tb = vis_t.astype(BF16)
    c_hi, c_mid, c_lo = _split3(_log_sigmoid(gc))
    b_cols = _dot(vis_tb, c_hi) + _dot(vis_tb, c_mid) + _dot(vis_tb, c_lo)
    i_off = 2 * ML_HEADS if reverse else 0
    f_off = i_off + ML_HEADS
    end = 0 if reverse else L - 1
    ones = jnp.ones((ONES_ROWS, L), BF16)

    def chain(h):
        hs = slice(h * ML_HEAD_DIM, (h + 1) * ML_HEAD_DIM)
        q = q_ref[:, hs]
        k = k_ref[:, hs]
        vt1 = jnp.concatenate([v_ref[:, hs].T, ones], axis=0)
        b_row = b_rows[f_off + h:f_off + h + 1, :]
        u_row = gr[i_off + h:i_off + h + 1, :] - b_row
        u_col = gc[:, i_off + h:i_off + h + 1] - b_cols[:, f_off + h:f_off + h + 1]
        g = b_row[:, end:end + 1]
        ct_old = jnp.where(fresh, 0.0, ct_ref[h])
        m_old = jnp.where(fresh, 0.0, m_ref[h])[:, 0:1]
        kq = _dot_nt(k, q)
        from_state = _dot_nt(ct_old.astype(BF16), q)
        yield
        u_mat = jnp.where(vis, u_col, NEG)
        mm = jnp.maximum(jnp.max(u_mat, axis=0, keepdims=True), m_old)
        st = (kq * jnp.exp(u_mat - mm)).astype(BF16)
        yield
        tot = _dot(vt1, st) + jnp.exp(m_old - mm) * from_state
        a_max = jnp.max(g + u_row, axis=-1, keepdims=True)
        m_new = jnp.maximum(g + m_old, a_max)
        s_old = jnp.exp(g + m_old - m_new)
        weighted = (vt1.astype(F32) * jnp.exp(g + u_row - m_new)).astype(BF16)
        new_state = _dot(weighted, k)
        yield
        den = tot[ML_HEAD_DIM:ML_HEAD_DIM + 1, :]
        floor = jnp.exp(-(b_row + mm))
        o_ref[hs, :] = (tot[0:ML_HEAD_DIM, :] * (1.0 / jnp.maximum(jnp.abs(den), floor))).astype(BF16)
        ct_ref[h] = s_old * ct_old + new_state
        m_ref[h] = jnp.broadcast_to(m_new, (1, ML_HEAD_DIM))

    return [chain(h) for h in range(ML_HEADS)]


def _mlstm(qk, z, gates_r, gates_c, seq):
    t = qk.shape[0]
    nc = t // ML_CHUNK

    def chunk_specs(ch):
        return [
            pl.BlockSpec((ML_CHUNK, ML_WIDTH), lambda i: (ch(i), 0)),
            pl.BlockSpec((ML_CHUNK, ML_WIDTH), lambda i: (ch(i), 1)),
            pl.BlockSpec((ML_CHUNK, ML_WIDTH), lambda i: (ch(i), ZB_MLV)),
            pl.BlockSpec((N_GATES, ML_CHUNK), lambda i: (0, ch(i))),
            pl.BlockSpec((ML_CHUNK, N_GATES), lambda i: (ch(i), 0)),
        ]

    fwd = lambda i: i
    bwd = lambda i: nc - 1 - i
    state = [
        pltpu.VMEM((ML_HEADS, ML_HEAD_DIM + ONES_ROWS, ML_HEAD_DIM), F32),
        pltpu.VMEM((ML_HEADS, 1, ML_HEAD_DIM), F32),
    ]
    operands = (qk, qk, z, gates_r, gates_c)
    return pl.pallas_call(
        functools.partial(_mlstm_kernel, seq, nc),
        grid=(nc,),
        in_specs=chunk_specs(fwd) + chunk_specs(bwd),
        out_specs=[pl.BlockSpec((ML_WIDTH, ML_CHUNK), lambda i: (0, fwd(i))),
                   pl.BlockSpec((ML_WIDTH, ML_CHUNK), lambda i: (0, bwd(i)))],
        out_shape=[jax.ShapeDtypeStruct((ML_WIDTH, t), BF16), jax.ShapeDtypeStruct((ML_WIDTH, t), BF16)],
        scratch_shapes=state + state,
        compiler_params=_cparams(("arbitrary",)),
        name="mlstm",
    )(*operands, *operands)


def _mix_out(n_p_tiles, toks, xp_ref, xs_ref, att_ref, hf_ref, hb_ref, og_ref, wa_ref, wm_ref, gain_ref, g_ref,
             b_ref):
    i = pl.program_id(0)
    x = jnp.where(i < n_p_tiles, xp_ref[toks, :], xs_ref[toks, :])
    h = hf_ref[:, toks].astype(F32) + hb_ref[:, toks].astype(F32)
    parts = []
    for hd in range(ML_HEADS):
        hh = h[hd * ML_HEAD_DIM:(hd + 1) * ML_HEAD_DIM, :]
        mu = jnp.mean(hh, axis=0, keepdims=True)
        hc = hh - mu
        var = jnp.mean(hc * hc, axis=0, keepdims=True)
        parts.append(hc * lax.rsqrt(var + LN_EPS))
    hn = jnp.concatenate(parts, axis=0) * gain_ref[...] * _sigmoid(og_ref[toks, :].T.astype(F32))
    mixed = _dot(att_ref[:, toks].T, wa_ref[...]) + _dot(hn.astype(BF16).T, wm_ref[...])
    return _layer_norm(DN_ALPHA * x + mixed, g_ref[...], b_ref[...])


def _mem_kv_kernel(m_ref, w_ref, o_ref):
    o_ref[...] = _dot(m_ref[...].astype(BF16), w_ref[...]).astype(BF16)


def _mem_kv(mem, wkv):
    rows = mem.shape[0]
    return pl.pallas_call(
        _mem_kv_kernel,
        grid=(rows // MEM_TOKENS,),
        in_specs=[pl.BlockSpec((MEM_TOKENS, D_MODEL), lambda i: (i, 0)),
                  pl.BlockSpec((D_MODEL, 2 * D_MODEL), lambda i: (0, 0))],
        out_specs=pl.BlockSpec((MEM_TOKENS, 2 * D_MODEL), lambda i: (i, 0)),
        out_shape=jax.ShapeDtypeStruct((rows, 2 * D_MODEL), BF16),
        compiler_params=_cparams(("arbitrary",)),
        name="mem_kv",
    )(mem, wkv)


def _pack_bf16_pairs(a, b, exact=False):
    if not exact:
        a, b = a.astype(BF16).astype(F32), b.astype(BF16).astype(F32)
    ua, ub = pltpu.bitcast(a, jnp.uint32), pltpu.bitcast(b, jnp.uint32)
    return (ua >> 16) | (ub & jnp.uint32(0xFFFF0000))


def _unpack_bf16_pairs(u):
    lo = pltpu.bitcast(u << 16, F32)
    hi = pltpu.bitcast(u & jnp.uint32(0xFFFF0000), F32)
    return lo, hi


TOKEN_GROUPS = 2


def _cross_router_kernel(n_p_tiles, *refs):
    mix_refs, (kv_ref, wq_ref, wo_ref, g_ref, b_ref, wr_ref, br_ref, tri_ref, etri_ref,
               x2_ref, x2b_ref, gate_ref, slot_ref, chunk_ref) = refs[:11], refs[11:]
    head_cols = [slice(h * MEM_HEAD_DIM, (h + 1) * MEM_HEAD_DIM) for h in range(MEM_HEADS)]

    def token_group(r):
        toks = slice(r * ROW_TILE // TOKEN_GROUPS, (r + 1) * ROW_TILE // TOKEN_GROUPS)
        x = _mix_out(n_p_tiles, toks, *mix_refs)
        yield
        q = _dot(x.astype(BF16), wq_ref[...]).astype(BF16)
        scores = [_dot_nt(q[:, hs], kv_ref[:, hs]) for hs in head_cols]
        yield
        probs, dens = [], []
        for logits in scores:
            p = jnp.exp(logits - jnp.max(logits, axis=-1, keepdims=True))
            dens.append(jnp.sum(p, axis=-1, keepdims=True))
            probs.append(p.astype(BF16))
        yield
        values = [_dot(p, kv_ref[:, D_MODEL + hs.start:D_MODEL + hs.stop]) for p, hs in zip(probs, head_cols)]
        o = jnp.concatenate([(v * (1.0 / den)).astype(BF16) for v, den in zip(values, dens)], axis=1)
        y = _dot(o, wo_ref[...])
        yield
        x2 = _layer_norm(DN_ALPHA * x + y, g_ref[...], b_ref[...])
        x2_ref[toks, :] = x2
        x2b_ref[toks, :] = x2.astype(BF16)

    _trace_stagewise([token_group(r) for r in range(TOKEN_GROUPS)])
    x2b = x2b_ref[...]

    logits = _dot_nt(wr_ref[...], x2b) + br_ref[...]
    expert = lax.broadcasted_iota(jnp.int32, logits.shape, 0)
    work = logits
    vals, sels = [], []
    for k in range(TOP_K):
        mx = jnp.max(work, axis=0, keepdims=True)
        ix = jnp.min(jnp.where(work == mx, expert, N_EXPERTS), axis=0, keepdims=True)
        sel = expert == ix
        work = jnp.where(sel, -jnp.inf, work)
        vals.append(mx)
        sels.append(sel)
    es = [jnp.exp(v - vals[0]) for v in vals]
    tot = es[0] + es[1] + es[2] + es[3]
    chosen = jnp.zeros(logits.shape, F32)
    for k in range(TOP_K):
        gate_ref[k:k + 1, :] = es[k] / tot
        chosen = chosen + sels[k].astype(F32)
    count = jnp.sum(chosen, axis=1, keepdims=True)
    chunks = jnp.floor((count + (SEG_ALIGN - 1)) * (1.0 / SEG_ALIGN))
    chunks_b = jnp.broadcast_to(chunks, (N_EXPERTS, 128))
    seg_first = _dot(etri_ref[...], chunks_b.astype(BF16))[:, 0:1] * SEG_ALIGN
    before = _dot(chosen.astype(BF16), tri_ref[...])
    local_row = before + seg_first
    for k in range(TOP_K):
        slot_ref[k:k + 1, :] = jnp.sum(jnp.where(sels[k], local_row, 0.0), axis=0, keepdims=True).astype(jnp.int32)
    chunk_ref[0] = chunks_b.astype(jnp.int32)


def _cross_router(xp, xs, att, hf, hb, z, w_att, w_ml, gain, g1, b1, kv, wq, wo, g, b, w_router, b_router,
                  mem_of_tile):
    t = att.shape[1]
    n_p = xp.shape[0] // ROW_TILE
    n = t // ROW_TILE
    const = lambda i: (0, 0)
    tile = lambda i: (i, 0)
    lanes = lambda i: (0, i)
    tri = jnp.asarray(np.triu(np.ones((ROW_TILE, ROW_TILE), np.float32), 1), BF16)
    etri = jnp.asarray(np.tril(np.ones((N_EXPERTS, N_EXPERTS), np.float32), -1), BF16)
    return pl.pallas_call(
        functools.partial(_cross_router_kernel, n_p),
        grid=(n,),
        in_specs=[
            pl.BlockSpec((ROW_TILE, D_MODEL), lambda i: (jnp.minimum(i, n_p - 1), 0)),
            pl.BlockSpec((ROW_TILE, D_MODEL), lambda i: (jnp.maximum(i - n_p, 0), 0)),
            pl.BlockSpec((ATT_WIDTH, ROW_TILE), lanes),
            pl.BlockSpec((ML_WIDTH, ROW_TILE), lanes),
            pl.BlockSpec((ML_WIDTH, ROW_TILE), lanes),
            pl.BlockSpec((ROW_TILE, ML_WIDTH), lambda i: (i, ZB_MLO)),
            pl.BlockSpec((ATT_WIDTH, D_MODEL), const),
            pl.BlockSpec((ML_WIDTH, D_MODEL), const),
            pl.BlockSpec((ML_WIDTH, 1), const),
            pl.BlockSpec((1, D_MODEL), const),
            pl.BlockSpec((1, D_MODEL), const),
            pl.BlockSpec((MEM_TOKENS, 2 * D_MODEL), lambda i: (mem_of_tile(i), 0)),
            pl.BlockSpec((D_MODEL, D_MODEL), const),
            pl.BlockSpec((D_MODEL, D_MODEL), const),
            pl.BlockSpec((1, D_MODEL), const),
            pl.BlockSpec((1, D_MODEL), const),
            pl.BlockSpec((N_EXPERTS, D_MODEL), const),
            pl.BlockSpec((N_EXPERTS, 1), const),
            pl.BlockSpec((ROW_TILE, ROW_TILE), const),
            pl.BlockSpec((N_EXPERTS, N_EXPERTS), const),
        ],
        out_specs=[
            pl.BlockSpec((ROW_TILE, D_MODEL), tile),
            pl.BlockSpec((ROW_TILE, D_MODEL), tile),
            pl.BlockSpec((TOP_K, ROW_TILE), lanes),
            pl.BlockSpec((TOP_K, ROW_TILE), lanes),
            pl.BlockSpec((1, N_EXPERTS, 128), lambda i: (i, 0, 0)),
        ],
        out_shape=[
            jax.ShapeDtypeStruct((t, D_MODEL), F32),
            jax.ShapeDtypeStruct((t, D_MODEL), BF16),
            jax.ShapeDtypeStruct((TOP_K, t), F32),
            jax.ShapeDtypeStruct((TOP_K, t), jnp.int32),
            jax.ShapeDtypeStruct((n, N_EXPERTS, 128), jnp.int32),
        ],
        compiler_params=_cparams(("arbitrary",)),
        name="mix_out_cross_attn_router",
    )(xp, xs, att, hf, hb, z, w_att, w_ml, gain.reshape(ML_WIDTH, 1), g1.reshape(1, D_MODEL), b1.reshape(1, D_MODEL),
      kv, wq, wo, g.reshape(1, D_MODEL), b.reshape(1, D_MODEL), w_router.T.astype(BF16),
      b_router.reshape(N_EXPERTS, 1), tri, etri)


SEG_ALIGN = 8
LOCAL_ROWS = ROW_TILE * TOP_K + N_EXPERTS * SEG_ALIGN
HALF = D_MODEL // 2


def _rows(chunks):
    return pl.multiple_of(chunks * SEG_ALIGN, SEG_ALIGN)


def _segment_copies(chunks_ref, seg_ref, loc_ref, tile, make_copy):
    def per_expert(e, carry):
        entry = tile * N_EXPERTS + e

        @pl.when(chunks_ref[entry] > 0)
        def _():
            make_copy(_rows(loc_ref[entry]), _rows(seg_ref[entry]), _rows(chunks_ref[entry])).start()

        return carry

    lax.fori_loop(0, N_EXPERTS, per_expert, 0)


def _wait_copies(chunks, make_copy):
    @pl.when(chunks > 0)
    def _():
        make_copy(0, 0, _rows(chunks)).wait()


def _dispatch_kernel(chunks_ref, seg_ref, loc_ref, total_ref, tail_ref, tailn_ref, nact_ref,
                     x_ref, slot_ref, rows_hbm, local_ref, zero_ref, sems):
    i = pl.program_id(0)
    n = pl.num_programs(0)
    buf = i % 2

    def copy_out(b):
        return lambda loc, seg, rows: pltpu.make_async_copy(
            local_ref.at[b, pl.ds(loc, rows), :], rows_hbm.at[pl.ds(seg, rows), :], sems.at[b])

    def run(b):
        @pl.when(i >= 2)
        def _():
            _wait_copies(total_ref[jnp.maximum(i - 2, 0)], copy_out(b))

        slots = slot_ref[...].astype(F32)
        row = lax.broadcasted_iota(jnp.int32, (LOCAL_ROWS, ROW_TILE), 0).astype(F32)
        miss = (slots[0:1, :] - row) * (slots[1:2, :] - row)
        for k in range(2, TOP_K):
            miss = miss * (slots[k:k + 1, :] - row)
        perm = jnp.where(miss == 0.0, 1.0, 0.0).astype(BF16)
        picked = _dot(perm, x_ref[...])
        local_ref[b] = _pack_bf16_pairs(picked[:, :HALF], picked[:, HALF:], exact=True)
        _segment_copies(chunks_ref, seg_ref, loc_ref, i, copy_out(b))

    for b in range(2):
        pl.when(buf == b)(functools.partial(run, b))

    @pl.when(i == n - 1)
    def _():
        for b in range(2):
            step = jnp.where(buf == b, i, i - 1)
            _wait_copies(total_ref[step], copy_out(b))
        zero_ref[...] = jnp.zeros_like(zero_ref)
        fill = lambda _, seg, rows: pltpu.make_async_copy(
            zero_ref.at[pl.ds(0, rows), :], rows_hbm.at[pl.ds(seg, rows), :], sems.at[2])

        def per_expert(e, carry):
            @pl.when(tailn_ref[e] > 0)
            def _():
                fill(0, _rows(tail_ref[e]), _rows(tailn_ref[e])).start()

            _wait_copies(tailn_ref[e], fill)
            return carry

        lax.fori_loop(0, N_EXPERTS, per_expert, 0)
        fill_block = lambda blk: pltpu.make_async_copy(
            zero_ref, rows_hbm.at[pl.ds(pl.multiple_of(blk * MOE_ROWS, MOE_ROWS), MOE_ROWS), :], sems.at[2])
        n_blocks = rows_hbm.shape[0] // MOE_ROWS

        def start_block(blk, carry):
            fill_block(blk).start()
            return carry

        def wait_block(blk, carry):
            fill_block(0).wait()
            return carry

        lax.fori_loop(nact_ref[0], n_blocks, start_block, 0)
        lax.fori_loop(nact_ref[0], n_blocks, wait_block, 0)


def _dispatch(x2b, slots, tables, n_rows):
    n = x2b.shape[0] // ROW_TILE
    assert n >= 2
    return pl.pallas_call(
        _dispatch_kernel,
        grid_spec=pltpu.PrefetchScalarGridSpec(
            num_scalar_prefetch=7,
            grid=(n,),
            in_specs=[
                pl.BlockSpec((ROW_TILE, D_MODEL), lambda i, *_: (i, 0)),
                pl.BlockSpec((TOP_K, ROW_TILE), lambda i, *_: (0, i)),
            ],
            out_specs=pl.BlockSpec(memory_space=pl.ANY),
            scratch_shapes=[
                pltpu.VMEM((2, LOCAL_ROWS, HALF), jnp.uint32),
                pltpu.VMEM((MOE_ROWS, HALF), jnp.uint32),
                pltpu.SemaphoreType.DMA((3,)),
            ],
        ),
        out_shape=jax.ShapeDtypeStruct((n_rows, HALF), jnp.uint32),
        compiler_params=_cparams(("arbitrary",)),
        name="moe_dispatch",
    )(*tables, x2b, slots)


def _expert_kernel(be_ref, na_ref, x_ref, wu_ref, bu_ref, wd_ref, bd_ref, o_ref, wub_ref, wdb_ref):
    i = pl.program_id(0)

    @pl.when((i == 0) | (be_ref[i] != be_ref[jnp.maximum(i - 1, 0)]))
    def _():
        wub_ref[...] = wu_ref[0].astype(BF16)
        wdb_ref[...] = wd_ref[0].astype(BF16)

    @pl.when(i < na_ref[0])
    def _():
        lo, hi = _unpack_bf16_pairs(x_ref[...])
        x = jnp.concatenate([lo, hi], axis=1).astype(BF16)
        hu = _dot(x, wub_ref[...]) + bu_ref[0]
        h_glu = jnp.minimum(hu[:, :D_FF], SWIGLU_LIMIT)
        h_lin = jnp.clip(hu[:, D_FF:], -SWIGLU_LIMIT, SWIGLU_LIMIT)
        hh = h_glu * _sigmoid(SWIGLU_ALPHA * h_glu) * (h_lin + 1.0)
        y = _dot(hh.astype(BF16), wdb_ref[...]) + bd_ref[0]
        o_ref[...] = _pack_bf16_pairs(y[:, :D_MODEL // 2], y[:, D_MODEL // 2:])

    @pl.when(i >= na_ref[0])
    def _():
        o_ref[...] = jnp.zeros_like(o_ref)


def _experts(xs, block_e, n_active, w_up, b_up, w_down, b_down):
    n_rows = xs.shape[0]
    n_blk = n_rows // MOE_ROWS
    return pl.pallas_call(
        _expert_kernel,
        grid_spec=pltpu.PrefetchScalarGridSpec(
            num_scalar_prefetch=2,
            grid=(n_blk,),
            in_specs=[
                pl.BlockSpec((MOE_ROWS, D_MODEL // 2), lambda i, be, na: (jnp.minimum(i, na[0] - 1), 0)),
                pl.BlockSpec((1, D_MODEL, 2 * D_FF), lambda i, be, na: (be[i], 0, 0)),
                pl.BlockSpec((1, 1, 2 * D_FF), lambda i, be, na: (be[i], 0, 0)),
                pl.BlockSpec((1, D_FF, D_MODEL), lambda i, be, na: (be[i], 0, 0)),
                pl.BlockSpec((1, 1, D_MODEL), lambda i, be, na: (be[i], 0, 0)),
            ],
            out_specs=pl.BlockSpec((MOE_ROWS, D_MODEL // 2), lambda i, be, na: (i, 0)),
            scratch_shapes=[pltpu.VMEM((D_MODEL, 2 * D_FF), BF16), pltpu.VMEM((D_FF, D_MODEL), BF16)],
        ),
        out_shape=jax.ShapeDtypeStruct((n_rows, D_MODEL // 2), jnp.uint32),
        compiler_params=_cparams(("arbitrary",)),
        name="experts",
    )(block_e, n_active, xs, w_up, b_up.reshape(N_EXPERTS, 1, 2 * D_FF), w_down, b_down.reshape(N_EXPERTS, 1, D_MODEL))


def _combine_kernel(first_tile, chunks_ref, seg_ref, loc_ref, total_ref,
                    x_ref, rows_hbm, slot_ref, gate_ref, g_ref, b_ref, o_ref, local_ref, sems):
    i = pl.program_id(0)
    n = pl.num_programs(0)
    tile = first_tile + i
    buf = i % 2

    def copy_in(b):
        return lambda loc, seg, rows: pltpu.make_async_copy(
            rows_hbm.at[pl.ds(seg, rows), :], local_ref.at[b, pl.ds(loc, rows), :], sems.at[b])

    @pl.when(i == 0)
    def _():
        local_ref[...] = jnp.zeros_like(local_ref)
        _segment_copies(chunks_ref, seg_ref, loc_ref, tile, copy_in(0))

    def run(b):
        @pl.when(i + 1 < n)
        def _():
            _segment_copies(chunks_ref, seg_ref, loc_ref, tile + 1, copy_in(1 - b))

        _wait_copies(total_ref[tile], copy_in(b))
        lo, hi = _unpack_bf16_pairs(local_ref[b])
        rows = jnp.concatenate([lo, hi], axis=1).astype(BF16)
        slots = slot_ref[...]
        gates = gate_ref[...]
        row = lax.broadcasted_iota(jnp.int32, (ROW_TILE, LOCAL_ROWS), 1)
        weight = jnp.zeros((ROW_TILE, LOCAL_ROWS), F32)
        for k in range(TOP_K):
            weight = jnp.where(slots[:, k:k + 1] == row, gates[:, k:k + 1], weight)
        y = _dot(weight.astype(BF16), rows)
        o_ref[...] = _layer_norm(DN_ALPHA * x_ref[...] + y, g_ref[...], b_ref[...])

    for b in range(2):
        pl.when(buf == b)(functools.partial(run, b))


def _combine(x2, rows_out, slots_c, gates_c, tables, g, b, first_tile, n_tiles):
    const = lambda i, *_: (0, 0)
    tile = lambda i, *_: (first_tile + i, 0)
    return pl.pallas_call(
        functools.partial(_combine_kernel, first_tile),
        grid_spec=pltpu.PrefetchScalarGridSpec(
            num_scalar_prefetch=4,
            grid=(n_tiles,),
            in_specs=[
                pl.BlockSpec((ROW_TILE, D_MODEL), tile),
                pl.BlockSpec(memory_space=pl.ANY),
                pl.BlockSpec((ROW_TILE, TOP_K), tile),
                pl.BlockSpec((ROW_TILE, TOP_K), tile),
                pl.BlockSpec((1, D_MODEL), const),
                pl.BlockSpec((1, D_MODEL), const),
            ],
            out_specs=pl.BlockSpec((ROW_TILE, D_MODEL), lambda i, *_: (i, 0)),
            scratch_shapes=[pltpu.VMEM((2, LOCAL_ROWS, HALF), jnp.uint32), pltpu.SemaphoreType.DMA((2,))],
        ),
        out_shape=jax.ShapeDtypeStruct((n_tiles * ROW_TILE, D_MODEL), F32),
        compiler_params=_cparams(("arbitrary",)),
        name="combine_ln3",
    )(*tables[:4], x2, rows_out, slots_c, gates_c, g.reshape(1, D_MODEL), b.reshape(1, D_MODEL))


def kernel(x_prompt, x_sample, mem_prompt, mem_sample, rel_bias, w_in, b_gates, conv_w, conv_b, mh_gain, attn_sink,
           w_out, ln1_g, ln1_b, wq_mem, wkv_mem, wo_mem, ln2_g, ln2_b, w_router, b_router, w_up, b_up, w_down,
           b_down, ln3_g, ln3_b):
    assert w_in.shape[0] == 1, "single layer"
    bp, sp, _ = x_prompt.shape
    bs, ss, _ = x_sample.shape
    tp, ts = bp * sp, bs * ss
    t = tp + ts
    assert sp % ATT_TILE == 0 and ss % ATT_TILE == 0 and ATT_TILE % ROW_TILE == 0 and ATT_TILE % ML_CHUNK == 0
    xp = x_prompt.reshape(tp, D_MODEL)
    xs = x_sample.reshape(ts, D_MODEL)

    def seq_blocks(rows):
        return (tp // rows, sp // rows, ss // rows)

    w = w_in[0]
    q_end, k_end, v_end = ATT_WIDTH, ATT_WIDTH + 128, ATT_WIDTH + 256
    qk_end, mv_end, mo_end = v_end + 2 * ML_WIDTH, v_end + 3 * ML_WIDTH, v_end + 4 * ML_WIDTH
    head_order = np.concatenate([[h, ATT_GROUP + h] for h in range(ATT_GROUP)])
    att_perm = (head_order[:, None] * ATT_HEAD_DIM + np.arange(ATT_HEAD_DIM)[None, :]).reshape(-1)
    w_qk = w[:, v_end:qk_end].astype(BF16)
    w_main = jnp.concatenate([
        w[:, qk_end:mv_end], w[:, mv_end:mo_end],
        w[:, :q_end][:, att_perm] * (ATT_HEAD_DIM ** -0.5), w[:, q_end:k_end], w[:, k_end:v_end]], axis=1).astype(BF16)
    w_g = w[:, mo_end:].astype(BF16)
    w_att = w_out[0][:ATT_WIDTH][att_perm].astype(BF16)
    w_ml = w_out[0][ATT_WIDTH:].astype(BF16)

    qk, z, gates_c, gates_r = _in_proj(xp, xs, w_qk, w_main, w_g, b_gates[0], conv_w[0], conv_b[0],
                                       seq_blocks(ROW_TILE))

    bias, sink = _attention_tables(rel_bias, attn_sink[0])
    att = _attention(z, bias, sink, seq_blocks(BLOCK))

    h_f, h_b = _mlstm(qk, z, gates_r, gates_c, seq_blocks(ML_CHUNK))

    mem = jnp.concatenate([mem_prompt.reshape(bp * MEM_TOKENS, D_MODEL), mem_sample.reshape(bs * MEM_TOKENS, D_MODEL)])
    kv = _mem_kv(mem, wkv_mem[0].astype(BF16))
    n_p_tiles, p_tiles_per_seq, s_tiles_per_seq = seq_blocks(ROW_TILE)
    mem_of_tile = lambda i: jnp.where(i < n_p_tiles, i // p_tiles_per_seq, bp + (i - n_p_tiles) // s_tiles_per_seq)
    wq = (wq_mem[0] * (MEM_HEAD_DIM ** -0.5)).astype(BF16)
    x2, x2b, top_gate, slots, tile_chunks = _cross_router(
        xp, xs, att, h_f, h_b, z, w_att, w_ml, mh_gain[0], ln1_g[0], ln1_b[0],
        kv, wq, wo_mem[0].astype(BF16), ln2_g[0], ln2_b[0], w_router[0], b_router[0], mem_of_tile)

    n_tiles = t // ROW_TILE
    blk_chunks = MOE_ROWS // SEG_ALIGN
    chunks = tile_chunks[:, :, 0]
    used = jnp.sum(chunks, axis=0)
    region = ((used + blk_chunks - 1) // blk_chunks) * blk_chunks
    region_end = jnp.cumsum(region)
    region_start = region_end - region
    seg_start = region_start[None, :] + jnp.cumsum(chunks, axis=0) - chunks
    loc_start = jnp.cumsum(chunks, axis=1) - chunks
    n_blk = -(-(t * TOP_K + n_tiles * N_EXPERTS * (SEG_ALIGN - 1) + N_EXPERTS * (MOE_ROWS - 1)) // MOE_ROWS)
    blk_first = jnp.arange(n_blk, dtype=jnp.int32) * blk_chunks
    block_e = jnp.minimum(jnp.sum(blk_first[:, None] >= region_end[None, :], axis=1), N_EXPERTS - 1).astype(jnp.int32)
    n_active = (region_end[-1:] // blk_chunks).astype(jnp.int32)
    tables = (chunks.reshape(-1), seg_start.reshape(-1), loc_start.reshape(-1), jnp.sum(chunks, axis=1),
              region_start + used, region - used, n_active)
    tables = tuple(tb.astype(jnp.int32) for tb in tables)

    rows_in = _dispatch(x2b, slots, tables, n_blk * MOE_ROWS)
    rows_out = _experts(rows_in, block_e, n_active, w_up[0], b_up[0], w_down[0], b_down[0])

    slots_c, gates_c = slots.T, top_gate.T
    y_p = _combine(x2, rows_out, slots_c, gates_c, tables, ln3_g[0], ln3_b[0], 0, n_p_tiles)
    y_s = _combine(x2, rows_out, slots_c, gates_c, tables, ln3_g[0], ln3_b[0], n_p_tiles, n_tiles - n_p_tiles)
    return y_p.reshape(bp, sp, D_MODEL), y_s.reshape(bs, ss, D_MODEL)
```

```python
import functools

import numpy as np
import jax
import jax.numpy as jnp
from jax import lax
from jax.experimental import pallas as pl
from jax.experimental.pallas import tpu as pltpu

F32 = jnp.float32
BF16 = jnp.bfloat16

D_MODEL = 1024
ATT_HEADS = 8
ATT_KV_HEADS = 2
ATT_GROUP = ATT_HEADS // ATT_KV_HEADS
ATT_HEAD_DIM = 64
ATT_WIDTH = ATT_HEADS * ATT_HEAD_DIM
WINDOW = 128
BLOCK = WINDOW
N_BUCKETS = 32
MAX_DISTANCE = 128
ML_HEADS = 4
ML_HEAD_DIM = 128
ML_WIDTH = ML_HEADS * ML_HEAD_DIM
ML_CHUNK = 256
CONV_WIDTH = 5
N_GATES = 4 * ML_HEADS
MEM_TOKENS = 256
MEM_HEADS = 4
MEM_HEAD_DIM = D_MODEL // MEM_HEADS
N_EXPERTS = 32
TOP_K = 4
D_FF = D_MODEL
SWIGLU_LIMIT = 7.0
SWIGLU_ALPHA = 1.702
LN_EPS = 1e-5
DN_ALPHA = 2.0 ** 0.25

QK_WIDTH = 2 * ML_WIDTH
Z_WIDTH = ML_WIDTH + ML_WIDTH + ATT_WIDTH + 2 * ATT_KV_HEADS * ATT_HEAD_DIM
ZB_MLV = 0
ZB_MLO = 1
ZB_ATTQ = 2
ZB_ATTK = 12
ZB_ATTV = 13

ROW_TILE = 512
MOE_ROWS = 1024
NEG = -1e30
VMEM_LIMIT = 56 * 1024 * 1024


def _cparams(sem):
    return pltpu.CompilerParams(dimension_semantics=sem, vmem_limit_bytes=VMEM_LIMIT)


def _dot(a, b):
    return jnp.dot(a, b, preferred_element_type=F32)


def _dot_nt(a, b):
    return lax.dot_general(a, b, (((1,), (1,)), ((), ())), preferred_element_type=F32)


def _dot_tn(a, b):
    return lax.dot_general(a, b, (((0,), (0,)), ((), ())), preferred_element_type=F32)


def _split3(x):
    hi = x.astype(BF16)
    rest = x - hi.astype(F32)
    mid = rest.astype(BF16)
    lo = (rest - mid.astype(F32)).astype(BF16)
    return hi, mid, lo


def _layer_norm(y, g, b):
    mu = jnp.mean(y, axis=-1, keepdims=True)
    yc = y - mu
    var = jnp.mean(yc * yc, axis=-1, keepdims=True)
    return yc * lax.rsqrt(var + LN_EPS) * g + b


def _log_sigmoid(x):
    return jnp.minimum(x, 0.0) - jnp.log1p(jnp.exp(-jnp.abs(x)))


def _sigmoid(x):
    return 1.0 / (1.0 + jnp.exp(-x))


def _trace_stagewise(chains):
    while chains:
        chains = [c for c in chains if next(c, "done") != "done"]


def _seq_pos(blk, n_p_blocks, p_blocks_per_seq, s_blocks_per_seq):
    in_p = blk < n_p_blocks
    local = jnp.where(in_p, blk % p_blocks_per_seq, (blk - n_p_blocks) % s_blocks_per_seq)
    per = jnp.where(in_p, p_blocks_per_seq, s_blocks_per_seq)
    return local == 0, local == per - 1


HALO = 8


def _in_proj_kernel(seq, xp_ref, xpl_ref, xpr_ref, xs_ref, xsl_ref, xsr_ref, wqk_ref, w_ref, bg_ref,
                    cw_ref, cb_ref, scale_ref, qk_ref, z_ref, gc_ref, gr_ref, buf_ref):
    i = pl.program_id(0)
    in_p = i < seq[0]
    first, last = _seq_pos(i, *seq)
    x = jnp.where(in_p, xp_ref[...], xs_ref[...])
    left = jnp.where(in_p, xpl_ref[...], xsl_ref[...])
    right = jnp.where(in_p, xpr_ref[...], xsr_ref[...])
    xb = x.astype(BF16)
    x_ext = jnp.concatenate([left, x, right], axis=0).astype(BF16)
    row = lax.broadcasted_iota(jnp.int32, (ROW_TILE + 2 * HALO, 1), 0)
    outside = (first & (row < HALO)) | (last & (row >= ROW_TILE + HALO))
    buf_ref[...] = jnp.where(outside, 0.0, _dot(x_ext, wqk_ref[...]))
    acc = jnp.zeros((ROW_TILE, QK_WIDTH), F32) + cb_ref[...]
    for j in range(CONV_WIDTH):
        off = HALO + j - CONV_WIDTH // 2
        acc = acc + buf_ref[off:off + ROW_TILE, :] * cw_ref[j:j + 1, :]
    qk_ref[...] = (acc * _sigmoid(acc) * scale_ref[...]).astype(BF16)
    zg = _dot(xb, w_ref[...])
    z_ref[...] = zg[:, :Z_WIDTH].astype(BF16)
    gates = zg[:, Z_WIDTH:] + bg_ref[...]
    gc_ref[...] = gates[:, :N_GATES]
    gr_ref[...] = gates.T[:N_GATES, :]


def _in_proj(xp, xs, w_qk, w_main, w_g, b_g, conv_w, conv_b, seq):
    w_main = jnp.pad(jnp.concatenate([w_main, w_g], axis=1), ((0, 0), (0, 128 - N_GATES)))
    b_g = jnp.pad(b_g, (0, 128 - N_GATES)).reshape(1, 128)
    tp, ts = xp.shape[0], xs.shape[0]
    t = tp + ts
    n_p = tp // ROW_TILE
    n = t // ROW_TILE
    r = ROW_TILE // HALO
    const = lambda i: (0, 0)
    p_tile = lambda i: jnp.minimum(i, n_p - 1)
    s_tile = lambda i: jnp.maximum(i - n_p, 0)
    scale = jnp.concatenate([jnp.ones((1, ML_WIDTH), F32), jnp.full((1, ML_WIDTH), ML_HEAD_DIM ** -0.5, F32)], axis=1)
    return pl.pallas_call(
        functools.partial(_in_proj_kernel, seq),
        grid=(n,),
        in_specs=[
            pl.BlockSpec((ROW_TILE, D_MODEL), lambda i: (p_tile(i), 0)),
            pl.BlockSpec((HALO, D_MODEL), lambda i: (jnp.maximum(p_tile(i) * r - 1, 0), 0)),
            pl.BlockSpec((HALO, D_MODEL), lambda i: (jnp.minimum((p_tile(i) + 1) * r, tp // HALO - 1), 0)),
            pl.BlockSpec((ROW_TILE, D_MODEL), lambda i: (s_tile(i), 0)),
            pl.BlockSpec((HALO, D_MODEL), lambda i: (jnp.maximum(s_tile(i) * r - 1, 0), 0)),
            pl.BlockSpec((HALO, D_MODEL), lambda i: (jnp.minimum((s_tile(i) + 1) * r, ts // HALO - 1), 0)),
            pl.BlockSpec((D_MODEL, QK_WIDTH), const),
            pl.BlockSpec((D_MODEL, Z_WIDTH + 128), const),
            pl.BlockSpec((1, 128), const),
            pl.BlockSpec((CONV_WIDTH, QK_WIDTH), const),
            pl.BlockSpec((1, QK_WIDTH), const),
            pl.BlockSpec((1, QK_WIDTH), const),
        ],
        out_specs=[
            pl.BlockSpec((ROW_TILE, QK_WIDTH), lambda i: (i, 0)),
            pl.BlockSpec((ROW_TILE, Z_WIDTH), lambda i: (i, 0)),
            pl.BlockSpec((ROW_TILE, N_GATES), lambda i: (i, 0)),
            pl.BlockSpec((N_GATES, ROW_TILE), lambda i: (0, i)),
        ],
        out_shape=[
            jax.ShapeDtypeStruct((t, QK_WIDTH), BF16),
            jax.ShapeDtypeStruct((t, Z_WIDTH), BF16),
            jax.ShapeDtypeStruct((t, N_GATES), F32),
            jax.ShapeDtypeStruct((N_GATES, t), F32),
        ],
        scratch_shapes=[pltpu.VMEM((ROW_TILE + 2 * HALO, QK_WIDTH), F32)],
        compiler_params=_cparams(("arbitrary",)),
        name="in_proj_conv",
    )(xp, xp, xp, xs, xs, xs, w_qk, w_main, b_g, conv_w, conv_b.reshape(1, QK_WIDTH), scale)


ATT_TILE = 1024
ATT_SUB = ATT_TILE // BLOCK


ONES_ROWS = 16


def _attention_kernel(seq, q_ref, kp_ref, kc_ref, kn_ref, vp_ref, vc_ref, vn_ref, bias_ref, sink_ref,
                      o_ref, klo_ref, khi_ref, vt_ref):
    i = pl.program_id(0)
    lane = lax.broadcasted_iota(jnp.int32, (ATT_TILE + 2 * BLOCK, 2 * ATT_HEAD_DIM), 1)
    kband = jnp.concatenate([kp_ref[...], kc_ref[...], kn_ref[...]], axis=0)
    zero = jnp.zeros_like(kband)
    klo_ref[...] = jnp.where(lane < ATT_HEAD_DIM, kband, zero)
    khi_ref[...] = jnp.where(lane < ATT_HEAD_DIM, zero, kband)
    vband = jnp.concatenate([vp_ref[...], vc_ref[...], vn_ref[...]], axis=0)
    vt_ref[0:2 * ATT_HEAD_DIM, :] = vband.T
    vt_ref[2 * ATT_HEAD_DIM:, :] = jnp.ones((ONES_ROWS, ATT_TILE + 2 * BLOCK), BF16)
    feat = lax.broadcasted_iota(jnp.int32, (2 * ATT_HEAD_DIM, ATT_GROUP * BLOCK), 0)
    def block(s):
        first, last = _seq_pos(i * ATT_SUB + s, *seq)
        variant = jnp.where(first, 1, jnp.where(last, 2, 0))
        q = q_ref[s * BLOCK:(s + 1) * BLOCK, :]
        q_all = jnp.concatenate([q[:, t * 128:(t + 1) * 128] for t in range(ATT_GROUP)], axis=0)
        vt = vt_ref[:, s * BLOCK:(s + 3) * BLOCK]
        scores = [_dot_nt(k_ref[s * BLOCK:(s + 3) * BLOCK, :], q_all) for k_ref in (klo_ref, khi_ref)]
        yield
        ms, ps = [], []
        for kv in range(ATT_KV_HEADS):
            logits = scores[kv] + bias_ref[variant, kv]
            ms.append(jnp.maximum(jnp.max(logits, axis=0, keepdims=True), sink_ref[kv]))
            ps.append(jnp.exp(logits - ms[kv]).astype(BF16))
        yield
        ovs = [_dot(vt, p) for p in ps]
        yield
        outs = []
        for kv in range(ATT_KV_HEADS):
            den = ovs[kv][2 * ATT_HEAD_DIM:2 * ATT_HEAD_DIM + 1, :] + jnp.exp(sink_ref[kv] - ms[kv])
            outs.append(ovs[kv][0:2 * ATT_HEAD_DIM, :] * (1.0 / den))
        both = jnp.where(feat < ATT_HEAD_DIM, outs[0], outs[1]).astype(BF16)
        for t in range(ATT_GROUP):
            o_ref[t * 128:(t + 1) * 128, s * BLOCK:(s + 1) * BLOCK] = both[:, t * BLOCK:(t + 1) * BLOCK]

    _trace_stagewise([block(s) for s in range(ATT_SUB)])


def _attention(z, bias, sink, seq):
    t = z.shape[0]
    n = t // ATT_TILE
    nblk = t // BLOCK
    band = ATT_TILE + 2 * BLOCK
    prev = lambda i: jnp.maximum(i * ATT_SUB - 1, 0)
    nxt = lambda i: jnp.minimum((i + 1) * ATT_SUB, nblk - 1)
    return pl.pallas_call(
        functools.partial(_attention_kernel, seq),
        grid=(n,),
        in_specs=[
            pl.BlockSpec((ATT_TILE, ATT_WIDTH), lambda i: (i, ZB_ATTQ)),
            pl.BlockSpec((BLOCK, 128), lambda i: (prev(i), ZB_ATTK)),
            pl.BlockSpec((ATT_TILE, 128), lambda i: (i, ZB_ATTK)),
            pl.BlockSpec((BLOCK, 128), lambda i: (nxt(i), ZB_ATTK)),
            pl.BlockSpec((BLOCK, 128), lambda i: (prev(i), ZB_ATTV)),
            pl.BlockSpec((ATT_TILE, 128), lambda i: (i, ZB_ATTV)),
            pl.BlockSpec((BLOCK, 128), lambda i: (nxt(i), ZB_ATTV)),
            pl.BlockSpec((3, ATT_KV_HEADS, 3 * BLOCK, ATT_GROUP * BLOCK), lambda i: (0, 0, 0, 0)),
            pl.BlockSpec((ATT_KV_HEADS, 1, ATT_GROUP * BLOCK), lambda i: (0, 0, 0)),
        ],
        out_specs=pl.BlockSpec((ATT_WIDTH, ATT_TILE), lambda i: (0, i)),
        out_shape=jax.ShapeDtypeStruct((ATT_WIDTH, t), BF16),
        scratch_shapes=[pltpu.VMEM((band, 128), BF16), pltpu.VMEM((band, 128), BF16),
                        pltpu.VMEM((2 * ATT_HEAD_DIM + ONES_ROWS, band), BF16)],
        compiler_params=_cparams(("arbitrary",)),
        name="window_attention",
    )(z, z, z, z, z, z, z, bias, sink)


def _t5_bucket(rel):
    half = N_BUCKETS // 2
    exact = half // 2
    n = np.abs(rel)
    large = exact + (np.log(np.maximum(n, 1) / exact) / np.log(MAX_DISTANCE / exact) * (half - exact)).astype(np.int32)
    large = np.minimum(large, half - 1)
    return ((rel > 0).astype(np.int32) * half + np.where(n < exact, n, large)).astype(np.int32)


def _attention_tables(rel_bias, attn_sink):
    rel = np.arange(3 * BLOCK)[:, None] - BLOCK - np.arange(BLOCK)[None, :]
    onehot = jnp.asarray(_t5_bucket(rel)[..., None] == np.arange(N_BUCKETS), F32)
    bias = jnp.einsum('kqb,bh->hkq', onehot, rel_bias.astype(F32), precision=lax.Precision.HIGHEST)
    bias = jnp.where(jnp.asarray(np.abs(rel) <= WINDOW)[None], bias, NEG)
    bias = bias.reshape(ATT_KV_HEADS, ATT_GROUP, 3 * BLOCK, BLOCK).transpose(0, 2, 1, 3)
    bias = bias.reshape(ATT_KV_HEADS, 3 * BLOCK, ATT_GROUP * BLOCK)
    key = np.arange(3 * BLOCK)[None, :, None]
    first = jnp.where(jnp.asarray(key < BLOCK), NEG, bias)
    last = jnp.where(jnp.asarray(key >= 2 * BLOCK), NEG, bias)
    sink = jnp.repeat(attn_sink.astype(F32), BLOCK).reshape(ATT_KV_HEADS, 1, ATT_GROUP * BLOCK)
    return jnp.stack([bias, first, last]), sink


def _mlstm_kernel(seq, n_chunks, *refs):
    fwd_in, bwd_in, (of_ref, ob_ref), state = refs[0:5], refs[5:10], refs[10:12], refs[12:]
    _trace_stagewise(_mlstm_chains(seq, False, n_chunks, *fwd_in, of_ref, *state[0:2])
                     + _mlstm_chains(seq, True, n_chunks, *bwd_in, ob_ref, *state[2:4]))


def _mlstm_chains(seq, reverse, n_chunks, q_ref, k_ref, v_ref, gr_ref, gc_ref, o_ref, ct_ref, m_ref):
    step = pl.program_id(0)
    chunk = (n_chunks - 1 - step) if reverse else step
    first, last = _seq_pos(chunk, *seq)
    fresh = last if reverse else first

    L = ML_CHUNK
    row = lax.broadcasted_iota(jnp.int32, (L, L), 0)
    col = lax.broadcasted_iota(jnp.int32, (L, L), 1)
    vis = (row >= col) if reverse else (row <= col)
    vis_t = (col >= row) if reverse else (col <= row)
    gr = gr_ref[...]
    gc = gc_ref[...]
    r3 = _dot(jnp.concatenate(_split3(_log_sigmoid(gr)), axis=0), vis.astype(BF16))
    b_rows = r3[0:N_GATES] + r3[N_GATES:2 * N_GATES] + r3[2 * N_GATES:]
    vis_tb = vis_t.astype(BF16)
    c_hi, c_mid, c_lo = _split3(_log_sigmoid(gc))
    b_cols = _dot(vis_tb, c_hi) + _dot(vis_tb, c_mid) + _dot(vis_tb, c_lo)
    i_off = 2 * ML_HEADS if reverse else 0
    f_off = i_off + ML_HEADS
    end = 0 if reverse else L - 1
    ones = jnp.ones((ONES_ROWS, L), BF16)

    def chain(h):
        hs = slice(h * ML_HEAD_DIM, (h + 1) * ML_HEAD_DIM)
        q = q_ref[:, hs]
        k = k_ref[:, hs]
        vt1 = jnp.concatenate([v_ref[:, hs].T, ones], axis=0)
        b_row = b_rows[f_off + h:f_off + h + 1, :]
        u_row = gr[i_off + h:i_off + h + 1, :] - b_row
        u_col = gc[:, i_off + h:i_off + h + 1] - b_cols[:, f_off + h:f_off + h + 1]
        g = b_row[:, end:end + 1]
        ct_old = jnp.where(fresh, 0.0, ct_ref[h])
        m_old = jnp.where(fresh, 0.0, m_ref[h])[:, 0:1]
        kq = _dot_nt(k, q)
        from_state = _dot_nt(ct_old.astype(BF16), q)
        yield
        u_mat = jnp.where(vis, u_col, NEG)
        mm = jnp.maximum(jnp.max(u_mat, axis=0, keepdims=True), m_old)
        st = (kq * jnp.exp(u_mat - mm)).astype(BF16)
        yield
        tot = _dot(vt1, st) + jnp.exp(m_old - mm) * from_state
        a_max = jnp.max(g + u_row, axis=-1, keepdims=True)
        m_new = jnp.maximum(g + m_old, a_max)
        s_old = jnp.exp(g + m_old - m_new)
        weighted = (vt1.astype(F32) * jnp.exp(g + u_row - m_new)).astype(BF16)
        new_state = _dot(weighted, k)
        yield
        den = tot[ML_HEAD_DIM:ML_HEAD_DIM + 1, :]
        floor = jnp.exp(-(b_row + mm))
        o_ref[hs, :] = (tot[0:ML_HEAD_DIM, :] * (1.0 / jnp.maximum(jnp.abs(den), floor))).astype(BF16)
        ct_ref[h] = s_old * ct_old + new_state
        m_ref[h] = jnp.broadcast_to(m_new, (1, ML_HEAD_DIM))

    return [chain(h) for h in range(ML_HEADS)]


def _mlstm(qk, z, gates_r, gates_c, seq):
    t = qk.shape[0]
    nc = t // ML_CHUNK

    def chunk_specs(ch):
        return [
            pl.BlockSpec((ML_CHUNK, ML_WIDTH), lambda i: (ch(i), 0)),
            pl.BlockSpec((ML_CHUNK, ML_WIDTH), lambda i: (ch(i), 1)),
            pl.BlockSpec((ML_CHUNK, ML_WIDTH), lambda i: (ch(i), ZB_MLV)),
            pl.BlockSpec((N_GATES, ML_CHUNK), lambda i: (0, ch(i))),
            pl.BlockSpec((ML_CHUNK, N_GATES), lambda i: (ch(i), 0)),
        ]

    fwd = lambda i: i
    bwd = lambda i: nc - 1 - i
    state = [
        pltpu.VMEM((ML_HEADS, ML_HEAD_DIM + ONES_ROWS, ML_HEAD_DIM), F32),
        pltpu.VMEM((ML_HEADS, 1, ML_HEAD_DIM), F32),
    ]
    operands = (qk, qk, z, gates_r, gates_c)
    return pl.pallas_call(
        functools.partial(_mlstm_kernel, seq, nc),
        grid=(nc,),
        in_specs=chunk_specs(fwd) + chunk_specs(bwd),
        out_specs=[pl.BlockSpec((ML_WIDTH, ML_CHUNK), lambda i: (0, fwd(i))),
                   pl.BlockSpec((ML_WIDTH, ML_CHUNK), lambda i: (0, bwd(i)))],
        out_shape=[jax.ShapeDtypeStruct((ML_WIDTH, t), BF16), jax.ShapeDtypeStruct((ML_WIDTH, t), BF16)],
        scratch_shapes=state + state,
        compiler_params=_cparams(("arbitrary",)),
        name="mlstm",
    )(*operands, *operands)


def _mix_out(n_p_tiles, toks, xp_ref, xs_ref, att_ref, hf_ref, hb_ref, og_ref, wa_ref, wm_ref, gain_ref, g_ref,
             b_ref):
    i = pl.program_id(0)
    x = jnp.where(i < n_p_tiles, xp_ref[toks, :], xs_ref[toks, :])
    h = hf_ref[:, toks].astype(F32) + hb_ref[:, toks].astype(F32)
    parts = []
    for hd in range(ML_HEADS):
        hh = h[hd * ML_HEAD_DIM:(hd + 1) * ML_HEAD_DIM, :]
        mu = jnp.mean(hh, axis=0, keepdims=True)
        hc = hh - mu
        var = jnp.mean(hc * hc, axis=0, keepdims=True)
        parts.append(hc * lax.rsqrt(var + LN_EPS))
    hn = jnp.concatenate(parts, axis=0) * gain_ref[...] * _sigmoid(og_ref[toks, :].T.astype(F32))
    mixed = _dot(att_ref[:, toks].T, wa_ref[...]) + _dot(hn.astype(BF16).T, wm_ref[...])
    return _layer_norm(DN_ALPHA * x + mixed, g_ref[...], b_ref[...])


def _mem_kv_kernel(m_ref, w_ref, o_ref):
    o_ref[...] = _dot(m_ref[...].astype(BF16), w_ref[...]).astype(BF16)


def _mem_kv(mem, wkv):
    rows = mem.shape[0]
    return pl.pallas_call(
        _mem_kv_kernel,
        grid=(rows // MEM_TOKENS,),
        in_specs=[pl.BlockSpec((MEM_TOKENS, D_MODEL), lambda i: (i, 0)),
                  pl.BlockSpec((D_MODEL, 2 * D_MODEL), lambda i: (0, 0))],
        out_specs=pl.BlockSpec((MEM_TOKENS, 2 * D_MODEL), lambda i: (i, 0)),
        out_shape=jax.ShapeDtypeStruct((rows, 2 * D_MODEL), BF16),
        compiler_params=_cparams(("arbitrary",)),
        name="mem_kv",
    )(mem, wkv)


def _pack_bf16_pairs(a, b, exact=False):
    if not exact:
        a, b = a.astype(BF16).astype(F32), b.astype(BF16).astype(F32)
    ua, ub = pltpu.bitcast(a, jnp.uint32), pltpu.bitcast(b, jnp.uint32)
    return (ua >> 16) | (ub & jnp.uint32(0xFFFF0000))


def _unpack_bf16_pairs(u):
    lo = pltpu.bitcast(u << 16, F32)
    hi = pltpu.bitcast(u & jnp.uint32(0xFFFF0000), F32)
    return lo, hi


TOKEN_GROUPS = 2


def _cross_router_kernel(n_p_tiles, *refs):
    mix_refs, (kv_ref, wq_ref, wo_ref, g_ref, b_ref, wr_ref, br_ref, tri_ref, etri_ref,
               x2_ref, x2b_ref, gate_ref, slot_ref, chunk_ref) = refs[:11], refs[11:]
    head_cols = [slice(h * MEM_HEAD_DIM, (h + 1) * MEM_HEAD_DIM) for h in range(MEM_HEADS)]

    def token_group(r):
        toks = slice(r * ROW_TILE // TOKEN_GROUPS, (r + 1) * ROW_TILE // TOKEN_GROUPS)
        x = _mix_out(n_p_tiles, toks, *mix_refs)
        yield
        q = _dot(x.astype(BF16), wq_ref[...]).astype(BF16)
        scores = [_dot_nt(q[:, hs], kv_ref[:, hs]) for hs in head_cols]
        yield
        probs, dens = [], []
        for logits in scores:
            p = jnp.exp(logits - jnp.max(logits, axis=-1, keepdims=True))
            dens.append(jnp.sum(p, axis=-1, keepdims=True))
            probs.append(p.astype(BF16))
        yield
        values = [_dot(p, kv_ref[:, D_MODEL + hs.start:D_MODEL + hs.stop]) for p, hs in zip(probs, head_cols)]
        o = jnp.concatenate([(v * (1.0 / den)).astype(BF16) for v, den in zip(values, dens)], axis=1)
        y = _dot(o, wo_ref[...])
        yield
        x2 = _layer_norm(DN_ALPHA * x + y, g_ref[...], b_ref[...])
        x2_ref[toks, :] = x2
        x2b_ref[toks, :] = x2.astype(BF16)

    _trace_stagewise([token_group(r) for r in range(TOKEN_GROUPS)])
    x2b = x2b_ref[...]

    logits = _dot_nt(wr_ref[...], x2b) + br_ref[...]
    expert = lax.broadcasted_iota(jnp.int32, logits.shape, 0)
    work = logits
    vals, sels = [], []
    for k in range(TOP_K):
        mx = jnp.max(work, axis=0, keepdims=True)
        ix = jnp.min(jnp.where(work == mx, expert, N_EXPERTS), axis=0, keepdims=True)
        sel = expert == ix
        work = jnp.where(sel, -jnp.inf, work)
        vals.append(mx)
        sels.append(sel)
    es = [jnp.exp(v - vals[0]) for v in vals]
    tot = es[0] + es[1] + es[2] + es[3]
    chosen = jnp.zeros(logits.shape, F32)
    for k in range(TOP_K):
        gate_ref[k:k + 1, :] = es[k] / tot
        chosen = chosen + sels[k].astype(F32)
    count = jnp.sum(chosen, axis=1, keepdims=True)
    chunks = jnp.floor((count + (SEG_ALIGN - 1)) * (1.0 / SEG_ALIGN))
    chunks_b = jnp.broadcast_to(chunks, (N_EXPERTS, 128))
    seg_first = _dot(etri_ref[...], chunks_b.astype(BF16))[:, 0:1] * SEG_ALIGN
    before = _dot(chosen.astype(BF16), tri_ref[...])
    local_row = before + seg_first
    for k in range(TOP_K):
        slot_ref[k:k + 1, :] = jnp.sum(jnp.where(sels[k], local_row, 0.0), axis=0, keepdims=True).astype(jnp.int32)
    chunk_ref[0] = chunks_b.astype(jnp.int32)


def _cross_router(xp, xs, att, hf, hb, z, w_att, w_ml, gain, g1, b1, kv, wq, wo, g, b, w_router, b_router,
                  mem_of_tile):
    t = att.shape[1]
    n_p = xp.shape[0] // ROW_TILE
    n = t // ROW_TILE
    const = lambda i: (0, 0)
    tile = lambda i: (i, 0)
    lanes = lambda i: (0, i)
    tri = jnp.asarray(np.triu(np.ones((ROW_TILE, ROW_TILE), np.float32), 1), BF16)
    etri = jnp.asarray(np.tril(np.ones((N_EXPERTS, N_EXPERTS), np.float32), -1), BF16)
    return pl.pallas_call(
        functools.partial(_cross_router_kernel, n_p),
        grid=(n,),
        in_specs=[
            pl.BlockSpec((ROW_TILE, D_MODEL), lambda i: (jnp.minimum(i, n_p - 1), 0)),
            pl.BlockSpec((ROW_TILE, D_MODEL), lambda i: (jnp.maximum(i - n_p, 0), 0)),
            pl.BlockSpec((ATT_WIDTH, ROW_TILE), lanes),
            pl.BlockSpec((ML_WIDTH, ROW_TILE), lanes),
            pl.BlockSpec((ML_WIDTH, ROW_TILE), lanes),
            pl.BlockSpec((ROW_TILE, ML_WIDTH), lambda i: (i, ZB_MLO)),
            pl.BlockSpec((ATT_WIDTH, D_MODEL), const),
            pl.BlockSpec((ML_WIDTH, D_MODEL), const),
            pl.BlockSpec((ML_WIDTH, 1), const),
            pl.BlockSpec((1, D_MODEL), const),
            pl.BlockSpec((1, D_MODEL), const),
            pl.BlockSpec((MEM_TOKENS, 2 * D_MODEL), lambda i: (mem_of_tile(i), 0)),
            pl.BlockSpec((D_MODEL, D_MODEL), const),
            pl.BlockSpec((D_MODEL, D_MODEL), const),
            pl.BlockSpec((1, D_MODEL), const),
            pl.BlockSpec((1, D_MODEL), const),
            pl.BlockSpec((N_EXPERTS, D_MODEL), const),
            pl.BlockSpec((N_EXPERTS, 1), const),
            pl.BlockSpec((ROW_TILE, ROW_TILE), const),
            pl.BlockSpec((N_EXPERTS, N_EXPERTS), const),
        ],
        out_specs=[
            pl.BlockSpec((ROW_TILE, D_MODEL), tile),
            pl.BlockSpec((ROW_TILE, D_MODEL), tile),
            pl.BlockSpec((TOP_K, ROW_TILE), lanes),
            pl.BlockSpec((TOP_K, ROW_TILE), lanes),
            pl.BlockSpec((1, N_EXPERTS, 128), lambda i: (i, 0, 0)),
        ],
        out_shape=[
            jax.ShapeDtypeStruct((t, D_MODEL), F32),
            jax.ShapeDtypeStruct((t, D_MODEL), BF16),
            jax.ShapeDtypeStruct((TOP_K, t), F32),
            jax.ShapeDtypeStruct((TOP_K, t), jnp.int32),
            jax.ShapeDtypeStruct((n, N_EXPERTS, 128), jnp.int32),
        ],
        compiler_params=_cparams(("arbitrary",)),
        name="mix_out_cross_attn_router",
    )(xp, xs, att, hf, hb, z, w_att, w_ml, gain.reshape(ML_WIDTH, 1), g1.reshape(1, D_MODEL), b1.reshape(1, D_MODEL),
      kv, wq, wo, g.reshape(1, D_MODEL), b.reshape(1, D_MODEL), w_router.T.astype(BF16),
      b_router.reshape(N_EXPERTS, 1), tri, etri)


SEG_ALIGN = 8
LOCAL_ROWS = ROW_TILE * TOP_K + N_EXPERTS * SEG_ALIGN
HALF = D_MODEL // 2


def _rows(chunks):
    return pl.multiple_of(chunks * SEG_ALIGN, SEG_ALIGN)


def _segment_copies(chunks_ref, seg_ref, loc_ref, tile, make_copy):
    def per_expert(e, carry):
        entry = tile * N_EXPERTS + e

        @pl.when(chunks_ref[entry] > 0)
        def _():
            make_copy(_rows(loc_ref[entry]), _rows(seg_ref[entry]), _rows(chunks_ref[entry])).start()

        return carry

    lax.fori_loop(0, N_EXPERTS, per_expert, 0)


def _wait_copies(chunks, make_copy):
    @pl.when(chunks > 0)
    def _():
        make_copy(0, 0, _rows(chunks)).wait()


def _dispatch_kernel(chunks_ref, seg_ref, loc_ref, total_ref, tail_ref, tailn_ref, nact_ref,
                     x_ref, slot_ref, rows_hbm, local_ref, zero_ref, sems):
    i = pl.program_id(0)
    n = pl.num_programs(0)
    buf = i % 2

    def copy_out(b):
        return lambda loc, seg, rows: pltpu.make_async_copy(
            local_ref.at[b, pl.ds(loc, rows), :], rows_hbm.at[pl.ds(seg, rows), :], sems.at[b])

    def run(b):
        @pl.when(i >= 2)
        def _():
            _wait_copies(total_ref[jnp.maximum(i - 2, 0)], copy_out(b))

        slots = slot_ref[...].astype(F32)
        row = lax.broadcasted_iota(jnp.int32, (LOCAL_ROWS, ROW_TILE), 0).astype(F32)
        miss = (slots[0:1, :] - row) * (slots[1:2, :] - row)
        for k in range(2, TOP_K):
            miss = miss * (slots[k:k + 1, :] - row)
        perm = jnp.where(miss == 0.0, 1.0, 0.0).astype(BF16)
        picked = _dot(perm, x_ref[...])
        local_ref[b] = _pack_bf16_pairs(picked[:, :HALF], picked[:, HALF:], exact=True)
        _segment_copies(chunks_ref, seg_ref, loc_ref, i, copy_out(b))

    for b in range(2):
        pl.when(buf == b)(functools.partial(run, b))

    @pl.when(i == n - 1)
    def _():
        for b in range(2):
            step = jnp.where(buf == b, i, i - 1)
            _wait_copies(total_ref[step], copy_out(b))
        zero_ref[...] = jnp.zeros_like(zero_ref)
        fill = lambda _, seg, rows: pltpu.make_async_copy(
            zero_ref.at[pl.ds(0, rows), :], rows_hbm.at[pl.ds(seg, rows), :], sems.at[2])

        def per_expert(e, carry):
            @pl.when(tailn_ref[e] > 0)
            def _():
                fill(0, _rows(tail_ref[e]), _rows(tailn_ref[e])).start()

            _wait_copies(tailn_ref[e], fill)
            return carry

        lax.fori_loop(0, N_EXPERTS, per_expert, 0)
        fill_block = lambda blk: pltpu.make_async_copy(
            zero_ref, rows_hbm.at[pl.ds(pl.multiple_of(blk * MOE_ROWS, MOE_ROWS), MOE_ROWS), :], sems.at[2])
        n_blocks = rows_hbm.shape[0] // MOE_ROWS

        def start_block(blk, carry):
            fill_block(blk).start()
            return carry

        def wait_block(blk, carry):
            fill_block(0).wait()
            return carry

        lax.fori_loop(nact_ref[0], n_blocks, start_block, 0)
        lax.fori_loop(nact_ref[0], n_blocks, wait_block, 0)


def _dispatch(x2b, slots, tables, n_rows):
    n = x2b.shape[0] // ROW_TILE
    assert n >= 2
    return pl.pallas_call(
        _dispatch_kernel,
        grid_spec=pltpu.PrefetchScalarGridSpec(
            num_scalar_prefetch=7,
            grid=(n,),
            in_specs=[
                pl.BlockSpec((ROW_TILE, D_MODEL), lambda i, *_: (i, 0)),
                pl.BlockSpec((TOP_K, ROW_TILE), lambda i, *_: (0, i)),
            ],
            out_specs=pl.BlockSpec(memory_space=pl.ANY),
            scratch_shapes=[
                pltpu.VMEM((2, LOCAL_ROWS, HALF), jnp.uint32),
                pltpu.VMEM((MOE_ROWS, HALF), jnp.uint32),
                pltpu.SemaphoreType.DMA((3,)),
            ],
        ),
        out_shape=jax.ShapeDtypeStruct((n_rows, HALF), jnp.uint32),
        compiler_params=_cparams(("arbitrary",)),
        name="moe_dispatch",
    )(*tables, x2b, slots)


def _expert_kernel(be_ref, na_ref, half_ref, x_ref, wu_ref, bu_ref, wd_ref, bd_ref, o_ref, wub_ref, wdb_ref):
    i = pl.program_id(0)

    @pl.when((i == 0) | (be_ref[i] != be_ref[jnp.maximum(i - 1, 0)]))
    def _():
        wub_ref[...] = wu_ref[0].astype(BF16)
        wdb_ref[...] = wd_ref[0].astype(BF16)

    def mlp(rows):
        lo, hi = _unpack_bf16_pairs(x_ref[rows, :])
        x = jnp.concatenate([lo, hi], axis=1).astype(BF16)
        hu = _dot(x, wub_ref[...]) + bu_ref[0]
        h_glu = jnp.minimum(hu[:, :D_FF], SWIGLU_LIMIT)
        h_lin = jnp.clip(hu[:, D_FF:], -SWIGLU_LIMIT, SWIGLU_LIMIT)
        hh = h_glu * _sigmoid(SWIGLU_ALPHA * h_glu) * (h_lin + 1.0)
        y = _dot(hh.astype(BF16), wdb_ref[...]) + bd_ref[0]
        o_ref[rows, :] = _pack_bf16_pairs(y[:, :D_MODEL // 2], y[:, D_MODEL // 2:])

    active = i < na_ref[0]
    half_full = half_ref[i] == 1
    top, bottom = slice(0, MOE_ROWS // 2), slice(MOE_ROWS // 2, MOE_ROWS)

    @pl.when(active & jnp.logical_not(half_full))
    def _():
        mlp(slice(0, MOE_ROWS))

    @pl.when(active & half_full)
    def _():
        mlp(top)
        o_ref[bottom, :] = jnp.zeros((MOE_ROWS // 2, D_MODEL // 2), jnp.uint32)

    @pl.when(jnp.logical_not(active))
    def _():
        o_ref[...] = jnp.zeros_like(o_ref)


def _experts(xs, block_e, n_active, half_full, w_up, b_up, w_down, b_down):
    n_rows = xs.shape[0]
    n_blk = n_rows // MOE_ROWS
    return pl.pallas_call(
        _expert_kernel,
        grid_spec=pltpu.PrefetchScalarGridSpec(
            num_scalar_prefetch=3,
            grid=(n_blk,),
            in_specs=[
                pl.BlockSpec((MOE_ROWS, D_MODEL // 2), lambda i, be, na, hf: (jnp.minimum(i, na[0] - 1), 0)),
                pl.BlockSpec((1, D_MODEL, 2 * D_FF), lambda i, be, na, hf: (be[i], 0, 0)),
                pl.BlockSpec((1, 1, 2 * D_FF), lambda i, be, na, hf: (be[i], 0, 0)),
                pl.BlockSpec((1, D_FF, D_MODEL), lambda i, be, na, hf: (be[i], 0, 0)),
                pl.BlockSpec((1, 1, D_MODEL), lambda i, be, na, hf: (be[i], 0, 0)),
            ],
            out_specs=pl.BlockSpec((MOE_ROWS, D_MODEL // 2), lambda i, be, na, hf: (i, 0)),
            scratch_shapes=[pltpu.VMEM((D_MODEL, 2 * D_FF), BF16), pltpu.VMEM((D_FF, D_MODEL), BF16)],
        ),
        out_shape=jax.ShapeDtypeStruct((n_rows, D_MODEL // 2), jnp.uint32),
        compiler_params=_cparams(("arbitrary",)),
        name="experts",
    )(block_e, n_active, half_full, xs, w_up, b_up.reshape(N_EXPERTS, 1, 2 * D_FF), w_down,
      b_down.reshape(N_EXPERTS, 1, D_MODEL))


def _combine_kernel(first_tile, chunks_ref, seg_ref, loc_ref, total_ref,
                    x_ref, rows_hbm, slot_ref, gate_ref, g_ref, b_ref, o_ref, local_ref, sems):
    i = pl.program_id(0)
    n = pl.num_programs(0)
    tile = first_tile + i
    buf = i % 2

    def copy_in(b):
        return lambda loc, seg, rows: pltpu.make_async_copy(
            rows_hbm.at[pl.ds(seg, rows), :], local_ref.at[b, pl.ds(loc, rows), :], sems.at[b])

    @pl.when(i == 0)
    def _():
        local_ref[...] = jnp.zeros_like(local_ref)
        _segment_copies(chunks_ref, seg_ref, loc_ref, tile, copy_in(0))

    def run(b):
        @pl.when(i + 1 < n)
        def _():
            _segment_copies(chunks_ref, seg_ref, loc_ref, tile + 1, copy_in(1 - b))

        _wait_copies(total_ref[tile], copy_in(b))
        lo, hi = _unpack_bf16_pairs(local_ref[b])
        rows = jnp.concatenate([lo, hi], axis=1).astype(BF16)
        slots = slot_ref[...]
        gates = gate_ref[...]
        row = lax.broadcasted_iota(jnp.int32, (ROW_TILE, LOCAL_ROWS), 1)
        weight = jnp.zeros((ROW_TILE, LOCAL_ROWS), F32)
        for k in range(TOP_K):
            weight = jnp.where(slots[:, k:k + 1] == row, gates[:, k:k + 1], weight)
        y = _dot(weight.astype(BF16), rows)
        o_ref[...] = _layer_norm(DN_ALPHA * x_ref[...] + y, g_ref[...], b_ref[...])

    for b in range(2):
        pl.when(buf == b)(functools.partial(run, b))


def _combine(x2, rows_out, slots_c, gates_c, tables, g, b, first_tile, n_tiles):
    const = lambda i, *_: (0, 0)
    tile = lambda i, *_: (first_tile + i, 0)
    return pl.pallas_call(
        functools.partial(_combine_kernel, first_tile),
        grid_spec=pltpu.PrefetchScalarGridSpec(
            num_scalar_prefetch=4,
            grid=(n_tiles,),
            in_specs=[
                pl.BlockSpec((ROW_TILE, D_MODEL), tile),
                pl.BlockSpec(memory_space=pl.ANY),
                pl.BlockSpec((ROW_TILE, TOP_K), tile),
                pl.BlockSpec((ROW_TILE, TOP_K), tile),
                pl.BlockSpec((1, D_MODEL), const),
                pl.BlockSpec((1, D_MODEL), const),
            ],
            out_specs=pl.BlockSpec((ROW_TILE, D_MODEL), lambda i, *_: (i, 0)),
            scratch_shapes=[pltpu.VMEM((2, LOCAL_ROWS, HALF), jnp.uint32), pltpu.SemaphoreType.DMA((2,))],
        ),
        out_shape=jax.ShapeDtypeStruct((n_tiles * ROW_TILE, D_MODEL), F32),
        compiler_params=_cparams(("arbitrary",)),
        name="combine_ln3",
    )(*tables[:4], x2, rows_out, slots_c, gates_c, g.reshape(1, D_MODEL), b.reshape(1, D_MODEL))


def kernel(x_prompt, x_sample, mem_prompt, mem_sample, rel_bias, w_in, b_gates, conv_w, conv_b, mh_gain, attn_sink,
           w_out, ln1_g, ln1_b, wq_mem, wkv_mem, wo_mem, ln2_g, ln2_b, w_router, b_router, w_up, b_up, w_down,
           b_down, ln3_g, ln3_b):
    assert w_in.shape[0] == 1, "single layer"
    bp, sp, _ = x_prompt.shape
    bs, ss, _ = x_sample.shape
    tp, ts = bp * sp, bs * ss
    t = tp + ts
    assert sp % ATT_TILE == 0 and ss % ATT_TILE == 0 and ATT_TILE % ROW_TILE == 0 and ATT_TILE % ML_CHUNK == 0
    xp = x_prompt.reshape(tp, D_MODEL)
    xs = x_sample.reshape(ts, D_MODEL)

    def seq_blocks(rows):
        return (tp // rows, sp // rows, ss // rows)

    w = w_in[0]
    q_end, k_end, v_end = ATT_WIDTH, ATT_WIDTH + 128, ATT_WIDTH + 256
    qk_end, mv_end, mo_end = v_end + 2 * ML_WIDTH, v_end + 3 * ML_WIDTH, v_end + 4 * ML_WIDTH
    head_order = np.concatenate([[h, ATT_GROUP + h] for h in range(ATT_GROUP)])
    att_perm = (head_order[:, None] * ATT_HEAD_DIM + np.arange(ATT_HEAD_DIM)[None, :]).reshape(-1)
    w_qk = w[:, v_end:qk_end].astype(BF16)
    w_main = jnp.concatenate([
        w[:, qk_end:mv_end], w[:, mv_end:mo_end],
        w[:, :q_end][:, att_perm] * (ATT_HEAD_DIM ** -0.5), w[:, q_end:k_end], w[:, k_end:v_end]], axis=1).astype(BF16)
    w_g = w[:, mo_end:].astype(BF16)
    w_att = w_out[0][:ATT_WIDTH][att_perm].astype(BF16)
    w_ml = w_out[0][ATT_WIDTH:].astype(BF16)

    qk, z, gates_c, gates_r = _in_proj(xp, xs, w_qk, w_main, w_g, b_gates[0], conv_w[0], conv_b[0],
                                       seq_blocks(ROW_TILE))

    bias, sink = _attention_tables(rel_bias, attn_sink[0])
    att = _attention(z, bias, sink, seq_blocks(BLOCK))

    h_f, h_b = _mlstm(qk, z, gates_r, gates_c, seq_blocks(ML_CHUNK))

    mem = jnp.concatenate([mem_prompt.reshape(bp * MEM_TOKENS, D_MODEL), mem_sample.reshape(bs * MEM_TOKENS, D_MODEL)])
    kv = _mem_kv(mem, wkv_mem[0].astype(BF16))
    n_p_tiles, p_tiles_per_seq, s_tiles_per_seq = seq_blocks(ROW_TILE)
    mem_of_tile = lambda i: jnp.where(i < n_p_tiles, i // p_tiles_per_seq, bp + (i - n_p_tiles) // s_tiles_per_seq)
    wq = (wq_mem[0] * (MEM_HEAD_DIM ** -0.5)).astype(BF16)
    x2, x2b, top_gate, slots, tile_chunks = _cross_router(
        xp, xs, att, h_f, h_b, z, w_att, w_ml, mh_gain[0], ln1_g[0], ln1_b[0],
        kv, wq, wo_mem[0].astype(BF16), ln2_g[0], ln2_b[0], w_router[0], b_router[0], mem_of_tile)

    n_tiles = t // ROW_TILE
    blk_chunks = MOE_ROWS // SEG_ALIGN
    chunks = tile_chunks[:, :, 0]
    used = jnp.sum(chunks, axis=0)
    region = ((used + blk_chunks - 1) // blk_chunks) * blk_chunks
    region_end = jnp.cumsum(region)
    region_start = region_end - region
    seg_start = region_start[None, :] + jnp.cumsum(chunks, axis=0) - chunks
    loc_start = jnp.cumsum(chunks, axis=1) - chunks
    n_blk = -(-(t * TOP_K + n_tiles * N_EXPERTS * (SEG_ALIGN - 1) + N_EXPERTS * (MOE_ROWS - 1)) // MOE_ROWS)
    blk_first = jnp.arange(n_blk, dtype=jnp.int32) * blk_chunks
    block_e = jnp.minimum(jnp.sum(blk_first[:, None] >= region_end[None, :], axis=1), N_EXPERTS - 1).astype(jnp.int32)
    n_active = (region_end[-1:] // blk_chunks).astype(jnp.int32)
    owner = block_e[:, None] == jnp.arange(N_EXPERTS, dtype=jnp.int32)[None, :]
    used_end = jnp.sum(jnp.where(owner, (region_start + used)[None, :], 0), axis=1)
    half_full = (used_end - blk_first <= blk_chunks // 2).astype(jnp.int32)
    tables = (chunks.reshape(-1), seg_start.reshape(-1), loc_start.reshape(-1), jnp.sum(chunks, axis=1),
              region_start + used, region - used, n_active)
    tables = tuple(tb.astype(jnp.int32) for tb in tables)

    rows_in = _dispatch(x2b, slots, tables, n_blk * MOE_ROWS)
    rows_out = _experts(rows_in, block_e, n_active, half_full, w_up[0], b_up[0], w_down[0], b_down[0])

    slots_c, gates_c = slots.T, top_gate.T
    y_p = _combine(x2, rows_out, slots_c, gates_c, tables, ln3_g[0], ln3_b[0], 0, n_p_tiles)
    y_s = _combine(x2, rows_out, slots_c, gates_c, tables, ln3_g[0], ln3_b[0], n_p_tiles, n_tiles - n_p_tiles)
    return y_p.reshape(bp, sp, D_MODEL), y_s.reshape(bs, ss, D_MODEL)
```

```python
import functools

import numpy as np
import jax
import jax.numpy as jnp
from jax import lax
from jax.experimental import pallas as pl
from jax.experimental.pallas import tpu as pltpu

F32 = jnp.float32
BF16 = jnp.bfloat16

D_MODEL = 1024
ATT_HEADS = 8
ATT_KV_HEADS = 2
ATT_GROUP = ATT_HEADS // ATT_KV_HEADS
ATT_HEAD_DIM = 64
ATT_WIDTH = ATT_HEADS * ATT_HEAD_DIM
WINDOW = 128
BLOCK = WINDOW
N_BUCKETS = 32
MAX_DISTANCE = 128
ML_HEADS = 4
ML_HEAD_DIM = 128
ML_WIDTH = ML_HEADS * ML_HEAD_DIM
ML_CHUNK = 256
CONV_WIDTH = 5
N_GATES = 4 * ML_HEADS
MEM_TOKENS = 256
MEM_HEADS = 4
MEM_HEAD_DIM = D_MODEL // MEM_HEADS
N_EXPERTS = 32
TOP_K = 4
D_FF = D_MODEL
SWIGLU_LIMIT = 7.0
SWIGLU_ALPHA = 1.702
LN_EPS = 1e-5
DN_ALPHA = 2.0 ** 0.25

QK_WIDTH = 2 * ML_WIDTH
Z_WIDTH = ML_WIDTH + ML_WIDTH + ATT_WIDTH + 2 * ATT_KV_HEADS * ATT_HEAD_DIM
ZB_MLV = 0
ZB_MLO = 1
ZB_ATTQ = 2
ZB_ATTK = 12
ZB_ATTV = 13

ROW_TILE = 512
MOE_ROWS = 1024
NEG = -1e30
VMEM_LIMIT = 56 * 1024 * 1024


def _cparams(sem):
    return pltpu.CompilerParams(dimension_semantics=sem, vmem_limit_bytes=VMEM_LIMIT)


def _dot(a, b):
    return jnp.dot(a, b, preferred_element_type=F32)


def _dot_nt(a, b):
    return lax.dot_general(a, b, (((1,), (1,)), ((), ())), preferred_element_type=F32)


def _split3(x):
    hi = x.astype(BF16)
    rest = x - hi.astype(F32)
    mid = rest.astype(BF16)
    lo = (rest - mid.astype(F32)).astype(BF16)
    return hi, mid, lo


def _layer_norm(y, g, b):
    mu = jnp.mean(y, axis=-1, keepdims=True)
    yc = y - mu
    var = jnp.mean(yc * yc, axis=-1, keepdims=True)
    return yc * lax.rsqrt(var + LN_EPS) * g + b


def _log_sigmoid(x):
    return jnp.minimum(x, 0.0) - jnp.log1p(jnp.exp(-jnp.abs(x)))


def _sigmoid(x):
    return 1.0 / (1.0 + jnp.exp(-x))


def _trace_stagewise(chains):
    while chains:
        chains = [c for c in chains if next(c, "done") != "done"]


def _seq_pos(blk, n_p_blocks, p_blocks_per_seq, s_blocks_per_seq):
    in_p = blk < n_p_blocks
    local = jnp.where(in_p, blk % p_blocks_per_seq, (blk - n_p_blocks) % s_blocks_per_seq)
    per = jnp.where(in_p, p_blocks_per_seq, s_blocks_per_seq)
    return local == 0, local == per - 1


HALO = 8


def _in_proj_kernel(seq, xp_ref, xpl_ref, xpr_ref, xs_ref, xsl_ref, xsr_ref, wqk_ref, w_ref, bg_ref,
                    cw_ref, cb_ref, scale_ref, qk_ref, z_ref, gc_ref, gr_ref, buf_ref):
    i = pl.program_id(0)
    in_p = i < seq[0]
    first, last = _seq_pos(i, *seq)
    x = jnp.where(in_p, xp_ref[...], xs_ref[...])
    left = jnp.where(in_p, xpl_ref[...], xsl_ref[...])
    right = jnp.where(in_p, xpr_ref[...], xsr_ref[...])
    xb = x.astype(BF16)
    x_ext = jnp.concatenate([left, x, right], axis=0).astype(BF16)
    row = lax.broadcasted_iota(jnp.int32, (ROW_TILE + 2 * HALO, 1), 0)
    outside = (first & (row < HALO)) | (last & (row >= ROW_TILE + HALO))
    buf_ref[...] = jnp.where(outside, 0.0, _dot(x_ext, wqk_ref[...]))
    acc = jnp.zeros((ROW_TILE, QK_WIDTH), F32) + cb_ref[...]
    for j in range(CONV_WIDTH):
        off = HALO + j - CONV_WIDTH // 2
        acc = acc + buf_ref[off:off + ROW_TILE, :] * cw_ref[j:j + 1, :]
    qk_ref[...] = (acc * _sigmoid(acc) * scale_ref[...]).astype(BF16)
    zg = _dot(xb, w_ref[...])
    z_ref[...] = zg[:, :Z_WIDTH].astype(BF16)
    gates = zg[:, Z_WIDTH:] + bg_ref[...]
    gc_ref[...] = gates[:, :N_GATES]
    gr_ref[...] = gates.T[:N_GATES, :]


def _in_proj(xp, xs, w_qk, w_main, w_g, b_g, conv_w, conv_b, seq):
    w_main = jnp.pad(jnp.concatenate([w_main, w_g], axis=1), ((0, 0), (0, 128 - N_GATES)))
    b_g = jnp.pad(b_g, (0, 128 - N_GATES)).reshape(1, 128)
    tp, ts = xp.shape[0], xs.shape[0]
    t = tp + ts
    n_p = tp // ROW_TILE
    n = t // ROW_TILE
    r = ROW_TILE // HALO
    const = lambda i: (0, 0)
    p_tile = lambda i: jnp.minimum(i, n_p - 1)
    s_tile = lambda i: jnp.maximum(i - n_p, 0)
    scale = jnp.concatenate([jnp.ones((1, ML_WIDTH), F32), jnp.full((1, ML_WIDTH), ML_HEAD_DIM ** -0.5, F32)], axis=1)
    return pl.pallas_call(
        functools.partial(_in_proj_kernel, seq),
        grid=(n,),
        in_specs=[
            pl.BlockSpec((ROW_TILE, D_MODEL), lambda i: (p_tile(i), 0)),
            pl.BlockSpec((HALO, D_MODEL), lambda i: (jnp.maximum(p_tile(i) * r - 1, 0), 0)),
            pl.BlockSpec((HALO, D_MODEL), lambda i: (jnp.minimum((p_tile(i) + 1) * r, tp // HALO - 1), 0)),
            pl.BlockSpec((ROW_TILE, D_MODEL), lambda i: (s_tile(i), 0)),
            pl.BlockSpec((HALO, D_MODEL), lambda i: (jnp.maximum(s_tile(i) * r - 1, 0), 0)),
            pl.BlockSpec((HALO, D_MODEL), lambda i: (jnp.minimum((s_tile(i) + 1) * r, ts // HALO - 1), 0)),
            pl.BlockSpec((D_MODEL, QK_WIDTH), const),
            pl.BlockSpec((D_MODEL, Z_WIDTH + 128), const),
            pl.BlockSpec((1, 128), const),
            pl.BlockSpec((CONV_WIDTH, QK_WIDTH), const),
            pl.BlockSpec((1, QK_WIDTH), const),
            pl.BlockSpec((1, QK_WIDTH), const),
        ],
        out_specs=[
            pl.BlockSpec((ROW_TILE, QK_WIDTH), lambda i: (i, 0)),
            pl.BlockSpec((ROW_TILE, Z_WIDTH), lambda i: (i, 0)),
            pl.BlockSpec((ROW_TILE, N_GATES), lambda i: (i, 0)),
            pl.BlockSpec((N_GATES, ROW_TILE), lambda i: (0, i)),
        ],
        out_shape=[
            jax.ShapeDtypeStruct((t, QK_WIDTH), BF16),
            jax.ShapeDtypeStruct((t, Z_WIDTH), BF16),
            jax.ShapeDtypeStruct((t, N_GATES), F32),
            jax.ShapeDtypeStruct((N_GATES, t), F32),
        ],
        scratch_shapes=[pltpu.VMEM((ROW_TILE + 2 * HALO, QK_WIDTH), F32)],
        compiler_params=_cparams(("arbitrary",)),
        name="in_proj_conv",
    )(xp, xp, xp, xs, xs, xs, w_qk, w_main, b_g, conv_w, conv_b.reshape(1, QK_WIDTH), scale)


ATT_TILE = 1024
ATT_SUB = ATT_TILE // BLOCK


ONES_ROWS = 16


def _attention_kernel(seq, q_ref, kp_ref, kc_ref, kn_ref, vp_ref, vc_ref, vn_ref, bias_ref, sink_ref,
                      o_ref, klo_ref, khi_ref, vt_ref):
    i = pl.program_id(0)
    lane = lax.broadcasted_iota(jnp.int32, (ATT_TILE + 2 * BLOCK, 2 * ATT_HEAD_DIM), 1)
    kband = jnp.concatenate([kp_ref[...], kc_ref[...], kn_ref[...]], axis=0)
    zero = jnp.zeros_like(kband)
    klo_ref[...] = jnp.where(lane < ATT_HEAD_DIM, kband, zero)
    khi_ref[...] = jnp.where(lane < ATT_HEAD_DIM, zero, kband)
    vband = jnp.concatenate([vp_ref[...], vc_ref[...], vn_ref[...]], axis=0)
    vt_ref[0:2 * ATT_HEAD_DIM, :] = vband.T
    vt_ref[2 * ATT_HEAD_DIM:, :] = jnp.ones((ONES_ROWS, ATT_TILE + 2 * BLOCK), BF16)
    feat = lax.broadcasted_iota(jnp.int32, (2 * ATT_HEAD_DIM, ATT_GROUP * BLOCK), 0)
    def block(s):
        first, last = _seq_pos(i * ATT_SUB + s, *seq)
        variant = jnp.where(first, 1, jnp.where(last, 2, 0))
        q = q_ref[s * BLOCK:(s + 1) * BLOCK, :]
        q_all = jnp.concatenate([q[:, t * 128:(t + 1) * 128] for t in range(ATT_GROUP)], axis=0)
        vt = vt_ref[:, s * BLOCK:(s + 3) * BLOCK]
        scores = [_dot_nt(k_ref[s * BLOCK:(s + 3) * BLOCK, :], q_all) for k_ref in (klo_ref, khi_ref)]
        yield
        ms, ps = [], []
        for kv in range(ATT_KV_HEADS):
            logits = scores[kv] + bias_ref[variant, kv]
            ms.append(jnp.maximum(jnp.max(logits, axis=0, keepdims=True), sink_ref[kv]))
            ps.append(jnp.exp(logits - ms[kv]).astype(BF16))
        yield
        ovs = [_dot(vt, p) for p in ps]
        yield
        outs = []
        for kv in range(ATT_KV_HEADS):
            den = ovs[kv][2 * ATT_HEAD_DIM:2 * ATT_HEAD_DIM + 1, :] + jnp.exp(sink_ref[kv] - ms[kv])
            outs.append(ovs[kv][0:2 * ATT_HEAD_DIM, :] * (1.0 / den))
        both = jnp.where(feat < ATT_HEAD_DIM, outs[0], outs[1]).astype(BF16)
        for t in range(ATT_GROUP):
            o_ref[t * 128:(t + 1) * 128, s * BLOCK:(s + 1) * BLOCK] = both[:, t * BLOCK:(t + 1) * BLOCK]

    _trace_stagewise([block(s) for s in range(ATT_SUB)])


def _attention(z, bias, sink, seq):
    t = z.shape[0]
    n = t // ATT_TILE
    nblk = t // BLOCK
    band = ATT_TILE + 2 * BLOCK
    prev = lambda i: jnp.maximum(i * ATT_SUB - 1, 0)
    nxt = lambda i: jnp.minimum((i + 1) * ATT_SUB, nblk - 1)
    return pl.pallas_call(
        functools.partial(_attention_kernel, seq),
        grid=(n,),
        in_specs=[
            pl.BlockSpec((ATT_TILE, ATT_WIDTH), lambda i: (i, ZB_ATTQ)),
            pl.BlockSpec((BLOCK, 128), lambda i: (prev(i), ZB_ATTK)),
            pl.BlockSpec((ATT_TILE, 128), lambda i: (i, ZB_ATTK)),
            pl.BlockSpec((BLOCK, 128), lambda i: (nxt(i), ZB_ATTK)),
            pl.BlockSpec((BLOCK, 128), lambda i: (prev(i), ZB_ATTV)),
            pl.BlockSpec((ATT_TILE, 128), lambda i: (i, ZB_ATTV)),
            pl.BlockSpec((BLOCK, 128), lambda i: (nxt(i), ZB_ATTV)),
            pl.BlockSpec((3, ATT_KV_HEADS, 3 * BLOCK, ATT_GROUP * BLOCK), lambda i: (0, 0, 0, 0)),
            pl.BlockSpec((ATT_KV_HEADS, 1, ATT_GROUP * BLOCK), lambda i: (0, 0, 0)),
        ],
        out_specs=pl.BlockSpec((ATT_WIDTH, ATT_TILE), lambda i: (0, i)),
        out_shape=jax.ShapeDtypeStruct((ATT_WIDTH, t), BF16),
        scratch_shapes=[pltpu.VMEM((band, 128), BF16), pltpu.VMEM((band, 128), BF16),
                        pltpu.VMEM((2 * ATT_HEAD_DIM + ONES_ROWS, band), BF16)],
        compiler_params=_cparams(("arbitrary",)),
        name="window_attention",
    )(z, z, z, z, z, z, z, bias, sink)


def _t5_bucket(rel):
    half = N_BUCKETS // 2
    exact = half // 2
    n = np.abs(rel)
    large = exact + (np.log(np.maximum(n, 1) / exact) / np.log(MAX_DISTANCE / exact) * (half - exact)).astype(np.int32)
    large = np.minimum(large, half - 1)
    return ((rel > 0).astype(np.int32) * half + np.where(n < exact, n, large)).astype(np.int32)


def _attention_tables(rel_bias, attn_sink):
    rel = np.arange(3 * BLOCK)[:, None] - BLOCK - np.arange(BLOCK)[None, :]
    onehot = jnp.asarray(_t5_bucket(rel)[..., None] == np.arange(N_BUCKETS), F32)
    bias = jnp.einsum('kqb,bh->hkq', onehot, rel_bias.astype(F32), precision=lax.Precision.HIGHEST)
    bias = jnp.where(jnp.asarray(np.abs(rel) <= WINDOW)[None], bias, NEG)
    bias = bias.reshape(ATT_KV_HEADS, ATT_GROUP, 3 * BLOCK, BLOCK).transpose(0, 2, 1, 3)
    bias = bias.reshape(ATT_KV_HEADS, 3 * BLOCK, ATT_GROUP * BLOCK)
    key = np.arange(3 * BLOCK)[None, :, None]
    first = jnp.where(jnp.asarray(key < BLOCK), NEG, bias)
    last = jnp.where(jnp.asarray(key >= 2 * BLOCK), NEG, bias)
    sink = jnp.repeat(attn_sink.astype(F32), BLOCK).reshape(ATT_KV_HEADS, 1, ATT_GROUP * BLOCK)
    return jnp.stack([bias, first, last]), sink


def _mlstm_kernel(seq, n_chunks, *refs):
    fwd_in, bwd_in, (of_ref, ob_ref), state = refs[0:5], refs[5:10], refs[10:12], refs[12:]
    _trace_stagewise(_mlstm_chains(seq, False, n_chunks, *fwd_in, of_ref, *state[0:2])
                     + _mlstm_chains(seq, True, n_chunks, *bwd_in, ob_ref, *state[2:4]))


def _mlstm_chains(seq, reverse, n_chunks, q_ref, k_ref, v_ref, gr_ref, gc_ref, o_ref, ct_ref, m_ref):
    step = pl.program_id(0)
    chunk = (n_chunks - 1 - step) if reverse else step
    first, last = _seq_pos(chunk, *seq)
    fresh = last if reverse else first

    L = ML_CHUNK
    row = lax.broadcasted_iota(jnp.int32, (L, L), 0)
    col = lax.broadcasted_iota(jnp.int32, (L, L), 1)
    vis = (row >= col) if reverse else (row <= col)
    vis_t = (col >= row) if reverse else (col <= row)
    gr = gr_ref[...]
    gc = gc_ref[...]
    r3 = _dot(jnp.concatenate(_split3(_log_sigmoid(gr)), axis=0), vis.astype(BF16))
    b_rows = r3[0:N_GATES] + r3[N_GATES:2 * N_GATES] + r3[2 * N_GATES:]
    vis_tb = vis_t.astype(BF16)
    c_hi, c_mid, c_lo = _split3(_log_sigmoid(gc))
    b_cols = _dot(vis_tb, c_hi) + _dot(vis_tb, c_mid) + _dot(vis_tb, c_lo)
    i_off = 2 * ML_HEADS if reverse else 0
    f_off = i_off + ML_HEADS
    end = 0 if reverse else L - 1
    ones = jnp.ones((ONES_ROWS, L), BF16)

    def chain(h):
        hs = slice(h * ML_HEAD_DIM, (h + 1) * ML_HEAD_DIM)
        q = q_ref[:, hs]
        k = k_ref[:, hs]
        vt1 = jnp.concatenate([v_ref[:, hs].T, ones], axis=0)
        b_row = b_rows[f_off + h:f_off + h + 1, :]
        u_row = gr[i_off + h:i_off + h + 1, :] - b_row
        u_col = gc[:, i_off + h:i_off + h + 1] - b_cols[:, f_off + h:f_off + h + 1]
        g = b_row[:, end:end + 1]
        ct_old = jnp.where(fresh, 0.0, ct_ref[h])
        m_old = jnp.where(fresh, 0.0, m_ref[h])[:, 0:1]
        kq = _dot_nt(k, q)
        from_state = _dot_nt(ct_old.astype(BF16), q)
        yield
        u_mat = jnp.where(vis, u_col, NEG)
        mm = jnp.maximum(jnp.max(u_mat, axis=0, keepdims=True), m_old)
        st = (kq * jnp.exp(u_mat - mm)).astype(BF16)
        yield
        tot = _dot(vt1, st) + jnp.exp(m_old - mm) * from_state
        a_max = jnp.max(g + u_row, axis=-1, keepdims=True)
        m_new = jnp.maximum(g + m_old, a_max)
        s_old = jnp.exp(g + m_old - m_new)
        weighted = (vt1.astype(F32) * jnp.exp(g + u_row - m_new)).astype(BF16)
        new_state = _dot(weighted, k)
        yield
        den = tot[ML_HEAD_DIM:ML_HEAD_DIM + 1, :]
        floor = jnp.exp(-(b_row + mm))
        o_ref[hs, :] = (tot[0:ML_HEAD_DIM, :] * (1.0 / jnp.maximum(jnp.abs(den), floor))).astype(BF16)
        ct_ref[h] = s_old * ct_old + new_state
        m_ref[h] = jnp.broadcast_to(m_new, (1, ML_HEAD_DIM))

    return [chain(h) for h in range(ML_HEADS)]


def _mlstm(qk, z, gates_r, gates_c, seq):
    t = qk.shape[0]
    nc = t // ML_CHUNK

    def chunk_specs(ch):
        return [
            pl.BlockSpec((ML_CHUNK, ML_WIDTH), lambda i: (ch(i), 0)),
            pl.BlockSpec((ML_CHUNK, ML_WIDTH), lambda i: (ch(i), 1)),
            pl.BlockSpec((ML_CHUNK, ML_WIDTH), lambda i: (ch(i), ZB_MLV)),
            pl.BlockSpec((N_GATES, ML_CHUNK), lambda i: (0, ch(i))),
            pl.BlockSpec((ML_CHUNK, N_GATES), lambda i: (ch(i), 0)),
        ]

    fwd = lambda i: i
    bwd = lambda i: nc - 1 - i
    state = [
        pltpu.VMEM((ML_HEADS, ML_HEAD_DIM + ONES_ROWS, ML_HEAD_DIM), F32),
        pltpu.VMEM((ML_HEADS, 1, ML_HEAD_DIM), F32),
    ]
    operands = (qk, qk, z, gates_r, gates_c)
    return pl.pallas_call(
        functools.partial(_mlstm_kernel, seq, nc),
        grid=(nc,),
        in_specs=chunk_specs(fwd) + chunk_specs(bwd),
        out_specs=[pl.BlockSpec((ML_WIDTH, ML_CHUNK), lambda i: (0, fwd(i))),
                   pl.BlockSpec((ML_WIDTH, ML_CHUNK), lambda i: (0, bwd(i)))],
        out_shape=[jax.ShapeDtypeStruct((ML_WIDTH, t), BF16), jax.ShapeDtypeStruct((ML_WIDTH, t), BF16)],
        scratch_shapes=state + state,
        compiler_params=_cparams(("arbitrary",)),
        name="mlstm",
    )(*operands, *operands)


def _mix_out(n_p_tiles, toks, xp_ref, xs_ref, att_ref, hf_ref, hb_ref, og_ref, wa_ref, wm_ref, gain_ref, g_ref,
             b_ref):
    i = pl.program_id(0)
    x = jnp.where(i < n_p_tiles, xp_ref[toks, :], xs_ref[toks, :])
    h = hf_ref[:, toks].astype(F32) + hb_ref[:, toks].astype(F32)
    parts = []
    for hd in range(ML_HEADS):
        hh = h[hd * ML_HEAD_DIM:(hd + 1) * ML_HEAD_DIM, :]
        mu = jnp.mean(hh, axis=0, keepdims=True)
        hc = hh - mu
        var = jnp.mean(hc * hc, axis=0, keepdims=True)
        parts.append(hc * lax.rsqrt(var + LN_EPS))
    hn = jnp.concatenate(parts, axis=0) * gain_ref[...] * _sigmoid(og_ref[toks, :].T.astype(F32))
    mixed = _dot(att_ref[:, toks].T, wa_ref[...]) + _dot(hn.astype(BF16).T, wm_ref[...])
    return _layer_norm(DN_ALPHA * x + mixed, g_ref[...], b_ref[...])


def _mem_kv_kernel(m_ref, w_ref, o_ref):
    o_ref[...] = _dot(m_ref[...].astype(BF16), w_ref[...]).astype(BF16)


def _mem_kv(mem, wkv):
    rows = mem.shape[0]
    return pl.pallas_call(
        _mem_kv_kernel,
        grid=(rows // MEM_TOKENS,),
        in_specs=[pl.BlockSpec((MEM_TOKENS, D_MODEL), lambda i: (i, 0)),
                  pl.BlockSpec((D_MODEL, 2 * D_MODEL), lambda i: (0, 0))],
        out_specs=pl.BlockSpec((MEM_TOKENS, 2 * D_MODEL), lambda i: (i, 0)),
        out_shape=jax.ShapeDtypeStruct((rows, 2 * D_MODEL), BF16),
        compiler_params=_cparams(("arbitrary",)),
        name="mem_kv",
    )(mem, wkv)


def _pack_bf16_pairs(a, b, exact=False):
    if not exact:
        a, b = a.astype(BF16).astype(F32), b.astype(BF16).astype(F32)
    ua, ub = pltpu.bitcast(a, jnp.uint32), pltpu.bitcast(b, jnp.uint32)
    return (ua >> 16) | (ub & jnp.uint32(0xFFFF0000))


def _unpack_bf16_pairs(u):
    lo = pltpu.bitcast(u << 16, F32)
    hi = pltpu.bitcast(u & jnp.uint32(0xFFFF0000), F32)
    return lo, hi


TOKEN_GROUPS = 2


def _cross_router_kernel(n_p_tiles, *refs):
    mix_refs, (kv_ref, wq_ref, wo_ref, g_ref, b_ref, wr_ref, br_ref, tri_ref, etri_ref,
               x2_ref, x2b_ref, gate_ref, slot_ref, chunk_ref) = refs[:11], refs[11:]
    head_cols = [slice(h * MEM_HEAD_DIM, (h + 1) * MEM_HEAD_DIM) for h in range(MEM_HEADS)]

    def token_group(r):
        toks = slice(r * ROW_TILE // TOKEN_GROUPS, (r + 1) * ROW_TILE // TOKEN_GROUPS)
        x = _mix_out(n_p_tiles, toks, *mix_refs)
        yield
        q = _dot(x.astype(BF16), wq_ref[...]).astype(BF16)
        scores = [_dot_nt(q[:, hs], kv_ref[:, hs]) for hs in head_cols]
        yield
        probs, dens = [], []
        for logits in scores:
            p = jnp.exp(logits - jnp.max(logits, axis=-1, keepdims=True))
            dens.append(jnp.sum(p, axis=-1, keepdims=True))
            probs.append(p.astype(BF16))
        yield
        values = [_dot(p, kv_ref[:, D_MODEL + hs.start:D_MODEL + hs.stop]) for p, hs in zip(probs, head_cols)]
        o = jnp.concatenate([(v * (1.0 / den)).astype(BF16) for v, den in zip(values, dens)], axis=1)
        y = _dot(o, wo_ref[...])
        yield
        x2 = _layer_norm(DN_ALPHA * x + y, g_ref[...], b_ref[...])
        x2_ref[toks, :] = x2
        x2b_ref[toks, :] = x2.astype(BF16)

    _trace_stagewise([token_group(r) for r in range(TOKEN_GROUPS)])
    x2b = x2b_ref[...]

    logits = _dot_nt(wr_ref[...], x2b) + br_ref[...]
    expert = lax.broadcasted_iota(jnp.int32, logits.shape, 0)
    work = logits
    vals, sels = [], []
    for k in range(TOP_K):
        mx = jnp.max(work, axis=0, keepdims=True)
        ix = jnp.min(jnp.where(work == mx, expert, N_EXPERTS), axis=0, keepdims=True)
        sel = expert == ix
        work = jnp.where(sel, -jnp.inf, work)
        vals.append(mx)
        sels.append(sel)
    es = [jnp.exp(v - vals[0]) for v in vals]
    tot = es[0] + es[1] + es[2] + es[3]
    chosen = jnp.zeros(logits.shape, F32)
    for k in range(TOP_K):
        gate_ref[k:k + 1, :] = es[k] / tot
        chosen = chosen + sels[k].astype(F32)
    count = jnp.sum(chosen, axis=1, keepdims=True)
    chunks = jnp.floor((count + (SEG_ALIGN - 1)) * (1.0 / SEG_ALIGN))
    chunks_b = jnp.broadcast_to(chunks, (N_EXPERTS, 128))
    seg_first = _dot(etri_ref[...], chunks_b.astype(BF16))[:, 0:1] * SEG_ALIGN
    before = _dot(chosen.astype(BF16), tri_ref[...])
    local_row = before + seg_first
    for k in range(TOP_K):
        slot_ref[k:k + 1, :] = jnp.sum(jnp.where(sels[k], local_row, 0.0), axis=0, keepdims=True).astype(jnp.int32)
    chunk_ref[0] = chunks_b.astype(jnp.int32)


def _cross_router(xp, xs, att, hf, hb, z, w_att, w_ml, gain, g1, b1, kv, wq, wo, g, b, w_router, b_router,
                  mem_of_tile):
    t = att.shape[1]
    n_p = xp.shape[0] // ROW_TILE
    n = t // ROW_TILE
    const = lambda i: (0, 0)
    tile = lambda i: (i, 0)
    lanes = lambda i: (0, i)
    tri = jnp.asarray(np.triu(np.ones((ROW_TILE, ROW_TILE), np.float32), 1), BF16)
    etri = jnp.asarray(np.tril(np.ones((N_EXPERTS, N_EXPERTS), np.float32), -1), BF16)
    return pl.pallas_call(
        functools.partial(_cross_router_kernel, n_p),
        grid=(n,),
        in_specs=[
            pl.BlockSpec((ROW_TILE, D_MODEL), lambda i: (jnp.minimum(i, n_p - 1), 0)),
            pl.BlockSpec((ROW_TILE, D_MODEL), lambda i: (jnp.maximum(i - n_p, 0), 0)),
            pl.BlockSpec((ATT_WIDTH, ROW_TILE), lanes),
            pl.BlockSpec((ML_WIDTH, ROW_TILE), lanes),
            pl.BlockSpec((ML_WIDTH, ROW_TILE), lanes),
            pl.BlockSpec((ROW_TILE, ML_WIDTH), lambda i: (i, ZB_MLO)),
            pl.BlockSpec((ATT_WIDTH, D_MODEL), const),
            pl.BlockSpec((ML_WIDTH, D_MODEL), const),
            pl.BlockSpec((ML_WIDTH, 1), const),
            pl.BlockSpec((1, D_MODEL), const),
            pl.BlockSpec((1, D_MODEL), const),
            pl.BlockSpec((MEM_TOKENS, 2 * D_MODEL), lambda i: (mem_of_tile(i), 0)),
            pl.BlockSpec((D_MODEL, D_MODEL), const),
            pl.BlockSpec((D_MODEL, D_MODEL), const),
            pl.BlockSpec((1, D_MODEL), const),
            pl.BlockSpec((1, D_MODEL), const),
            pl.BlockSpec((N_EXPERTS, D_MODEL), const),
            pl.BlockSpec((N_EXPERTS, 1), const),
            pl.BlockSpec((ROW_TILE, ROW_TILE), const),
            pl.BlockSpec((N_EXPERTS, N_EXPERTS), const),
        ],
        out_specs=[
            pl.BlockSpec((ROW_TILE, D_MODEL), tile),
            pl.BlockSpec((ROW_TILE, D_MODEL), tile),
            pl.BlockSpec((TOP_K, ROW_TILE), lanes),
            pl.BlockSpec((TOP_K, ROW_TILE), lanes),
            pl.BlockSpec((1, N_EXPERTS, 128), lambda i: (i, 0, 0)),
        ],
        out_shape=[
            jax.ShapeDtypeStruct((t, D_MODEL), F32),
            jax.ShapeDtypeStruct((t, D_MODEL), BF16),
            jax.ShapeDtypeStruct((TOP_K, t), F32),
            jax.ShapeDtypeStruct((TOP_K, t), jnp.int32),
            jax.ShapeDtypeStruct((n, N_EXPERTS, 128), jnp.int32),
        ],
        compiler_params=_cparams(("arbitrary",)),
        name="mix_out_cross_attn_router",
    )(xp, xs, att, hf, hb, z, w_att, w_ml, gain.reshape(ML_WIDTH, 1), g1.reshape(1, D_MODEL), b1.reshape(1, D_MODEL),
      kv, wq, wo, g.reshape(1, D_MODEL), b.reshape(1, D_MODEL), w_router.T.astype(BF16),
      b_router.reshape(N_EXPERTS, 1), tri, etri)


SEG_ALIGN = 8
LOCAL_ROWS = ROW_TILE * TOP_K + N_EXPERTS * SEG_ALIGN
HALF = D_MODEL // 2


def _rows(chunks):
    return pl.multiple_of(chunks * SEG_ALIGN, SEG_ALIGN)


def _segment_copies(chunks_ref, seg_ref, loc_ref, tile, make_copy):
    def per_expert(e, carry):
        entry = tile * N_EXPERTS + e

        @pl.when(chunks_ref[entry] > 0)
        def _():
            make_copy(_rows(loc_ref[entry]), _rows(seg_ref[entry]), _rows(chunks_ref[entry])).start()

        return carry

    lax.fori_loop(0, N_EXPERTS, per_expert, 0)


def _wait_copies(chunks, make_copy):
    @pl.when(chunks > 0)
    def _():
        make_copy(0, 0, _rows(chunks)).wait()


def _dispatch_kernel(chunks_ref, seg_ref, loc_ref, total_ref, tail_ref, tailn_ref, nact_ref,
                     x_ref, slot_ref, rows_hbm, local_ref, zero_ref, sems):
    i = pl.program_id(0)
    n = pl.num_programs(0)
    buf = i % 2

    def copy_out(b):
        return lambda loc, seg, rows: pltpu.make_async_copy(
            local_ref.at[b, pl.ds(loc, rows), :], rows_hbm.at[pl.ds(seg, rows), :], sems.at[b])

    def run(b):
        @pl.when(i >= 2)
        def _():
            _wait_copies(total_ref[jnp.maximum(i - 2, 0)], copy_out(b))

        slots = slot_ref[...].astype(F32)
        row = lax.broadcasted_iota(jnp.int32, (LOCAL_ROWS, ROW_TILE), 0).astype(F32)
        miss = (slots[0:1, :] - row) * (slots[1:2, :] - row)
        for k in range(2, TOP_K):
            miss = miss * (slots[k:k + 1, :] - row)
        perm = jnp.where(miss == 0.0, 1.0, 0.0).astype(BF16)
        picked = _dot(perm, x_ref[...])
        local_ref[b] = _pack_bf16_pairs(picked[:, :HALF], picked[:, HALF:], exact=True)
        _segment_copies(chunks_ref, seg_ref, loc_ref, i, copy_out(b))

    for b in range(2):
        pl.when(buf == b)(functools.partial(run, b))

    @pl.when(i == n - 1)
    def _():
        for b in range(2):
            step = jnp.where(buf == b, i, i - 1)
            _wait_copies(total_ref[step], copy_out(b))
        zero_ref[...] = jnp.zeros_like(zero_ref)
        fill = lambda _, seg, rows: pltpu.make_async_copy(
            zero_ref.at[pl.ds(0, rows), :], rows_hbm.at[pl.ds(seg, rows), :], sems.at[2])

        def per_expert(e, carry):
            @pl.when(tailn_ref[e] > 0)
            def _():
                fill(0, _rows(tail_ref[e]), _rows(tailn_ref[e])).start()

            _wait_copies(tailn_ref[e], fill)
            return carry

        lax.fori_loop(0, N_EXPERTS, per_expert, 0)
        fill_block = lambda blk: pltpu.make_async_copy(
            zero_ref, rows_hbm.at[pl.ds(pl.multiple_of(blk * MOE_ROWS, MOE_ROWS), MOE_ROWS), :], sems.at[2])
        n_blocks = rows_hbm.shape[0] // MOE_ROWS

        def start_block(blk, carry):
            fill_block(blk).start()
            return carry

        def wait_block(blk, carry):
            fill_block(0).wait()
            return carry

        lax.fori_loop(nact_ref[0], n_blocks, start_block, 0)
        lax.fori_loop(nact_ref[0], n_blocks, wait_block, 0)


def _dispatch(x2b, slots, tables, n_rows):
    n = x2b.shape[0] // ROW_TILE
    assert n >= 2
    return pl.pallas_call(
        _dispatch_kernel,
        grid_spec=pltpu.PrefetchScalarGridSpec(
            num_scalar_prefetch=7,
            grid=(n,),
            in_specs=[
                pl.BlockSpec((ROW_TILE, D_MODEL), lambda i, *_: (i, 0)),
                pl.BlockSpec((TOP_K, ROW_TILE), lambda i, *_: (0, i)),
            ],
            out_specs=pl.BlockSpec(memory_space=pl.ANY),
            scratch_shapes=[
                pltpu.VMEM((2, LOCAL_ROWS, HALF), jnp.uint32),
                pltpu.VMEM((MOE_ROWS, HALF), jnp.uint32),
                pltpu.SemaphoreType.DMA((3,)),
            ],
        ),
        out_shape=jax.ShapeDtypeStruct((n_rows, HALF), jnp.uint32),
        compiler_params=_cparams(("arbitrary",)),
        name="moe_dispatch",
    )(*tables, x2b, slots)


def _expert_kernel(be_ref, na_ref, half_ref, x_ref, wu_ref, bu_ref, wd_ref, bd_ref, o_ref, wub_ref, wdb_ref):
    i = pl.program_id(0)

    @pl.when((i == 0) | (be_ref[i] != be_ref[jnp.maximum(i - 1, 0)]))
    def _():
        wub_ref[...] = wu_ref[0].astype(BF16)
        wdb_ref[...] = wd_ref[0].astype(BF16)

    def mlp(rows):
        lo, hi = _unpack_bf16_pairs(x_ref[rows, :])
        x = jnp.concatenate([lo, hi], axis=1).astype(BF16)
        hu = _dot(x, wub_ref[...]) + bu_ref[0]
        h_glu = jnp.minimum(hu[:, :D_FF], SWIGLU_LIMIT)
        h_lin = jnp.clip(hu[:, D_FF:], -SWIGLU_LIMIT, SWIGLU_LIMIT)
        hh = h_glu * _sigmoid(SWIGLU_ALPHA * h_glu) * (h_lin + 1.0)
        y = _dot(hh.astype(BF16), wdb_ref[...]) + bd_ref[0]
        o_ref[rows, :] = _pack_bf16_pairs(y[:, :D_MODEL // 2], y[:, D_MODEL // 2:])

    active = i < na_ref[0]
    half_full = half_ref[i] == 1
    top, bottom = slice(0, MOE_ROWS // 2), slice(MOE_ROWS // 2, MOE_ROWS)

    @pl.when(active & jnp.logical_not(half_full))
    def _():
        mlp(slice(0, MOE_ROWS))

    @pl.when(active & half_full)
    def _():
        mlp(top)
        o_ref[bottom, :] = jnp.zeros((MOE_ROWS // 2, D_MODEL // 2), jnp.uint32)

    @pl.when(jnp.logical_not(active))
    def _():
        o_ref[...] = jnp.zeros_like(o_ref)


def _experts(xs, block_e, n_active, half_full, w_up, b_up, w_down, b_down):
    n_rows = xs.shape[0]
    n_blk = n_rows // MOE_ROWS
    return pl.pallas_call(
        _expert_kernel,
        grid_spec=pltpu.PrefetchScalarGridSpec(
            num_scalar_prefetch=3,
            grid=(n_blk,),
            in_specs=[
                pl.BlockSpec((MOE_ROWS, D_MODEL // 2), lambda i, be, na, hf: (jnp.minimum(i, na[0] - 1), 0)),
                pl.BlockSpec((1, D_MODEL, 2 * D_FF), lambda i, be, na, hf: (be[i], 0, 0)),
                pl.BlockSpec((1, 1, 2 * D_FF), lambda i, be, na, hf: (be[i], 0, 0)),
                pl.BlockSpec((1, D_FF, D_MODEL), lambda i, be, na, hf: (be[i], 0, 0)),
                pl.BlockSpec((1, 1, D_MODEL), lambda i, be, na, hf: (be[i], 0, 0)),
            ],
            out_specs=pl.BlockSpec((MOE_ROWS, D_MODEL // 2), lambda i, be, na, hf: (i, 0)),
            scratch_shapes=[pltpu.VMEM((D_MODEL, 2 * D_FF), BF16), pltpu.VMEM((D_FF, D_MODEL), BF16)],
        ),
        out_shape=jax.ShapeDtypeStruct((n_rows, D_MODEL // 2), jnp.uint32),
        compiler_params=_cparams(("arbitrary",)),
        name="experts",
    )(block_e, n_active, half_full, xs, w_up, b_up.reshape(N_EXPERTS, 1, 2 * D_FF), w_down,
      b_down.reshape(N_EXPERTS, 1, D_MODEL))


def _combine_kernel(first_tile, chunks_ref, seg_ref, loc_ref, total_ref,
                    x_ref, rows_hbm, slot_ref, gate_ref, g_ref, b_ref, o_ref, local_ref, sems):
    i = pl.program_id(0)
    n = pl.num_programs(0)
    tile = first_tile + i
    buf = i % 2

    def copy_in(b):
        return lambda loc, seg, rows: pltpu.make_async_copy(
            rows_hbm.at[pl.ds(seg, rows), :], local_ref.at[b, pl.ds(loc, rows), :], sems.at[b])

    @pl.when(i == 0)
    def _():
        local_ref[...] = jnp.zeros_like(local_ref)
        _segment_copies(chunks_ref, seg_ref, loc_ref, tile, copy_in(0))

    def run(b):
        @pl.when(i + 1 < n)
        def _():
            _segment_copies(chunks_ref, seg_ref, loc_ref, tile + 1, copy_in(1 - b))

        _wait_copies(total_ref[tile], copy_in(b))
        lo, hi = _unpack_bf16_pairs(local_ref[b])
        rows = jnp.concatenate([lo, hi], axis=1).astype(BF16)
        slots = slot_ref[...]
        gates = gate_ref[...]
        row = lax.broadcasted_iota(jnp.int32, (ROW_TILE, LOCAL_ROWS), 1)
        weight = jnp.zeros((ROW_TILE, LOCAL_ROWS), F32)
        for k in range(TOP_K):
            weight = jnp.where(slots[:, k:k + 1] == row, gates[:, k:k + 1], weight)
        y = _dot(weight.astype(BF16), rows)
        o_ref[...] = _layer_norm(DN_ALPHA * x_ref[...] + y, g_ref[...], b_ref[...])

    for b in range(2):
        pl.when(buf == b)(functools.partial(run, b))


def _combine(x2, rows_out, slots_c, gates_c, tables, g, b, first_tile, n_tiles):
    const = lambda i, *_: (0, 0)
    tile = lambda i, *_: (first_tile + i, 0)
    return pl.pallas_call(
        functools.partial(_combine_kernel, first_tile),
        grid_spec=pltpu.PrefetchScalarGridSpec(
            num_scalar_prefetch=4,
            grid=(n_tiles,),
            in_specs=[
                pl.BlockSpec((ROW_TILE, D_MODEL), tile),
                pl.BlockSpec(memory_space=pl.ANY),
                pl.BlockSpec((ROW_TILE, TOP_K), tile),
                pl.BlockSpec((ROW_TILE, TOP_K), tile),
                pl.BlockSpec((1, D_MODEL), const),
                pl.BlockSpec((1, D_MODEL), const),
            ],
            out_specs=pl.BlockSpec((ROW_TILE, D_MODEL), lambda i, *_: (i, 0)),
            scratch_shapes=[pltpu.VMEM((2, LOCAL_ROWS, HALF), jnp.uint32), pltpu.SemaphoreType.DMA((2,))],
        ),
        out_shape=jax.ShapeDtypeStruct((n_tiles * ROW_TILE, D_MODEL), F32),
        compiler_params=_cparams(("arbitrary",)),
        name="combine_ln3",
    )(*tables[:4], x2, rows_out, slots_c, gates_c, g.reshape(1, D_MODEL), b.reshape(1, D_MODEL))


def kernel(x_prompt, x_sample, mem_prompt, mem_sample, rel_bias, w_in, b_gates, conv_w, conv_b, mh_gain, attn_sink,
           w_out, ln1_g, ln1_b, wq_mem, wkv_mem, wo_mem, ln2_g, ln2_b, w_router, b_router, w_up, b_up, w_down,
           b_down, ln3_g, ln3_b):
    assert w_in.shape[0] == 1, "single layer"
    bp, sp, _ = x_prompt.shape
    bs, ss, _ = x_sample.shape
    tp, ts = bp * sp, bs * ss
    t = tp + ts
    assert sp % ATT_TILE == 0 and ss % ATT_TILE == 0 and ATT_TILE % ROW_TILE == 0 and ATT_TILE % ML_CHUNK == 0
    xp = x_prompt.reshape(tp, D_MODEL)
    xs = x_sample.reshape(ts, D_MODEL)

    def seq_blocks(rows):
        return (tp // rows, sp // rows, ss // rows)

    w = w_in[0]
    q_end, k_end, v_end = ATT_WIDTH, ATT_WIDTH + 128, ATT_WIDTH + 256
    qk_end, mv_end, mo_end = v_end + 2 * ML_WIDTH, v_end + 3 * ML_WIDTH, v_end + 4 * ML_WIDTH
    head_order = np.concatenate([[h, ATT_GROUP + h] for h in range(ATT_GROUP)])
    att_perm = (head_order[:, None] * ATT_HEAD_DIM + np.arange(ATT_HEAD_DIM)[None, :]).reshape(-1)
    w_qk = w[:, v_end:qk_end].astype(BF16)
    w_main = jnp.concatenate([
        w[:, qk_end:mv_end], w[:, mv_end:mo_end],
        w[:, :q_end][:, att_perm] * (ATT_HEAD_DIM ** -0.5), w[:, q_end:k_end], w[:, k_end:v_end]], axis=1).astype(BF16)
    w_g = w[:, mo_end:].astype(BF16)
    w_att = w_out[0][:ATT_WIDTH][att_perm].astype(BF16)
    w_ml = w_out[0][ATT_WIDTH:].astype(BF16)

    qk, z, gates_c, gates_r = _in_proj(xp, xs, w_qk, w_main, w_g, b_gates[0], conv_w[0], conv_b[0],
                                       seq_blocks(ROW_TILE))

    bias, sink = _attention_tables(rel_bias, attn_sink[0])
    att = _attention(z, bias, sink, seq_blocks(BLOCK))

    h_f, h_b = _mlstm(qk, z, gates_r, gates_c, seq_blocks(ML_CHUNK))

    mem = jnp.concatenate([mem_prompt.reshape(bp * MEM_TOKENS, D_MODEL), mem_sample.reshape(bs * MEM_TOKENS, D_MODEL)])
    kv = _mem_kv(mem, wkv_mem[0].astype(BF16))
    n_p_tiles, p_tiles_per_seq, s_tiles_per_seq = seq_blocks(ROW_TILE)
    mem_of_tile = lambda i: jnp.where(i < n_p_tiles, i // p_tiles_per_seq, bp + (i - n_p_tiles) // s_tiles_per_seq)
    wq = (wq_mem[0] * (MEM_HEAD_DIM ** -0.5)).astype(BF16)
    x2, x2b, top_gate, slots, tile_chunks = _cross_router(
        xp, xs, att, h_f, h_b, z, w_att, w_ml, mh_gain[0], ln1_g[0], ln1_b[0],
        kv, wq, wo_mem[0].astype(BF16), ln2_g[0], ln2_b[0], w_router[0], b_router[0], mem_of_tile)

    n_tiles = t // ROW_TILE
    blk_chunks = MOE_ROWS // SEG_ALIGN
    chunks = tile_chunks[:, :, 0]
    used = jnp.sum(chunks, axis=0)
    region = ((used + blk_chunks - 1) // blk_chunks) * blk_chunks
    region_end = jnp.cumsum(region)
    region_start = region_end - region
    seg_start = region_start[None, :] + jnp.cumsum(chunks, axis=0) - chunks
    loc_start = jnp.cumsum(chunks, axis=1) - chunks
    n_blk = -(-(t * TOP_K + n_tiles * N_EXPERTS * (SEG_ALIGN - 1) + N_EXPERTS * (MOE_ROWS - 1)) // MOE_ROWS)
    blk_first = jnp.arange(n_blk, dtype=jnp.int32) * blk_chunks
    block_e = jnp.minimum(jnp.sum(blk_first[:, None] >= region_end[None, :], axis=1), N_EXPERTS - 1).astype(jnp.int32)
    n_active = (region_end[-1:] // blk_chunks).astype(jnp.int32)
    owner = block_e[:, None] == jnp.arange(N_EXPERTS, dtype=jnp.int32)[None, :]
    used_end = jnp.sum(jnp.where(owner, (region_start + used)[None, :], 0), axis=1)
    half_full = (used_end - blk_first <= blk_chunks // 2).astype(jnp.int32)
    tables = (chunks.reshape(-1), seg_start.reshape(-1), loc_start.reshape(-1), jnp.sum(chunks, axis=1),
              region_start + used, region - used, n_active)
    tables = tuple(tb.astype(jnp.int32) for tb in tables)

    rows_in = _dispatch(x2b, slots, tables, n_blk * MOE_ROWS)
    rows_out = _experts(rows_in, block_e, n_active, half_full, w_up[0], b_up[0], w_down[0], b_down[0])

    slots_c, gates_c = slots.T, top_gate.T
    y_p = _combine(x2, rows_out, slots_c, gates_c, tables, ln3_g[0], ln3_b[0], 0, n_p_tiles)
    y_s = _combine(x2, rows_out, slots_c, gates_c, tables, ln3_g[0], ln3_b[0], n_p_tiles, n_tiles - n_p_tiles)
    return y_p.reshape(bp, sp, D_MODEL), y_s.reshape(bs, ss, D_MODEL)
```

```python
import functools

import numpy as np
import jax
import jax.numpy as jnp
from jax import lax
from jax.experimental import pallas as pl
from jax.experimental.pallas import tpu as pltpu

F32 = jnp.float32
BF16 = jnp.bfloat16

D_MODEL = 1024
ATT_HEADS = 8
ATT_KV_HEADS = 2
ATT_GROUP = ATT_HEADS // ATT_KV_HEADS
ATT_HEAD_DIM = 64
ATT_WIDTH = ATT_HEADS * ATT_HEAD_DIM
WINDOW = 128
BLOCK = WINDOW
N_BUCKETS = 32
MAX_DISTANCE = 128
ML_HEADS = 4
ML_HEAD_DIM = 128
ML_WIDTH = ML_HEADS * ML_HEAD_DIM
ML_CHUNK = 256
CONV_WIDTH = 5
N_GATES = 4 * ML_HEADS
MEM_TOKENS = 256
MEM_HEADS = 4
MEM_HEAD_DIM = D_MODEL // MEM_HEADS
N_EXPERTS = 32
TOP_K = 4
D_FF = D_MODEL
SWIGLU_LIMIT = 7.0
SWIGLU_ALPHA = 1.702
LN_EPS = 1e-5
DN_ALPHA = 2.0 ** 0.25

QK_WIDTH = 2 * ML_WIDTH
Z_WIDTH = ML_WIDTH + ML_WIDTH + ATT_WIDTH + 2 * ATT_KV_HEADS * ATT_HEAD_DIM
ZB_MLV = 0
ZB_MLO = 1
ZB_ATTQ = 2
ZB_ATTK = 12
ZB_ATTV = 13

ROW_TILE = 512
MOE_ROWS = 1024
NEG = -1e30
VMEM_LIMIT = 56 * 1024 * 1024


def _cparams(sem):
    return pltpu.CompilerParams(dimension_semantics=sem, vmem_limit_bytes=VMEM_LIMIT)


def _dot(a, b):
    return jnp.dot(a, b, preferred_element_type=F32)


def _dot_nt(a, b):
    return lax.dot_general(a, b, (((1,), (1,)), ((), ())), preferred_element_type=F32)


def _dot_tn(a, b):
    return lax.dot_general(a, b, (((0,), (0,)), ((), ())), preferred_element_type=F32)


def _split3(x):
    hi = x.astype(BF16)
    rest = x - hi.astype(F32)
    mid = rest.astype(BF16)
    lo = (rest - mid.astype(F32)).astype(BF16)
    return hi, mid, lo


def _layer_norm(y, g, b):
    mu = jnp.mean(y, axis=-1, keepdims=True)
    yc = y - mu
    var = jnp.mean(yc * yc, axis=-1, keepdims=True)
    return yc * lax.rsqrt(var + LN_EPS) * g + b


def _log_sigmoid(x):
    return jnp.minimum(x, 0.0) - jnp.log1p(jnp.exp(-jnp.abs(x)))


def _sigmoid(x):
    return 1.0 / (1.0 + jnp.exp(-x))


def _trace_stagewise(chains):
    while chains:
        chains = [c for c in chains if next(c, "done") != "done"]


def _seq_pos(blk, n_p_blocks, p_blocks_per_seq, s_blocks_per_seq):
    in_p = blk < n_p_blocks
    local = jnp.where(in_p, blk % p_blocks_per_seq, (blk - n_p_blocks) % s_blocks_per_seq)
    per = jnp.where(in_p, p_blocks_per_seq, s_blocks_per_seq)
    return local == 0, local == per - 1


HALO = 8


def _in_proj_kernel(seq, xp_ref, xpl_ref, xpr_ref, xs_ref, xsl_ref, xsr_ref, wqk_ref, w_ref, bg_ref,
                    cw_ref, cb_ref, scale_ref, qk_ref, z_ref, gc_ref, gr_ref, buf_ref):
    i = pl.program_id(0)
    in_p = i < seq[0]
    first, last = _seq_pos(i, *seq)
    x = jnp.where(in_p, xp_ref[...], xs_ref[...])
    left = jnp.where(in_p, xpl_ref[...], xsl_ref[...])
    right = jnp.where(in_p, xpr_ref[...], xsr_ref[...])
    xb = x.astype(BF16)
    x_ext = jnp.concatenate([left, x, right], axis=0).astype(BF16)
    row = lax.broadcasted_iota(jnp.int32, (ROW_TILE + 2 * HALO, 1), 0)
    outside = (first & (row < HALO)) | (last & (row >= ROW_TILE + HALO))
    buf_ref[...] = jnp.where(outside, 0.0, _dot(x_ext, wqk_ref[...]))
    acc = jnp.zeros((ROW_TILE, QK_WIDTH), F32) + cb_ref[...]
    for j in range(CONV_WIDTH):
        off = HALO + j - CONV_WIDTH // 2
        acc = acc + buf_ref[off:off + ROW_TILE, :] * cw_ref[j:j + 1, :]
    qk_ref[...] = (acc * _sigmoid(acc) * scale_ref[...]).astype(BF16)
    zg = _dot(xb, w_ref[...])
    z_ref[...] = zg[:, :Z_WIDTH].astype(BF16)
    gates = zg[:, Z_WIDTH:] + bg_ref[...]
    gc_ref[...] = gates[:, :N_GATES]
    gr_ref[...] = gates.T[:N_GATES, :]


def _in_proj(xp, xs, w_qk, w_main, w_g, b_g, conv_w, conv_b, seq):
    w_main = jnp.pad(jnp.concatenate([w_main, w_g], axis=1), ((0, 0), (0, 128 - N_GATES)))
    b_g = jnp.pad(b_g, (0, 128 - N_GATES)).reshape(1, 128)
    tp, ts = xp.shape[0], xs.shape[0]
    t = tp + ts
    n_p = tp // ROW_TILE
    n = t // ROW_TILE
    r = ROW_TILE // HALO
    const = lambda i: (0, 0)
    p_tile = lambda i: jnp.minimum(i, n_p - 1)
    s_tile = lambda i: jnp.maximum(i - n_p, 0)
    scale = jnp.concatenate([jnp.ones((1, ML_WIDTH), F32), jnp.full((1, ML_WIDTH), ML_HEAD_DIM ** -0.5, F32)], axis=1)
    return pl.pallas_call(
        functools.partial(_in_proj_kernel, seq),
        grid=(n,),
        in_specs=[
            pl.BlockSpec((ROW_TILE, D_MODEL), lambda i: (p_tile(i), 0)),
            pl.BlockSpec((HALO, D_MODEL), lambda i: (jnp.maximum(p_tile(i) * r - 1, 0), 0)),
            pl.BlockSpec((HALO, D_MODEL), lambda i: (jnp.minimum((p_tile(i) + 1) * r, tp // HALO - 1), 0)),
            pl.BlockSpec((ROW_TILE, D_MODEL), lambda i: (s_tile(i), 0)),
            pl.BlockSpec((HALO, D_MODEL), lambda i: (jnp.maximum(s_tile(i) * r - 1, 0), 0)),
            pl.BlockSpec((HALO, D_MODEL), lambda i: (jnp.minimum((s_tile(i) + 1) * r, ts // HALO - 1), 0)),
            pl.BlockSpec((D_MODEL, QK_WIDTH), const),
            pl.BlockSpec((D_MODEL, Z_WIDTH + 128), const),
            pl.BlockSpec((1, 128), const),
            pl.BlockSpec((CONV_WIDTH, QK_WIDTH), const),
            pl.BlockSpec((1, QK_WIDTH), const),
            pl.BlockSpec((1, QK_WIDTH), const),
        ],
        out_specs=[
            pl.BlockSpec((ROW_TILE, QK_WIDTH), lambda i: (i, 0)),
            pl.BlockSpec((ROW_TILE, Z_WIDTH), lambda i: (i, 0)),
            pl.BlockSpec((ROW_TILE, N_GATES), lambda i: (i, 0)),
            pl.BlockSpec((N_GATES, ROW_TILE), lambda i: (0, i)),
        ],
        out_shape=[
            jax.ShapeDtypeStruct((t, QK_WIDTH), BF16),
            jax.ShapeDtypeStruct((t, Z_WIDTH), BF16),
            jax.ShapeDtypeStruct((t, N_GATES), F32),
            jax.ShapeDtypeStruct((N_GATES, t), F32),
        ],
        scratch_shapes=[pltpu.VMEM((ROW_TILE + 2 * HALO, QK_WIDTH), F32)],
        compiler_params=_cparams(("arbitrary",)),
        name="in_proj_conv",
    )(xp, xp, xp, xs, xs, xs, w_qk, w_main, b_g, conv_w, conv_b.reshape(1, QK_WIDTH), scale)


ATT_TILE = 1024
ATT_SUB = ATT_TILE // BLOCK


ONES_ROWS = 16


def _attention_kernel(seq, q_ref, kp_ref, kc_ref, kn_ref, vp_ref, vc_ref, vn_ref, bias_ref, sink_ref,
                      o_ref, klo_ref, khi_ref, vt_ref):
    i = pl.program_id(0)
    lane = lax.broadcasted_iota(jnp.int32, (ATT_TILE + 2 * BLOCK, 2 * ATT_HEAD_DIM), 1)
    kband = jnp.concatenate([kp_ref[...], kc_ref[...], kn_ref[...]], axis=0)
    zero = jnp.zeros_like(kband)
    klo_ref[...] = jnp.where(lane < ATT_HEAD_DIM, kband, zero)
    khi_ref[...] = jnp.where(lane < ATT_HEAD_DIM, zero, kband)
    vband = jnp.concatenate([vp_ref[...], vc_ref[...], vn_ref[...]], axis=0)
    vt_ref[0:2 * ATT_HEAD_DIM, :] = vband.T
    vt_ref[2 * ATT_HEAD_DIM:, :] = jnp.ones((ONES_ROWS, ATT_TILE + 2 * BLOCK), BF16)
    feat = lax.broadcasted_iota(jnp.int32, (2 * ATT_HEAD_DIM, ATT_GROUP * BLOCK), 0)
    def block(s):
        first, last = _seq_pos(i * ATT_SUB + s, *seq)
        variant = jnp.where(first, 1, jnp.where(last, 2, 0))
        q = q_ref[s * BLOCK:(s + 1) * BLOCK, :]
        q_all = jnp.concatenate([q[:, t * 128:(t + 1) * 128] for t in range(ATT_GROUP)], axis=0)
        vt = vt_ref[:, s * BLOCK:(s + 3) * BLOCK]
        scores = [_dot_nt(k_ref[s * BLOCK:(s + 3) * BLOCK, :], q_all) for k_ref in (klo_ref, khi_ref)]
        yield
        ms, ps = [], []
        for kv in range(ATT_KV_HEADS):
            logits = scores[kv] + bias_ref[variant, kv]
            ms.append(jnp.maximum(jnp.max(logits, axis=0, keepdims=True), sink_ref[kv]))
            ps.append(jnp.exp(logits - ms[kv]).astype(BF16))
        yield
        ovs = [_dot(vt, p) for p in ps]
        yield
        outs = []
        for kv in range(ATT_KV_HEADS):
            den = ovs[kv][2 * ATT_HEAD_DIM:2 * ATT_HEAD_DIM + 1, :] + jnp.exp(sink_ref[kv] - ms[kv])
            outs.append(ovs[kv][0:2 * ATT_HEAD_DIM, :] * (1.0 / den))
        both = jnp.where(feat < ATT_HEAD_DIM, outs[0], outs[1]).astype(BF16)
        for t in range(ATT_GROUP):
            o_ref[t * 128:(t + 1) * 128, s * BLOCK:(s + 1) * BLOCK] = both[:, t * BLOCK:(t + 1) * BLOCK]

    _trace_stagewise([block(s) for s in range(ATT_SUB)])


def _attention(z, bias, sink, seq):
    t = z.shape[0]
    n = t // ATT_TILE
    nblk = t // BLOCK
    band = ATT_TILE + 2 * BLOCK
    prev = lambda i: jnp.maximum(i * ATT_SUB - 1, 0)
    nxt = lambda i: jnp.minimum((i + 1) * ATT_SUB, nblk - 1)
    return pl.pallas_call(
        functools.partial(_attention_kernel, seq),
        grid=(n,),
        in_specs=[
            pl.BlockSpec((ATT_TILE, ATT_WIDTH), lambda i: (i, ZB_ATTQ)),
            pl.BlockSpec((BLOCK, 128), lambda i: (prev(i), ZB_ATTK)),
            pl.BlockSpec((ATT_TILE, 128), lambda i: (i, ZB_ATTK)),
            pl.BlockSpec((BLOCK, 128), lambda i: (nxt(i), ZB_ATTK)),
            pl.BlockSpec((BLOCK, 128), lambda i: (prev(i), ZB_ATTV)),
            pl.BlockSpec((ATT_TILE, 128), lambda i: (i, ZB_ATTV)),
            pl.BlockSpec((BLOCK, 128), lambda i: (nxt(i), ZB_ATTV)),
            pl.BlockSpec((3, ATT_KV_HEADS, 3 * BLOCK, ATT_GROUP * BLOCK), lambda i: (0, 0, 0, 0)),
            pl.BlockSpec((ATT_KV_HEADS, 1, ATT_GROUP * BLOCK), lambda i: (0, 0, 0)),
        ],
        out_specs=pl.BlockSpec((ATT_WIDTH, ATT_TILE), lambda i: (0, i)),
        out_shape=jax.ShapeDtypeStruct((ATT_WIDTH, t), BF16),
        scratch_shapes=[pltpu.VMEM((band, 128), BF16), pltpu.VMEM((band, 128), BF16),
                        pltpu.VMEM((2 * ATT_HEAD_DIM + ONES_ROWS, band), BF16)],
        compiler_params=_cparams(("arbitrary",)),
        name="window_attention",
    )(z, z, z, z, z, z, z, bias, sink)


def _t5_bucket(rel):
    half = N_BUCKETS // 2
    exact = half // 2
    n = np.abs(rel)
    large = exact + (np.log(np.maximum(n, 1) / exact) / np.log(MAX_DISTANCE / exact) * (half - exact)).astype(np.int32)
    large = np.minimum(large, half - 1)
    return ((rel > 0).astype(np.int32) * half + np.where(n < exact, n, large)).astype(np.int32)


def _attention_tables(rel_bias, attn_sink):
    rel = np.arange(3 * BLOCK)[:, None] - BLOCK - np.arange(BLOCK)[None, :]
    onehot = jnp.asarray(_t5_bucket(rel)[..., None] == np.arange(N_BUCKETS), F32)
    bias = jnp.einsum('kqb,bh->hkq', onehot, rel_bias.astype(F32), precision=lax.Precision.HIGHEST)
    bias = jnp.where(jnp.asarray(np.abs(rel) <= WINDOW)[None], bias, NEG)
    bias = bias.reshape(ATT_KV_HEADS, ATT_GROUP, 3 * BLOCK, BLOCK).transpose(0, 2, 1, 3)
    bias = bias.reshape(ATT_KV_HEADS, 3 * BLOCK, ATT_GROUP * BLOCK)
    key = np.arange(3 * BLOCK)[None, :, None]
    first = jnp.where(jnp.asarray(key < BLOCK), NEG, bias)
    last = jnp.where(jnp.asarray(key >= 2 * BLOCK), NEG, bias)
    sink = jnp.repeat(attn_sink.astype(F32), BLOCK).reshape(ATT_KV_HEADS, 1, ATT_GROUP * BLOCK)
    return jnp.stack([bias, first, last]), sink


def _mlstm_kernel(seq, n_chunks, *refs):
    fwd_in, bwd_in, (of_ref, ob_ref), state = refs[0:5], refs[5:10], refs[10:12], refs[12:]
    _trace_stagewise(_mlstm_chains(seq, False, n_chunks, *fwd_in, of_ref, *state[0:2])
                     + _mlstm_chains(seq, True, n_chunks, *bwd_in, ob_ref, *state[2:4]))


def _mlstm_chains(seq, reverse, n_chunks, q_ref, k_ref, v_ref, gr_ref, gc_ref, o_ref, ct_ref, m_ref):
    step = pl.program_id(0)
    chunk = (n_chunks - 1 - step) if reverse else step
    first, last = _seq_pos(chunk, *seq)
    fresh = last if reverse else first

    L = ML_CHUNK
    row = lax.broadcasted_iota(jnp.int32, (L, L), 0)
    col = lax.broadcasted_iota(jnp.int32, (L, L), 1)
    vis = (row >= col) if reverse else (row <= col)
    vis_t = (col >= row) if reverse else (col <= row)
    gr = gr_ref[...]
    gc = gc_ref[...]
    r3 = _dot(jnp.concatenate(_split3(_log_sigmoid(gr)), axis=0), vis.astype(BF16))
    b_rows = r3[0:N_GATES] + r3[N_GATES:2 * N_GATES] + r3[2 * N_GATES:]
    vis_tb = vis_t.astype(BF16)
    c_hi, c_mid, c_lo = _split3(_log_sigmoid(gc))
    b_cols = _dot(vis_tb, c_hi) + _dot(vis_tb, c_mid) + _dot(vis_tb, c_lo)
    i_off = 2 * ML_HEADS if reverse else 0
    f_off = i_off + ML_HEADS
    end = 0 if reverse else L - 1
    ones = jnp.ones((ONES_ROWS, L), BF16)

    def chain(h):
        hs = slice(h * ML_HEAD_DIM, (h + 1) * ML_HEAD_DIM)
        q = q_ref[:, hs]
        k = k_ref[:, hs]
        vt1 = jnp.concatenate([v_ref[:, hs].T, ones], axis=0)
        b_row = b_rows[f_off + h:f_off + h + 1, :]
        u_row = gr[i_off + h:i_off + h + 1, :] - b_row
        u_col = gc[:, i_off + h:i_off + h + 1] - b_cols[:, f_off + h:f_off + h + 1]
        g = b_row[:, end:end + 1]
        ct_old = jnp.where(fresh, 0.0, ct_ref[h])
        m_old = jnp.where(fresh, 0.0, m_ref[h])[:, 0:1]
        kq = _dot_nt(k, q)
        from_state = _dot_nt(ct_old.astype(BF16), q)
        yield
        u_mat = jnp.where(vis, u_col, NEG)
        mm = jnp.maximum(jnp.max(u_mat, axis=0, keepdims=True), m_old)
        st = (kq * jnp.exp(u_mat - mm)).astype(BF16)
        yield
        tot = _dot(vt1, st) + jnp.exp(m_old - mm) * from_state
        a_max = jnp.max(g + u_row, axis=-1, keepdims=True)
        m_new = jnp.maximum(g + m_old, a_max)
        s_old = jnp.exp(g + m_old - m_new)
        weighted = (vt1.astype(F32) * jnp.exp(g + u_row - m_new)).astype(BF16)
        new_state = _dot(weighted, k)
        yield
        den = tot[ML_HEAD_DIM:ML_HEAD_DIM + 1, :]
        floor = jnp.exp(-(b_row + mm))
        o_ref[hs, :] = (tot[0:ML_HEAD_DIM, :] * (1.0 / jnp.maximum(jnp.abs(den), floor))).astype(BF16)
        ct_ref[h] = s_old * ct_old + new_state
        m_ref[h] = jnp.broadcast_to(m_new, (1, ML_HEAD_DIM))

    return [chain(h) for h in range(ML_HEADS)]


def _mlstm(qk, z, gates_r, gates_c, seq):
    t = qk.shape[0]
    nc = t // ML_CHUNK

    def chunk_specs(ch):
        return [
            pl.BlockSpec((ML_CHUNK, ML_WIDTH), lambda i: (ch(i), 0)),
            pl.BlockSpec((ML_CHUNK, ML_WIDTH), lambda i: (ch(i), 1)),
            pl.BlockSpec((ML_CHUNK, ML_WIDTH), lambda i: (ch(i), ZB_MLV)),
            pl.BlockSpec((N_GATES, ML_CHUNK), lambda i: (0, ch(i))),
            pl.BlockSpec((ML_CHUNK, N_GATES), lambda i: (ch(i), 0)),
        ]

    fwd = lambda i: i
    bwd = lambda i: nc - 1 - i
    state = [
        pltpu.VMEM((ML_HEADS, ML_HEAD_DIM + ONES_ROWS, ML_HEAD_DIM), F32),
        pltpu.VMEM((ML_HEADS, 1, ML_HEAD_DIM), F32),
    ]
    operands = (qk, qk, z, gates_r, gates_c)
    return pl.pallas_call(
        functools.partial(_mlstm_kernel, seq, nc),
        grid=(nc,),
        in_specs=chunk_specs(fwd) + chunk_specs(bwd),
        out_specs=[pl.BlockSpec((ML_WIDTH, ML_CHUNK), lambda i: (0, fwd(i))),
                   pl.BlockSpec((ML_WIDTH, ML_CHUNK), lambda i: (0, bwd(i)))],
        out_shape=[jax.ShapeDtypeStruct((ML_WIDTH, t), BF16), jax.ShapeDtypeStruct((ML_WIDTH, t), BF16)],
        scratch_shapes=state + state,
        compiler_params=_cparams(("arbitrary",)),
        name="mlstm",
    )(*operands, *operands)


def _mix_out(n_p_tiles, toks, xp_ref, xs_ref, att_ref, hf_ref, hb_ref, og_ref, wa_ref, wm_ref, gain_ref, g_ref,
             b_ref):
    i = pl.program_id(0)
    x = jnp.where(i < n_p_tiles, xp_ref[toks, :], xs_ref[toks, :])
    h = hf_ref[:, toks].astype(F32) + hb_ref[:, toks].astype(F32)
    parts = []
    for hd in range(ML_HEADS):
        hh = h[hd * ML_HEAD_DIM:(hd + 1) * ML_HEAD_DIM, :]
        mu = jnp.mean(hh, axis=0, keepdims=True)
        hc = hh - mu
        var = jnp.mean(hc * hc, axis=0, keepdims=True)
        parts.append(hc * lax.rsqrt(var + LN_EPS))
    hn = jnp.concatenate(parts, axis=0) * gain_ref[...] * _sigmoid(og_ref[toks, :].T.astype(F32))
    mixed = _dot(att_ref[:, toks].T, wa_ref[...]) + _dot(hn.astype(BF16).T, wm_ref[...])
    return _layer_norm(DN_ALPHA * x + mixed, g_ref[...], b_ref[...])


def _mem_kv_kernel(m_ref, w_ref, o_ref):
    o_ref[...] = _dot(m_ref[...].astype(BF16), w_ref[...]).astype(BF16)


def _mem_kv(mem, wkv):
    rows = mem.shape[0]
    return pl.pallas_call(
        _mem_kv_kernel,
        grid=(rows // MEM_TOKENS,),
        in_specs=[pl.BlockSpec((MEM_TOKENS, D_MODEL), lambda i: (i, 0)),
                  pl.BlockSpec((D_MODEL, 2 * D_MODEL), lambda i: (0, 0))],
        out_specs=pl.BlockSpec((MEM_TOKENS, 2 * D_MODEL), lambda i: (i, 0)),
        out_shape=jax.ShapeDtypeStruct((rows, 2 * D_MODEL), BF16),
        compiler_params=_cparams(("arbitrary",)),
        name="mem_kv",
    )(mem, wkv)


def _pack_bf16_pairs(a, b, exact=False):
    if not exact:
        a, b = a.astype(BF16).astype(F32), b.astype(BF16).astype(F32)
    ua, ub = pltpu.bitcast(a, jnp.uint32), pltpu.bitcast(b, jnp.uint32)
    return (ua >> 16) | (ub & jnp.uint32(0xFFFF0000))


def _unpack_bf16_pairs(u):
    lo = pltpu.bitcast(u << 16, F32)
    hi = pltpu.bitcast(u & jnp.uint32(0xFFFF0000), F32)
    return lo, hi


TOKEN_GROUPS = 2


def _cross_router_kernel(n_p_tiles, *refs):
    mix_refs, (kv_ref, wq_ref, wo_ref, g_ref, b_ref, wr_ref, br_ref, tri_ref, etri_ref,
               x2_ref, x2b_ref, gate_ref, slot_ref, chunk_ref) = refs[:11], refs[11:]
    head_cols = [slice(h * MEM_HEAD_DIM, (h + 1) * MEM_HEAD_DIM) for h in range(MEM_HEADS)]

    def token_group(r):
        toks = slice(r * ROW_TILE // TOKEN_GROUPS, (r + 1) * ROW_TILE // TOKEN_GROUPS)
        x = _mix_out(n_p_tiles, toks, *mix_refs)
        yield
        q = _dot(x.astype(BF16), wq_ref[...]).astype(BF16)
        scores = [_dot_nt(q[:, hs], kv_ref[:, hs]) for hs in head_cols]
        yield
        probs, dens = [], []
        for logits in scores:
            p = jnp.exp(logits - jnp.max(logits, axis=-1, keepdims=True))
            dens.append(jnp.sum(p, axis=-1, keepdims=True))
            probs.append(p.astype(BF16))
        yield
        values = [_dot(p, kv_ref[:, D_MODEL + hs.start:D_MODEL + hs.stop]) for p, hs in zip(probs, head_cols)]
        o = jnp.concatenate([(v * (1.0 / den)).astype(BF16) for v, den in zip(values, dens)], axis=1)
        y = _dot(o, wo_ref[...])
        yield
        x2 = _layer_norm(DN_ALPHA * x + y, g_ref[...], b_ref[...])
        x2_ref[toks, :] = x2
        x2b_ref[toks, :] = x2.astype(BF16)

    _trace_stagewise([token_group(r) for r in range(TOKEN_GROUPS)])
    x2b = x2b_ref[...]

    logits = _dot_nt(wr_ref[...], x2b) + br_ref[...]
    expert = lax.broadcasted_iota(jnp.int32, logits.shape, 0)
    work = logits
    vals, sels = [], []
    for k in range(TOP_K):
        mx = jnp.max(work, axis=0, keepdims=True)
        ix = jnp.min(jnp.where(work == mx, expert, N_EXPERTS), axis=0, keepdims=True)
        sel = expert == ix
        work = jnp.where(sel, -jnp.inf, work)
        vals.append(mx)
        sels.append(sel)
    es = [jnp.exp(v - vals[0]) for v in vals]
    tot = es[0] + es[1] + es[2] + es[3]
    chosen = jnp.zeros(logits.shape, F32)
    for k in range(TOP_K):
        gate_ref[k:k + 1, :] = es[k] / tot
        chosen = chosen + sels[k].astype(F32)
    count = jnp.sum(chosen, axis=1, keepdims=True)
    chunks = jnp.floor((count + (SEG_ALIGN - 1)) * (1.0 / SEG_ALIGN))
    chunks_b = jnp.broadcast_to(chunks, (N_EXPERTS, 128))
    seg_first = _dot(etri_ref[...], chunks_b.astype(BF16))[:, 0:1] * SEG_ALIGN
    before = _dot(chosen.astype(BF16), tri_ref[...])
    local_row = before + seg_first
    for k in range(TOP_K):
        slot_ref[k:k + 1, :] = jnp.sum(jnp.where(sels[k], local_row, 0.0), axis=0, keepdims=True).astype(jnp.int32)
    chunk_ref[0] = chunks_b.astype(jnp.int32)


def _cross_router(xp, xs, att, hf, hb, z, w_att, w_ml, gain, g1, b1, kv, wq, wo, g, b, w_router, b_router,
                  mem_of_tile):
    t = att.shape[1]
    n_p = xp.shape[0] // ROW_TILE
    n = t // ROW_TILE
    const = lambda i: (0, 0)
    tile = lambda i: (i, 0)
    lanes = lambda i: (0, i)
    tri = jnp.asarray(np.triu(np.ones((ROW_TILE, ROW_TILE), np.float32), 1), BF16)
    etri = jnp.asarray(np.tril(np.ones((N_EXPERTS, N_EXPERTS), np.float32), -1), BF16)
    return pl.pallas_call(
        functools.partial(_cross_router_kernel, n_p),
        grid=(n,),
        in_specs=[
            pl.BlockSpec((ROW_TILE, D_MODEL), lambda i: (jnp.minimum(i, n_p - 1), 0)),
            pl.BlockSpec((ROW_TILE, D_MODEL), lambda i: (jnp.maximum(i - n_p, 0), 0)),
            pl.BlockSpec((ATT_WIDTH, ROW_TILE), lanes),
            pl.BlockSpec((ML_WIDTH, ROW_TILE), lanes),
            pl.BlockSpec((ML_WIDTH, ROW_TILE), lanes),
            pl.BlockSpec((ROW_TILE, ML_WIDTH), lambda i: (i, ZB_MLO)),
            pl.BlockSpec((ATT_WIDTH, D_MODEL), const),
            pl.BlockSpec((ML_WIDTH, D_MODEL), const),
            pl.BlockSpec((ML_WIDTH, 1), const),
            pl.BlockSpec((1, D_MODEL), const),
            pl.BlockSpec((1, D_MODEL), const),
            pl.BlockSpec((MEM_TOKENS, 2 * D_MODEL), lambda i: (mem_of_tile(i), 0)),
            pl.BlockSpec((D_MODEL, D_MODEL), const),
            pl.BlockSpec((D_MODEL, D_MODEL), const),
            pl.BlockSpec((1, D_MODEL), const),
            pl.BlockSpec((1, D_MODEL), const),
            pl.BlockSpec((N_EXPERTS, D_MODEL), const),
            pl.BlockSpec((N_EXPERTS, 1), const),
            pl.BlockSpec((ROW_TILE, ROW_TILE), const),
            pl.BlockSpec((N_EXPERTS, N_EXPERTS), const),
        ],
        out_specs=[
            pl.BlockSpec((ROW_TILE, D_MODEL), tile),
            pl.BlockSpec((ROW_TILE, D_MODEL), tile),
            pl.BlockSpec((TOP_K, ROW_TILE), lanes),
            pl.BlockSpec((TOP_K, ROW_TILE), lanes),
            pl.BlockSpec((1, N_EXPERTS, 128), lambda i: (i, 0, 0)),
        ],
        out_shape=[
            jax.ShapeDtypeStruct((t, D_MODEL), F32),
            jax.ShapeDtypeStruct((t, D_MODEL), BF16),
            jax.ShapeDtypeStruct((TOP_K, t), F32),
            jax.ShapeDtypeStruct((TOP_K, t), jnp.int32),
            jax.ShapeDtypeStruct((n, N_EXPERTS, 128), jnp.int32),
        ],
        compiler_params=_cparams(("arbitrary",)),
        name="mix_out_cross_attn_router",
    )(xp, xs, att, hf, hb, z, w_att, w_ml, gain.reshape(ML_WIDTH, 1), g1.reshape(1, D_MODEL), b1.reshape(1, D_MODEL),
      kv, wq, wo, g.reshape(1, D_MODEL), b.reshape(1, D_MODEL), w_router.T.astype(BF16),
      b_router.reshape(N_EXPERTS, 1), tri, etri)


SEG_ALIGN = 8
LOCAL_ROWS = ROW_TILE * TOP_K + N_EXPERTS * SEG_ALIGN
HALF = D_MODEL // 2


def _rows(chunks):
    return pl.multiple_of(chunks * SEG_ALIGN, SEG_ALIGN)


def _segment_copies(chunks_ref, seg_ref, loc_ref, tile, make_copy):
    for e in range(N_EXPERTS):
        entry = tile * N_EXPERTS + e

        @pl.when(chunks_ref[entry] > 0)
        def _():
            make_copy(_rows(loc_ref[entry]), _rows(seg_ref[entry]), _rows(chunks_ref[entry])).start(priority=e % 2)


def _wait_copies(chunks, make_copy):
    @pl.when(chunks > 0)
    def _():
        make_copy(0, 0, _rows(chunks)).wait()


def _dispatch_kernel(chunks_ref, seg_ref, loc_ref, total_ref, tail_ref, tailn_ref, nact_ref,
                     x_ref, slot_ref, rows_hbm, local_ref, zero_ref, sems):
    i = pl.program_id(0)
    n = pl.num_programs(0)
    buf = i % 2

    def copy_out(b):
        return lambda loc, seg, rows: pltpu.make_async_copy(
            local_ref.at[b, pl.ds(loc, rows), :], rows_hbm.at[pl.ds(seg, rows), :], sems.at[b])

    def run(b):
        @pl.when(i >= 2)
        def _():
            _wait_copies(total_ref[jnp.maximum(i - 2, 0)], copy_out(b))

        slots = slot_ref[...].astype(F32)
        row = lax.broadcasted_iota(jnp.int32, (LOCAL_ROWS, ROW_TILE), 0).astype(F32)
        miss = (slots[0:1, :] - row) * (slots[1:2, :] - row)
        for k in range(2, TOP_K):
            miss = miss * (slots[k:k + 1, :] - row)
        perm = jnp.where(miss == 0.0, 1.0, 0.0).astype(BF16)
        picked = _dot(perm, x_ref[...])
        local_ref[b] = _pack_bf16_pairs(picked[:, :HALF], picked[:, HALF:], exact=True)
        _segment_copies(chunks_ref, seg_ref, loc_ref, i, copy_out(b))

    for b in range(2):
        pl.when(buf == b)(functools.partial(run, b))

    @pl.when(i == n - 1)
    def _():
        for b in range(2):
            step = jnp.where(buf == b, i, i - 1)
            _wait_copies(total_ref[step], copy_out(b))
        zero_ref[...] = jnp.zeros_like(zero_ref)
        fill = lambda _, seg, rows: pltpu.make_async_copy(
            zero_ref.at[pl.ds(0, rows), :], rows_hbm.at[pl.ds(seg, rows), :], sems.at[2])

        def per_expert(e, carry):
            @pl.when(tailn_ref[e] > 0)
            def _():
                fill(0, _rows(tail_ref[e]), _rows(tailn_ref[e])).start()

            _wait_copies(tailn_ref[e], fill)
            return carry

        lax.fori_loop(0, N_EXPERTS, per_expert, 0)
        fill_block = lambda blk: pltpu.make_async_copy(
            zero_ref, rows_hbm.at[pl.ds(pl.multiple_of(blk * MOE_ROWS, MOE_ROWS), MOE_ROWS), :], sems.at[2])
        n_blocks = rows_hbm.shape[0] // MOE_ROWS

        def start_block(blk, carry):
            fill_block(blk).start()
            return carry

        def wait_block(blk, carry):
            fill_block(0).wait()
            return carry

        lax.fori_loop(nact_ref[0], n_blocks, start_block, 0)
        lax.fori_loop(nact_ref[0], n_blocks, wait_block, 0)


def _dispatch(x2b, slots, tables, n_rows):
    n = x2b.shape[0] // ROW_TILE
    assert n >= 2
    return pl.pallas_call(
        _dispatch_kernel,
        grid_spec=pltpu.PrefetchScalarGridSpec(
            num_scalar_prefetch=7,
            grid=(n,),
            in_specs=[
                pl.BlockSpec((ROW_TILE, D_MODEL), lambda i, *_: (i, 0)),
                pl.BlockSpec((TOP_K, ROW_TILE), lambda i, *_: (0, i)),
            ],
            out_specs=pl.BlockSpec(memory_space=pl.ANY),
            scratch_shapes=[
                pltpu.VMEM((2, LOCAL_ROWS, HALF), jnp.uint32),
                pltpu.VMEM((MOE_ROWS, HALF), jnp.uint32),
                pltpu.SemaphoreType.DMA((3,)),
            ],
        ),
        out_shape=jax.ShapeDtypeStruct((n_rows, HALF), jnp.uint32),
        compiler_params=_cparams(("arbitrary",)),
        name="moe_dispatch",
    )(*tables, x2b, slots)


def _expert_kernel(be_ref, na_ref, half_ref, x_ref, wu_ref, bu_ref, wd_ref, bd_ref, o_ref, wub_ref, wdb_ref):
    i = pl.program_id(0)

    @pl.when((i == 0) | (be_ref[i] != be_ref[jnp.maximum(i - 1, 0)]))
    def _():
        wub_ref[...] = wu_ref[0].astype(BF16)
        wdb_ref[...] = wd_ref[0].astype(BF16)

    def mlp(rows):
        lo, hi = _unpack_bf16_pairs(x_ref[rows, :])
        x = jnp.concatenate([lo, hi], axis=1).astype(BF16)
        hu = _dot(x, wub_ref[...]) + bu_ref[0]
        h_glu = jnp.minimum(hu[:, :D_FF], SWIGLU_LIMIT)
        h_lin = jnp.clip(hu[:, D_FF:], -SWIGLU_LIMIT, SWIGLU_LIMIT)
        hh = h_glu * _sigmoid(SWIGLU_ALPHA * h_glu) * (h_lin + 1.0)
        y = _dot(hh.astype(BF16), wdb_ref[...]) + bd_ref[0]
        o_ref[rows, :] = _pack_bf16_pairs(y[:, :D_MODEL // 2], y[:, D_MODEL // 2:])

    active = i < na_ref[0]
    half_full = half_ref[i] == 1
    top, bottom = slice(0, MOE_ROWS // 2), slice(MOE_ROWS // 2, MOE_ROWS)

    @pl.when(active & jnp.logical_not(half_full))
    def _():
        mlp(slice(0, MOE_ROWS))

    @pl.when(active & half_full)
    def _():
        mlp(top)
        o_ref[bottom, :] = jnp.zeros((MOE_ROWS // 2, D_MODEL // 2), jnp.uint32)

    @pl.when(jnp.logical_not(active))
    def _():
        o_ref[...] = jnp.zeros_like(o_ref)


def _experts(xs, block_e, n_active, half_full, w_up, b_up, w_down, b_down):
    n_rows = xs.shape[0]
    n_blk = n_rows // MOE_ROWS
    return pl.pallas_call(
        _expert_kernel,
        grid_spec=pltpu.PrefetchScalarGridSpec(
            num_scalar_prefetch=3,
            grid=(n_blk,),
            in_specs=[
                pl.BlockSpec((MOE_ROWS, D_MODEL // 2), lambda i, be, na, hf: (jnp.minimum(i, na[0] - 1), 0)),
                pl.BlockSpec((1, D_MODEL, 2 * D_FF), lambda i, be, na, hf: (be[i], 0, 0)),
                pl.BlockSpec((1, 1, 2 * D_FF), lambda i, be, na, hf: (be[i], 0, 0)),
                pl.BlockSpec((1, D_FF, D_MODEL), lambda i, be, na, hf: (be[i], 0, 0)),
                pl.BlockSpec((1, 1, D_MODEL), lambda i, be, na, hf: (be[i], 0, 0)),
            ],
            out_specs=pl.BlockSpec((MOE_ROWS, D_MODEL // 2), lambda i, be, na, hf: (i, 0)),
            scratch_shapes=[pltpu.VMEM((D_MODEL, 2 * D_FF), BF16), pltpu.VMEM((D_FF, D_MODEL), BF16)],
        ),
        out_shape=jax.ShapeDtypeStruct((n_rows, D_MODEL // 2), jnp.uint32),
        compiler_params=_cparams(("arbitrary",)),
        name="experts",
    )(block_e, n_active, half_full, xs, w_up, b_up.reshape(N_EXPERTS, 1, 2 * D_FF), w_down,
      b_down.reshape(N_EXPERTS, 1, D_MODEL))


def _combine_kernel(first_tile, chunks_ref, seg_ref, loc_ref, total_ref,
                    x_ref, rows_hbm, slot_ref, gate_ref, g_ref, b_ref, o_ref, local_ref, sems):
    i = pl.program_id(0)
    n = pl.num_programs(0)
    tile = first_tile + i
    buf = i % 2

    def copy_in(b):
        return lambda loc, seg, rows: pltpu.make_async_copy(
            rows_hbm.at[pl.ds(seg, rows), :], local_ref.at[b, pl.ds(loc, rows), :], sems.at[b])

    @pl.when(i == 0)
    def _():
        local_ref[...] = jnp.zeros_like(local_ref)
        _segment_copies(chunks_ref, seg_ref, loc_ref, tile, copy_in(0))

    def run(b):
        @pl.when(i + 1 < n)
        def _():
            _segment_copies(chunks_ref, seg_ref, loc_ref, tile + 1, copy_in(1 - b))

        _wait_copies(total_ref[tile], copy_in(b))
        lo, hi = _unpack_bf16_pairs(local_ref[b])
        rows = jnp.concatenate([lo, hi], axis=1).astype(BF16)
        slots = slot_ref[...]
        gates = gate_ref[...]
        row = lax.broadcasted_iota(jnp.int32, (ROW_TILE, LOCAL_ROWS), 1)
        weight = jnp.zeros((ROW_TILE, LOCAL_ROWS), F32)
        for k in range(TOP_K):
            weight = jnp.where(slots[:, k:k + 1] == row, gates[:, k:k + 1], weight)
        y = _dot(weight.astype(BF16), rows)
        o_ref[...] = _layer_norm(DN_ALPHA * x_ref[...] + y, g_ref[...], b_ref[...])

    for b in range(2):
        pl.when(buf == b)(functools.partial(run, b))


def _combine(x2, rows_out, slots_c, gates_c, tables, g, b, first_tile, n_tiles):
    const = lambda i, *_: (0, 0)
    tile = lambda i, *_: (first_tile + i, 0)
    return pl.pallas_call(
        functools.partial(_combine_kernel, first_tile),
        grid_spec=pltpu.PrefetchScalarGridSpec(
            num_scalar_prefetch=4,
            grid=(n_tiles,),
            in_specs=[
                pl.BlockSpec((ROW_TILE, D_MODEL), tile),
                pl.BlockSpec(memory_space=pl.ANY),
                pl.BlockSpec((ROW_TILE, TOP_K), tile),
                pl.BlockSpec((ROW_TILE, TOP_K), tile),
                pl.BlockSpec((1, D_MODEL), const),
                pl.BlockSpec((1, D_MODEL), const),
            ],
            out_specs=pl.BlockSpec((ROW_TILE, D_MODEL), lambda i, *_: (i, 0)),
            scratch_shapes=[pltpu.VMEM((2, LOCAL_ROWS, HALF), jnp.uint32), pltpu.SemaphoreType.DMA((2,))],
        ),
        out_shape=jax.ShapeDtypeStruct((n_tiles * ROW_TILE, D_MODEL), F32),
        compiler_params=_cparams(("arbitrary",)),
        name="combine_ln3",
    )(*tables[:4], x2, rows_out, slots_c, gates_c, g.reshape(1, D_MODEL), b.reshape(1, D_MODEL))


def kernel(x_prompt, x_sample, mem_prompt, mem_sample, rel_bias, w_in, b_gates, conv_w, conv_b, mh_gain, attn_sink,
           w_out, ln1_g, ln1_b, wq_mem, wkv_mem, wo_mem, ln2_g, ln2_b, w_router, b_router, w_up, b_up, w_down,
           b_down, ln3_g, ln3_b):
    assert w_in.shape[0] == 1, "single layer"
    bp, sp, _ = x_prompt.shape
    bs, ss, _ = x_sample.shape
    tp, ts = bp * sp, bs * ss
    t = tp + ts
    assert sp % ATT_TILE == 0 and ss % ATT_TILE == 0 and ATT_TILE % ROW_TILE == 0 and ATT_TILE % ML_CHUNK == 0
    xp = x_prompt.reshape(tp, D_MODEL)
    xs = x_sample.reshape(ts, D_MODEL)

    def seq_blocks(rows):
        return (tp // rows, sp // rows, ss // rows)

    w = w_in[0]
    q_end, k_end, v_end = ATT_WIDTH, ATT_WIDTH + 128, ATT_WIDTH + 256
    qk_end, mv_end, mo_end = v_end + 2 * ML_WIDTH, v_end + 3 * ML_WIDTH, v_end + 4 * ML_WIDTH
    head_order = np.concatenate([[h, ATT_GROUP + h] for h in range(ATT_GROUP)])
    att_perm = (head_order[:, None] * ATT_HEAD_DIM + np.arange(ATT_HEAD_DIM)[None, :]).reshape(-1)
    w_qk = w[:, v_end:qk_end].astype(BF16)
    w_main = jnp.concatenate([
        w[:, qk_end:mv_end], w[:, mv_end:mo_end],
        w[:, :q_end][:, att_perm] * (ATT_HEAD_DIM ** -0.5), w[:, q_end:k_end], w[:, k_end:v_end]], axis=1).astype(BF16)
    w_g = w[:, mo_end:].astype(BF16)
    w_att = w_out[0][:ATT_WIDTH][att_perm].astype(BF16)
    w_ml = w_out[0][ATT_WIDTH:].astype(BF16)

    qk, z, gates_c, gates_r = _in_proj(xp, xs, w_qk, w_main, w_g, b_gates[0], conv_w[0], conv_b[0],
                                       seq_blocks(ROW_TILE))

    bias, sink = _attention_tables(rel_bias, attn_sink[0])
    att = _attention(z, bias, sink, seq_blocks(BLOCK))

    h_f, h_b = _mlstm(qk, z, gates_r, gates_c, seq_blocks(ML_CHUNK))

    mem = jnp.concatenate([mem_prompt.reshape(bp * MEM_TOKENS, D_MODEL), mem_sample.reshape(bs * MEM_TOKENS, D_MODEL)])
    kv = _mem_kv(mem, wkv_mem[0].astype(BF16))
    n_p_tiles, p_tiles_per_seq, s_tiles_per_seq = seq_blocks(ROW_TILE)
    mem_of_tile = lambda i: jnp.where(i < n_p_tiles, i // p_tiles_per_seq, bp + (i - n_p_tiles) // s_tiles_per_seq)
    wq = (wq_mem[0] * (MEM_HEAD_DIM ** -0.5)).astype(BF16)
    x2, x2b, top_gate, slots, tile_chunks = _cross_router(
        xp, xs, att, h_f, h_b, z, w_att, w_ml, mh_gain[0], ln1_g[0], ln1_b[0],
        kv, wq, wo_mem[0].astype(BF16), ln2_g[0], ln2_b[0], w_router[0], b_router[0], mem_of_tile)

    n_tiles = t // ROW_TILE
    blk_chunks = MOE_ROWS // SEG_ALIGN
    chunks = tile_chunks[:, :, 0]
    used = jnp.sum(chunks, axis=0)
    region = ((used + blk_chunks - 1) // blk_chunks) * blk_chunks
    region_end = jnp.cumsum(region)
    region_start = region_end - region
    seg_start = region_start[None, :] + jnp.cumsum(chunks, axis=0) - chunks
    loc_start = jnp.cumsum(chunks, axis=1) - chunks
    n_blk = -(-(t * TOP_K + n_tiles * N_EXPERTS * (SEG_ALIGN - 1) + N_EXPERTS * (MOE_ROWS - 1)) // MOE_ROWS)
    blk_first = jnp.arange(n_blk, dtype=jnp.int32) * blk_chunks
    block_e = jnp.minimum(jnp.sum(blk_first[:, None] >= region_end[None, :], axis=1), N_EXPERTS - 1).astype(jnp.int32)
    n_active = (region_end[-1:] // blk_chunks).astype(jnp.int32)
    owner = block_e[:, None] == jnp.arange(N_EXPERTS, dtype=jnp.int32)[None, :]
    used_end = jnp.sum(jnp.where(owner, (region_start + used)[None, :], 0), axis=1)
    half_full = (used_end - blk_first <= blk_chunks // 2).astype(jnp.int32)
    tables = (chunks.reshape(-1), seg_start.reshape(-1), loc_start.reshape(-1), jnp.sum(chunks, axis=1),
              region_start + used, region - used, n_active)
    tables = tuple(tb.astype(jnp.int32) for tb in tables)

    rows_in = _dispatch(x2b, slots, tables, n_blk * MOE_ROWS)
    rows_out = _experts(rows_in, block_e, n_active, half_full, w_up[0], b_up[0], w_down[0], b_down[0])

    slots_c, gates_c = slots.T, top_gate.T
    y_p = _combine(x2, rows_out, slots_c, gates_c, tables, ln3_g[0], ln3_b[0], 0, n_p_tiles)
    y_s = _combine(x2, rows_out, slots_c, gates_c, tables, ln3_g[0], ln3_b[0], n_p_tiles, n_tiles - n_p_tiles)
    return y_p.reshape(bp, sp, D_MODEL), y_s.reshape(bs, ss, D_MODEL)
```
